```python
import math, functools
import jax, jax.numpy as jnp
from jax import lax
import numpy as np

D_MODEL = 2048
BATCH = 1
SEQ = 8192
DEPTH = 1
DEC_BATCH = 128
DEC_SEQ = 4
PAST_LEN = 16384
PAGE_SIZE = 128

N_HEADS = 32
N_KV_HEADS = 8
HEAD_DIM = 64
GROUP = N_HEADS // N_KV_HEADS
ATTN_WIDTH = N_HEADS * HEAD_DIM
KV_WIDTH = N_KV_HEADS * HEAD_DIM
WINDOW = 128
Q_BLOCK = 128
ATTN_SCALE = HEAD_DIM ** -0.5
N_BUCKETS = 32
MAX_DISTANCE = 128
C_CONV = D_MODEL // 2
CONV_WIDTH = 31
D_FF = 5632
NORM_EPS = 1e-6
NEG_INF = -1e30
IN_WIDTH = ATTN_WIDTH + 2 * KV_WIDTH + 2 * C_CONV + 2 * D_MODEL

kernel_name = 'gated_swa_sink_conformer_macaron_adaln_step'


def rms_norm(x, g):
    xf = x.astype(jnp.float32)
    y = xf * lax.rsqrt(jnp.mean(xf * xf, axis=-1, keepdims=True) + NORM_EPS)
    return (y * g.astype(jnp.float32)).astype(x.dtype)


def layer_norm(x, g, b):
    xf = x.astype(jnp.float32)
    mu = jnp.mean(xf, axis=-1, keepdims=True)
    xc = xf - mu
    y = xc * lax.rsqrt(jnp.mean(xc * xc, axis=-1, keepdims=True) + NORM_EPS)
    return (y * g.astype(jnp.float32) + b.astype(jnp.float32)).astype(x.dtype)


def modulate(x, shift, scale):
    return x * (1.0 + scale[:, None, :]) + shift[:, None, :]


def adaln(c, w, b):
    return jax.nn.silu(c) @ w + b


def swiglu(x, w1, w3, w2):
    return (jax.nn.silu(x @ w1) * (x @ w3)) @ w2


def t5_bucket(dist):
    exact = N_BUCKETS // 2
    d = jnp.maximum(dist, 0)
    df = jnp.maximum(d, 1).astype(jnp.float32)
    large = exact + (jnp.log(df / exact) / math.log(MAX_DISTANCE / exact) * (N_BUCKETS - exact)).astype(jnp.int32)
    large = jnp.minimum(large, N_BUCKETS - 1)
    return jnp.where(d < exact, d, large)


def head_bias(rel_table, dist):
    b = jnp.moveaxis(rel_table[t5_bucket(dist)], -1, 0)
    return b.reshape((N_KV_HEADS, GROUP) + dist.shape).astype(jnp.float32)


def sink_softmax_values(s, sinks, v, eq):
    sink = jnp.broadcast_to(sinks.astype(jnp.float32).reshape(N_KV_HEADS, GROUP, 1, 1), s.shape[:-1] + (1,))
    p = jax.nn.softmax(jnp.concatenate([s, sink], axis=-1), axis=-1)[..., :-1]
    return jnp.einsum(eq, p.astype(v.dtype), v)


def window_attn_prompt(q, k, v, sinks, rel_table):
    B, S = q.shape[0], q.shape[1]
    nb = S // Q_BLOCK
    qb = q.reshape(B, nb, Q_BLOCK, N_KV_HEADS, GROUP, HEAD_DIM)

    def band(t):
        tp = jnp.pad(t, ((0, 0), (Q_BLOCK, 0), (0, 0), (0, 0)))
        tb = tp.reshape(B, nb + 1, Q_BLOCK, N_KV_HEADS, HEAD_DIM)
        return jnp.concatenate([tb[:, :-1], tb[:, 1:]], axis=2)

    kb, vb = band(k), band(v)
    s = jnp.einsum('bnqkgd,bnjkd->bnkgqj', qb, kb).astype(jnp.float32) * ATTN_SCALE
    qi = jnp.arange(Q_BLOCK)[:, None]
    kj = jnp.arange(2 * Q_BLOCK)[None, :]
    dist = qi + Q_BLOCK - kj
    kpos = (jnp.arange(nb) * Q_BLOCK - Q_BLOCK)[:, None, None] + kj[None]
    valid = (dist >= 0) & (dist <= WINDOW) & (kpos >= 0)
    s = s + head_bias(rel_table, dist)
    s = jnp.where(valid[None, :, None, None], s, NEG_INF)
    o = sink_softmax_values(s, sinks, vb, 'bnkgqj,bnjkd->bnqkgd')
    w = min(WINDOW, S)
    return o.reshape(B, S, ATTN_WIDTH), k[:, -w:], v[:, -w:]


def window_attn_sample(q, k, v, sinks, k_buf, v_buf, rel_table):
    Bd, T = q.shape[0], q.shape[1]
    wb = k_buf.shape[1]
    kk = jnp.concatenate([k_buf, k], axis=1)
    vv = jnp.concatenate([v_buf, v], axis=1)
    qpos = PAST_LEN + jnp.arange(T)
    kpos = PAST_LEN - wb + jnp.arange(wb + T)
    dist = qpos[:, None] - kpos[None, :]
    valid = (dist >= 0) & (dist <= WINDOW)
    qg = q.reshape(Bd, T, N_KV_HEADS, GROUP, HEAD_DIM)
    s = jnp.einsum('btkgd,bjkd->bkgtj', qg, kk).astype(jnp.float32) * ATTN_SCALE + head_bias(rel_table, dist)
    s = jnp.where(valid, s, NEG_INF)
    o = sink_softmax_values(s, sinks, vv, 'bkgtj,bjkd->btkgd')
    return o.reshape(Bd, T, ATTN_WIDTH), kk[:, -wb:], vv[:, -wb:]


def causal_dwconv(zh, w, b):
    out = lax.conv_general_dilated(zh, w[:, None, :], window_strides=(1,), padding='VALID',
                                   dimension_numbers=('NWC', 'WIO', 'NWC'), feature_group_count=C_CONV)
    return out + b


def conformer_conv(glu_in, hist, dw_w, dw_b, ln_g, ln_b, w_co):
    a, g = jnp.split(glu_in, 2, axis=-1)
    z = a * jax.nn.sigmoid(g)
    zh = jnp.concatenate([hist, z], axis=1)
    y = causal_dwconv(zh, dw_w, dw_b)
    y = jax.nn.silu(layer_norm(y, ln_g, ln_b))
    return y @ w_co, zh[:, -(CONV_WIDTH - 1):]


def trunk_layer(x, c, lp, attend, conv_hist):
    (n1, f1w1, f1w3, f1w2, n2, w_in, sinks, dw_w, dw_b, ln_g, ln_b, w_co, w_ao, w_out,
     n3, f2w1, f2w3, f2w2, w_ada, b_ada) = lp
    sh1, sc1, g1, sh2, sc2, g2, sh3, sc3, g3 = jnp.split(adaln(c, w_ada, b_ada), 9, axis=-1)
    B, T, _ = x.shape
    h = x + 0.5 * g1[:, None] * swiglu(modulate(rms_norm(x, n1), sh1, sc1), f1w1, f1w3, f1w2)
    u = modulate(rms_norm(h, n2), sh2, sc2)
    proj = u @ w_in
    cuts = np.cumsum([ATTN_WIDTH, KV_WIDTH, KV_WIDTH, 2 * C_CONV, D_MODEL]).tolist()
    q, k, v, glu_in, gate_a, gate_c = jnp.split(proj, cuts, axis=-1)
    attn, k_state, v_state = attend(q.reshape(B, T, N_HEADS, HEAD_DIM),
                                    k.reshape(B, T, N_KV_HEADS, HEAD_DIM),
                                    v.reshape(B, T, N_KV_HEADS, HEAD_DIM), sinks)
    conv, conv_state = conformer_conv(glu_in, conv_hist, dw_w, dw_b, ln_g, ln_b, w_co)
    merged = jax.nn.sigmoid(gate_a) * (attn @ w_ao) + jax.nn.sigmoid(gate_c) * conv
    h = h + g2[:, None] * (merged @ w_out)
    h = h + 0.5 * g3[:, None] * swiglu(modulate(rms_norm(h, n3), sh3, sc3), f2w1, f2w3, f2w2)
    return h, k_state, v_state, conv_state


def setup_inputs(seed: int = 0) -> dict:
    key = jax.random.key(seed)
    ks = iter(jax.random.split(key, 40))

    def nrm(shape, scale):
        return scale * jax.random.normal(next(ks), shape, jnp.float32)

    L = DEPTH
    wbuf = min(WINDOW, PAST_LEN)
    return {
        'x_prompt': nrm((BATCH, SEQ, D_MODEL), 1.0),
        'x_sample': nrm((DEC_BATCH, DEC_SEQ, D_MODEL), 1.0),
        'cache_k': nrm((L, DEC_BATCH, wbuf, N_KV_HEADS, HEAD_DIM), 1.0),
        'cache_v': nrm((L, DEC_BATCH, wbuf, N_KV_HEADS, HEAD_DIM), 1.0),
        'state_conv': nrm((L, DEC_BATCH, CONV_WIDTH - 1, C_CONV), 0.5),
        'c_prompt': nrm((BATCH, D_MODEL), 1.0),
        'c_sample': nrm((DEC_BATCH, D_MODEL), 1.0),
        'rel_bias_table': nrm((N_BUCKETS, N_HEADS), 0.5),
        'norm1_g': 1.0 + nrm((L, D_MODEL), 0.02),
        'ffn1_w1': nrm((L, D_MODEL, D_FF), D_MODEL ** -0.5),
        'ffn1_w3': nrm((L, D_MODEL, D_FF), D_MODEL ** -0.5),
        'ffn1_w2': nrm((L, D_FF, D_MODEL), D_FF ** -0.5),
        'norm2_g': 1.0 + nrm((L, D_MODEL), 0.02),
        'w_in': nrm((L, D_MODEL, IN_WIDTH), D_MODEL ** -0.5),
        'attn_sinks': nrm((L, N_HEADS), 0.5),
        'conv_dw_w': nrm((L, CONV_WIDTH, C_CONV), CONV_WIDTH ** -0.5),
        'conv_dw_b': nrm((L, C_CONV), 0.02),
        'conv_ln_g': 1.0 + nrm((L, C_CONV), 0.02),
        'conv_ln_b': nrm((L, C_CONV), 0.02),
        'w_conv_out': nrm((L, C_CONV, D_MODEL), C_CONV ** -0.5),
        'w_attn_out': nrm((L, ATTN_WIDTH, D_MODEL), ATTN_WIDTH ** -0.5),
        'w_out': nrm((L, D_MODEL, D_MODEL), D_MODEL ** -0.5),
        'norm3_g': 1.0 + nrm((L, D_MODEL), 0.02),
        'ffn2_w1': nrm((L, D_MODEL, D_FF), D_MODEL ** -0.5),
        'ffn2_w3': nrm((L, D_MODEL, D_FF), D_MODEL ** -0.5),
        'ffn2_w2': nrm((L, D_FF, D_MODEL), D_FF ** -0.5),
        'w_ada': nrm((L, D_MODEL, 9 * D_MODEL), 0.5 * D_MODEL ** -0.5),
        'b_ada': nrm((L, 9 * D_MODEL), 0.02),
        'final_norm_g': 1.0 + nrm((D_MODEL,), 0.02),
        'w_ada_final': nrm((D_MODEL, 2 * D_MODEL), 0.5 * D_MODEL ** -0.5),
        'b_ada_final': nrm((2 * D_MODEL,), 0.02),
    }


def reference(x_prompt, x_sample, cache_k, cache_v, state_conv, c_prompt, c_sample, rel_bias_table,
              norm1_g, ffn1_w1, ffn1_w3, ffn1_w2, norm2_g, w_in, attn_sinks, conv_dw_w, conv_dw_b,
              conv_ln_g, conv_ln_b, w_conv_out, w_attn_out, w_out, norm3_g, ffn2_w1, ffn2_w3, ffn2_w2,
              w_ada, b_ada, final_norm_g, w_ada_final, b_ada_final):
    xp, xs = x_prompt, x_sample
    kp_l, vp_l, cp_l, ks_l, vs_l, cs_l = [], [], [], [], [], []
    attend_p = functools.partial(window_attn_prompt, rel_table=rel_bias_table)
    for l in range(DEPTH):
        lp = (norm1_g[l], ffn1_w1[l], ffn1_w3[l], ffn1_w2[l], norm2_g[l], w_in[l], attn_sinks[l],
              conv_dw_w[l], conv_dw_b[l], conv_ln_g[l], conv_ln_b[l], w_conv_out[l], w_attn_out[l],
              w_out[l], norm3_g[l], ffn2_w1[l], ffn2_w3[l], ffn2_w2[l], w_ada[l], b_ada[l])
        hist_p = jnp.zeros((xp.shape[0], CONV_WIDTH - 1, C_CONV), xp.dtype)
        xp, kp, vp, cp = trunk_layer(xp, c_prompt, lp, attend_p, hist_p)
        attend_s = functools.partial(window_attn_sample, k_buf=cache_k[l], v_buf=cache_v[l],
                                     rel_table=rel_bias_table)
        xs, kss, vss, css = trunk_layer(xs, c_sample, lp, attend_s, state_conv[l])
        kp_l.append(kp); vp_l.append(vp); cp_l.append(cp)
        ks_l.append(kss); vs_l.append(vss); cs_l.append(css)
    shp, scp = jnp.split(adaln(c_prompt, w_ada_final, b_ada_final), 2, axis=-1)
    shs, scs = jnp.split(adaln(c_sample, w_ada_final, b_ada_final), 2, axis=-1)
    y_prompt = modulate(rms_norm(xp, final_norm_g), shp, scp)
    y_sample = modulate(rms_norm(xs, final_norm_g), shs, scs)
    return (y_prompt, y_sample, jnp.stack(kp_l), jnp.stack(vp_l), jnp.stack(cp_l),
            jnp.stack(ks_l), jnp.stack(vs_l), jnp.stack(cs_l))
```

```python
import functools

import numpy as np
import jax
import jax.numpy as jnp
from jax import lax
from jax.experimental import pallas as pl
from jax.experimental.pallas import tpu as pltpu

D_MODEL = 2048
SEQ = 8192
DEC_BATCH = 128
DEC_SEQ = 4
PAST_LEN = 16384
N_HEADS = 32
N_KV_HEADS = 8
HEAD_DIM = 64
GROUP = N_HEADS // N_KV_HEADS
ATTN_WIDTH = N_HEADS * HEAD_DIM
KV_WIDTH = N_KV_HEADS * HEAD_DIM
WINDOW = 128
Q_BLOCK = 128
ATTN_SCALE = HEAD_DIM ** -0.5
N_BUCKETS = 32
MAX_DISTANCE = 128
C_CONV = D_MODEL // 2
CONV_WIDTH = 31
HIST = CONV_WIDTH - 1
D_FF = 5632
NORM_EPS = 1e-6
NEG_INF = -1e30
IN_WIDTH = ATTN_WIDTH + 2 * KV_WIDTH + 2 * C_CONV + 2 * D_MODEL
WBUF = min(WINDOW, PAST_LEN)

S_ROWS = DEC_BATCH * DEC_SEQ
N_COND = 136
PROMPT_MOD_BLOCK = DEC_BATCH // 8
S_KEYS = 136
SINK_COL = WBUF + DEC_SEQ

VMEM_LIMIT = 56 * 1024 * 1024

F32 = jnp.float32
BF16 = jnp.bfloat16


def _cparams(n_axes):
    return pltpu.CompilerParams(dimension_semantics=("arbitrary",) * n_axes,
                                vmem_limit_bytes=VMEM_LIMIT)


def _dot(a, b):
    return jnp.dot(a, b, preferred_element_type=F32)


def _rms_mod(x, g, sh, sc):
    ms = jnp.mean(x * x, axis=-1, keepdims=True)
    y = x * lax.rsqrt(ms + NORM_EPS) * g
    return y * (1.0 + sc) + sh


def _mod_val(ref, per_row):
    return ref[...] if per_row else ref[0:1, :]


def _mod_spec(chunk, per_row, width=D_MODEL):
    if per_row:
        return pl.BlockSpec((DEC_BATCH, width), lambda *_: (0, chunk))
    return pl.BlockSpec((8, width), lambda *_: (PROMPT_MOD_BLOCK, chunk))


def _adaln_kernel(c_ref, w_ref, b_ref, o_ref, s_ref):
    @pl.when(pl.program_id(0) == 0)
    def _():
        s_ref[...] = jax.nn.silu(c_ref[...]).astype(BF16)

    o_ref[...] = _dot(s_ref[...], w_ref[...].astype(BF16)) + b_ref[...]


def _adaln(c_all, w, b, tn=1024):
    n = w.shape[1]
    return pl.pallas_call(
        _adaln_kernel,
        out_shape=jax.ShapeDtypeStruct((N_COND, n), F32),
        grid=(n // tn,),
        in_specs=[pl.BlockSpec((N_COND, D_MODEL), lambda j: (0, 0)),
                  pl.BlockSpec((D_MODEL, tn), lambda j: (0, j)),
                  pl.BlockSpec((1, tn), lambda j: (0, j))],
        out_specs=pl.BlockSpec((N_COND, tn), lambda j: (0, j)),
        scratch_shapes=[pltpu.VMEM((N_COND, D_MODEL), BF16)],
        compiler_params=_cparams(1),
        name="adaln",
    )(c_all, w, b)


def _ffn_kernel(*refs, tm, sub, per_row, n_f, final):
    if final:
        (x_ref, n_ref, sh_ref, sc_ref, g_ref, w1_ref, w3_ref, w2_ref,
         gf_ref, shf_ref, scf_ref, o_ref, u_ref, acc_ref) = refs
    else:
        (x_ref, n_ref, sh_ref, sc_ref, g_ref, w1_ref, w3_ref, w2_ref,
         o_ref, u_ref, acc_ref) = refs
    f = pl.program_id(1)

    @pl.when(f == 0)
    def _():
        sh = _mod_val(sh_ref, per_row)
        sc = _mod_val(sc_ref, per_row)
        for s in range(tm // sub):
            rows = slice(s * sub, (s + 1) * sub)
            u_ref[rows, :] = _rms_mod(x_ref[rows, :], n_ref[...], sh, sc).astype(BF16)
        acc_ref[...] = jnp.zeros_like(acc_ref)

    u = u_ref[...]
    h1 = _dot(u, w1_ref[...])
    h3 = _dot(u, w3_ref[...])
    a = (jax.nn.silu(h1) * h3).astype(BF16)
    acc_ref[...] += _dot(a, w2_ref[...])

    @pl.when(f == n_f - 1)
    def _():
        g = _mod_val(g_ref, per_row)
        for s in range(tm // sub):
            rows = slice(s * sub, (s + 1) * sub)
            h = x_ref[rows, :] + 0.5 * g * acc_ref[rows, :]
            if final:
                h = _rms_mod(h, gf_ref[...], _mod_val(shf_ref, per_row), _mod_val(scf_ref, per_row))
            o_ref[rows, :] = h


def _ffn(x, norm_g, mod, chunk0, w1, w3, w2, *, per_row, tm, tf=512, final=None):
    rows = x.shape[0]
    n_f = D_FF // tf
    sub = DEC_BATCH if per_row else min(tm, 256)
    row_spec = pl.BlockSpec((tm, D_MODEL), lambda i, f: (i, 0))
    vec_spec = pl.BlockSpec((1, D_MODEL), lambda i, f: (0, 0))
    in_specs = [row_spec, vec_spec,
                _mod_spec(chunk0, per_row), _mod_spec(chunk0 + 1, per_row), _mod_spec(chunk0 + 2, per_row),
                pl.BlockSpec((D_MODEL, tf), lambda i, f: (0, f)),
                pl.BlockSpec((D_MODEL, tf), lambda i, f: (0, f)),
                pl.BlockSpec((tf, D_MODEL), lambda i, f: (f, 0))]
    args = [x, norm_g, mod, mod, mod, w1, w3, w2]
    if final is not None:
        gf, mod_f = final
        in_specs += [vec_spec, _mod_spec(0, per_row), _mod_spec(1, per_row)]
        args += [gf, mod_f, mod_f]
    return pl.pallas_call(
        functools.partial(_ffn_kernel, tm=tm, sub=sub, per_row=per_row, n_f=n_f, final=final is not None),
        out_shape=jax.ShapeDtypeStruct((rows, D_MODEL), F32),
        grid=(rows // tm, n_f),
        in_specs=in_specs,
        out_specs=row_spec,
        scratch_shapes=[pltpu.VMEM((tm, D_MODEL), BF16), pltpu.VMEM((tm, D_MODEL), F32)],
        compiler_params=_cparams(2),
        name="ffn_final" if final is not None else "ffn",
    )(*args)


PROJ_TN = 1024
PROJ_STEPS = IN_WIDTH // PROJ_TN


def _proj_kernel(h_ref, n_ref, sh_ref, sc_ref, w_ref, q_ref, kv_ref, z_ref, ga_ref, gc_ref, u_ref,
                 *, tm, sub, per_row):
    j = pl.program_id(1)

    @pl.when(j == 0)
    def _():
        sh = _mod_val(sh_ref, per_row)
        sc = _mod_val(sc_ref, per_row)
        for s in range(tm // sub):
            rows = slice(s * sub, (s + 1) * sub)
            u_ref[rows, :] = _rms_mod(h_ref[rows, :], n_ref[...], sh, sc).astype(BF16)

    acc = _dot(u_ref[...], w_ref[...])

    @pl.when(j < 2)
    def _():
        q_ref[...] = acc.astype(q_ref.dtype)

    @pl.when(j == 2)
    def _():
        kv_ref[...] = acc

    @pl.when((j == 3) | (j == 4))
    def _():
        half = PROJ_TN // 2
        z_ref[...] = acc[:, :half] * jax.nn.sigmoid(acc[:, half:])

    @pl.when((j == 5) | (j == 6))
    def _():
        ga_ref[...] = jax.nn.sigmoid(acc).astype(BF16)

    @pl.when(j >= 7)
    def _():
        gc_ref[...] = jax.nn.sigmoid(acc).astype(BF16)


def _proj(h, norm_g, mod, w_pack, *, per_row, tm, q_dtype):
    rows = h.shape[0]
    sub = DEC_BATCH if per_row else min(tm, 256)

    def col_spec(width, first):
        return pl.BlockSpec((tm, width), lambda i, j: (i, jnp.clip(j - first, 0, 1)))

    return pl.pallas_call(
        functools.partial(_proj_kernel, tm=tm, sub=sub, per_row=per_row),
        out_shape=(jax.ShapeDtypeStruct((rows, ATTN_WIDTH), q_dtype),
                   jax.ShapeDtypeStruct((rows, 2 * KV_WIDTH), F32),
                   jax.ShapeDtypeStruct((rows, C_CONV), F32),
                   jax.ShapeDtypeStruct((rows, D_MODEL), BF16),
                   jax.ShapeDtypeStruct((rows, D_MODEL), BF16)),
        grid=(rows // tm, PROJ_STEPS),
        in_specs=[pl.BlockSpec((tm, D_MODEL), lambda i, j: (i, 0)),
                  pl.BlockSpec((1, D_MODEL), lambda i, j: (0, 0)),
                  _mod_spec(3, per_row), _mod_spec(4, per_row),
                  pl.BlockSpec((D_MODEL, PROJ_TN), lambda i, j: (0, j))],
        out_specs=(col_spec(PROJ_TN, 0),
                   pl.BlockSpec((tm, 2 * KV_WIDTH), lambda i, j: (i, 0)),
                   col_spec(PROJ_TN // 2, 3),
                   col_spec(PROJ_TN, 5),
                   col_spec(PROJ_TN, 7)),
        scratch_shapes=[pltpu.VMEM((tm, D_MODEL), BF16)],
        compiler_params=_cparams(2),
        name="proj",
    )(h, norm_g, mod, mod, w_pack)


def _t5_bucket_np(dist):
    exact = N_BUCKETS // 2
    d = np.maximum(dist, 0)
    df = np.maximum(d, 1).astype(np.float32)
    large = exact + (np.log(df / np.float32(exact)) / np.float32(np.log(MAX_DISTANCE / exact))
                     * np.float32(N_BUCKETS - exact)).astype(np.int32)
    large = np.minimum(large, N_BUCKETS - 1)
    return np.where(d < exact, d, large).astype(np.int32)


def _prompt_codes():
    qi = np.arange(Q_BLOCK)[:, None]
    kj = np.arange(2 * Q_BLOCK)[None, :]
    dist = qi + Q_BLOCK - kj
    valid = (dist >= 0) & (dist <= WINDOW)
    return np.where(valid, _t5_bucket_np(dist), -1).astype(np.int32)


def _sample_codes():
    t = np.repeat(np.arange(DEC_SEQ), GROUP * N_KV_HEADS)[:, None]
    j = np.arange(S_KEYS)[None, :]
    dist = t + WBUF - j
    valid = (dist >= 0) & (dist <= WINDOW) & (j < WBUF + DEC_SEQ)
    code = np.where(valid, _t5_bucket_np(dist), -1)
    code = np.where(j == SINK_COL, N_BUCKETS, code)
    return code.astype(np.int32)


def _attn_p_kernel(tab_ref, sink_ref, code_ref, q_ref, kvc_ref, kvp_ref, o_ref, bias_ref):
    n = pl.program_id(0)

    @pl.when(n == 0)
    def _():
        code = code_ref[...]

        def head_body(h, carry):
            def bucket_body(bk, acc):
                return jnp.where(code == bk, tab_ref[bk * N_HEADS + h], acc)
            bias_ref[h] = lax.fori_loop(0, N_BUCKETS, bucket_body,
                                        jnp.full((Q_BLOCK, 2 * Q_BLOCK), NEG_INF, F32))
            return carry
        lax.fori_loop(0, N_HEADS, head_body, 0)

    col = lax.broadcasted_iota(jnp.int32, (1, 2 * Q_BLOCK), 1)
    pen = jnp.where((col < Q_BLOCK) & (n == 0), NEG_INF, 0.0).astype(F32)
    for kv in range(N_KV_HEADS):
        kc = slice(kv * HEAD_DIM, (kv + 1) * HEAD_DIM)
        vc = slice(KV_WIDTH + kv * HEAD_DIM, KV_WIDTH + (kv + 1) * HEAD_DIM)
        kb = jnp.concatenate([kvp_ref[:, kc], kvc_ref[:, kc]], axis=0).astype(BF16)
        vb = jnp.concatenate([kvp_ref[:, vc], kvc_ref[:, vc]], axis=0).astype(BF16)
        for g in range(GROUP):
            h = kv * GROUP + g
            qc = slice(g * KV_WIDTH + kv * HEAD_DIM, g * KV_WIDTH + (kv + 1) * HEAD_DIM)
            s = lax.dot_general(q_ref[:, qc], kb, (((1,), (1,)), ((), ())),
                                preferred_element_type=F32)
            s = s * ATTN_SCALE + bias_ref[h] + pen
            sink = sink_ref[h]
            m = jnp.maximum(jnp.max(s, axis=-1, keepdims=True), sink)
            p = jnp.exp(s - m)
            l = jnp.sum(p, axis=-1, keepdims=True) + jnp.exp(sink - m)
            o = _dot(p.astype(BF16), vb) / l
            o_ref[:, qc] = o.astype(o_ref.dtype)


def _attn_prompt(q, kv, tab_flat, sinks):
    nb = SEQ // Q_BLOCK
    code = jnp.asarray(_prompt_codes())
    smem = pl.BlockSpec(memory_space=pltpu.SMEM)
    return pl.pallas_call(
        _attn_p_kernel,
        out_shape=jax.ShapeDtypeStruct((SEQ, ATTN_WIDTH), BF16),
        grid=(nb,),
        in_specs=[smem, smem,
                  pl.BlockSpec((Q_BLOCK, 2 * Q_BLOCK), lambda n: (0, 0)),
                  pl.BlockSpec((Q_BLOCK, ATTN_WIDTH), lambda n: (n, 0)),
                  pl.BlockSpec((Q_BLOCK, 2 * KV_WIDTH), lambda n: (n, 0)),
                  pl.BlockSpec((Q_BLOCK, 2 * KV_WIDTH), lambda n: (jnp.maximum(n - 1, 0), 0))],
        out_specs=pl.BlockSpec((Q_BLOCK, ATTN_WIDTH), lambda n: (n, 0)),
        scratch_shapes=[pltpu.VMEM((N_HEADS, Q_BLOCK, 2 * Q_BLOCK), F32)],
        compiler_params=_cparams(1),
        name="attn_prompt",
    )(tab_flat, sinks, code, q, kv, kv)


S_BB = 8


def _attn_s_kernel(th_ref, code_ref, q_ref, kv_ref, ck_ref, cv_ref, o_ref, kw_ref, vw_ref,
                   bias_ref, kf_ref, vf_ref, qe_ref):
    @pl.when(pl.program_id(0) == 0)
    def _():
        code = code_ref[...]
        acc = jnp.full((DEC_SEQ * N_HEADS, S_KEYS), NEG_INF, F32)
        for bk in range(N_BUCKETS + 1):
            acc = jnp.where(code == bk, th_ref[:, bk:bk + 1], acc)
        bias_ref[...] = acc
        kf_ref[...] = jnp.zeros_like(kf_ref)
        vf_ref[...] = jnp.zeros_like(vf_ref)

    lane_head = lax.broadcasted_iota(jnp.int32, (N_KV_HEADS, KV_WIDTH), 1) // HEAD_DIM
    row_head = lax.broadcasted_iota(jnp.int32, (N_KV_HEADS, KV_WIDTH), 0)
    diag = lane_head == row_head
    rows_all = DEC_SEQ * N_HEADS
    out_mask = (lax.broadcasted_iota(jnp.int32, (rows_all, KV_WIDTH), 0) % N_KV_HEADS
                == lax.broadcasted_iota(jnp.int32, (rows_all, KV_WIDTH), 1) // HEAD_DIM)
    bias = bias_ref[...]

    for b in range(S_BB):
        kf_ref[0:WBUF, :] = ck_ref[b]
        vf_ref[0:WBUF, :] = cv_ref[b]
        kw_ref[b, 0:WBUF - DEC_SEQ, :] = ck_ref[b, DEC_SEQ:WBUF, :]
        vw_ref[b, 0:WBUF - DEC_SEQ, :] = cv_ref[b, DEC_SEQ:WBUF, :]
        for t in range(DEC_SEQ):
            k_new = kv_ref[t, b:b + 1, 0:KV_WIDTH]
            v_new = kv_ref[t, b:b + 1, KV_WIDTH:2 * KV_WIDTH]
            kf_ref[WBUF + t:WBUF + t + 1, :] = k_new
            vf_ref[WBUF + t:WBUF + t + 1, :] = v_new
            kw_ref[b, WBUF - DEC_SEQ + t:WBUF - DEC_SEQ + t + 1, :] = k_new
            vw_ref[b, WBUF - DEC_SEQ + t:WBUF - DEC_SEQ + t + 1, :] = v_new
            for g in range(GROUP):
                q_row = q_ref[t, b:b + 1, g * KV_WIDTH:(g + 1) * KV_WIDTH]
                piece = jnp.where(diag, jnp.broadcast_to(q_row, (N_KV_HEADS, KV_WIDTH)), 0.0)
                r0 = (t * GROUP + g) * N_KV_HEADS
                qe_ref[r0:r0 + N_KV_HEADS, :] = piece
        s = lax.dot_general(qe_ref[...].astype(BF16), kf_ref[...].astype(BF16),
                            (((1,), (1,)), ((), ())), preferred_element_type=F32)
        s = s * ATTN_SCALE + bias
        m = jnp.max(s, axis=-1, keepdims=True)
        p = jnp.exp(s - m)
        l = jnp.sum(p, axis=-1, keepdims=True)
        o = _dot(p.astype(BF16), vf_ref[...].astype(BF16)) / l
        o = jnp.where(out_mask, o, 0.0)
        for t in range(DEC_SEQ):
            for g in range(GROUP):
                r0 = (t * GROUP + g) * N_KV_HEADS
                row = jnp.sum(o[r0:r0 + N_KV_HEADS, :], axis=0, keepdims=True)
                o_ref[t, b:b + 1, g * KV_WIDTH:(g + 1) * KV_WIDTH] = row


def _attn_sample(q, kv, cache_k, cache_v, th):
    code = jnp.asarray(_sample_codes())
    rows_all = DEC_SEQ * N_HEADS
    tb_spec = lambda w: pl.BlockSpec((DEC_SEQ, S_BB, w), lambda i: (0, i, 0))
    cache_spec = pl.BlockSpec((S_BB, WBUF, KV_WIDTH), lambda i: (i, 0, 0))
    cache_shape = jax.ShapeDtypeStruct((DEC_BATCH, WBUF, KV_WIDTH), F32)
    return pl.pallas_call(
        _attn_s_kernel,
        out_shape=(jax.ShapeDtypeStruct((DEC_SEQ, DEC_BATCH, ATTN_WIDTH), F32), cache_shape, cache_shape),
        grid=(DEC_BATCH // S_BB,),
        in_specs=[pl.BlockSpec((rows_all, N_BUCKETS + 1), lambda i: (0, 0)),
                  pl.BlockSpec((rows_all, S_KEYS), lambda i: (0, 0)),
                  tb_spec(ATTN_WIDTH), tb_spec(2 * KV_WIDTH), cache_spec, cache_spec],
        out_specs=(tb_spec(ATTN_WIDTH), cache_spec, cache_spec),
        scratch_shapes=[pltpu.VMEM((rows_all, S_KEYS), F32),
                        pltpu.VMEM((S_KEYS, KV_WIDTH), F32),
                        pltpu.VMEM((S_KEYS, KV_WIDTH), F32),
                        pltpu.VMEM((rows_all, KV_WIDTH), F32)],
        compiler_params=_cparams(1),
        name="attn_sample",
    )(th, code, q, kv, cache_k, cache_v)


CONV_RC = 32
CONV_HALO = 32


def _ln_silu(acc, lg, lb):
    mu = jnp.mean(acc, axis=-1, keepdims=True)
    xc = acc - mu
    var = jnp.mean(xc * xc, axis=-1, keepdims=True)
    return jax.nn.silu(xc * lax.rsqrt(var + NORM_EPS) * lg + lb)


def _conv_p_kernel(zc_ref, zh_ref, w_ref, b_ref, lg_ref, lb_ref, y_ref, s_ref, *, tc):
    i = pl.program_id(0)
    s_ref[0:CONV_HALO, :] = jnp.where(i == 0, 0.0, zh_ref[...])
    s_ref[CONV_HALO:, :] = zc_ref[...]
    off = CONV_HALO - HIST
    for r in range(tc // CONV_RC):
        acc = jnp.broadcast_to(b_ref[...], (CONV_RC, C_CONV))
        for j in range(CONV_WIDTH):
            r0 = r * CONV_RC + j + off
            acc = acc + w_ref[j:j + 1, :] * s_ref[r0:r0 + CONV_RC, :]
        y_ref[r * CONV_RC:(r + 1) * CONV_RC, :] = _ln_silu(acc, lg_ref[...], lb_ref[...]).astype(BF16)


def _conv_prompt(z, dw_w, dw_b, ln_g, ln_b, tc=256):
    rows = z.shape[0]
    ratio = tc // CONV_HALO
    vec = pl.BlockSpec((1, C_CONV), lambda i: (0, 0))
    return pl.pallas_call(
        functools.partial(_conv_p_kernel, tc=tc),
        out_shape=jax.ShapeDtypeStruct((rows, C_CONV), BF16),
        grid=(rows // tc,),
        in_specs=[pl.BlockSpec((tc, C_CONV), lambda i: (i, 0)),
                  pl.BlockSpec((CONV_HALO, C_CONV), lambda i: (jnp.maximum(i * ratio - 1, 0), 0)),
                  pl.BlockSpec((CONV_WIDTH, C_CONV), lambda i: (0, 0)), vec, vec, vec],
        out_specs=pl.BlockSpec((tc, C_CONV), lambda i: (i, 0)),
        scratch_shapes=[pltpu.VMEM((tc + CONV_HALO, C_CONV), F32)],
        compiler_params=_cparams(1),
        name="conv_prompt",
    )(z, z, dw_w, dw_b, ln_g, ln_b)


CONV_S_BB = 32


def _conv_s_kernel(z_ref, hist_ref, w_ref, b_ref, lg_ref, lb_ref, y_ref, ns_ref):
    for t in range(DEC_SEQ):
        acc = jnp.broadcast_to(b_ref[...], (CONV_S_BB, C_CONV))
        for j in range(CONV_WIDTH):
            i = t + j
            src = hist_ref[:, i * C_CONV:(i + 1) * C_CONV] if i < HIST else z_ref[i - HIST]
            acc = acc + w_ref[j:j + 1, :] * src
        y_ref[t] = _ln_silu(acc, lg_ref[...], lb_ref[...]).astype(BF16)
    keep = HIST - DEC_SEQ
    ns_ref[:, 0:keep * C_CONV] = hist_ref[:, DEC_SEQ * C_CONV:HIST * C_CONV]
    for t in range(DEC_SEQ):
        ns_ref[:, (keep + t) * C_CONV:(keep + t + 1) * C_CONV] = z_ref[t]


def _conv_sample(z, hist2d, dw_w, dw_b, ln_g, ln_b):
    vec = pl.BlockSpec((1, C_CONV), lambda i: (0, 0))
    tb = pl.BlockSpec((DEC_SEQ, CONV_S_BB, C_CONV), lambda i: (0, i, 0))
    st = pl.BlockSpec((CONV_S_BB, HIST * C_CONV), lambda i: (i, 0))
    return pl.pallas_call(
        _conv_s_kernel,
        out_shape=(jax.ShapeDtypeStruct((DEC_SEQ, DEC_BATCH, C_CONV), BF16),
                   jax.ShapeDtypeStruct((DEC_BATCH, HIST * C_CONV), F32)),
        grid=(DEC_BATCH // CONV_S_BB,),
        in_specs=[tb, st, pl.BlockSpec((CONV_WIDTH, C_CONV), lambda i: (0, 0)), vec, vec, vec],
        out_specs=(tb, st),
        compiler_params=_cparams(1),
        name="conv_sample",
    )(z, hist2d, dw_w, dw_b, ln_g, ln_b)


def _mix_kernel(at_ref, y_ref, ga_ref, gc_ref, h_ref, g2_ref, wao_ref, wco_ref, wout_ref, o_ref,
                *, per_row):
    a = _dot(at_ref[...].astype(BF16), wao_ref[...])
    c = _dot(y_ref[...], wco_ref[...])
    merged = (ga_ref[...].astype(F32) * a + gc_ref[...].astype(F32) * c).astype(BF16)
    r = _dot(merged, wout_ref[...])
    o_ref[...] = h_ref[...] + _mod_val(g2_ref, per_row) * r


def _mix(attn, y, ga, gc, h, mod, w_ao, w_co, w_out, *, per_row, tm):
    rows = h.shape[0]
    row = lambda w: pl.BlockSpec((tm, w), lambda i: (i, 0))
    resident = lambda shape: pl.BlockSpec(shape, lambda i: (0, 0), pipeline_mode=pl.Buffered(1))
    return pl.pallas_call(
        functools.partial(_mix_kernel, per_row=per_row),
        out_shape=jax.ShapeDtypeStruct((rows, D_MODEL), F32),
        grid=(rows // tm,),
        in_specs=[row(ATTN_WIDTH), row(C_CONV), row(D_MODEL), row(D_MODEL), row(D_MODEL),
                  _mod_spec(5, per_row),
                  resident((ATTN_WIDTH, D_MODEL)), resident((C_CONV, D_MODEL)), resident((D_MODEL, D_MODEL))],
        out_specs=row(D_MODEL),
        compiler_params=_cparams(1),
        name="mix",
    )(attn, y, ga, gc, h, mod, w_ao, w_co, w_out)


def _pack_w_in(w):
    wq = w[:, :ATTN_WIDTH].reshape(D_MODEL, N_KV_HEADS, GROUP, HEAD_DIM)
    wq = wq.transpose(0, 2, 1, 3).reshape(D_MODEL, ATTN_WIDTH)
    c0 = ATTN_WIDTH
    wkv = w[:, c0:c0 + 2 * KV_WIDTH]
    c0 += 2 * KV_WIDTH
    wa = w[:, c0:c0 + C_CONV]
    wg = w[:, c0 + C_CONV:c0 + 2 * C_CONV]
    half = C_CONV // 2
    wglu = jnp.concatenate([wa[:, :half], wg[:, :half], wa[:, half:], wg[:, half:]], axis=1)
    wgates = w[:, c0 + 2 * C_CONV:]
    return jnp.concatenate([wq, wkv, wglu, wgates], axis=1).astype(BF16)


def kernel(x_prompt, x_sample, cache_k, cache_v, state_conv, c_prompt, c_sample, rel_bias_table, norm1_g, ffn1_w1, ffn1_w3, ffn1_w2, norm2_g, w_in, attn_sinks, conv_dw_w, conv_dw_b, conv_ln_g, conv_ln_b, w_conv_out, w_attn_out, w_out, norm3_g, ffn2_w1, ffn2_w3, ffn2_w2, w_ada, b_ada, final_norm_g, w_ada_final, b_ada_final):
    c_all = jnp.concatenate([c_sample, c_prompt, jnp.zeros((N_COND - DEC_BATCH - 1, D_MODEL), F32)], axis=0)
    mod = _adaln(c_all, w_ada[0], b_ada)
    mod_f = _adaln(c_all, w_ada_final, b_ada_final[None, :])

    f1 = (ffn1_w1[0].astype(BF16), ffn1_w3[0].astype(BF16), ffn1_w2[0].astype(BF16))
    f2 = (ffn2_w1[0].astype(BF16), ffn2_w3[0].astype(BF16), ffn2_w2[0].astype(BF16))
    w_pack = _pack_w_in(w_in[0])
    w_ao = w_attn_out[0].reshape(N_KV_HEADS, GROUP, HEAD_DIM, D_MODEL).transpose(1, 0, 2, 3)
    w_ao = w_ao.reshape(ATTN_WIDTH, D_MODEL).astype(BF16)
    w_co = w_conv_out[0].astype(BF16)
    w_o = w_out[0].astype(BF16)

    sinks = attn_sinks[0]
    tab_flat = rel_bias_table.reshape(-1)
    tab_ext = jnp.concatenate([rel_bias_table, sinks[None, :]], axis=0)
    th = tab_ext.T.reshape(N_KV_HEADS, GROUP, N_BUCKETS + 1).transpose(1, 0, 2)
    th = jnp.tile(th.reshape(N_HEADS, N_BUCKETS + 1), (DEC_SEQ, 1))

    xp = x_prompt[0]
    xs = x_sample.transpose(1, 0, 2).reshape(S_ROWS, D_MODEL)

    def trunk(x, per_row, tm, attend, convolve, q_dtype, tm_mix):
        h = _ffn(x, norm1_g, mod, 0, *f1, per_row=per_row, tm=tm)
        q, kv, z, ga, gc = _proj(h, norm2_g, mod, w_pack, per_row=per_row, tm=tm, q_dtype=q_dtype)
        attn, states = attend(q, kv)
        y, conv_state = convolve(z)
        h = _mix(attn, y, ga, gc, h, mod, w_ao, w_co, w_o, per_row=per_row, tm=tm_mix)
        out = _ffn(h, norm3_g, mod, 6, *f2, per_row=per_row, tm=tm, final=(final_norm_g[None, :], mod_f))
        return out, kv, z, states, conv_state

    def attend_p(q, kv):
        return _attn_prompt(q, kv, tab_flat, sinks), None

    def conv_p(z):
        return _conv_prompt(z, conv_dw_w[0], conv_dw_b, conv_ln_g, conv_ln_b), None

    def attend_s(q, kv):
        o, kw, vw = _attn_sample(q.reshape(DEC_SEQ, DEC_BATCH, ATTN_WIDTH),
                                 kv.reshape(DEC_SEQ, DEC_BATCH, 2 * KV_WIDTH),
                                 cache_k.reshape(DEC_BATCH, WBUF, KV_WIDTH),
                                 cache_v.reshape(DEC_BATCH, WBUF, KV_WIDTH), th)
        return o.reshape(S_ROWS, ATTN_WIDTH), (kw, vw)

    def conv_s(z):
        y, ns = _conv_sample(z.reshape(DEC_SEQ, DEC_BATCH, C_CONV),
                             state_conv.reshape(DEC_BATCH, HIST * C_CONV),
                             conv_dw_w[0], conv_dw_b, conv_ln_g, conv_ln_b)
        return y.reshape(S_ROWS, C_CONV), ns

    yp, kv_p, z_p, _, _ = trunk(xp, False, 512, attend_p, conv_p, BF16, 256)
    ys, _, _, (kw, vw), ns = trunk(xs, True, S_ROWS, attend_s, conv_s, F32, DEC_BATCH)

    w = min(WINDOW, SEQ)
    kv_shape = (1, 1, w, N_KV_HEADS, HEAD_DIM)
    k_win_p = kv_p[SEQ - w:, :KV_WIDTH].reshape(kv_shape)
    v_win_p = kv_p[SEQ - w:, KV_WIDTH:].reshape(kv_shape)
    conv_p_state = z_p[SEQ - HIST:].reshape(1, 1, HIST, C_CONV)
    s_shape = (1, DEC_BATCH, WBUF, N_KV_HEADS, HEAD_DIM)
    y_prompt = yp[None]
    y_sample = ys.reshape(DEC_SEQ, DEC_BATCH, D_MODEL).transpose(1, 0, 2)
    return (y_prompt, y_sample, k_win_p, v_win_p, conv_p_state,
            kw.reshape(s_shape), vw.reshape(s_shape), ns.reshape(1, DEC_BATCH, HIST, C_CONV))
```

```python
import functools

import numpy as np
import jax
import jax.numpy as jnp
from jax import lax
from jax.experimental import pallas as pl
from jax.experimental.pallas import tpu as pltpu

D_MODEL = 2048
SEQ = 8192
DEC_BATCH = 128
DEC_SEQ = 4
PAST_LEN = 16384
N_HEADS = 32
N_KV_HEADS = 8
HEAD_DIM = 64
GROUP = N_HEADS // N_KV_HEADS
ATTN_WIDTH = N_HEADS * HEAD_DIM
KV_WIDTH = N_KV_HEADS * HEAD_DIM
WINDOW = 128
Q_BLOCK = 128
ATTN_SCALE = HEAD_DIM ** -0.5
N_BUCKETS = 32
MAX_DISTANCE = 128
C_CONV = D_MODEL // 2
CONV_WIDTH = 31
HIST = CONV_WIDTH - 1
D_FF = 5632
NORM_EPS = 1e-6
NEG_INF = -1e30
IN_WIDTH = ATTN_WIDTH + 2 * KV_WIDTH + 2 * C_CONV + 2 * D_MODEL
WBUF = min(WINDOW, PAST_LEN)

S_ROWS = DEC_BATCH * DEC_SEQ
N_COND = 136
PROMPT_MOD_BLOCK = DEC_BATCH // 8
S_KEYS = 136
SINK_COL = WBUF + DEC_SEQ

VMEM_LIMIT = 60 * 1024 * 1024

F32 = jnp.float32
BF16 = jnp.bfloat16


def _cparams(n_axes):
    return pltpu.CompilerParams(dimension_semantics=("arbitrary",) * n_axes,
                                vmem_limit_bytes=VMEM_LIMIT)


def _dot(a, b):
    return jnp.dot(a, b, preferred_element_type=F32)


def _rms_mod(x, g, sh, sc):
    ms = jnp.mean(x * x, axis=-1, keepdims=True)
    y = x * lax.rsqrt(ms + NORM_EPS) * g
    return y * (1.0 + sc) + sh


def _mod_val(ref, per_row):
    return ref[...] if per_row else ref[0:1, :]


def _mod_spec(chunk, per_row, width=D_MODEL):
    if per_row:
        return pl.BlockSpec((DEC_BATCH, width), lambda *_: (0, chunk))
    return pl.BlockSpec((8, width), lambda *_: (PROMPT_MOD_BLOCK, chunk))


def _adaln_kernel(c_ref, w_ref, b_ref, o_ref, s_ref):
    @pl.when(pl.program_id(0) == 0)
    def _():
        s_ref[...] = jax.nn.silu(c_ref[...]).astype(BF16)

    o_ref[...] = _dot(s_ref[...], w_ref[...].astype(BF16)) + b_ref[...]


def _adaln(c_all, w, b, tn=1024):
    n = w.shape[1]
    return pl.pallas_call(
        _adaln_kernel,
        out_shape=jax.ShapeDtypeStruct((N_COND, n), F32),
        grid=(n // tn,),
        in_specs=[pl.BlockSpec((N_COND, D_MODEL), lambda j: (0, 0)),
                  pl.BlockSpec((D_MODEL, tn), lambda j: (0, j)),
                  pl.BlockSpec((1, tn), lambda j: (0, j))],
        out_specs=pl.BlockSpec((N_COND, tn), lambda j: (0, j)),
        scratch_shapes=[pltpu.VMEM((N_COND, D_MODEL), BF16)],
        compiler_params=_cparams(1),
        name="adaln",
    )(c_all, w, b)


FFN_ROWS = 512


def _ffn_kernel(*refs, tm, sub, per_row, n_f, final):
    if final:
        (x_ref, n_ref, sh_ref, sc_ref, g_ref, w1_ref, w3_ref, w2_ref,
         gf_ref, shf_ref, scf_ref, o_ref, u_ref) = refs
    else:
        (x_ref, n_ref, sh_ref, sc_ref, g_ref, w1_ref, w3_ref, w2_ref,
         o_ref, u_ref) = refs
    f = pl.program_id(1)

    @pl.when(f == 0)
    def _():
        def body(s, carry):
            rows = pl.ds(pl.multiple_of(s * sub, sub), sub)
            u_ref[rows, :] = _rms_mod(x_ref[rows, :], n_ref[...], _mod_val(sh_ref, per_row),
                                      _mod_val(sc_ref, per_row)).astype(BF16)
            o_ref[rows, :] = jnp.zeros((sub, D_MODEL), F32)
            return carry
        lax.fori_loop(0, tm // sub, body, 0)

    for s in range(tm // FFN_ROWS):
        rows = slice(s * FFN_ROWS, (s + 1) * FFN_ROWS)
        u = u_ref[rows, :]
        h1 = _dot(u, w1_ref[...])
        h3 = _dot(u, w3_ref[...])
        a = (jax.nn.silu(h1) * h3).astype(BF16)
        o_ref[rows, :] += _dot(a, w2_ref[...])

    @pl.when(f == n_f - 1)
    def _():
        def body(s, carry):
            rows = pl.ds(pl.multiple_of(s * sub, sub), sub)
            h = x_ref[rows, :] + 0.5 * _mod_val(g_ref, per_row) * o_ref[rows, :]
            if final:
                h = _rms_mod(h, gf_ref[...], _mod_val(shf_ref, per_row), _mod_val(scf_ref, per_row))
            o_ref[rows, :] = h
            return carry
        lax.fori_loop(0, tm // sub, body, 0)


def _ffn(x, norm_g, mod, chunk0, w1, w3, w2, *, per_row, tm, tf=512, final=None):
    rows = x.shape[0]
    n_f = D_FF // tf
    sub = DEC_BATCH
    row_spec = pl.BlockSpec((tm, D_MODEL), lambda i, f: (i, 0))
    vec_spec = pl.BlockSpec((1, D_MODEL), lambda i, f: (0, 0))
    in_specs = [row_spec, vec_spec,
                _mod_spec(chunk0, per_row), _mod_spec(chunk0 + 1, per_row), _mod_spec(chunk0 + 2, per_row),
                pl.BlockSpec((D_MODEL, tf), lambda i, f: (0, f)),
                pl.BlockSpec((D_MODEL, tf), lambda i, f: (0, f)),
                pl.BlockSpec((tf, D_MODEL), lambda i, f: (f, 0))]
    args = [x, norm_g, mod, mod, mod, w1, w3, w2]
    if final is not None:
        gf, mod_f = final
        in_specs += [vec_spec, _mod_spec(0, per_row), _mod_spec(1, per_row)]
        args += [gf, mod_f, mod_f]
    return pl.pallas_call(
        functools.partial(_ffn_kernel, tm=tm, sub=sub, per_row=per_row, n_f=n_f, final=final is not None),
        out_shape=jax.ShapeDtypeStruct((rows, D_MODEL), F32),
        grid=(rows // tm, n_f),
        in_specs=in_specs,
        out_specs=row_spec,
        scratch_shapes=[pltpu.VMEM((tm, D_MODEL), BF16)],
        compiler_params=_cparams(2),
        name="ffn_final" if final is not None else "ffn",
    )(*args)


PROJ_TN = 1024
PROJ_STEPS = IN_WIDTH // PROJ_TN
PROJ_CHUNK = 256


def _proj_kernel(h_ref, n_ref, sh_ref, sc_ref, w_ref, q_ref, kv_ref, z_ref, ga_ref, gc_ref, u_ref,
                 *, tm, sub, per_row):
    j = pl.program_id(1)

    @pl.when(j == 0)
    def _():
        def body(s, carry):
            rows = pl.ds(pl.multiple_of(s * sub, sub), sub)
            u_ref[rows, :] = _rms_mod(h_ref[rows, :], n_ref[...], _mod_val(sh_ref, per_row),
                                      _mod_val(sc_ref, per_row)).astype(BF16)
            return carry
        lax.fori_loop(0, tm // sub, body, 0)

    def chunk(c):
        cols = slice(c * PROJ_CHUNK, (c + 1) * PROJ_CHUNK)
        return cols, _dot(u_ref[...], w_ref[:, cols])

    n_chunks = PROJ_TN // PROJ_CHUNK

    @pl.when(j < 2)
    def _():
        for c in range(n_chunks):
            cols, acc = chunk(c)
            q_ref[:, cols] = acc.astype(q_ref.dtype)

    @pl.when(j == 2)
    def _():
        for c in range(n_chunks):
            cols, acc = chunk(c)
            kv_ref[:, cols] = acc

    @pl.when((j == 3) | (j == 4))
    def _():
        for c in range(n_chunks // 2):
            cols, a = chunk(c)
            _, g = chunk(c + n_chunks // 2)
            z_ref[:, cols] = a * jax.nn.sigmoid(g)

    @pl.when((j == 5) | (j == 6))
    def _():
        for c in range(n_chunks):
            cols, acc = chunk(c)
            ga_ref[:, cols] = jax.nn.sigmoid(acc).astype(BF16)

    @pl.when(j >= 7)
    def _():
        for c in range(n_chunks):
            cols, acc = chunk(c)
            gc_ref[:, cols] = jax.nn.sigmoid(acc).astype(BF16)


def _proj(h, norm_g, mod, w_pack, *, per_row, tm, q_dtype):
    rows = h.shape[0]
    sub = DEC_BATCH

    def col_spec(width, first):
        return pl.BlockSpec((tm, width), lambda i, j: (i, jnp.clip(j - first, 0, 1)))

    return pl.pallas_call(
        functools.partial(_proj_kernel, tm=tm, sub=sub, per_row=per_row),
        out_shape=(jax.ShapeDtypeStruct((rows, ATTN_WIDTH), q_dtype),
                   jax.ShapeDtypeStruct((rows, 2 * KV_WIDTH), F32),
                   jax.ShapeDtypeStruct((rows, C_CONV), F32),
                   jax.ShapeDtypeStruct((rows, D_MODEL), BF16),
                   jax.ShapeDtypeStruct((rows, D_MODEL), BF16)),
        grid=(rows // tm, PROJ_STEPS),
        in_specs=[pl.BlockSpec((tm, D_MODEL), lambda i, j: (i, 0)),
                  pl.BlockSpec((1, D_MODEL), lambda i, j: (0, 0)),
                  _mod_spec(3, per_row), _mod_spec(4, per_row),
                  pl.BlockSpec((D_MODEL, PROJ_TN), lambda i, j: (0, j))],
        out_specs=(col_spec(PROJ_TN, 0),
                   pl.BlockSpec((tm, 2 * KV_WIDTH), lambda i, j: (i, 0)),
                   col_spec(PROJ_TN // 2, 3),
                   col_spec(PROJ_TN, 5),
                   col_spec(PROJ_TN, 7)),
        scratch_shapes=[pltpu.VMEM((tm, D_MODEL), BF16)],
        compiler_params=_cparams(2),
        name="proj",
    )(h, norm_g, mod, mod, w_pack)


def _t5_bucket_np(dist):
    exact = N_BUCKETS // 2
    d = np.maximum(dist, 0)
    df = np.maximum(d, 1).astype(np.float32)
    large = exact + (np.log(df / np.float32(exact)) / np.float32(np.log(MAX_DISTANCE / exact))
                     * np.float32(N_BUCKETS - exact)).astype(np.int32)
    large = np.minimum(large, N_BUCKETS - 1)
    return np.where(d < exact, d, large).astype(np.int32)


def _prompt_codes():
    qi = np.arange(Q_BLOCK)[:, None]
    kj = np.arange(2 * Q_BLOCK)[None, :]
    dist = qi + Q_BLOCK - kj
    valid = (dist >= 0) & (dist <= WINDOW)
    return np.where(valid, _t5_bucket_np(dist), -1).astype(np.int32)


def _sample_codes():
    t = np.repeat(np.arange(DEC_SEQ), GROUP * N_KV_HEADS)[:, None]
    j = np.arange(S_KEYS)[None, :]
    dist = t + WBUF - j
    valid = (dist >= 0) & (dist <= WINDOW) & (j < WBUF + DEC_SEQ)
    code = np.where(valid, _t5_bucket_np(dist), -1)
    code = np.where(j == SINK_COL, N_BUCKETS, code)
    return code.astype(np.int32)


def _attn_p_kernel(tab_ref, sink_ref, code_ref, q_ref, kvc_ref, kvp_ref, o_ref, bias_ref):
    n = pl.program_id(0)

    @pl.when(n == 0)
    def _():
        code = code_ref[...]

        def head_body(h, carry):
            def bucket_body(bk, acc):
                return jnp.where(code == bk, tab_ref[bk * N_HEADS + h], acc)
            bias_ref[h] = lax.fori_loop(0, N_BUCKETS, bucket_body,
                                        jnp.full((Q_BLOCK, 2 * Q_BLOCK), NEG_INF, F32))
            return carry
        lax.fori_loop(0, N_HEADS, head_body, 0)

    col = lax.broadcasted_iota(jnp.int32, (1, 2 * Q_BLOCK), 1)
    pen = jnp.where((col < Q_BLOCK) & (n == 0), NEG_INF, 0.0).astype(F32)
    for kv in range(N_KV_HEADS):
        kc = slice(kv * HEAD_DIM, (kv + 1) * HEAD_DIM)
        vc = slice(KV_WIDTH + kv * HEAD_DIM, KV_WIDTH + (kv + 1) * HEAD_DIM)
        kb = jnp.concatenate([kvp_ref[:, kc], kvc_ref[:, kc]], axis=0).astype(BF16)
        vb = jnp.concatenate([kvp_ref[:, vc], kvc_ref[:, vc]], axis=0).astype(BF16)
        for g in range(GROUP):
            h = kv * GROUP + g
            qc = slice(g * KV_WIDTH + kv * HEAD_DIM, g * KV_WIDTH + (kv + 1) * HEAD_DIM)
            s = lax.dot_general(q_ref[:, qc], kb, (((1,), (1,)), ((), ())),
                                preferred_element_type=F32)
            s = s * ATTN_SCALE + bias_ref[h] + pen
            sink = sink_ref[h]
            m = jnp.maximum(jnp.max(s, axis=-1, keepdims=True), sink)
            p = jnp.exp(s - m)
            l = jnp.sum(p, axis=-1, keepdims=True) + jnp.exp(sink - m)
            o = _dot(p.astype(BF16), vb) / l
            o_ref[:, qc] = o.astype(o_ref.dtype)


def _attn_prompt(q, kv, tab_flat, sinks):
    nb = SEQ // Q_BLOCK
    code = jnp.asarray(_prompt_codes())
    smem = pl.BlockSpec(memory_space=pltpu.SMEM)
    return pl.pallas_call(
        _attn_p_kernel,
        out_shape=jax.ShapeDtypeStruct((SEQ, ATTN_WIDTH), BF16),
        grid=(nb,),
        in_specs=[smem, smem,
                  pl.BlockSpec((Q_BLOCK, 2 * Q_BLOCK), lambda n: (0, 0)),
                  pl.BlockSpec((Q_BLOCK, ATTN_WIDTH), lambda n: (n, 0)),
                  pl.BlockSpec((Q_BLOCK, 2 * KV_WIDTH), lambda n: (n, 0)),
                  pl.BlockSpec((Q_BLOCK, 2 * KV_WIDTH), lambda n: (jnp.maximum(n - 1, 0), 0))],
        out_specs=pl.BlockSpec((Q_BLOCK, ATTN_WIDTH), lambda n: (n, 0)),
        scratch_shapes=[pltpu.VMEM((N_HEADS, Q_BLOCK, 2 * Q_BLOCK), F32)],
        compiler_params=_cparams(1),
        name="attn_prompt",
    )(tab_flat, sinks, code, q, kv, kv)


S_BB = 8


def _attn_s_kernel(th_ref, code_ref, q_ref, kv_ref, ck_ref, cv_ref, o_ref, kw_ref, vw_ref,
                   bias_ref, kf_ref, vf_ref, qe_ref):
    @pl.when(pl.program_id(0) == 0)
    def _():
        code = code_ref[...]
        acc = jnp.full((DEC_SEQ * N_HEADS, S_KEYS), NEG_INF, F32)
        for bk in range(N_BUCKETS + 1):
            acc = jnp.where(code == bk, th_ref[:, bk:bk + 1], acc)
        bias_ref[...] = acc
        kf_ref[...] = jnp.zeros_like(kf_ref)
        vf_ref[...] = jnp.zeros_like(vf_ref)

    lane_head = lax.broadcasted_iota(jnp.int32, (N_KV_HEADS, KV_WIDTH), 1) // HEAD_DIM
    row_head = lax.broadcasted_iota(jnp.int32, (N_KV_HEADS, KV_WIDTH), 0)
    diag = lane_head == row_head
    rows_all = DEC_SEQ * N_HEADS
    out_mask = (lax.broadcasted_iota(jnp.int32, (rows_all, KV_WIDTH), 0) % N_KV_HEADS
                == lax.broadcasted_iota(jnp.int32, (rows_all, KV_WIDTH), 1) // HEAD_DIM)
    bias = bias_ref[...]

    for b in range(S_BB):
        kf_ref[0:WBUF, :] = ck_ref[b]
        vf_ref[0:WBUF, :] = cv_ref[b]
        kw_ref[b, 0:WBUF - DEC_SEQ, :] = ck_ref[b, DEC_SEQ:WBUF, :]
        vw_ref[b, 0:WBUF - DEC_SEQ, :] = cv_ref[b, DEC_SEQ:WBUF, :]
        for t in range(DEC_SEQ):
            k_new = kv_ref[t, b:b + 1, 0:KV_WIDTH]
            v_new = kv_ref[t, b:b + 1, KV_WIDTH:2 * KV_WIDTH]
            kf_ref[WBUF + t:WBUF + t + 1, :] = k_new
            vf_ref[WBUF + t:WBUF + t + 1, :] = v_new
            kw_ref[b, WBUF - DEC_SEQ + t:WBUF - DEC_SEQ + t + 1, :] = k_new
            vw_ref[b, WBUF - DEC_SEQ + t:WBUF - DEC_SEQ + t + 1, :] = v_new
            for g in range(GROUP):
                q_row = q_ref[t, b:b + 1, g * KV_WIDTH:(g + 1) * KV_WIDTH]
                piece = jnp.where(diag, jnp.broadcast_to(q_row, (N_KV_HEADS, KV_WIDTH)), 0.0)
                r0 = (t * GROUP + g) * N_KV_HEADS
                qe_ref[r0:r0 + N_KV_HEADS, :] = piece
        s = lax.dot_general(qe_ref[...].astype(BF16), kf_ref[...].astype(BF16),
                            (((1,), (1,)), ((), ())), preferred_element_type=F32)
        s = s * ATTN_SCALE + bias
        m = jnp.max(s, axis=-1, keepdims=True)
        p = jnp.exp(s - m)
        l = jnp.sum(p, axis=-1, keepdims=True)
        o = _dot(p.astype(BF16), vf_ref[...].astype(BF16)) / l
        o = jnp.where(out_mask, o, 0.0)
        for t in range(DEC_SEQ):
            for g in range(GROUP):
                r0 = (t * GROUP + g) * N_KV_HEADS
                row = jnp.sum(o[r0:r0 + N_KV_HEADS, :], axis=0, keepdims=True)
                o_ref[t, b:b + 1, g * KV_WIDTH:(g + 1) * KV_WIDTH] = row


def _attn_sample(q, kv, cache_k, cache_v, th):
    code = jnp.asarray(_sample_codes())
    rows_all = DEC_SEQ * N_HEADS
    tb_spec = lambda w: pl.BlockSpec((DEC_SEQ, S_BB, w), lambda i: (0, i, 0))
    cache_spec = pl.BlockSpec((S_BB, WBUF, KV_WIDTH), lambda i: (i, 0, 0))
    cache_shape = jax.ShapeDtypeStruct((DEC_BATCH, WBUF, KV_WIDTH), F32)
    return pl.pallas_call(
        _attn_s_kernel,
        out_shape=(jax.ShapeDtypeStruct((DEC_SEQ, DEC_BATCH, ATTN_WIDTH), F32), cache_shape, cache_shape),
        grid=(DEC_BATCH // S_BB,),
        in_specs=[pl.BlockSpec((rows_all, N_BUCKETS + 1), lambda i: (0, 0)),
                  pl.BlockSpec((rows_all, S_KEYS), lambda i: (0, 0)),
                  tb_spec(ATTN_WIDTH), tb_spec(2 * KV_WIDTH), cache_spec, cache_spec],
        out_specs=(tb_spec(ATTN_WIDTH), cache_spec, cache_spec),
        scratch_shapes=[pltpu.VMEM((rows_all, S_KEYS), F32),
                        pltpu.VMEM((S_KEYS, KV_WIDTH), F32),
                        pltpu.VMEM((S_KEYS, KV_WIDTH), F32),
                        pltpu.VMEM((rows_all, KV_WIDTH), F32)],
        compiler_params=_cparams(1),
        name="attn_sample",
    )(th, code, q, kv, cache_k, cache_v)


SUBLANES = 8
LANES = 128
CONV_RC = 32
CONV_ROWS = 128
CONV_HALO = 32


def _ln_silu(acc, lg, lb):
    mu = jnp.mean(acc, axis=-1, keepdims=True)
    xc = acc - mu
    var = jnp.mean(xc * xc, axis=-1, keepdims=True)
    return jax.nn.silu(xc * lax.rsqrt(var + NORM_EPS) * lg + lb)


def _conv_p_kernel(zc_ref, zh_ref, w_ref, b_ref, lg_ref, lb_ref, y_ref, s_ref, c_ref, *, tc):
    i = pl.program_id(0)
    s_ref[0:CONV_HALO, :] = jnp.where(i == 0, 0.0, zh_ref[...])
    s_ref[CONV_HALO:, :] = zc_ref[...]
    off = CONV_HALO - HIST
    groups = [[j for j in range(CONV_WIDTH) if (j + off) % SUBLANES == r] for r in range(SUBLANES)]
    for rc in range(tc // CONV_ROWS):
        t0 = rc * CONV_ROWS
        for lc in range(C_CONV // LANES):
            lanes = slice(lc * LANES, (lc + 1) * LANES)
            out = jnp.broadcast_to(b_ref[:, lanes], (CONV_ROWS, LANES))
            for r, taps in enumerate(groups):
                n_rows = CONV_ROWS + (SUBLANES if r else 0)
                part = None
                for j in taps:
                    base = t0 + (j + off) - r
                    term = w_ref[j:j + 1, lanes] * s_ref[base:base + n_rows, lanes]
                    part = term if part is None else part + term
                out = out + part[r:r + CONV_ROWS, :]
            c_ref[t0:t0 + CONV_ROWS, lanes] = out
    for r in range(tc // CONV_RC):
        rows = slice(r * CONV_RC, (r + 1) * CONV_RC)
        y_ref[rows, :] = _ln_silu(c_ref[rows, :], lg_ref[...], lb_ref[...]).astype(BF16)


def _conv_prompt(z, dw_w, dw_b, ln_g, ln_b, tc=256):
    rows = z.shape[0]
    ratio = tc // CONV_HALO
    vec = pl.BlockSpec((1, C_CONV), lambda i: (0, 0))
    return pl.pallas_call(
        functools.partial(_conv_p_kernel, tc=tc),
        out_shape=jax.ShapeDtypeStruct((rows, C_CONV), BF16),
        grid=(rows // tc,),
        in_specs=[pl.BlockSpec((tc, C_CONV), lambda i: (i, 0)),
                  pl.BlockSpec((CONV_HALO, C_CONV), lambda i: (jnp.maximum(i * ratio - 1, 0), 0)),
                  pl.BlockSpec((CONV_WIDTH, C_CONV), lambda i: (0, 0)), vec, vec, vec],
        out_specs=pl.BlockSpec((tc, C_CONV), lambda i: (i, 0)),
        scratch_shapes=[pltpu.VMEM((tc + CONV_HALO, C_CONV), F32), pltpu.VMEM((tc, C_CONV), F32)],
        compiler_params=_cparams(1),
        name="conv_prompt",
    )(z, z, dw_w, dw_b, ln_g, ln_b)


CONV_S_BB = 32


def _conv_s_kernel(z_ref, hist_ref, w_ref, b_ref, lg_ref, lb_ref, y_ref, ns_ref):
    for t in range(DEC_SEQ):
        acc = jnp.broadcast_to(b_ref[...], (CONV_S_BB, C_CONV))
        for j in range(CONV_WIDTH):
            i = t + j
            src = hist_ref[:, i * C_CONV:(i + 1) * C_CONV] if i < HIST else z_ref[i - HIST]
            acc = acc + w_ref[j:j + 1, :] * src
        y_ref[t] = _ln_silu(acc, lg_ref[...], lb_ref[...]).astype(BF16)
    keep = HIST - DEC_SEQ
    ns_ref[:, 0:keep * C_CONV] = hist_ref[:, DEC_SEQ * C_CONV:HIST * C_CONV]
    for t in range(DEC_SEQ):
        ns_ref[:, (keep + t) * C_CONV:(keep + t + 1) * C_CONV] = z_ref[t]


def _conv_sample(z, hist2d, dw_w, dw_b, ln_g, ln_b):
    vec = pl.BlockSpec((1, C_CONV), lambda i: (0, 0))
    tb = pl.BlockSpec((DEC_SEQ, CONV_S_BB, C_CONV), lambda i: (0, i, 0))
    st = pl.BlockSpec((CONV_S_BB, HIST * C_CONV), lambda i: (i, 0))
    return pl.pallas_call(
        _conv_s_kernel,
        out_shape=(jax.ShapeDtypeStruct((DEC_SEQ, DEC_BATCH, C_CONV), BF16),
                   jax.ShapeDtypeStruct((DEC_BATCH, HIST * C_CONV), F32)),
        grid=(DEC_BATCH // CONV_S_BB,),
        in_specs=[tb, st, pl.BlockSpec((CONV_WIDTH, C_CONV), lambda i: (0, 0)), vec, vec, vec],
        out_specs=(tb, st),
        compiler_params=_cparams(1),
        name="conv_sample",
    )(z, hist2d, dw_w, dw_b, ln_g, ln_b)


def _mix_kernel(at_ref, y_ref, ga_ref, gc_ref, h_ref, g2_ref, wao_ref, wco_ref, wout_ref, o_ref,
                *, per_row):
    a = _dot(at_ref[...].astype(BF16), wao_ref[...])
    c = _dot(y_ref[...], wco_ref[...])
    merged = (ga_ref[...].astype(F32) * a + gc_ref[...].astype(F32) * c).astype(BF16)
    r = _dot(merged, wout_ref[...])
    o_ref[...] = h_ref[...] + _mod_val(g2_ref, per_row) * r


def _mix(attn, y, ga, gc, h, mod, w_ao, w_co, w_out, *, per_row, tm):
    rows = h.shape[0]
    row = lambda w: pl.BlockSpec((tm, w), lambda i: (i, 0))
    resident = lambda shape: pl.BlockSpec(shape, lambda i: (0, 0), pipeline_mode=pl.Buffered(1))
    return pl.pallas_call(
        functools.partial(_mix_kernel, per_row=per_row),
        out_shape=jax.ShapeDtypeStruct((rows, D_MODEL), F32),
        grid=(rows // tm,),
        in_specs=[row(ATTN_WIDTH), row(C_CONV), row(D_MODEL), row(D_MODEL), row(D_MODEL),
                  _mod_spec(5, per_row),
                  resident((ATTN_WIDTH, D_MODEL)), resident((C_CONV, D_MODEL)), resident((D_MODEL, D_MODEL))],
        out_specs=row(D_MODEL),
        compiler_params=_cparams(1),
        name="mix",
    )(attn, y, ga, gc, h, mod, w_ao, w_co, w_out)


def _pack_w_in(w):
    wq = w[:, :ATTN_WIDTH].reshape(D_MODEL, N_KV_HEADS, GROUP, HEAD_DIM)
    wq = wq.transpose(0, 2, 1, 3).reshape(D_MODEL, ATTN_WIDTH)
    c0 = ATTN_WIDTH
    wkv = w[:, c0:c0 + 2 * KV_WIDTH]
    c0 += 2 * KV_WIDTH
    wa = w[:, c0:c0 + C_CONV]
    wg = w[:, c0 + C_CONV:c0 + 2 * C_CONV]
    half = C_CONV // 2
    wglu = jnp.concatenate([wa[:, :half], wg[:, :half], wa[:, half:], wg[:, half:]], axis=1)
    wgates = w[:, c0 + 2 * C_CONV:]
    return jnp.concatenate([wq, wkv, wglu, wgates], axis=1).astype(BF16)


def kernel(x_prompt, x_sample, cache_k, cache_v, state_conv, c_prompt, c_sample, rel_bias_table, norm1_g, ffn1_w1, ffn1_w3, ffn1_w2, norm2_g, w_in, attn_sinks, conv_dw_w, conv_dw_b, conv_ln_g, conv_ln_b, w_conv_out, w_attn_out, w_out, norm3_g, ffn2_w1, ffn2_w3, ffn2_w2, w_ada, b_ada, final_norm_g, w_ada_final, b_ada_final):
    c_all = jnp.concatenate([c_sample, c_prompt, jnp.zeros((N_COND - DEC_BATCH - 1, D_MODEL), F32)], axis=0)
    mod = _adaln(c_all, w_ada[0], b_ada)
    mod_f = _adaln(c_all, w_ada_final, b_ada_final[None, :])

    f1 = (ffn1_w1[0].astype(BF16), ffn1_w3[0].astype(BF16), ffn1_w2[0].astype(BF16))
    f2 = (ffn2_w1[0].astype(BF16), ffn2_w3[0].astype(BF16), ffn2_w2[0].astype(BF16))
    w_pack = _pack_w_in(w_in[0])
    w_ao = w_attn_out[0].reshape(N_KV_HEADS, GROUP, HEAD_DIM, D_MODEL).transpose(1, 0, 2, 3)
    w_ao = w_ao.reshape(ATTN_WIDTH, D_MODEL).astype(BF16)
    w_co = w_conv_out[0].astype(BF16)
    w_o = w_out[0].astype(BF16)

    sinks = attn_sinks[0]
    tab_flat = rel_bias_table.reshape(-1)
    tab_ext = jnp.concatenate([rel_bias_table, sinks[None, :]], axis=0)
    th = tab_ext.T.reshape(N_KV_HEADS, GROUP, N_BUCKETS + 1).transpose(1, 0, 2)
    th = jnp.tile(th.reshape(N_HEADS, N_BUCKETS + 1), (DEC_SEQ, 1))

    xp = x_prompt[0]
    xs = x_sample.transpose(1, 0, 2).reshape(S_ROWS, D_MODEL)

    def trunk(x, per_row, tm_ffn, tm, attend, convolve, q_dtype, tm_mix):
        h = _ffn(x, norm1_g, mod, 0, *f1, per_row=per_row, tm=tm_ffn)
        q, kv, z, ga, gc = _proj(h, norm2_g, mod, w_pack, per_row=per_row, tm=tm, q_dtype=q_dtype)
        attn, states = attend(q, kv)
        y, conv_state = convolve(z)
        h = _mix(attn, y, ga, gc, h, mod, w_ao, w_co, w_o, per_row=per_row, tm=tm_mix)
        out = _ffn(h, norm3_g, mod, 6, *f2, per_row=per_row, tm=tm_ffn, final=(final_norm_g[None, :], mod_f))
        return out, kv, z, states, conv_state

    def attend_p(q, kv):
        return _attn_prompt(q, kv, tab_flat, sinks), None

    def conv_p(z):
        return _conv_prompt(z, conv_dw_w[0], conv_dw_b, conv_ln_g, conv_ln_b), None

    def attend_s(q, kv):
        o, kw, vw = _attn_sample(q.reshape(DEC_SEQ, DEC_BATCH, ATTN_WIDTH),
                                 kv.reshape(DEC_SEQ, DEC_BATCH, 2 * KV_WIDTH),
                                 cache_k.reshape(DEC_BATCH, WBUF, KV_WIDTH),
                                 cache_v.reshape(DEC_BATCH, WBUF, KV_WIDTH), th)
        return o.reshape(S_ROWS, ATTN_WIDTH), (kw, vw)

    def conv_s(z):
        y, ns = _conv_sample(z.reshape(DEC_SEQ, DEC_BATCH, C_CONV),
                             state_conv.reshape(DEC_BATCH, HIST * C_CONV),
                             conv_dw_w[0], conv_dw_b, conv_ln_g, conv_ln_b)
        return y.reshape(S_ROWS, C_CONV), ns

    yp, kv_p, z_p, _, _ = trunk(xp, False, 1024, 512, attend_p, conv_p, BF16, 256)
    ys, _, _, (kw, vw), ns = trunk(xs, True, S_ROWS, S_ROWS, attend_s, conv_s, F32, DEC_BATCH)

    w = min(WINDOW, SEQ)
    kv_shape = (1, 1, w, N_KV_HEADS, HEAD_DIM)
    k_win_p = kv_p[SEQ - w:, :KV_WIDTH].reshape(kv_shape)
    v_win_p = kv_p[SEQ - w:, KV_WIDTH:].reshape(kv_shape)
    conv_p_state = z_p[SEQ - HIST:].reshape(1, 1, HIST, C_CONV)
    s_shape = (1, DEC_BATCH, WBUF, N_KV_HEADS, HEAD_DIM)
    y_prompt = yp[None]
    y_sample = ys.reshape(DEC_SEQ, DEC_BATCH, D_MODEL).transpose(1, 0, 2)
    return (y_prompt, y_sample, k_win_p, v_win_p, conv_p_state,
            kw.reshape(s_shape), vw.reshape(s_shape), ns.reshape(1, DEC_BATCH, HIST, C_CONV))
```

```python
import functools

import numpy as np
import jax
import jax.numpy as jnp
from jax import lax
from jax.experimental import pallas as pl
from jax.experimental.pallas import tpu as pltpu

D_MODEL = 2048
SEQ = 8192
DEC_BATCH = 128
DEC_SEQ = 4
PAST_LEN = 16384
N_HEADS = 32
N_KV_HEADS = 8
HEAD_DIM = 64
GROUP = N_HEADS // N_KV_HEADS
ATTN_WIDTH = N_HEADS * HEAD_DIM
KV_WIDTH = N_KV_HEADS * HEAD_DIM
WINDOW = 128
Q_BLOCK = 128
ATTN_SCALE = HEAD_DIM ** -0.5
N_BUCKETS = 32
MAX_DISTANCE = 128
C_CONV = D_MODEL // 2
CONV_WIDTH = 31
HIST = CONV_WIDTH - 1
D_FF = 5632
NORM_EPS = 1e-6
NEG_INF = -1e30
IN_WIDTH = ATTN_WIDTH + 2 * KV_WIDTH + 2 * C_CONV + 2 * D_MODEL
WBUF = min(WINDOW, PAST_LEN)

S_ROWS = DEC_BATCH * DEC_SEQ
N_COND = 136
PROMPT_MOD_BLOCK = DEC_BATCH // 8
S_KEYS = 136
SINK_COL = WBUF + DEC_SEQ

VMEM_LIMIT = 60 * 1024 * 1024

F32 = jnp.float32
BF16 = jnp.bfloat16


def _cparams(n_axes):
    return pltpu.CompilerParams(dimension_semantics=("arbitrary",) * n_axes,
                                vmem_limit_bytes=VMEM_LIMIT)


def _dot(a, b):
    return jnp.dot(a, b, preferred_element_type=F32)


def _rms_mod(x, g, sh, sc):
    ms = jnp.mean(x * x, axis=-1, keepdims=True)
    y = x * lax.rsqrt(ms + NORM_EPS) * g
    return y * (1.0 + sc) + sh


def _mod_val(ref, per_row):
    return ref[...] if per_row else ref[0:1, :]


def _mod_spec(chunk, per_row, width=D_MODEL):
    if per_row:
        return pl.BlockSpec((DEC_BATCH, width), lambda *_: (0, chunk))
    return pl.BlockSpec((8, width), lambda *_: (PROMPT_MOD_BLOCK, chunk))


def _adaln_kernel(c_ref, w_ref, b_ref, o_ref, s_ref):
    @pl.when(pl.program_id(0) == 0)
    def _():
        s_ref[...] = jax.nn.silu(c_ref[...]).astype(BF16)

    o_ref[...] = _dot(s_ref[...], w_ref[...].astype(BF16)) + b_ref[...]


def _adaln(c_all, w, b, tn=1024):
    n = w.shape[1]
    return pl.pallas_call(
        _adaln_kernel,
        out_shape=jax.ShapeDtypeStruct((N_COND, n), F32),
        grid=(n // tn,),
        in_specs=[pl.BlockSpec((N_COND, D_MODEL), lambda j: (0, 0)),
                  pl.BlockSpec((D_MODEL, tn), lambda j: (0, j)),
                  pl.BlockSpec((1, tn), lambda j: (0, j))],
        out_specs=pl.BlockSpec((N_COND, tn), lambda j: (0, j)),
        scratch_shapes=[pltpu.VMEM((N_COND, D_MODEL), BF16)],
        compiler_params=_cparams(1),
        name="adaln",
    )(c_all, w, b)


FFN_ROWS = 512


def _ffn_kernel(*refs, tm, sub, per_row, n_f, final):
    if final:
        (x_ref, n_ref, sh_ref, sc_ref, g_ref, w1_ref, w3_ref, w2_ref,
         gf_ref, shf_ref, scf_ref, o_ref, u_ref) = refs
    else:
        (x_ref, n_ref, sh_ref, sc_ref, g_ref, w1_ref, w3_ref, w2_ref,
         o_ref, u_ref) = refs
    f = pl.program_id(1)

    @pl.when(f == 0)
    def _():
        def body(s, carry):
            rows = pl.ds(pl.multiple_of(s * sub, sub), sub)
            u_ref[rows, :] = _rms_mod(x_ref[rows, :], n_ref[...], _mod_val(sh_ref, per_row),
                                      _mod_val(sc_ref, per_row)).astype(BF16)
            o_ref[rows, :] = jnp.zeros((sub, D_MODEL), F32)
            return carry
        lax.fori_loop(0, tm // sub, body, 0)

    for s in range(tm // FFN_ROWS):
        rows = slice(s * FFN_ROWS, (s + 1) * FFN_ROWS)
        u = u_ref[rows, :]
        h1 = _dot(u, w1_ref[...])
        h3 = _dot(u, w3_ref[...])
        a = (jax.nn.silu(h1) * h3).astype(BF16)
        o_ref[rows, :] += _dot(a, w2_ref[...])

    @pl.when(f == n_f - 1)
    def _():
        def body(s, carry):
            rows = pl.ds(pl.multiple_of(s * sub, sub), sub)
            h = x_ref[rows, :] + 0.5 * _mod_val(g_ref, per_row) * o_ref[rows, :]
            if final:
                h = _rms_mod(h, gf_ref[...], _mod_val(shf_ref, per_row), _mod_val(scf_ref, per_row))
            o_ref[rows, :] = h
            return carry
        lax.fori_loop(0, tm // sub, body, 0)


def _ffn(x, norm_g, mod, chunk0, w1, w3, w2, *, per_row, tm, tf=512, final=None):
    rows = x.shape[0]
    n_f = D_FF // tf
    sub = DEC_BATCH
    row_spec = pl.BlockSpec((tm, D_MODEL), lambda i, f: (i, 0))
    vec_spec = pl.BlockSpec((1, D_MODEL), lambda i, f: (0, 0))
    in_specs = [row_spec, vec_spec,
                _mod_spec(chunk0, per_row), _mod_spec(chunk0 + 1, per_row), _mod_spec(chunk0 + 2, per_row),
                pl.BlockSpec((D_MODEL, tf), lambda i, f: (0, f)),
                pl.BlockSpec((D_MODEL, tf), lambda i, f: (0, f)),
                pl.BlockSpec((tf, D_MODEL), lambda i, f: (f, 0))]
    args = [x, norm_g, mod, mod, mod, w1, w3, w2]
    if final is not None:
        gf, mod_f = final
        in_specs += [vec_spec, _mod_spec(0, per_row), _mod_spec(1, per_row)]
        args += [gf, mod_f, mod_f]
    return pl.pallas_call(
        functools.partial(_ffn_kernel, tm=tm, sub=sub, per_row=per_row, n_f=n_f, final=final is not None),
        out_shape=jax.ShapeDtypeStruct((rows, D_MODEL), F32),
        grid=(rows // tm, n_f),
        in_specs=in_specs,
        out_specs=row_spec,
        scratch_shapes=[pltpu.VMEM((tm, D_MODEL), BF16)],
        compiler_params=_cparams(2),
        name="ffn_final" if final is not None else "ffn",
    )(*args)


PROJ_TN = 1024
PROJ_STEPS = IN_WIDTH // PROJ_TN
PROJ_CHUNK = 256


def _proj_kernel(h_ref, n_ref, sh_ref, sc_ref, w_ref, q_ref, kv_ref, z_ref, ga_ref, gc_ref, u_ref,
                 *, tm, sub, per_row):
    j = pl.program_id(1)

    @pl.when(j == 0)
    def _():
        def body(s, carry):
            rows = pl.ds(pl.multiple_of(s * sub, sub), sub)
            u_ref[rows, :] = _rms_mod(h_ref[rows, :], n_ref[...], _mod_val(sh_ref, per_row),
                                      _mod_val(sc_ref, per_row)).astype(BF16)
            return carry
        lax.fori_loop(0, tm // sub, body, 0)

    def chunk(c):
        cols = slice(c * PROJ_CHUNK, (c + 1) * PROJ_CHUNK)
        return cols, _dot(u_ref[...], w_ref[:, cols])

    n_chunks = PROJ_TN // PROJ_CHUNK

    @pl.when(j < 2)
    def _():
        for c in range(n_chunks):
            cols, acc = chunk(c)
            q_ref[:, cols] = acc.astype(q_ref.dtype)

    @pl.when(j == 2)
    def _():
        for c in range(n_chunks):
            cols, acc = chunk(c)
            kv_ref[:, cols] = acc

    @pl.when((j == 3) | (j == 4))
    def _():
        for c in range(n_chunks // 2):
            cols, a = chunk(c)
            _, g = chunk(c + n_chunks // 2)
            z_ref[:, cols] = a * jax.nn.sigmoid(g)

    @pl.when((j == 5) | (j == 6))
    def _():
        for c in range(n_chunks):
            cols, acc = chunk(c)
            ga_ref[:, cols] = jax.nn.sigmoid(acc).astype(BF16)

    @pl.when(j >= 7)
    def _():
        for c in range(n_chunks):
            cols, acc = chunk(c)
            gc_ref[:, cols] = jax.nn.sigmoid(acc).astype(BF16)


def _proj(h, norm_g, mod, w_pack, *, per_row, tm, q_dtype):
    rows = h.shape[0]
    sub = DEC_BATCH

    def col_spec(width, first):
        return pl.BlockSpec((tm, width), lambda i, j: (i, jnp.clip(j - first, 0, 1)))

    return pl.pallas_call(
        functools.partial(_proj_kernel, tm=tm, sub=sub, per_row=per_row),
        out_shape=(jax.ShapeDtypeStruct((rows, ATTN_WIDTH), q_dtype),
                   jax.ShapeDtypeStruct((rows, 2 * KV_WIDTH), F32),
                   jax.ShapeDtypeStruct((rows, C_CONV), F32),
                   jax.ShapeDtypeStruct((rows, D_MODEL), BF16),
                   jax.ShapeDtypeStruct((rows, D_MODEL), BF16)),
        grid=(rows // tm, PROJ_STEPS),
        in_specs=[pl.BlockSpec((tm, D_MODEL), lambda i, j: (i, 0)),
                  pl.BlockSpec((1, D_MODEL), lambda i, j: (0, 0)),
                  _mod_spec(3, per_row), _mod_spec(4, per_row),
                  pl.BlockSpec((D_MODEL, PROJ_TN), lambda i, j: (0, j))],
        out_specs=(col_spec(PROJ_TN, 0),
                   pl.BlockSpec((tm, 2 * KV_WIDTH), lambda i, j: (i, 0)),
                   col_spec(PROJ_TN // 2, 3),
                   col_spec(PROJ_TN, 5),
                   col_spec(PROJ_TN, 7)),
        scratch_shapes=[pltpu.VMEM((tm, D_MODEL), BF16)],
        compiler_params=_cparams(2),
        name="proj",
    )(h, norm_g, mod, mod, w_pack)


def _t5_bucket_np(dist):
    exact = N_BUCKETS // 2
    d = np.maximum(dist, 0)
    df = np.maximum(d, 1).astype(np.float32)
    large = exact + (np.log(df / np.float32(exact)) / np.float32(np.log(MAX_DISTANCE / exact))
                     * np.float32(N_BUCKETS - exact)).astype(np.int32)
    large = np.minimum(large, N_BUCKETS - 1)
    return np.where(d < exact, d, large).astype(np.int32)


def _prompt_codes():
    qi = np.arange(Q_BLOCK)[:, None]
    kj = np.arange(2 * Q_BLOCK)[None, :]
    dist = qi + Q_BLOCK - kj
    valid = (dist >= 0) & (dist <= WINDOW)
    return np.where(valid, _t5_bucket_np(dist), -1).astype(np.int32)


def _sample_codes():
    t = np.repeat(np.arange(DEC_SEQ), GROUP * N_KV_HEADS)[:, None]
    j = np.arange(S_KEYS)[None, :]
    dist = t + WBUF - j
    valid = (dist >= 0) & (dist <= WINDOW) & (j < WBUF + DEC_SEQ)
    code = np.where(valid, _t5_bucket_np(dist), -1)
    code = np.where(j == SINK_COL, N_BUCKETS, code)
    return code.astype(np.int32)


def _attn_p_kernel(tab_ref, sink_ref, code_ref, q_ref, kvc_ref, kvp_ref, o_ref, bias_ref):
    n = pl.program_id(0)
    n_keys = 2 * Q_BLOCK
    wide = GROUP * Q_BLOCK

    @pl.when(n <= 1)
    def _():
        key = lax.broadcasted_iota(jnp.int32, (n_keys, Q_BLOCK), 0)
        code = jnp.where((n == 0) & (key < Q_BLOCK), -1, code_ref[...])

        def kv_body(kv, carry):
            for g in range(GROUP):
                h = kv * GROUP + g

                def bucket_body(bk, acc):
                    return jnp.where(code == bk, tab_ref[bk * N_HEADS + h], acc)
                bias_ref[kv, :, g * Q_BLOCK:(g + 1) * Q_BLOCK] = lax.fori_loop(
                    0, N_BUCKETS, bucket_body, jnp.full((n_keys, Q_BLOCK), NEG_INF, F32))
            return carry
        lax.fori_loop(0, N_KV_HEADS, kv_body, 0)

    q_t = q_ref[...].astype(F32).T
    ones = jnp.ones((16, n_keys), BF16)
    zeros = jnp.zeros((HEAD_DIM, wide), BF16)
    def pair_kv(pair):
        lanes = slice(pair * LANES, (pair + 1) * LANES)
        v_lanes = slice(KV_WIDTH + pair * LANES, KV_WIDTH + (pair + 1) * LANES)
        k2 = jnp.concatenate([kvp_ref[:, lanes], kvc_ref[:, lanes]], axis=0).astype(BF16)
        v2_t = jnp.concatenate([kvp_ref[:, v_lanes], kvc_ref[:, v_lanes]], axis=0).T.astype(BF16)
        return k2, v2_t

    def scores(kv, k2):
        q4 = jnp.concatenate(
            [q_t[g * KV_WIDTH + kv * HEAD_DIM:g * KV_WIDTH + (kv + 1) * HEAD_DIM, :] for g in range(GROUP)],
            axis=1)
        q4 = (q4 * ATTN_SCALE).astype(BF16)
        q4 = jnp.concatenate([q4, zeros] if kv % 2 == 0 else [zeros, q4], axis=0)
        return _dot(k2, q4)

    kvs = [pair_kv(pair) for pair in range(N_KV_HEADS // 2)]
    s_next = scores(0, kvs[0][0])
    outs = []
    for kv in range(N_KV_HEADS):
        pair, parity = divmod(kv, 2)
        s = s_next
        if kv + 1 < N_KV_HEADS:
            s_next = scores(kv + 1, kvs[(kv + 1) // 2][0])
        s = s + bias_ref[kv]
        sink = jnp.concatenate(
            [jnp.full((1, Q_BLOCK), sink_ref[kv * GROUP + g], F32) for g in range(GROUP)], axis=1)
        m = jnp.maximum(jnp.max(s, axis=0, keepdims=True), sink)
        p = jnp.exp(s - m).astype(BF16)
        lhs = jnp.concatenate([kvs[pair][1][parity * HEAD_DIM:(parity + 1) * HEAD_DIM, :], ones], axis=0)
        o_t = _dot(lhs, p)
        l = o_t[HEAD_DIM:HEAD_DIM + 1, :] + jnp.exp(sink - m)
        outs.append(o_t[0:HEAD_DIM, :] * (1.0 / l))
        if parity == 1:
            o_pair = jnp.concatenate(outs, axis=0)
            outs = []
            for g in range(GROUP):
                cols = slice(g * KV_WIDTH + pair * LANES, g * KV_WIDTH + (pair + 1) * LANES)
                o_ref[:, cols] = o_pair[:, g * Q_BLOCK:(g + 1) * Q_BLOCK].T.astype(o_ref.dtype)


def _attn_prompt(q, kv, tab_flat, sinks):
    nb = SEQ // Q_BLOCK
    code_t = jnp.asarray(np.ascontiguousarray(_prompt_codes().T))
    smem = pl.BlockSpec(memory_space=pltpu.SMEM)
    return pl.pallas_call(
        _attn_p_kernel,
        out_shape=jax.ShapeDtypeStruct((SEQ, ATTN_WIDTH), BF16),
        grid=(nb,),
        in_specs=[smem, smem,
                  pl.BlockSpec((2 * Q_BLOCK, Q_BLOCK), lambda n: (0, 0)),
                  pl.BlockSpec((Q_BLOCK, ATTN_WIDTH), lambda n: (n, 0)),
                  pl.BlockSpec((Q_BLOCK, 2 * KV_WIDTH), lambda n: (n, 0)),
                  pl.BlockSpec((Q_BLOCK, 2 * KV_WIDTH), lambda n: (jnp.maximum(n - 1, 0), 0))],
        out_specs=pl.BlockSpec((Q_BLOCK, ATTN_WIDTH), lambda n: (n, 0)),
        scratch_shapes=[pltpu.VMEM((N_KV_HEADS, 2 * Q_BLOCK, GROUP * Q_BLOCK), F32)],
        compiler_params=_cparams(1),
        name="attn_prompt",
    )(tab_flat, sinks, code_t, q, kv, kv)


S_BB = 8


def _attn_s_kernel(th_ref, code_ref, q_ref, kv_ref, ck_ref, cv_ref, o_ref, kw_ref, vw_ref,
                   bias_ref, kf_ref, vf_ref, qe_ref):
    @pl.when(pl.program_id(0) == 0)
    def _():
        code = code_ref[...]
        acc = jnp.full((DEC_SEQ * N_HEADS, S_KEYS), NEG_INF, F32)
        for bk in range(N_BUCKETS + 1):
            acc = jnp.where(code == bk, th_ref[:, bk:bk + 1], acc)
        bias_ref[...] = acc
        kf_ref[...] = jnp.zeros_like(kf_ref)
        vf_ref[...] = jnp.zeros_like(vf_ref)

    lane_head = lax.broadcasted_iota(jnp.int32, (N_KV_HEADS, KV_WIDTH), 1) // HEAD_DIM
    row_head = lax.broadcasted_iota(jnp.int32, (N_KV_HEADS, KV_WIDTH), 0)
    diag = lane_head == row_head
    rows_all = DEC_SEQ * N_HEADS
    out_mask = (lax.broadcasted_iota(jnp.int32, (rows_all, KV_WIDTH), 0) % N_KV_HEADS
                == lax.broadcasted_iota(jnp.int32, (rows_all, KV_WIDTH), 1) // HEAD_DIM)
    bias = bias_ref[...]

    for b in range(S_BB):
        kf_ref[0:WBUF, :] = ck_ref[b]
        vf_ref[0:WBUF, :] = cv_ref[b]
        kw_ref[b, 0:WBUF - DEC_SEQ, :] = ck_ref[b, DEC_SEQ:WBUF, :]
        vw_ref[b, 0:WBUF - DEC_SEQ, :] = cv_ref[b, DEC_SEQ:WBUF, :]
        for t in range(DEC_SEQ):
            k_new = kv_ref[t, b:b + 1, 0:KV_WIDTH]
            v_new = kv_ref[t, b:b + 1, KV_WIDTH:2 * KV_WIDTH]
            kf_ref[WBUF + t:WBUF + t + 1, :] = k_new
            vf_ref[WBUF + t:WBUF + t + 1, :] = v_new
            kw_ref[b, WBUF - DEC_SEQ + t:WBUF - DEC_SEQ + t + 1, :] = k_new
            vw_ref[b, WBUF - DEC_SEQ + t:WBUF - DEC_SEQ + t + 1, :] = v_new
            for g in range(GROUP):
                q_row = q_ref[t, b:b + 1, g * KV_WIDTH:(g + 1) * KV_WIDTH]
                piece = jnp.where(diag, jnp.broadcast_to(q_row, (N_KV_HEADS, KV_WIDTH)), 0.0)
                r0 = (t * GROUP + g) * N_KV_HEADS
                qe_ref[r0:r0 + N_KV_HEADS, :] = piece
        s = lax.dot_general(qe_ref[...].astype(BF16), kf_ref[...].astype(BF16),
                            (((1,), (1,)), ((), ())), preferred_element_type=F32)
        s = s * ATTN_SCALE + bias
        m = jnp.max(s, axis=-1, keepdims=True)
        p = jnp.exp(s - m)
        l = jnp.sum(p, axis=-1, keepdims=True)
        o = _dot(p.astype(BF16), vf_ref[...].astype(BF16)) / l
        o = jnp.where(out_mask, o, 0.0)
        for t in range(DEC_SEQ):
            for g in range(GROUP):
                r0 = (t * GROUP + g) * N_KV_HEADS
                row = jnp.sum(o[r0:r0 + N_KV_HEADS, :], axis=0, keepdims=True)
                o_ref[t, b:b + 1, g * KV_WIDTH:(g + 1) * KV_WIDTH] = row


def _attn_sample(q, kv, cache_k, cache_v, th):
    code = jnp.asarray(_sample_codes())
    rows_all = DEC_SEQ * N_HEADS
    tb_spec = lambda w: pl.BlockSpec((DEC_SEQ, S_BB, w), lambda i: (0, i, 0))
    cache_spec = pl.BlockSpec((S_BB, WBUF, KV_WIDTH), lambda i: (i, 0, 0))
    cache_shape = jax.ShapeDtypeStruct((DEC_BATCH, WBUF, KV_WIDTH), F32)
    return pl.pallas_call(
        _attn_s_kernel,
        out_shape=(jax.ShapeDtypeStruct((DEC_SEQ, DEC_BATCH, ATTN_WIDTH), F32), cache_shape, cache_shape),
        grid=(DEC_BATCH // S_BB,),
        in_specs=[pl.BlockSpec((rows_all, N_BUCKETS + 1), lambda i: (0, 0)),
                  pl.BlockSpec((rows_all, S_KEYS), lambda i: (0, 0)),
                  tb_spec(ATTN_WIDTH), tb_spec(2 * KV_WIDTH), cache_spec, cache_spec],
        out_specs=(tb_spec(ATTN_WIDTH), cache_spec, cache_spec),
        scratch_shapes=[pltpu.VMEM((rows_all, S_KEYS), F32),
                        pltpu.VMEM((S_KEYS, KV_WIDTH), F32),
                        pltpu.VMEM((S_KEYS, KV_WIDTH), F32),
                        pltpu.VMEM((rows_all, KV_WIDTH), F32)],
        compiler_params=_cparams(1),
        name="attn_sample",
    )(th, code, q, kv, cache_k, cache_v)


SUBLANES = 8
LANES = 128
CONV_RC = 32
CONV_ROWS = 128
CONV_HALO = 32


def _ln_silu(acc, lg, lb):
    mu = jnp.mean(acc, axis=-1, keepdims=True)
    xc = acc - mu
    var = jnp.mean(xc * xc, axis=-1, keepdims=True)
    return jax.nn.silu(xc * lax.rsqrt(var + NORM_EPS) * lg + lb)


def _conv_p_kernel(zc_ref, zh_ref, w_ref, b_ref, lg_ref, lb_ref, y_ref, s_ref, c_ref, *, tc):
    i = pl.program_id(0)
    s_ref[0:CONV_HALO, :] = jnp.where(i == 0, 0.0, zh_ref[...])
    s_ref[CONV_HALO:, :] = zc_ref[...]
    off = CONV_HALO - HIST
    groups = [[j for j in range(CONV_WIDTH) if (j + off) % SUBLANES == r] for r in range(SUBLANES)]
    for rc in range(tc // CONV_ROWS):
        t0 = rc * CONV_ROWS
        for lc in range(C_CONV // LANES):
            lanes = slice(lc * LANES, (lc + 1) * LANES)
            out = jnp.broadcast_to(b_ref[:, lanes], (CONV_ROWS, LANES))
            for r, taps in enumerate(groups):
                n_rows = CONV_ROWS + (SUBLANES if r else 0)
                part = None
                for j in taps:
                    base = t0 + (j + off) - r
                    term = w_ref[j:j + 1, lanes] * s_ref[base:base + n_rows, lanes]
                    part = term if part is None else part + term
                out = out + part[r:r + CONV_ROWS, :]
            c_ref[t0:t0 + CONV_ROWS, lanes] = out
    for r in range(tc // CONV_RC):
        rows = slice(r * CONV_RC, (r + 1) * CONV_RC)
        y_ref[rows, :] = _ln_silu(c_ref[rows, :], lg_ref[...], lb_ref[...]).astype(BF16)


def _conv_prompt(z, dw_w, dw_b, ln_g, ln_b, tc=256):
    rows = z.shape[0]
    ratio = tc // CONV_HALO
    vec = pl.BlockSpec((1, C_CONV), lambda i: (0, 0))
    return pl.pallas_call(
        functools.partial(_conv_p_kernel, tc=tc),
        out_shape=jax.ShapeDtypeStruct((rows, C_CONV), BF16),
        grid=(rows // tc,),
        in_specs=[pl.BlockSpec((tc, C_CONV), lambda i: (i, 0)),
                  pl.BlockSpec((CONV_HALO, C_CONV), lambda i: (jnp.maximum(i * ratio - 1, 0), 0)),
                  pl.BlockSpec((CONV_WIDTH, C_CONV), lambda i: (0, 0)), vec, vec, vec],
        out_specs=pl.BlockSpec((tc, C_CONV), lambda i: (i, 0)),
        scratch_shapes=[pltpu.VMEM((tc + CONV_HALO, C_CONV), F32), pltpu.VMEM((tc, C_CONV), F32)],
        compiler_params=_cparams(1),
        name="conv_prompt",
    )(z, z, dw_w, dw_b, ln_g, ln_b)


CONV_S_BB = 32


def _conv_s_kernel(z_ref, hist_ref, w_ref, b_ref, lg_ref, lb_ref, y_ref, ns_ref):
    for t in range(DEC_SEQ):
        acc = jnp.broadcast_to(b_ref[...], (CONV_S_BB, C_CONV))
        for j in range(CONV_WIDTH):
            i = t + j
            src = hist_ref[:, i * C_CONV:(i + 1) * C_CONV] if i < HIST else z_ref[i - HIST]
            acc = acc + w_ref[j:j + 1, :] * src
        y_ref[t] = _ln_silu(acc, lg_ref[...], lb_ref[...]).astype(BF16)
    keep = HIST - DEC_SEQ
    ns_ref[:, 0:keep * C_CONV] = hist_ref[:, DEC_SEQ * C_CONV:HIST * C_CONV]
    for t in range(DEC_SEQ):
        ns_ref[:, (keep + t) * C_CONV:(keep + t + 1) * C_CONV] = z_ref[t]


def _conv_sample(z, hist2d, dw_w, dw_b, ln_g, ln_b):
    vec = pl.BlockSpec((1, C_CONV), lambda i: (0, 0))
    tb = pl.BlockSpec((DEC_SEQ, CONV_S_BB, C_CONV), lambda i: (0, i, 0))
    st = pl.BlockSpec((CONV_S_BB, HIST * C_CONV), lambda i: (i, 0))
    return pl.pallas_call(
        _conv_s_kernel,
        out_shape=(jax.ShapeDtypeStruct((DEC_SEQ, DEC_BATCH, C_CONV), BF16),
                   jax.ShapeDtypeStruct((DEC_BATCH, HIST * C_CONV), F32)),
        grid=(DEC_BATCH // CONV_S_BB,),
        in_specs=[tb, st, pl.BlockSpec((CONV_WIDTH, C_CONV), lambda i: (0, 0)), vec, vec, vec],
        out_specs=(tb, st),
        compiler_params=_cparams(1),
        name="conv_sample",
    )(z, hist2d, dw_w, dw_b, ln_g, ln_b)


def _mix_kernel(at_ref, y_ref, ga_ref, gc_ref, h_ref, g2_ref, wao_ref, wco_ref, wout_ref, o_ref,
                *, per_row):
    a = _dot(at_ref[...].astype(BF16), wao_ref[...])
    c = _dot(y_ref[...], wco_ref[...])
    merged = (ga_ref[...].astype(F32) * a + gc_ref[...].astype(F32) * c).astype(BF16)
    r = _dot(merged, wout_ref[...])
    o_ref[...] = h_ref[...] + _mod_val(g2_ref, per_row) * r


def _mix(attn, y, ga, gc, h, mod, w_ao, w_co, w_out, *, per_row, tm):
    rows = h.shape[0]
    row = lambda w: pl.BlockSpec((tm, w), lambda i: (i, 0))
    resident = lambda shape: pl.BlockSpec(shape, lambda i: (0, 0), pipeline_mode=pl.Buffered(1))
    return pl.pallas_call(
        functools.partial(_mix_kernel, per_row=per_row),
        out_shape=jax.ShapeDtypeStruct((rows, D_MODEL), F32),
        grid=(rows // tm,),
        in_specs=[row(ATTN_WIDTH), row(C_CONV), row(D_MODEL), row(D_MODEL), row(D_MODEL),
                  _mod_spec(5, per_row),
                  resident((ATTN_WIDTH, D_MODEL)), resident((C_CONV, D_MODEL)), resident((D_MODEL, D_MODEL))],
        out_specs=row(D_MODEL),
        compiler_params=_cparams(1),
        name="mix",
    )(attn, y, ga, gc, h, mod, w_ao, w_co, w_out)


def _pack_w_in(w):
    wq = w[:, :ATTN_WIDTH].reshape(D_MODEL, N_KV_HEADS, GROUP, HEAD_DIM)
    wq = wq.transpose(0, 2, 1, 3).reshape(D_MODEL, ATTN_WIDTH)
    c0 = ATTN_WIDTH
    wkv = w[:, c0:c0 + 2 * KV_WIDTH]
    c0 += 2 * KV_WIDTH
    wa = w[:, c0:c0 + C_CONV]
    wg = w[:, c0 + C_CONV:c0 + 2 * C_CONV]
    half = C_CONV // 2
    wglu = jnp.concatenate([wa[:, :half], wg[:, :half], wa[:, half:], wg[:, half:]], axis=1)
    wgates = w[:, c0 + 2 * C_CONV:]
    return jnp.concatenate([wq, wkv, wglu, wgates], axis=1).astype(BF16)


def kernel(x_prompt, x_sample, cache_k, cache_v, state_conv, c_prompt, c_sample, rel_bias_table, norm1_g, ffn1_w1, ffn1_w3, ffn1_w2, norm2_g, w_in, attn_sinks, conv_dw_w, conv_dw_b, conv_ln_g, conv_ln_b, w_conv_out, w_attn_out, w_out, norm3_g, ffn2_w1, ffn2_w3, ffn2_w2, w_ada, b_ada, final_norm_g, w_ada_final, b_ada_final):
    c_all = jnp.concatenate([c_sample, c_prompt, jnp.zeros((N_COND - DEC_BATCH - 1, D_MODEL), F32)], axis=0)
    mod = _adaln(c_all, w_ada[0], b_ada)
    mod_f = _adaln(c_all, w_ada_final, b_ada_final[None, :])

    f1 = (ffn1_w1[0].astype(BF16), ffn1_w3[0].astype(BF16), ffn1_w2[0].astype(BF16))
    f2 = (ffn2_w1[0].astype(BF16), ffn2_w3[0].astype(BF16), ffn2_w2[0].astype(BF16))
    w_pack = _pack_w_in(w_in[0])
    w_ao = w_attn_out[0].reshape(N_KV_HEADS, GROUP, HEAD_DIM, D_MODEL).transpose(1, 0, 2, 3)
    w_ao = w_ao.reshape(ATTN_WIDTH, D_MODEL).astype(BF16)
    w_co = w_conv_out[0].astype(BF16)
    w_o = w_out[0].astype(BF16)

    sinks = attn_sinks[0]
    tab_flat = rel_bias_table.reshape(-1)
    tab_ext = jnp.concatenate([rel_bias_table, sinks[None, :]], axis=0)
    th = tab_ext.T.reshape(N_KV_HEADS, GROUP, N_BUCKETS + 1).transpose(1, 0, 2)
    th = jnp.tile(th.reshape(N_HEADS, N_BUCKETS + 1), (DEC_SEQ, 1))

    xp = x_prompt[0]
    xs = x_sample.transpose(1, 0, 2).reshape(S_ROWS, D_MODEL)

    def trunk(x, per_row, tm_ffn, tm, attend, convolve, q_dtype, tm_mix):
        h = _ffn(x, norm1_g, mod, 0, *f1, per_row=per_row, tm=tm_ffn)
        q, kv, z, ga, gc = _proj(h, norm2_g, mod, w_pack, per_row=per_row, tm=tm, q_dtype=q_dtype)
        attn, states = attend(q, kv)
        y, conv_state = convolve(z)
        h = _mix(attn, y, ga, gc, h, mod, w_ao, w_co, w_o, per_row=per_row, tm=tm_mix)
        out = _ffn(h, norm3_g, mod, 6, *f2, per_row=per_row, tm=tm_ffn, final=(final_norm_g[None, :], mod_f))
        return out, kv, z, states, conv_state

    def attend_p(q, kv):
        return _attn_prompt(q, kv, tab_flat, sinks), None

    def conv_p(z):
        return _conv_prompt(z, conv_dw_w[0], conv_dw_b, conv_ln_g, conv_ln_b), None

    def attend_s(q, kv):
        o, kw, vw = _attn_sample(q.reshape(DEC_SEQ, DEC_BATCH, ATTN_WIDTH),
                                 kv.reshape(DEC_SEQ, DEC_BATCH, 2 * KV_WIDTH),
                                 cache_k.reshape(DEC_BATCH, WBUF, KV_WIDTH),
                                 cache_v.reshape(DEC_BATCH, WBUF, KV_WIDTH), th)
        return o.reshape(S_ROWS, ATTN_WIDTH), (kw, vw)

    def conv_s(z):
        y, ns = _conv_sample(z.reshape(DEC_SEQ, DEC_BATCH, C_CONV),
                             state_conv.reshape(DEC_BATCH, HIST * C_CONV),
                             conv_dw_w[0], conv_dw_b, conv_ln_g, conv_ln_b)
        return y.reshape(S_ROWS, C_CONV), ns

    yp, kv_p, z_p, _, _ = trunk(xp, False, 1024, 512, attend_p, conv_p, BF16, 256)
    ys, _, _, (kw, vw), ns = trunk(xs, True, S_ROWS, S_ROWS, attend_s, conv_s, F32, DEC_BATCH)

    w = min(WINDOW, SEQ)
    kv_shape = (1, 1, w, N_KV_HEADS, HEAD_DIM)
    k_win_p = kv_p[SEQ - w:, :KV_WIDTH].reshape(kv_shape)
    v_win_p = kv_p[SEQ - w:, KV_WIDTH:].reshape(kv_shape)
    conv_p_state = z_p[SEQ - HIST:].reshape(1, 1, HIST, C_CONV)
    s_shape = (1, DEC_BATCH, WBUF, N_KV_HEADS, HEAD_DIM)
    y_prompt = yp[None]
    y_sample = ys.reshape(DEC_SEQ, DEC_BATCH, D_MODEL).transpose(1, 0, 2)
    return (y_prompt, y_sample, k_win_p, v_win_p, conv_p_state,
            kw.reshape(s_shape), vw.reshape(s_shape), ns.reshape(1, DEC_BATCH, HIST, C_CONV))
```

```python
import functools

import numpy as np
import jax
import jax.numpy as jnp
from jax import lax
from jax.experimental import pallas as pl
from jax.experimental.pallas import tpu as pltpu

D_MODEL = 2048
SEQ = 8192
DEC_BATCH = 128
DEC_SEQ = 4
PAST_LEN = 16384
N_HEADS = 32
N_KV_HEADS = 8
HEAD_DIM = 64
GROUP = N_HEADS // N_KV_HEADS
ATTN_WIDTH = N_HEADS * HEAD_DIM
KV_WIDTH = N_KV_HEADS * HEAD_DIM
WINDOW = 128
Q_BLOCK = 128
ATTN_SCALE = HEAD_DIM ** -0.5
N_BUCKETS = 32
MAX_DISTANCE = 128
C_CONV = D_MODEL // 2
CONV_WIDTH = 31
HIST = CONV_WIDTH - 1
D_FF = 5632
NORM_EPS = 1e-6
NEG_INF = -1e30
IN_WIDTH = ATTN_WIDTH + 2 * KV_WIDTH + 2 * C_CONV + 2 * D_MODEL
WBUF = min(WINDOW, PAST_LEN)

S_ROWS = DEC_BATCH * DEC_SEQ
N_COND = 136
PROMPT_MOD_BLOCK = DEC_BATCH // 8
S_KEYS = 136
SINK_COL = WBUF + DEC_SEQ

VMEM_LIMIT = 60 * 1024 * 1024

F32 = jnp.float32
BF16 = jnp.bfloat16


def _cparams(n_axes, flags=None):
    return pltpu.CompilerParams(dimension_semantics=("arbitrary",) * n_axes,
                                vmem_limit_bytes=VMEM_LIMIT, flags=flags)


def _dot(a, b):
    return jnp.dot(a, b, preferred_element_type=F32)


def _rms_mod(x, g, sh, sc):
    ms = jnp.mean(x * x, axis=-1, keepdims=True)
    y = x * lax.rsqrt(ms + NORM_EPS) * g
    return y * (1.0 + sc) + sh


def _mod_val(ref, per_row):
    return ref[...] if per_row else ref[0:1, :]


def _mod_spec(chunk, per_row, width=D_MODEL):
    if per_row:
        return pl.BlockSpec((DEC_BATCH, width), lambda *_: (0, chunk))
    return pl.BlockSpec((8, width), lambda *_: (PROMPT_MOD_BLOCK, chunk))


def _adaln_kernel(c_ref, w_ref, b_ref, o_ref, s_ref):
    @pl.when(pl.program_id(0) == 0)
    def _():
        s_ref[...] = jax.nn.silu(c_ref[...]).astype(BF16)

    o_ref[...] = _dot(s_ref[...], w_ref[...].astype(BF16)) + b_ref[...]


def _adaln(c_all, w, b, tn=1024):
    n = w.shape[1]
    return pl.pallas_call(
        _adaln_kernel,
        out_shape=jax.ShapeDtypeStruct((N_COND, n), F32),
        grid=(n // tn,),
        in_specs=[pl.BlockSpec((N_COND, D_MODEL), lambda j: (0, 0)),
                  pl.BlockSpec((D_MODEL, tn), lambda j: (0, j)),
                  pl.BlockSpec((1, tn), lambda j: (0, j))],
        out_specs=pl.BlockSpec((N_COND, tn), lambda j: (0, j)),
        scratch_shapes=[pltpu.VMEM((N_COND, D_MODEL), BF16)],
        compiler_params=_cparams(1),
        name="adaln",
    )(c_all, w, b)


FFN_ROWS = 512
FFN_TM = 1024
FFN_TF = 512
FFN_TF_CAST = 256
PROJ_TM = 1024


def _ffn_kernel(*refs, tm, sub, per_row, n_f, final, cast_out):
    refs = list(refs)
    x_ref, n_ref, sh_ref, sc_ref, g_ref, w1_ref, w3_ref, w2_ref = refs[:8]
    del refs[:8]
    if final:
        gf_ref, shf_ref, scf_ref = refs[:3]
        del refs[:3]
    o_ref = refs.pop(0)
    if cast_out:
        w1o_ref, w3o_ref, w2o_ref = refs[:3]
        del refs[:3]
    (u_ref,) = refs
    f = pl.program_id(1)

    @pl.when(f == 0)
    def _():
        def body(s, carry):
            rows = pl.ds(pl.multiple_of(s * sub, sub), sub)
            u_ref[rows, :] = _rms_mod(x_ref[rows, :], n_ref[...], _mod_val(sh_ref, per_row),
                                      _mod_val(sc_ref, per_row)).astype(BF16)
            o_ref[rows, :] = jnp.zeros((sub, D_MODEL), F32)
            return carry
        lax.fori_loop(0, tm // sub, body, 0)

    w1, w3, w2 = w1_ref[...], w3_ref[...], w2_ref[...]
    if cast_out:
        w1, w3, w2 = w1.astype(BF16), w3.astype(BF16), w2.astype(BF16)
        w1o_ref[...], w3o_ref[...], w2o_ref[...] = w1, w3, w2
    for s in range(tm // FFN_ROWS):
        rows = slice(s * FFN_ROWS, (s + 1) * FFN_ROWS)
        u = u_ref[rows, :]
        h1 = _dot(u, w1)
        h3 = _dot(u, w3)
        a = (jax.nn.silu(h1) * h3).astype(BF16)
        o_ref[rows, :] += _dot(a, w2)

    @pl.when(f == n_f - 1)
    def _():
        def body(s, carry):
            rows = pl.ds(pl.multiple_of(s * sub, sub), sub)
            h = x_ref[rows, :] + 0.5 * _mod_val(g_ref, per_row) * o_ref[rows, :]
            if final:
                h = _rms_mod(h, gf_ref[...], _mod_val(shf_ref, per_row), _mod_val(scf_ref, per_row))
            o_ref[rows, :] = h
            return carry
        lax.fori_loop(0, tm // sub, body, 0)


def _ffn(x, norm_g, mod, chunk0, w1, w3, w2, *, per_row, tm, tf, final=None, cast_out=False):
    rows = x.shape[0]
    n_f = D_FF // tf
    sub = DEC_BATCH
    row_spec = pl.BlockSpec((tm, D_MODEL), lambda i, f: (i, 0))
    vec_spec = pl.BlockSpec((1, D_MODEL), lambda i, f: (0, 0))
    w_specs = [pl.BlockSpec((D_MODEL, tf), lambda i, f: (0, f)),
               pl.BlockSpec((D_MODEL, tf), lambda i, f: (0, f)),
               pl.BlockSpec((tf, D_MODEL), lambda i, f: (f, 0))]
    in_specs = [row_spec, vec_spec,
                _mod_spec(chunk0, per_row), _mod_spec(chunk0 + 1, per_row), _mod_spec(chunk0 + 2, per_row)]
    in_specs += w_specs
    args = [x, norm_g, mod, mod, mod, w1, w3, w2]
    if final is not None:
        gf, mod_f = final
        in_specs += [vec_spec, _mod_spec(0, per_row), _mod_spec(1, per_row)]
        args += [gf, mod_f, mod_f]
    out_shape = [jax.ShapeDtypeStruct((rows, D_MODEL), F32)]
    out_specs = [row_spec]
    if cast_out:
        assert rows == tm, "every weight block must be visited exactly once"
        out_shape += [jax.ShapeDtypeStruct(w.shape, BF16) for w in (w1, w3, w2)]
        out_specs += w_specs
    outs = pl.pallas_call(
        functools.partial(_ffn_kernel, tm=tm, sub=sub, per_row=per_row, n_f=n_f,
                          final=final is not None, cast_out=cast_out),
        out_shape=out_shape,
        grid=(rows // tm, n_f),
        in_specs=in_specs,
        out_specs=out_specs,
        scratch_shapes=[pltpu.VMEM((tm, D_MODEL), BF16)],
        compiler_params=_cparams(2),
        name="ffn_final" if final is not None else "ffn",
    )(*args)
    return (outs[0], tuple(outs[1:])) if cast_out else outs[0]


PROJ_TN = 1024
PROJ_STEPS = IN_WIDTH // PROJ_TN
PROJ_CHUNK = 256


def _proj_kernel(h_ref, n_ref, sh_ref, sc_ref, w_ref, q_ref, kv_ref, z_ref, ga_ref, gc_ref, u_ref,
                 *, tm, sub, per_row):
    j = pl.program_id(1)

    @pl.when(j == 0)
    def _():
        def body(s, carry):
            rows = pl.ds(pl.multiple_of(s * sub, sub), sub)
            u_ref[rows, :] = _rms_mod(h_ref[rows, :], n_ref[...], _mod_val(sh_ref, per_row),
                                      _mod_val(sc_ref, per_row)).astype(BF16)
            return carry
        lax.fori_loop(0, tm // sub, body, 0)

    def chunk(c):
        cols = slice(c * PROJ_CHUNK, (c + 1) * PROJ_CHUNK)
        return cols, _dot(u_ref[...], w_ref[:, cols])

    n_chunks = PROJ_TN // PROJ_CHUNK

    @pl.when(j < 2)
    def _():
        for c in range(n_chunks):
            cols, acc = chunk(c)
            q_ref[:, cols] = acc.astype(q_ref.dtype)

    @pl.when(j == 2)
    def _():
        for c in range(n_chunks):
            cols, acc = chunk(c)
            kv_ref[:, cols] = acc

    @pl.when((j == 3) | (j == 4))
    def _():
        for c in range(n_chunks // 2):
            cols, a = chunk(c)
            _, g = chunk(c + n_chunks // 2)
            z_ref[:, cols] = a * jax.nn.sigmoid(g)

    @pl.when((j == 5) | (j == 6))
    def _():
        for c in range(n_chunks):
            cols, acc = chunk(c)
            ga_ref[:, cols] = jax.nn.sigmoid(acc).astype(BF16)

    @pl.when(j >= 7)
    def _():
        for c in range(n_chunks):
            cols, acc = chunk(c)
            gc_ref[:, cols] = jax.nn.sigmoid(acc).astype(BF16)


def _proj(h, norm_g, mod, w_pack, *, per_row, tm, q_dtype):
    rows = h.shape[0]
    sub = DEC_BATCH

    def col_spec(width, first):
        return pl.BlockSpec((tm, width), lambda i, j: (i, jnp.clip(j - first, 0, 1)))

    return pl.pallas_call(
        functools.partial(_proj_kernel, tm=tm, sub=sub, per_row=per_row),
        out_shape=(jax.ShapeDtypeStruct((rows, ATTN_WIDTH), q_dtype),
                   jax.ShapeDtypeStruct((rows, 2 * KV_WIDTH), F32),
                   jax.ShapeDtypeStruct((rows, C_CONV), F32),
                   jax.ShapeDtypeStruct((rows, D_MODEL), BF16),
                   jax.ShapeDtypeStruct((rows, D_MODEL), BF16)),
        grid=(rows // tm, PROJ_STEPS),
        in_specs=[pl.BlockSpec((tm, D_MODEL), lambda i, j: (i, 0)),
                  pl.BlockSpec((1, D_MODEL), lambda i, j: (0, 0)),
                  _mod_spec(3, per_row), _mod_spec(4, per_row),
                  pl.BlockSpec((D_MODEL, PROJ_TN), lambda i, j: (0, j))],
        out_specs=(col_spec(PROJ_TN, 0),
                   pl.BlockSpec((tm, 2 * KV_WIDTH), lambda i, j: (i, 0)),
                   col_spec(PROJ_TN // 2, 3),
                   col_spec(PROJ_TN, 5),
                   col_spec(PROJ_TN, 7)),
        scratch_shapes=[pltpu.VMEM((tm, D_MODEL), BF16)],
        compiler_params=_cparams(2),
        name="proj",
    )(h, norm_g, mod, mod, w_pack)


def _t5_bucket_np(dist):
    exact = N_BUCKETS // 2
    d = np.maximum(dist, 0)
    df = np.maximum(d, 1).astype(np.float32)
    large = exact + (np.log(df / np.float32(exact)) / np.float32(np.log(MAX_DISTANCE / exact))
                     * np.float32(N_BUCKETS - exact)).astype(np.int32)
    large = np.minimum(large, N_BUCKETS - 1)
    return np.where(d < exact, d, large).astype(np.int32)


def _prompt_codes():
    qi = np.arange(Q_BLOCK)[:, None]
    kj = np.arange(2 * Q_BLOCK)[None, :]
    dist = qi + Q_BLOCK - kj
    valid = (dist >= 0) & (dist <= WINDOW)
    return np.where(valid, _t5_bucket_np(dist), -1).astype(np.int32)


def _sample_codes():
    t = np.repeat(np.arange(DEC_SEQ), GROUP * N_KV_HEADS)[:, None]
    j = np.arange(S_KEYS)[None, :]
    dist = t + WBUF - j
    valid = (dist >= 0) & (dist <= WINDOW) & (j < WBUF + DEC_SEQ)
    code = np.where(valid, _t5_bucket_np(dist), -1)
    code = np.where(j == SINK_COL, N_BUCKETS, code)
    return code.astype(np.int32)


def _attn_p_kernel(tab_ref, sink_ref, code_ref, q_ref, kvc_ref, kvp_ref, o_ref, bias_ref):
    n = pl.program_id(0)
    n_keys = 2 * Q_BLOCK
    wide = GROUP * Q_BLOCK

    @pl.when(n <= 1)
    def _():
        key = lax.broadcasted_iota(jnp.int32, (n_keys, Q_BLOCK), 0)
        code = jnp.where((n == 0) & (key < Q_BLOCK), -1, code_ref[...])

        def kv_body(kv, carry):
            for g in range(GROUP):
                h = kv * GROUP + g

                def bucket_body(bk, acc):
                    return jnp.where(code == bk, tab_ref[bk * N_HEADS + h], acc)
                bias_ref[kv, :, g * Q_BLOCK:(g + 1) * Q_BLOCK] = lax.fori_loop(
                    0, N_BUCKETS, bucket_body, jnp.full((n_keys, Q_BLOCK), NEG_INF, F32))
            return carry
        lax.fori_loop(0, N_KV_HEADS, kv_body, 0)

    q_t = q_ref[...].astype(F32).T
    ones = jnp.ones((16, n_keys), BF16)
    zeros = jnp.zeros((HEAD_DIM, wide), BF16)
    def pair_kv(pair):
        lanes = slice(pair * LANES, (pair + 1) * LANES)
        v_lanes = slice(KV_WIDTH + pair * LANES, KV_WIDTH + (pair + 1) * LANES)
        k2 = jnp.concatenate([kvp_ref[:, lanes], kvc_ref[:, lanes]], axis=0).astype(BF16)
        v2_t = jnp.concatenate([kvp_ref[:, v_lanes], kvc_ref[:, v_lanes]], axis=0).T.astype(BF16)
        return k2, v2_t

    def scores(kv, k2):
        q4 = jnp.concatenate(
            [q_t[g * KV_WIDTH + kv * HEAD_DIM:g * KV_WIDTH + (kv + 1) * HEAD_DIM, :] for g in range(GROUP)],
            axis=1)
        q4 = (q4 * ATTN_SCALE).astype(BF16)
        q4 = jnp.concatenate([q4, zeros] if kv % 2 == 0 else [zeros, q4], axis=0)
        return _dot(k2, q4)

    kvs = [pair_kv(pair) for pair in range(N_KV_HEADS // 2)]
    s_next = scores(0, kvs[0][0])
    outs = []
    for kv in range(N_KV_HEADS):
        pair, parity = divmod(kv, 2)
        s = s_next
        if kv + 1 < N_KV_HEADS:
            s_next = scores(kv + 1, kvs[(kv + 1) // 2][0])
        s = s + bias_ref[kv]
        sink = jnp.concatenate(
            [jnp.full((1, Q_BLOCK), sink_ref[kv * GROUP + g], F32) for g in range(GROUP)], axis=1)
        m = jnp.maximum(jnp.max(s, axis=0, keepdims=True), sink)
        p = jnp.exp(s - m).astype(BF16)
        lhs = jnp.concatenate([kvs[pair][1][parity * HEAD_DIM:(parity + 1) * HEAD_DIM, :], ones], axis=0)
        o_t = _dot(lhs, p)
        l = o_t[HEAD_DIM:HEAD_DIM + 1, :] + jnp.exp(sink - m)
        outs.append(o_t[0:HEAD_DIM, :] * (1.0 / l))
        if parity == 1:
            o_pair = jnp.concatenate(outs, axis=0)
            outs = []
            for g in range(GROUP):
                cols = slice(g * KV_WIDTH + pair * LANES, g * KV_WIDTH + (pair + 1) * LANES)
                o_ref[:, cols] = o_pair[:, g * Q_BLOCK:(g + 1) * Q_BLOCK].T.astype(o_ref.dtype)


def _attn_prompt(q, kv, tab_flat, sinks):
    nb = SEQ // Q_BLOCK
    code_t = jnp.asarray(np.ascontiguousarray(_prompt_codes().T))
    smem = pl.BlockSpec(memory_space=pltpu.SMEM)
    return pl.pallas_call(
        _attn_p_kernel,
        out_shape=jax.ShapeDtypeStruct((SEQ, ATTN_WIDTH), BF16),
        grid=(nb,),
        in_specs=[smem, smem,
                  pl.BlockSpec((2 * Q_BLOCK, Q_BLOCK), lambda n: (0, 0)),
                  pl.BlockSpec((Q_BLOCK, ATTN_WIDTH), lambda n: (n, 0)),
                  pl.BlockSpec((Q_BLOCK, 2 * KV_WIDTH), lambda n: (n, 0)),
                  pl.BlockSpec((Q_BLOCK, 2 * KV_WIDTH), lambda n: (jnp.maximum(n - 1, 0), 0))],
        out_specs=pl.BlockSpec((Q_BLOCK, ATTN_WIDTH), lambda n: (n, 0)),
        scratch_shapes=[pltpu.VMEM((N_KV_HEADS, 2 * Q_BLOCK, GROUP * Q_BLOCK), F32)],
        compiler_params=_cparams(1),
        name="attn_prompt",
    )(tab_flat, sinks, code_t, q, kv, kv)


S_BB = 8


def _attn_s_kernel(th_ref, code_ref, q_ref, kv_ref, ck_ref, cv_ref, o_ref, kw_ref, vw_ref,
                   bias_ref, kf_ref, vf_ref, qe_ref):
    @pl.when(pl.program_id(0) == 0)
    def _():
        code = code_ref[...]
        acc = jnp.full((DEC_SEQ * N_HEADS, S_KEYS), NEG_INF, F32)
        for bk in range(N_BUCKETS + 1):
            acc = jnp.where(code == bk, th_ref[:, bk:bk + 1], acc)
        bias_ref[...] = acc
        kf_ref[...] = jnp.zeros_like(kf_ref)
        vf_ref[...] = jnp.zeros_like(vf_ref)

    lane_head = lax.broadcasted_iota(jnp.int32, (N_KV_HEADS, KV_WIDTH), 1) // HEAD_DIM
    row_head = lax.broadcasted_iota(jnp.int32, (N_KV_HEADS, KV_WIDTH), 0)
    diag = lane_head == row_head
    rows_all = DEC_SEQ * N_HEADS
    out_mask = (lax.broadcasted_iota(jnp.int32, (rows_all, KV_WIDTH), 0) % N_KV_HEADS
                == lax.broadcasted_iota(jnp.int32, (rows_all, KV_WIDTH), 1) // HEAD_DIM)
    bias = bias_ref[...]

    for b in range(S_BB):
        kf_ref[0:WBUF, :] = ck_ref[b]
        vf_ref[0:WBUF, :] = cv_ref[b]
        kw_ref[b, 0:WBUF - DEC_SEQ, :] = ck_ref[b, DEC_SEQ:WBUF, :]
        vw_ref[b, 0:WBUF - DEC_SEQ, :] = cv_ref[b, DEC_SEQ:WBUF, :]
        for t in range(DEC_SEQ):
            k_new = kv_ref[t, b:b + 1, 0:KV_WIDTH]
            v_new = kv_ref[t, b:b + 1, KV_WIDTH:2 * KV_WIDTH]
            kf_ref[WBUF + t:WBUF + t + 1, :] = k_new
            vf_ref[WBUF + t:WBUF + t + 1, :] = v_new
            kw_ref[b, WBUF - DEC_SEQ + t:WBUF - DEC_SEQ + t + 1, :] = k_new
            vw_ref[b, WBUF - DEC_SEQ + t:WBUF - DEC_SEQ + t + 1, :] = v_new
            for g in range(GROUP):
                q_row = q_ref[t, b:b + 1, g * KV_WIDTH:(g + 1) * KV_WIDTH]
                piece = jnp.where(diag, jnp.broadcast_to(q_row, (N_KV_HEADS, KV_WIDTH)), 0.0)
                r0 = (t * GROUP + g) * N_KV_HEADS
                qe_ref[r0:r0 + N_KV_HEADS, :] = piece
        s = lax.dot_general(qe_ref[...].astype(BF16), kf_ref[...].astype(BF16),
                            (((1,), (1,)), ((), ())), preferred_element_type=F32)
        s = s * ATTN_SCALE + bias
        m = jnp.max(s, axis=-1, keepdims=True)
        p = jnp.exp(s - m)
        l = jnp.sum(p, axis=-1, keepdims=True)
        o = _dot(p.astype(BF16), vf_ref[...].astype(BF16)) / l
        o = jnp.where(out_mask, o, 0.0)
        for t in range(DEC_SEQ):
            for g in range(GROUP):
                r0 = (t * GROUP + g) * N_KV_HEADS
                row = jnp.sum(o[r0:r0 + N_KV_HEADS, :], axis=0, keepdims=True)
                o_ref[t, b:b + 1, g * KV_WIDTH:(g + 1) * KV_WIDTH] = row


def _attn_sample(q, kv, cache_k, cache_v, th):
    code = jnp.asarray(_sample_codes())
    rows_all = DEC_SEQ * N_HEADS
    tb_spec = lambda w: pl.BlockSpec((DEC_SEQ, S_BB, w), lambda i: (0, i, 0))
    cache_spec = pl.BlockSpec((S_BB, WBUF, KV_WIDTH), lambda i: (i, 0, 0))
    cache_shape = jax.ShapeDtypeStruct((DEC_BATCH, WBUF, KV_WIDTH), F32)
    return pl.pallas_call(
        _attn_s_kernel,
        out_shape=(jax.ShapeDtypeStruct((DEC_SEQ, DEC_BATCH, ATTN_WIDTH), F32), cache_shape, cache_shape),
        grid=(DEC_BATCH // S_BB,),
        in_specs=[pl.BlockSpec((rows_all, N_BUCKETS + 1), lambda i: (0, 0)),
                  pl.BlockSpec((rows_all, S_KEYS), lambda i: (0, 0)),
                  tb_spec(ATTN_WIDTH), tb_spec(2 * KV_WIDTH), cache_spec, cache_spec],
        out_specs=(tb_spec(ATTN_WIDTH), cache_spec, cache_spec),
        scratch_shapes=[pltpu.VMEM((rows_all, S_KEYS), F32),
                        pltpu.VMEM((S_KEYS, KV_WIDTH), F32),
                        pltpu.VMEM((S_KEYS, KV_WIDTH), F32),
                        pltpu.VMEM((rows_all, KV_WIDTH), F32)],
        compiler_params=_cparams(1),
        name="attn_sample",
    )(th, code, q, kv, cache_k, cache_v)


SUBLANES = 8
LANES = 128
CONV_RC = 32
CONV_ROWS = 128
CONV_HALO = 32


def _ln_silu(acc, lg, lb):
    mu = jnp.mean(acc, axis=-1, keepdims=True)
    xc = acc - mu
    var = jnp.mean(xc * xc, axis=-1, keepdims=True)
    return jax.nn.silu(xc * lax.rsqrt(var + NORM_EPS) * lg + lb)


def _conv_tile(first, zc_ref, zh_ref, w_ref, b_ref, lg_ref, lb_ref, y_ref, s_ref, c_ref, tc):
    s_ref[0:CONV_HALO, :] = jnp.where(first, 0.0, zh_ref[...])
    s_ref[CONV_HALO:, :] = zc_ref[...]
    off = CONV_HALO - HIST
    groups = [[j for j in range(CONV_WIDTH) if (j + off) % SUBLANES == r] for r in range(SUBLANES)]
    for rc in range(tc // CONV_ROWS):
        t0 = rc * CONV_ROWS
        for lc in range(C_CONV // LANES):
            lanes = slice(lc * LANES, (lc + 1) * LANES)
            out = jnp.broadcast_to(b_ref[:, lanes], (CONV_ROWS, LANES))
            for r, taps in enumerate(groups):
                n_rows = CONV_ROWS + (SUBLANES if r else 0)
                part = None
                for j in taps:
                    base = t0 + (j + off) - r
                    term = w_ref[j:j + 1, lanes] * s_ref[base:base + n_rows, lanes]
                    part = term if part is None else part + term
                out = out + part[r:r + CONV_ROWS, :]
            c_ref[t0:t0 + CONV_ROWS, lanes] = out
    for r in range(tc // CONV_RC):
        rows = slice(r * CONV_RC, (r + 1) * CONV_RC)
        y_ref[rows, :] = _ln_silu(c_ref[rows, :], lg_ref[...], lb_ref[...]).astype(BF16)


def _mix_conv_kernel(zc_ref, zh_ref, w_ref, b_ref, lg_ref, lb_ref,
                     at_ref, ga_ref, gc_ref, h_ref, g2_ref, wao_ref, wco_ref, wout_ref,
                     o_ref, s_ref, c_ref, y_ref, *, tm):
    _conv_tile(pl.program_id(0) == 0, zc_ref, zh_ref, w_ref, b_ref, lg_ref, lb_ref, y_ref, s_ref, c_ref, tm)
    a = _dot(at_ref[...], wao_ref[...])
    c = _dot(y_ref[...], wco_ref[...])
    merged = (ga_ref[...].astype(F32) * a + gc_ref[...].astype(F32) * c).astype(BF16)
    r = _dot(merged, wout_ref[...])
    o_ref[...] = h_ref[...] + g2_ref[0:1, :] * r


def _mix_conv_prompt(z, dw_w, dw_b, ln_g, ln_b, attn, ga, gc, h, mod, w_ao, w_co, w_out, tm=256):
    rows = h.shape[0]
    n = rows // tm
    ratio = tm // CONV_HALO
    vec = pl.BlockSpec((1, C_CONV), lambda s: (0, 0))
    row = lambda w: pl.BlockSpec((tm, w), lambda s: (s, 0))
    resident = lambda shape: pl.BlockSpec(shape, lambda s: (0, 0), pipeline_mode=pl.Buffered(1))
    return pl.pallas_call(
        functools.partial(_mix_conv_kernel, tm=tm),
        out_shape=jax.ShapeDtypeStruct((rows, D_MODEL), F32),
        grid=(n,),
        in_specs=[row(C_CONV),
                  pl.BlockSpec((CONV_HALO, C_CONV), lambda s: (jnp.maximum(s * ratio - 1, 0), 0)),
                  pl.BlockSpec((CONV_WIDTH, C_CONV), lambda s: (0, 0)), vec, vec, vec,
                  row(ATTN_WIDTH), row(D_MODEL), row(D_MODEL), row(D_MODEL),
                  _mod_spec(5, False),
                  resident((ATTN_WIDTH, D_MODEL)), resident((C_CONV, D_MODEL)), resident((D_MODEL, D_MODEL))],
        out_specs=row(D_MODEL),
        scratch_shapes=[pltpu.VMEM((tm + CONV_HALO, C_CONV), F32), pltpu.VMEM((tm, C_CONV), F32),
                        pltpu.VMEM((tm, C_CONV), BF16)],
        compiler_params=_cparams(1),
        name="mix_conv",
    )(z, z, dw_w, dw_b, ln_g, ln_b, attn, ga, gc, h, mod, w_ao, w_co, w_out)


CONV_S_BB = 32


def _conv_s_kernel(z_ref, hist_ref, w_ref, b_ref, lg_ref, lb_ref, y_ref, ns_ref):
    for t in range(DEC_SEQ):
        acc = jnp.broadcast_to(b_ref[...], (CONV_S_BB, C_CONV))
        for j in range(CONV_WIDTH):
            i = t + j
            src = hist_ref[:, i * C_CONV:(i + 1) * C_CONV] if i < HIST else z_ref[i - HIST]
            acc = acc + w_ref[j:j + 1, :] * src
        y_ref[t] = _ln_silu(acc, lg_ref[...], lb_ref[...]).astype(BF16)
    keep = HIST - DEC_SEQ
    ns_ref[:, 0:keep * C_CONV] = hist_ref[:, DEC_SEQ * C_CONV:HIST * C_CONV]
    for t in range(DEC_SEQ):
        ns_ref[:, (keep + t) * C_CONV:(keep + t + 1) * C_CONV] = z_ref[t]


def _conv_sample(z, hist2d, dw_w, dw_b, ln_g, ln_b):
    vec = pl.BlockSpec((1, C_CONV), lambda i: (0, 0))
    tb = pl.BlockSpec((DEC_SEQ, CONV_S_BB, C_CONV), lambda i: (0, i, 0))
    st = pl.BlockSpec((CONV_S_BB, HIST * C_CONV), lambda i: (i, 0))
    return pl.pallas_call(
        _conv_s_kernel,
        out_shape=(jax.ShapeDtypeStruct((DEC_SEQ, DEC_BATCH, C_CONV), BF16),
                   jax.ShapeDtypeStruct((DEC_BATCH, HIST * C_CONV), F32)),
        grid=(DEC_BATCH // CONV_S_BB,),
        in_specs=[tb, st, pl.BlockSpec((CONV_WIDTH, C_CONV), lambda i: (0, 0)), vec, vec, vec],
        out_specs=(tb, st),
        compiler_params=_cparams(1),
        name="conv_sample",
    )(z, hist2d, dw_w, dw_b, ln_g, ln_b)


def _mix_kernel(at_ref, y_ref, ga_ref, gc_ref, h_ref, g2_ref, wao_ref, wco_ref, wout_ref, o_ref,
                *, per_row):
    a = _dot(at_ref[...].astype(BF16), wao_ref[...])
    c = _dot(y_ref[...], wco_ref[...])
    merged = (ga_ref[...].astype(F32) * a + gc_ref[...].astype(F32) * c).astype(BF16)
    r = _dot(merged, wout_ref[...])
    o_ref[...] = h_ref[...] + _mod_val(g2_ref, per_row) * r


def _mix(attn, y, ga, gc, h, mod, w_ao, w_co, w_out, *, per_row, tm):
    rows = h.shape[0]
    row = lambda w: pl.BlockSpec((tm, w), lambda i: (i, 0))
    resident = lambda shape: pl.BlockSpec(shape, lambda i: (0, 0), pipeline_mode=pl.Buffered(1))
    return pl.pallas_call(
        functools.partial(_mix_kernel, per_row=per_row),
        out_shape=jax.ShapeDtypeStruct((rows, D_MODEL), F32),
        grid=(rows // tm,),
        in_specs=[row(ATTN_WIDTH), row(C_CONV), row(D_MODEL), row(D_MODEL), row(D_MODEL),
                  _mod_spec(5, per_row),
                  resident((ATTN_WIDTH, D_MODEL)), resident((C_CONV, D_MODEL)), resident((D_MODEL, D_MODEL))],
        out_specs=row(D_MODEL),
        compiler_params=_cparams(1),
        name="mix",
    )(attn, y, ga, gc, h, mod, w_ao, w_co, w_out)


def _pack_w_in(w):
    wq = w[:, :ATTN_WIDTH].reshape(D_MODEL, N_KV_HEADS, GROUP, HEAD_DIM)
    wq = wq.transpose(0, 2, 1, 3).reshape(D_MODEL, ATTN_WIDTH)
    c0 = ATTN_WIDTH
    wkv = w[:, c0:c0 + 2 * KV_WIDTH]
    c0 += 2 * KV_WIDTH
    wa = w[:, c0:c0 + C_CONV]
    wg = w[:, c0 + C_CONV:c0 + 2 * C_CONV]
    half = C_CONV // 2
    wglu = jnp.concatenate([wa[:, :half], wg[:, :half], wa[:, half:], wg[:, half:]], axis=1)
    wgates = w[:, c0 + 2 * C_CONV:]
    return jnp.concatenate([wq, wkv, wglu, wgates], axis=1).astype(BF16)


def kernel(x_prompt, x_sample, cache_k, cache_v, state_conv, c_prompt, c_sample, rel_bias_table, norm1_g, ffn1_w1, ffn1_w3, ffn1_w2, norm2_g, w_in, attn_sinks, conv_dw_w, conv_dw_b, conv_ln_g, conv_ln_b, w_conv_out, w_attn_out, w_out, norm3_g, ffn2_w1, ffn2_w3, ffn2_w2, w_ada, b_ada, final_norm_g, w_ada_final, b_ada_final):
    c_all = jnp.concatenate([c_sample, c_prompt, jnp.zeros((N_COND - DEC_BATCH - 1, D_MODEL), F32)], axis=0)
    mod = _adaln(c_all, w_ada[0], b_ada)
    mod_f = _adaln(c_all, w_ada_final, b_ada_final[None, :])

    w_pack = _pack_w_in(w_in[0])
    w_ao = w_attn_out[0].reshape(N_KV_HEADS, GROUP, HEAD_DIM, D_MODEL).transpose(1, 0, 2, 3)
    w_ao = w_ao.reshape(ATTN_WIDTH, D_MODEL).astype(BF16)
    w_co = w_conv_out[0].astype(BF16)
    w_o = w_out[0].astype(BF16)

    sinks = attn_sinks[0]
    tab_flat = rel_bias_table.reshape(-1)
    tab_ext = jnp.concatenate([rel_bias_table, sinks[None, :]], axis=0)
    th = tab_ext.T.reshape(N_KV_HEADS, GROUP, N_BUCKETS + 1).transpose(1, 0, 2)
    th = jnp.tile(th.reshape(N_HEADS, N_BUCKETS + 1), (DEC_SEQ, 1))

    xp = x_prompt[0]
    xs = x_sample.transpose(1, 0, 2).reshape(S_ROWS, D_MODEL)

    conv_w = (conv_dw_w[0], conv_dw_b, conv_ln_g, conv_ln_b)

    final = (final_norm_g[None, :], mod_f)

    def mixers_p(q, kv, z, ga, gc, h):
        attn = _attn_prompt(q, kv, tab_flat, sinks)
        return _mix_conv_prompt(z, *conv_w, attn, ga, gc, h, mod, w_ao, w_co, w_o), None

    def mixers_s(q, kv, z, ga, gc, h):
        o, kw, vw = _attn_sample(q.reshape(DEC_SEQ, DEC_BATCH, ATTN_WIDTH),
                                 kv.reshape(DEC_SEQ, DEC_BATCH, 2 * KV_WIDTH),
                                 cache_k.reshape(DEC_BATCH, WBUF, KV_WIDTH),
                                 cache_v.reshape(DEC_BATCH, WBUF, KV_WIDTH), th)
        y, ns = _conv_sample(z.reshape(DEC_SEQ, DEC_BATCH, C_CONV),
                             state_conv.reshape(DEC_BATCH, HIST * C_CONV), *conv_w)
        h = _mix(o.reshape(S_ROWS, ATTN_WIDTH), y.reshape(S_ROWS, C_CONV), ga, gc, h, mod,
                 w_ao, w_co, w_o, per_row=True, tm=DEC_BATCH)
        return h, (kw, vw, ns)

    hs, f1 = _ffn(xs, norm1_g, mod, 0, ffn1_w1[0], ffn1_w3[0], ffn1_w2[0], per_row=True,
                  tm=S_ROWS, tf=FFN_TF_CAST, cast_out=True)
    hp = _ffn(xp, norm1_g, mod, 0, *f1, per_row=False, tm=FFN_TM, tf=FFN_TF)

    qs, kv_s, z_s, ga_s, gc_s = _proj(hs, norm2_g, mod, w_pack, per_row=True, tm=S_ROWS, q_dtype=F32)
    qp, kv_p, z_p, ga_p, gc_p = _proj(hp, norm2_g, mod, w_pack, per_row=False, tm=PROJ_TM, q_dtype=BF16)

    hs, (kw, vw, ns) = mixers_s(qs, kv_s, z_s, ga_s, gc_s, hs)
    hp, _ = mixers_p(qp, kv_p, z_p, ga_p, gc_p, hp)

    ys, f2 = _ffn(hs, norm3_g, mod, 6, ffn2_w1[0], ffn2_w3[0], ffn2_w2[0], per_row=True,
                  tm=S_ROWS, tf=FFN_TF_CAST, final=final, cast_out=True)
    yp = _ffn(hp, norm3_g, mod, 6, *f2, per_row=False, tm=FFN_TM, tf=FFN_TF, final=final)

    w = min(WINDOW, SEQ)
    kv_shape = (1, 1, w, N_KV_HEADS, HEAD_DIM)
    k_win_p = kv_p[SEQ - w:, :KV_WIDTH].reshape(kv_shape)
    v_win_p = kv_p[SEQ - w:, KV_WIDTH:].reshape(kv_shape)
    conv_p_state = z_p[SEQ - HIST:].reshape(1, 1, HIST, C_CONV)
    s_shape = (1, DEC_BATCH, WBUF, N_KV_HEADS, HEAD_DIM)
    y_prompt = yp[None]
    y_sample = ys.reshape(DEC_SEQ, DEC_BATCH, D_MODEL).transpose(1, 0, 2)
    return (y_prompt, y_sample, k_win_p, v_win_p, conv_p_state,
            kw.reshape(s_shape), vw.reshape(s_shape), ns.reshape(1, DEC_BATCH, HIST, C_CONV))
```

```python
import functools

import numpy as np
import jax
import jax.numpy as jnp
from jax import lax
from jax.experimental import pallas as pl
from jax.experimental.pallas import tpu as pltpu

D_MODEL = 2048
SEQ = 8192
DEC_BATCH = 128
DEC_SEQ = 4
PAST_LEN = 16384
N_HEADS = 32
N_KV_HEADS = 8
HEAD_DIM = 64
GROUP = N_HEADS // N_KV_HEADS
ATTN_WIDTH = N_HEADS * HEAD_DIM
KV_WIDTH = N_KV_HEADS * HEAD_DIM
WINDOW = 128
Q_BLOCK = 128
ATTN_SCALE = HEAD_DIM ** -0.5
N_BUCKETS = 32
MAX_DISTANCE = 128
C_CONV = D_MODEL // 2
CONV_WIDTH = 31
HIST = CONV_WIDTH - 1
D_FF = 5632
NORM_EPS = 1e-6
NEG_INF = -1e30
IN_WIDTH = ATTN_WIDTH + 2 * KV_WIDTH + 2 * C_CONV + 2 * D_MODEL
WBUF = min(WINDOW, PAST_LEN)

S_ROWS = DEC_BATCH * DEC_SEQ
N_COND = 136
PROMPT_MOD_BLOCK = DEC_BATCH // 8
S_KEYS = 136
SINK_COL = WBUF + DEC_SEQ

VMEM_LIMIT = 60 * 1024 * 1024
SUBLANES = 8
LANES = 128

F32 = jnp.float32
BF16 = jnp.bfloat16


def _cparams(n_axes, flags=None):
    return pltpu.CompilerParams(dimension_semantics=("arbitrary",) * n_axes,
                                vmem_limit_bytes=VMEM_LIMIT, flags=flags)


def _dot(a, b):
    return jnp.dot(a, b, preferred_element_type=F32)


def _rms_mod(x, g, sh, sc):
    ms = jnp.mean(x * x, axis=-1, keepdims=True)
    y = x * lax.rsqrt(ms + NORM_EPS) * g
    return y * (1.0 + sc) + sh


def _mod_val(ref, per_row):
    return ref[...] if per_row else ref[0:1, :]


def _mod_spec(chunk, per_row, width=D_MODEL):
    if per_row:
        return pl.BlockSpec((DEC_BATCH, width), lambda *_: (0, chunk))
    return pl.BlockSpec((8, width), lambda *_: (PROMPT_MOD_BLOCK, chunk))


def _adaln_kernel(c_ref, w_ref, b_ref, o_ref, s_ref):
    @pl.when(pl.program_id(0) == 0)
    def _():
        s_ref[...] = jax.nn.silu(c_ref[...]).astype(BF16)

    o_ref[...] = _dot(s_ref[...], w_ref[...].astype(BF16)) + b_ref[...]


def _adaln(c_all, w, b, tn=1024):
    n = w.shape[1]
    return pl.pallas_call(
        _adaln_kernel,
        out_shape=jax.ShapeDtypeStruct((N_COND, n), F32),
        grid=(n // tn,),
        in_specs=[pl.BlockSpec((N_COND, D_MODEL), lambda j: (0, 0)),
                  pl.BlockSpec((D_MODEL, tn), lambda j: (0, j)),
                  pl.BlockSpec((1, tn), lambda j: (0, j))],
        out_specs=pl.BlockSpec((N_COND, tn), lambda j: (0, j)),
        scratch_shapes=[pltpu.VMEM((N_COND, D_MODEL), BF16)],
        compiler_params=_cparams(1),
        name="adaln",
    )(c_all, w, b)


FFN_ROWS = 512
FFN_TM = 1024
FFN_TF = 512
FFN_TF_CAST = 256
PROJ_TM = 1024


def _ffn_kernel(*refs, tm, sub, per_row, n_f, final, cast_out):
    refs = list(refs)
    x_ref, n_ref, sh_ref, sc_ref, g_ref, w1_ref, w3_ref, w2_ref = refs[:8]
    del refs[:8]
    if final:
        gf_ref, shf_ref, scf_ref = refs[:3]
        del refs[:3]
    o_ref = refs.pop(0)
    if cast_out:
        w1o_ref, w3o_ref, w2o_ref = refs[:3]
        del refs[:3]
    (u_ref,) = refs
    f = pl.program_id(1)

    @pl.when(f == 0)
    def _():
        def body(s, carry):
            rows = pl.ds(pl.multiple_of(s * sub, sub), sub)
            u_ref[rows, :] = _rms_mod(x_ref[rows, :], n_ref[...], _mod_val(sh_ref, per_row),
                                      _mod_val(sc_ref, per_row)).astype(BF16)
            o_ref[rows, :] = jnp.zeros((sub, D_MODEL), F32)
            return carry
        lax.fori_loop(0, tm // sub, body, 0)

    w1, w3, w2 = w1_ref[...], w3_ref[...], w2_ref[...]
    if cast_out:
        w1, w3, w2 = w1.astype(BF16), w3.astype(BF16), w2.astype(BF16)
        w1o_ref[...], w3o_ref[...], w2o_ref[...] = w1, w3, w2
    for s in range(tm // FFN_ROWS):
        rows = slice(s * FFN_ROWS, (s + 1) * FFN_ROWS)
        u = u_ref[rows, :]
        h1 = _dot(u, w1)
        h3 = _dot(u, w3)
        a = (jax.nn.silu(h1) * h3).astype(BF16)
        o_ref[rows, :] += _dot(a, w2)

    @pl.when(f == n_f - 1)
    def _():
        def body(s, carry):
            rows = pl.ds(pl.multiple_of(s * sub, sub), sub)
            h = x_ref[rows, :] + 0.5 * _mod_val(g_ref, per_row) * o_ref[rows, :]
            if final:
                h = _rms_mod(h, gf_ref[...], _mod_val(shf_ref, per_row), _mod_val(scf_ref, per_row))
            o_ref[rows, :] = h
            return carry
        lax.fori_loop(0, tm // sub, body, 0)


def _ffn(x, norm_g, mod, chunk0, w1, w3, w2, *, per_row, tm, tf, final=None, cast_out=False):
    rows = x.shape[0]
    n_f = D_FF // tf
    sub = DEC_BATCH
    row_spec = pl.BlockSpec((tm, D_MODEL), lambda i, f: (i, 0))
    vec_spec = pl.BlockSpec((1, D_MODEL), lambda i, f: (0, 0))
    w_specs = [pl.BlockSpec((D_MODEL, tf), lambda i, f: (0, f)),
               pl.BlockSpec((D_MODEL, tf), lambda i, f: (0, f)),
               pl.BlockSpec((tf, D_MODEL), lambda i, f: (f, 0))]
    in_specs = [row_spec, vec_spec,
                _mod_spec(chunk0, per_row), _mod_spec(chunk0 + 1, per_row), _mod_spec(chunk0 + 2, per_row)]
    in_specs += w_specs
    args = [x, norm_g, mod, mod, mod, w1, w3, w2]
    if final is not None:
        gf, mod_f = final
        in_specs += [vec_spec, _mod_spec(0, per_row), _mod_spec(1, per_row)]
        args += [gf, mod_f, mod_f]
    out_shape = [jax.ShapeDtypeStruct((rows, D_MODEL), F32)]
    out_specs = [row_spec]
    if cast_out:
        assert rows == tm, "every weight block must be visited exactly once"
        out_shape += [jax.ShapeDtypeStruct(w.shape, BF16) for w in (w1, w3, w2)]
        out_specs += w_specs
    outs = pl.pallas_call(
        functools.partial(_ffn_kernel, tm=tm, sub=sub, per_row=per_row, n_f=n_f,
                          final=final is not None, cast_out=cast_out),
        out_shape=out_shape,
        grid=(rows // tm, n_f),
        in_specs=in_specs,
        out_specs=out_specs,
        scratch_shapes=[pltpu.VMEM((tm, D_MODEL), BF16)],
        compiler_params=_cparams(2),
        name="ffn_final" if final is not None else "ffn",
    )(*args)
    return (outs[0], tuple(outs[1:])) if cast_out else outs[0]


PROJ_CHUNK = 256
PROJ_CW = 512


def _proj_layout():
    cw = PROJ_CW
    n_q = ATTN_WIDTH // (2 * cw)
    n_kv = 2 * KV_WIDTH // (2 * cw)
    n_glu = C_CONV // cw
    n_gate = D_MODEL // (2 * cw)
    pre = n_q + n_kv
    steps = pre + n_glu + 2 * n_gate

    def block_a(j):
        return jnp.where((j >= pre) & (j < pre + n_glu), j + pre, 2 * j)

    def block_b(j):
        return jnp.where((j >= pre) & (j < pre + n_glu), j + pre + n_glu, 2 * j + 1)

    return n_q, n_kv, n_glu, n_gate, steps, block_a, block_b


def _proj_kernel(*refs, tm, sub, per_row, cast_out):
    refs = list(refs)
    h_ref, n_ref, sh_ref, sc_ref, wa_ref, wb_ref, q_ref, kv_ref, z_ref, ga_ref, gc_ref = refs[:11]
    del refs[:11]
    if cast_out:
        wao_ref, wbo_ref = refs[:2]
        del refs[:2]
    (u_ref,) = refs
    cw = PROJ_CW
    n_q, n_kv, n_glu, n_gate, _, _, _ = _proj_layout()
    j = pl.program_id(1)

    @pl.when(j == 0)
    def _():
        def body(s, carry):
            rows = pl.ds(pl.multiple_of(s * sub, sub), sub)
            u_ref[rows, :] = _rms_mod(h_ref[rows, :], n_ref[...], _mod_val(sh_ref, per_row),
                                      _mod_val(sc_ref, per_row)).astype(BF16)
            return carry
        lax.fori_loop(0, tm // sub, body, 0)

    if cast_out:
        wao_ref[...] = wa_ref[...].astype(BF16)
        wbo_ref[...] = wb_ref[...].astype(BF16)
        wa_ref, wb_ref = wao_ref, wbo_ref

    n_chunks = cw // PROJ_CHUNK

    def chunks(w_ref):
        for c in range(n_chunks):
            cols = slice(c * PROJ_CHUNK, (c + 1) * PROJ_CHUNK)
            yield c * PROJ_CHUNK, _dot(u_ref[...], w_ref[:, cols])

    def pair_store(o_ref, fn):
        for half, w_ref in enumerate((wa_ref, wb_ref)):
            for c0, acc in chunks(w_ref):
                o_ref[:, half * cw + c0:half * cw + c0 + PROJ_CHUNK] = fn(acc).astype(o_ref.dtype)

    lo = 0

    @pl.when(j < n_q)
    def _():
        pair_store(q_ref, lambda acc: acc)

    lo += n_q

    @pl.when((j >= lo) & (j < lo + n_kv))
    def _():
        pair_store(kv_ref, lambda acc: acc)

    lo += n_kv

    @pl.when((j >= lo) & (j < lo + n_glu))
    def _():
        for (c0, a), (_, g) in zip(chunks(wa_ref), chunks(wb_ref)):
            z_ref[:, c0:c0 + PROJ_CHUNK] = a * jax.nn.sigmoid(g)

    lo += n_glu

    @pl.when((j >= lo) & (j < lo + n_gate))
    def _():
        pair_store(ga_ref, jax.nn.sigmoid)

    lo += n_gate

    @pl.when(j >= lo)
    def _():
        pair_store(gc_ref, jax.nn.sigmoid)


def _proj(h, norm_g, mod, w_a, w_b, *, per_row, tm, q_dtype, cast_out=False):
    rows = h.shape[0]
    sub = DEC_BATCH
    cw = PROJ_CW
    n_q, n_kv, n_glu, n_gate, steps, block_a, block_b = _proj_layout()

    def out_spec(width, first, count):
        return pl.BlockSpec((tm, width), lambda i, j: (i, jnp.clip(j - first, 0, count - 1)))

    step_spec = pl.BlockSpec((D_MODEL, cw), lambda i, j: (0, j))
    if cast_out:
        wa_spec = pl.BlockSpec((D_MODEL, cw), lambda i, j: (0, block_a(j)))
        wb_spec = pl.BlockSpec((D_MODEL, cw), lambda i, j: (0, block_b(j)))
    else:
        wa_spec = wb_spec = step_spec
    out_shape = [jax.ShapeDtypeStruct((rows, ATTN_WIDTH), q_dtype),
                 jax.ShapeDtypeStruct((rows, 2 * KV_WIDTH), F32),
                 jax.ShapeDtypeStruct((rows, C_CONV), F32),
                 jax.ShapeDtypeStruct((rows, D_MODEL), BF16),
                 jax.ShapeDtypeStruct((rows, D_MODEL), BF16)]
    out_specs = [out_spec(2 * cw, 0, n_q),
                 out_spec(2 * cw, n_q, n_kv),
                 out_spec(cw, n_q + n_kv, n_glu),
                 out_spec(2 * cw, n_q + n_kv + n_glu, n_gate),
                 out_spec(2 * cw, n_q + n_kv + n_glu + n_gate, n_gate)]
    if cast_out:
        assert rows == tm, "every weight block must be visited exactly once"
        out_shape += [jax.ShapeDtypeStruct((D_MODEL, steps * cw), BF16)] * 2
        out_specs += [step_spec, step_spec]
    return pl.pallas_call(
        functools.partial(_proj_kernel, tm=tm, sub=sub, per_row=per_row, cast_out=cast_out),
        out_shape=out_shape,
        grid=(rows // tm, steps),
        in_specs=[pl.BlockSpec((tm, D_MODEL), lambda i, j: (i, 0)),
                  pl.BlockSpec((1, D_MODEL), lambda i, j: (0, 0)),
                  _mod_spec(3, per_row), _mod_spec(4, per_row), wa_spec, wb_spec],
        out_specs=out_specs,
        scratch_shapes=[pltpu.VMEM((tm, D_MODEL), BF16)],
        compiler_params=_cparams(2),
        name="proj",
    )(h, norm_g, mod, mod, w_a, w_b)


def _t5_bucket_np(dist):
    exact = N_BUCKETS // 2
    d = np.maximum(dist, 0)
    df = np.maximum(d, 1).astype(np.float32)
    large = exact + (np.log(df / np.float32(exact)) / np.float32(np.log(MAX_DISTANCE / exact))
                     * np.float32(N_BUCKETS - exact)).astype(np.int32)
    large = np.minimum(large, N_BUCKETS - 1)
    return np.where(d < exact, d, large).astype(np.int32)


def _prompt_codes():
    qi = np.arange(Q_BLOCK)[:, None]
    kj = np.arange(2 * Q_BLOCK)[None, :]
    dist = qi + Q_BLOCK - kj
    valid = (dist >= 0) & (dist <= WINDOW)
    return np.where(valid, _t5_bucket_np(dist), -1).astype(np.int32)


def _sample_codes():
    t = np.repeat(np.arange(DEC_SEQ), GROUP * N_KV_HEADS)[:, None]
    j = np.arange(S_KEYS)[None, :]
    dist = t + WBUF - j
    valid = (dist >= 0) & (dist <= WINDOW) & (j < WBUF + DEC_SEQ)
    code = np.where(valid, _t5_bucket_np(dist), -1)
    code = np.where(j == SINK_COL, N_BUCKETS, code)
    return code.astype(np.int32)


def _attn_p_kernel(tab_ref, sink_ref, code_ref, q_ref, kvc_ref, kvp_ref, o_ref, bias_ref):
    n = pl.program_id(0)
    n_keys = 2 * Q_BLOCK
    wide = GROUP * Q_BLOCK

    @pl.when(n <= 1)
    def _():
        key = lax.broadcasted_iota(jnp.int32, (n_keys, Q_BLOCK), 0)
        code = jnp.where((n == 0) & (key < Q_BLOCK), -1, code_ref[...])

        def kv_body(kv, carry):
            for g in range(GROUP):
                h = kv * GROUP + g

                def bucket_body(bk, acc):
                    return jnp.where(code == bk, tab_ref[bk * N_HEADS + h], acc)
                bias_ref[kv, :, g * Q_BLOCK:(g + 1) * Q_BLOCK] = lax.fori_loop(
                    0, N_BUCKETS, bucket_body, jnp.full((n_keys, Q_BLOCK), NEG_INF, F32))
            return carry
        lax.fori_loop(0, N_KV_HEADS, kv_body, 0)

    q_t = q_ref[...].astype(F32).T
    ones = jnp.ones((16, n_keys), BF16)
    zeros = jnp.zeros((HEAD_DIM, wide), BF16)
    def pair_kv(pair):
        lanes = slice(pair * LANES, (pair + 1) * LANES)
        v_lanes = slice(KV_WIDTH + pair * LANES, KV_WIDTH + (pair + 1) * LANES)
        k2 = jnp.concatenate([kvp_ref[:, lanes], kvc_ref[:, lanes]], axis=0).astype(BF16)
        v2_t = jnp.concatenate([kvp_ref[:, v_lanes], kvc_ref[:, v_lanes]], axis=0).T.astype(BF16)
        return k2, v2_t

    def scores(kv, k2):
        q4 = jnp.concatenate(
            [q_t[(kv * GROUP + g) * HEAD_DIM:(kv * GROUP + g + 1) * HEAD_DIM, :] for g in range(GROUP)],
            axis=1)
        q4 = (q4 * ATTN_SCALE).astype(BF16)
        q4 = jnp.concatenate([q4, zeros] if kv % 2 == 0 else [zeros, q4], axis=0)
        return _dot(k2, q4)

    kvs = [pair_kv(pair) for pair in range(N_KV_HEADS // 2)]
    s_next = scores(0, kvs[0][0])
    for kv in range(N_KV_HEADS):
        pair, parity = divmod(kv, 2)
        s = s_next
        if kv + 1 < N_KV_HEADS:
            s_next = scores(kv + 1, kvs[(kv + 1) // 2][0])
        s = s + bias_ref[kv]
        sink = jnp.concatenate(
            [jnp.full((1, Q_BLOCK), sink_ref[kv * GROUP + g], F32) for g in range(GROUP)], axis=1)
        m = jnp.maximum(jnp.max(s, axis=0, keepdims=True), sink)
        p = jnp.exp(s - m).astype(BF16)
        lhs = jnp.concatenate([kvs[pair][1][parity * HEAD_DIM:(parity + 1) * HEAD_DIM, :], ones], axis=0)
        o_t = _dot(lhs, p)
        l = o_t[HEAD_DIM:HEAD_DIM + 1, :] + jnp.exp(sink - m)
        o_n = o_t[0:HEAD_DIM, :] * (1.0 / l)
        o_kv = jnp.concatenate([o_n[:, g * Q_BLOCK:(g + 1) * Q_BLOCK] for g in range(GROUP)], axis=0)
        o_ref[:, kv * GROUP * HEAD_DIM:(kv + 1) * GROUP * HEAD_DIM] = o_kv.T.astype(o_ref.dtype)


def _attn_prompt(q, kv, tab_flat, sinks):
    nb = SEQ // Q_BLOCK
    code_t = jnp.asarray(np.ascontiguousarray(_prompt_codes().T))
    smem = pl.BlockSpec(memory_space=pltpu.SMEM)
    return pl.pallas_call(
        _attn_p_kernel,
        out_shape=jax.ShapeDtypeStruct((SEQ, ATTN_WIDTH), BF16),
        grid=(nb,),
        in_specs=[smem, smem,
                  pl.BlockSpec((2 * Q_BLOCK, Q_BLOCK), lambda n: (0, 0)),
                  pl.BlockSpec((Q_BLOCK, ATTN_WIDTH), lambda n: (n, 0)),
                  pl.BlockSpec((Q_BLOCK, 2 * KV_WIDTH), lambda n: (n, 0)),
                  pl.BlockSpec((Q_BLOCK, 2 * KV_WIDTH), lambda n: (jnp.maximum(n - 1, 0), 0))],
        out_specs=pl.BlockSpec((Q_BLOCK, ATTN_WIDTH), lambda n: (n, 0)),
        scratch_shapes=[pltpu.VMEM((N_KV_HEADS, 2 * Q_BLOCK, GROUP * Q_BLOCK), F32)],
        compiler_params=_cparams(1),
        name="attn_prompt",
    )(tab_flat, sinks, code_t, q, kv, kv)


S_BB = 8


def _regroup_heads(x, to_group_major):
    half = HEAD_DIM
    assert LANES == 2 * half
    low = lax.broadcasted_iota(jnp.int32, (x.shape[0], LANES), 1) < half
    n_cols = ATTN_WIDTH // LANES
    outer, inner = (GROUP, N_KV_HEADS) if to_group_major else (N_KV_HEADS, GROUP)
    cols = []
    for c in range(n_cols):
        o, i = divmod(2 * c, inner)
        src = [(i + d) * outer + o for d in range(2)]
        a = x[:, (src[0] // 2) * LANES:(src[0] // 2 + 1) * LANES]
        b = x[:, (src[1] // 2) * LANES:(src[1] // 2 + 1) * LANES]
        assert src[0] % 2 == src[1] % 2
        if src[0] % 2 == 0:
            cols.append(jnp.where(low, a, pltpu.roll(b, half, axis=1)))
        else:
            cols.append(jnp.where(low, pltpu.roll(a, half, axis=1), b))
    return jnp.concatenate(cols, axis=1)


def _attn_s_kernel(th_ref, code_ref, q_ref, kv_ref, ck_ref, cv_ref, o_ref, kw_ref, vw_ref,
                   bias_ref, kf_ref, vf_ref, qe_ref, qg_ref, og_ref):
    for t in range(DEC_SEQ):
        qg_ref[t] = _regroup_heads(q_ref[t], True)

    @pl.when(pl.program_id(0) == 0)
    def _():
        code = code_ref[...]
        acc = jnp.full((DEC_SEQ * N_HEADS, S_KEYS), NEG_INF, F32)
        for bk in range(N_BUCKETS + 1):
            acc = jnp.where(code == bk, th_ref[:, bk:bk + 1], acc)
        bias_ref[...] = acc
        kf_ref[...] = jnp.zeros_like(kf_ref)
        vf_ref[...] = jnp.zeros_like(vf_ref)

    lane_head = lax.broadcasted_iota(jnp.int32, (N_KV_HEADS, KV_WIDTH), 1) // HEAD_DIM
    row_head = lax.broadcasted_iota(jnp.int32, (N_KV_HEADS, KV_WIDTH), 0)
    diag = lane_head == row_head
    rows_all = DEC_SEQ * N_HEADS
    out_mask = (lax.broadcasted_iota(jnp.int32, (rows_all, KV_WIDTH), 0) % N_KV_HEADS
                == lax.broadcasted_iota(jnp.int32, (rows_all, KV_WIDTH), 1) // HEAD_DIM)
    bias = bias_ref[...]

    for b in range(S_BB):
        kf_ref[0:WBUF, :] = ck_ref[b]
        vf_ref[0:WBUF, :] = cv_ref[b]
        kw_ref[b, 0:WBUF - DEC_SEQ, :] = ck_ref[b, DEC_SEQ:WBUF, :]
        vw_ref[b, 0:WBUF - DEC_SEQ, :] = cv_ref[b, DEC_SEQ:WBUF, :]
        for t in range(DEC_SEQ):
            k_new = kv_ref[t, b:b + 1, 0:KV_WIDTH]
            v_new = kv_ref[t, b:b + 1, KV_WIDTH:2 * KV_WIDTH]
            kf_ref[WBUF + t:WBUF + t + 1, :] = k_new
            vf_ref[WBUF + t:WBUF + t + 1, :] = v_new
            kw_ref[b, WBUF - DEC_SEQ + t:WBUF - DEC_SEQ + t + 1, :] = k_new
            vw_ref[b, WBUF - DEC_SEQ + t:WBUF - DEC_SEQ + t + 1, :] = v_new
            for g in range(GROUP):
                q_row = qg_ref[t, b:b + 1, g * KV_WIDTH:(g + 1) * KV_WIDTH]
                piece = jnp.where(diag, jnp.broadcast_to(q_row, (N_KV_HEADS, KV_WIDTH)), 0.0)
                r0 = (t * GROUP + g) * N_KV_HEADS
                qe_ref[r0:r0 + N_KV_HEADS, :] = piece
        s = lax.dot_general(qe_ref[...].astype(BF16), kf_ref[...].astype(BF16),
                            (((1,), (1,)), ((), ())), preferred_element_type=F32)
        s = s * ATTN_SCALE + bias
        m = jnp.max(s, axis=-1, keepdims=True)
        p = jnp.exp(s - m)
        l = jnp.sum(p, axis=-1, keepdims=True)
        o = _dot(p.astype(BF16), vf_ref[...].astype(BF16)) / l
        o = jnp.where(out_mask, o, 0.0)
        for t in range(DEC_SEQ):
            for g in range(GROUP):
                r0 = (t * GROUP + g) * N_KV_HEADS
                row = jnp.sum(o[r0:r0 + N_KV_HEADS, :], axis=0, keepdims=True)
                og_ref[t, b:b + 1, g * KV_WIDTH:(g + 1) * KV_WIDTH] = row
    for t in range(DEC_SEQ):
        o_ref[t] = _regroup_heads(og_ref[t], False)


def _attn_sample(q, kv, cache_k, cache_v, th):
    code = jnp.asarray(_sample_codes())
    rows_all = DEC_SEQ * N_HEADS
    tb_spec = lambda w: pl.BlockSpec((DEC_SEQ, S_BB, w), lambda i: (0, i, 0))
    cache_spec = pl.BlockSpec((S_BB, WBUF, KV_WIDTH), lambda i: (i, 0, 0))
    cache_shape = jax.ShapeDtypeStruct((DEC_BATCH, WBUF, KV_WIDTH), F32)
    return pl.pallas_call(
        _attn_s_kernel,
        out_shape=(jax.ShapeDtypeStruct((DEC_SEQ, DEC_BATCH, ATTN_WIDTH), F32), cache_shape, cache_shape),
        grid=(DEC_BATCH // S_BB,),
        in_specs=[pl.BlockSpec((rows_all, N_BUCKETS + 1), lambda i: (0, 0)),
                  pl.BlockSpec((rows_all, S_KEYS), lambda i: (0, 0)),
                  tb_spec(ATTN_WIDTH), tb_spec(2 * KV_WIDTH), cache_spec, cache_spec],
        out_specs=(tb_spec(ATTN_WIDTH), cache_spec, cache_spec),
        scratch_shapes=[pltpu.VMEM((rows_all, S_KEYS), F32),
                        pltpu.VMEM((S_KEYS, KV_WIDTH), F32),
                        pltpu.VMEM((S_KEYS, KV_WIDTH), F32),
                        pltpu.VMEM((rows_all, KV_WIDTH), F32),
                        pltpu.VMEM((DEC_SEQ, S_BB, ATTN_WIDTH), F32),
                        pltpu.VMEM((DEC_SEQ, S_BB, ATTN_WIDTH), F32)],
        compiler_params=_cparams(1),
        name="attn_sample",
    )(th, code, q, kv, cache_k, cache_v)


CONV_RC = 32
CONV_ROWS = 128
CONV_HALO = 32


def _ln_silu(acc, lg, lb):
    mu = jnp.mean(acc, axis=-1, keepdims=True)
    xc = acc - mu
    var = jnp.mean(xc * xc, axis=-1, keepdims=True)
    return jax.nn.silu(xc * lax.rsqrt(var + NORM_EPS) * lg + lb)


def _conv_tile(first, zc_ref, zh_ref, w_ref, b_ref, lg_ref, lb_ref, y_ref, s_ref, c_ref, tc):
    s_ref[0:CONV_HALO, :] = jnp.where(first, 0.0, zh_ref[...])
    s_ref[CONV_HALO:, :] = zc_ref[...]
    off = CONV_HALO - HIST
    groups = [[j for j in range(CONV_WIDTH) if (j + off) % SUBLANES == r] for r in range(SUBLANES)]
    for rc in range(tc // CONV_ROWS):
        t0 = rc * CONV_ROWS
        for lc in range(C_CONV // LANES):
            lanes = slice(lc * LANES, (lc + 1) * LANES)
            out = jnp.broadcast_to(b_ref[:, lanes], (CONV_ROWS, LANES))
            for r, taps in enumerate(groups):
                n_rows = CONV_ROWS + (SUBLANES if r else 0)
                part = None
                for j in taps:
                    base = t0 + (j + off) - r
                    term = w_ref[j:j + 1, lanes] * s_ref[base:base + n_rows, lanes]
                    part = term if part is None else part + term
                out = out + part[r:r + CONV_ROWS, :]
            c_ref[t0:t0 + CONV_ROWS, lanes] = out
    for r in range(tc // CONV_RC):
        rows = slice(r * CONV_RC, (r + 1) * CONV_RC)
        y_ref[rows, :] = _ln_silu(c_ref[rows, :], lg_ref[...], lb_ref[...]).astype(BF16)


def _mix_conv_kernel(zc_ref, zh_ref, w_ref, b_ref, lg_ref, lb_ref,
                     at_ref, ga_ref, gc_ref, h_ref, g2_ref, wao_ref, wco_ref, wout_ref,
                     o_ref, s_ref, c_ref, y_ref, *, tm):
    _conv_tile(pl.program_id(0) == 0, zc_ref, zh_ref, w_ref, b_ref, lg_ref, lb_ref, y_ref, s_ref, c_ref, tm)
    a = _dot(at_ref[...], wao_ref[...])
    c = _dot(y_ref[...], wco_ref[...])
    merged = (ga_ref[...].astype(F32) * a + gc_ref[...].astype(F32) * c).astype(BF16)
    r = _dot(merged, wout_ref[...])
    o_ref[...] = h_ref[...] + g2_ref[0:1, :] * r


def _mix_conv_prompt(z, dw_w, dw_b, ln_g, ln_b, attn, ga, gc, h, mod, w_ao, w_co, w_out, tm=256):
    rows = h.shape[0]
    n = rows // tm
    ratio = tm // CONV_HALO
    vec = pl.BlockSpec((1, C_CONV), lambda s: (0, 0))
    row = lambda w: pl.BlockSpec((tm, w), lambda s: (s, 0))
    resident = lambda shape: pl.BlockSpec(shape, lambda s: (0, 0), pipeline_mode=pl.Buffered(1))
    return pl.pallas_call(
        functools.partial(_mix_conv_kernel, tm=tm),
        out_shape=jax.ShapeDtypeStruct((rows, D_MODEL), F32),
        grid=(n,),
        in_specs=[row(C_CONV),
                  pl.BlockSpec((CONV_HALO, C_CONV), lambda s: (jnp.maximum(s * ratio - 1, 0), 0)),
                  pl.BlockSpec((CONV_WIDTH, C_CONV), lambda s: (0, 0)), vec, vec, vec,
                  row(ATTN_WIDTH), row(D_MODEL), row(D_MODEL), row(D_MODEL),
                  _mod_spec(5, False),
                  resident((ATTN_WIDTH, D_MODEL)), resident((C_CONV, D_MODEL)), resident((D_MODEL, D_MODEL))],
        out_specs=row(D_MODEL),
        scratch_shapes=[pltpu.VMEM((tm + CONV_HALO, C_CONV), F32), pltpu.VMEM((tm, C_CONV), F32),
                        pltpu.VMEM((tm, C_CONV), BF16)],
        compiler_params=_cparams(1),
        name="mix_conv",
    )(z, z, dw_w, dw_b, ln_g, ln_b, attn, ga, gc, h, mod, w_ao, w_co, w_out)


CONV_S_BB = 32


def _conv_s_kernel(z_ref, hist_ref, w_ref, b_ref, lg_ref, lb_ref, y_ref, ns_ref):
    for t in range(DEC_SEQ):
        acc = jnp.broadcast_to(b_ref[...], (CONV_S_BB, C_CONV))
        for j in range(CONV_WIDTH):
            i = t + j
            src = hist_ref[:, i * C_CONV:(i + 1) * C_CONV] if i < HIST else z_ref[i - HIST]
            acc = acc + w_ref[j:j + 1, :] * src
        y_ref[t] = _ln_silu(acc, lg_ref[...], lb_ref[...]).astype(BF16)
    keep = HIST - DEC_SEQ
    ns_ref[:, 0:keep * C_CONV] = hist_ref[:, DEC_SEQ * C_CONV:HIST * C_CONV]
    for t in range(DEC_SEQ):
        ns_ref[:, (keep + t) * C_CONV:(keep + t + 1) * C_CONV] = z_ref[t]


def _conv_sample(z, hist2d, dw_w, dw_b, ln_g, ln_b):
    vec = pl.BlockSpec((1, C_CONV), lambda i: (0, 0))
    tb = pl.BlockSpec((DEC_SEQ, CONV_S_BB, C_CONV), lambda i: (0, i, 0))
    st = pl.BlockSpec((CONV_S_BB, HIST * C_CONV), lambda i: (i, 0))
    return pl.pallas_call(
        _conv_s_kernel,
        out_shape=(jax.ShapeDtypeStruct((DEC_SEQ, DEC_BATCH, C_CONV), BF16),
                   jax.ShapeDtypeStruct((DEC_BATCH, HIST * C_CONV), F32)),
        grid=(DEC_BATCH // CONV_S_BB,),
        in_specs=[tb, st, pl.BlockSpec((CONV_WIDTH, C_CONV), lambda i: (0, 0)), vec, vec, vec],
        out_specs=(tb, st),
        compiler_params=_cparams(1),
        name="conv_sample",
    )(z, hist2d, dw_w, dw_b, ln_g, ln_b)


def _mix_kernel(at_ref, y_ref, ga_ref, gc_ref, h_ref, g2_ref, wao_ref, wco_ref, wout_ref, o_ref,
                *, per_row):
    a = _dot(at_ref[...].astype(BF16), wao_ref[...])
    c = _dot(y_ref[...], wco_ref[...])
    merged = (ga_ref[...].astype(F32) * a + gc_ref[...].astype(F32) * c).astype(BF16)
    r = _dot(merged, wout_ref[...])
    o_ref[...] = h_ref[...] + _mod_val(g2_ref, per_row) * r


def _mix(attn, y, ga, gc, h, mod, w_ao, w_co, w_out, *, per_row, tm):
    rows = h.shape[0]
    row = lambda w: pl.BlockSpec((tm, w), lambda i: (i, 0))
    resident = lambda shape: pl.BlockSpec(shape, lambda i: (0, 0), pipeline_mode=pl.Buffered(1))
    return pl.pallas_call(
        functools.partial(_mix_kernel, per_row=per_row),
        out_shape=jax.ShapeDtypeStruct((rows, D_MODEL), F32),
        grid=(rows // tm,),
        in_specs=[row(ATTN_WIDTH), row(C_CONV), row(D_MODEL), row(D_MODEL), row(D_MODEL),
                  _mod_spec(5, per_row),
                  resident((ATTN_WIDTH, D_MODEL)), resident((C_CONV, D_MODEL)), resident((D_MODEL, D_MODEL))],
        out_specs=row(D_MODEL),
        compiler_params=_cparams(1),
        name="mix",
    )(attn, y, ga, gc, h, mod, w_ao, w_co, w_out)


def kernel(x_prompt, x_sample, cache_k, cache_v, state_conv, c_prompt, c_sample, rel_bias_table, norm1_g, ffn1_w1, ffn1_w3, ffn1_w2, norm2_g, w_in, attn_sinks, conv_dw_w, conv_dw_b, conv_ln_g, conv_ln_b, w_conv_out, w_attn_out, w_out, norm3_g, ffn2_w1, ffn2_w3, ffn2_w2, w_ada, b_ada, final_norm_g, w_ada_final, b_ada_final):
    c_all = jnp.concatenate([c_sample, c_prompt, jnp.zeros((N_COND - DEC_BATCH - 1, D_MODEL), F32)], axis=0)
    mod = _adaln(c_all, w_ada[0], b_ada)
    mod_f = _adaln(c_all, w_ada_final, b_ada_final[None, :])

    w_ao = w_attn_out[0].astype(BF16)
    w_co = w_conv_out[0].astype(BF16)
    w_o = w_out[0].astype(BF16)

    sinks = attn_sinks[0]
    tab_flat = rel_bias_table.reshape(-1)
    tab_ext = jnp.concatenate([rel_bias_table, sinks[None, :]], axis=0)
    th = tab_ext.T.reshape(N_KV_HEADS, GROUP, N_BUCKETS + 1).transpose(1, 0, 2)
    th = jnp.tile(th.reshape(N_HEADS, N_BUCKETS + 1), (DEC_SEQ, 1))

    xp = x_prompt[0]
    xs = x_sample.transpose(1, 0, 2).reshape(S_ROWS, D_MODEL)

    conv_w = (conv_dw_w[0], conv_dw_b, conv_ln_g, conv_ln_b)

    final = (final_norm_g[None, :], mod_f)

    def mixers_p(q, kv, z, ga, gc, h):
        attn = _attn_prompt(q, kv, tab_flat, sinks)
        return _mix_conv_prompt(z, *conv_w, attn, ga, gc, h, mod, w_ao, w_co, w_o), None

    def mixers_s(q, kv, z, ga, gc, h):
        o, kw, vw = _attn_sample(q.reshape(DEC_SEQ, DEC_BATCH, ATTN_WIDTH),
                                 kv.reshape(DEC_SEQ, DEC_BATCH, 2 * KV_WIDTH),
                                 cache_k.reshape(DEC_BATCH, WBUF, KV_WIDTH),
                                 cache_v.reshape(DEC_BATCH, WBUF, KV_WIDTH), th)
        y, ns = _conv_sample(z.reshape(DEC_SEQ, DEC_BATCH, C_CONV),
                             state_conv.reshape(DEC_BATCH, HIST * C_CONV), *conv_w)
        h = _mix(o.reshape(S_ROWS, ATTN_WIDTH), y.reshape(S_ROWS, C_CONV), ga, gc, h, mod,
                 w_ao, w_co, w_o, per_row=True, tm=DEC_BATCH)
        return h, (kw, vw, ns)

    hs, f1 = _ffn(xs, norm1_g, mod, 0, ffn1_w1[0], ffn1_w3[0], ffn1_w2[0], per_row=True,
                  tm=S_ROWS, tf=FFN_TF_CAST, cast_out=True)
    hp = _ffn(xp, norm1_g, mod, 0, *f1, per_row=False, tm=FFN_TM, tf=FFN_TF)

    qs, kv_s, z_s, ga_s, gc_s, w_a, w_b = _proj(hs, norm2_g, mod, w_in[0], w_in[0], per_row=True,
                                                tm=S_ROWS, q_dtype=F32, cast_out=True)
    qp, kv_p, z_p, ga_p, gc_p = _proj(hp, norm2_g, mod, w_a, w_b, per_row=False, tm=PROJ_TM, q_dtype=BF16)

    hs, (kw, vw, ns) = mixers_s(qs, kv_s, z_s, ga_s, gc_s, hs)
    hp, _ = mixers_p(qp, kv_p, z_p, ga_p, gc_p, hp)

    ys, f2 = _ffn(hs, norm3_g, mod, 6, ffn2_w1[0], ffn2_w3[0], ffn2_w2[0], per_row=True,
                  tm=S_ROWS, tf=FFN_TF_CAST, final=final, cast_out=True)
    yp = _ffn(hp, norm3_g, mod, 6, *f2, per_row=False, tm=FFN_TM, tf=FFN_TF, final=final)

    w = min(WINDOW, SEQ)
    kv_shape = (1, 1, w, N_KV_HEADS, HEAD_DIM)
    k_win_p = kv_p[SEQ - w:, :KV_WIDTH].reshape(kv_shape)
    v_win_p = kv_p[SEQ - w:, KV_WIDTH:].reshape(kv_shape)
    conv_p_state = z_p[SEQ - HIST:].reshape(1, 1, HIST, C_CONV)
    s_shape = (1, DEC_BATCH, WBUF, N_KV_HEADS, HEAD_DIM)
    y_prompt = yp[None]
    y_sample = ys.reshape(DEC_SEQ, DEC_BATCH, D_MODEL).transpose(1, 0, 2)
    return (y_prompt, y_sample, k_win_p, v_win_p, conv_p_state,
            kw.reshape(s_shape), vw.reshape(s_shape), ns.reshape(1, DEC_BATCH, HIST, C_CONV))
```

```python
import functools

import numpy as np
import jax
import jax.numpy as jnp
from jax import lax
from jax.experimental import pallas as pl
from jax.experimental.pallas import tpu as pltpu

D_MODEL = 2048
SEQ = 8192
DEC_BATCH = 128
DEC_SEQ = 4
PAST_LEN = 16384
N_HEADS = 32
N_KV_HEADS = 8
HEAD_DIM = 64
GROUP = N_HEADS // N_KV_HEADS
ATTN_WIDTH = N_HEADS * HEAD_DIM
KV_WIDTH = N_KV_HEADS * HEAD_DIM
WINDOW = 128
Q_BLOCK = 128
ATTN_SCALE = HEAD_DIM ** -0.5
N_BUCKETS = 32
MAX_DISTANCE = 128
C_CONV = D_MODEL // 2
CONV_WIDTH = 31
HIST = CONV_WIDTH - 1
D_FF = 5632
NORM_EPS = 1e-6
NEG_INF = -1e30
IN_WIDTH = ATTN_WIDTH + 2 * KV_WIDTH + 2 * C_CONV + 2 * D_MODEL
WBUF = min(WINDOW, PAST_LEN)

S_ROWS = DEC_BATCH * DEC_SEQ
N_COND = 136
PROMPT_MOD_BLOCK = DEC_BATCH // 8
S_KEYS = 136
SINK_COL = WBUF + DEC_SEQ

VMEM_LIMIT = 60 * 1024 * 1024
SUBLANES = 8
LANES = 128

F32 = jnp.float32
BF16 = jnp.bfloat16


def _cparams(n_axes, flags=None):
    return pltpu.CompilerParams(dimension_semantics=("arbitrary",) * n_axes,
                                vmem_limit_bytes=VMEM_LIMIT, flags=flags)


def _dot(a, b):
    return jnp.dot(a, b, preferred_element_type=F32)


def _rms_mod(x, g, sh, sc):
    ms = jnp.mean(x * x, axis=-1, keepdims=True)
    y = x * lax.rsqrt(ms + NORM_EPS) * g
    return y * (1.0 + sc) + sh


def _mod_val(ref, per_row):
    return ref[...] if per_row else ref[0:1, :]


def _mod_spec(chunk, per_row, width=D_MODEL):
    if per_row:
        return pl.BlockSpec((DEC_BATCH, width), lambda *_: (0, chunk))
    return pl.BlockSpec((8, width), lambda *_: (PROMPT_MOD_BLOCK, chunk))


def _adaln_kernel(c_ref, w_ref, b_ref, o_ref, s_ref):
    @pl.when(pl.program_id(0) == 0)
    def _():
        s_ref[...] = jax.nn.silu(c_ref[...]).astype(BF16)

    o_ref[...] = _dot(s_ref[...], w_ref[...].astype(BF16)) + b_ref[...]


def _adaln(c_all, w, b, tn=1024):
    n = w.shape[1]
    return pl.pallas_call(
        _adaln_kernel,
        out_shape=jax.ShapeDtypeStruct((N_COND, n), F32),
        grid=(n // tn,),
        in_specs=[pl.BlockSpec((N_COND, D_MODEL), lambda j: (0, 0)),
                  pl.BlockSpec((D_MODEL, tn), lambda j: (0, j)),
                  pl.BlockSpec((1, tn), lambda j: (0, j))],
        out_specs=pl.BlockSpec((N_COND, tn), lambda j: (0, j)),
        scratch_shapes=[pltpu.VMEM((N_COND, D_MODEL), BF16)],
        compiler_params=_cparams(1),
        name="adaln",
    )(c_all, w, b)


FFN_ROWS = 512
FFN_TM = 1024
FFN_TF = 512
FFN_TF_CAST = 256
PROJ_TM = 1024


def _ffn_kernel(*refs, tm, sub, per_row, n_f, final, cast_out):
    refs = list(refs)
    x_ref, n_ref, sh_ref, sc_ref, g_ref, w1_ref, w3_ref, w2_ref = refs[:8]
    del refs[:8]
    if final:
        gf_ref, shf_ref, scf_ref = refs[:3]
        del refs[:3]
    o_ref = refs.pop(0)
    if cast_out:
        w1o_ref, w3o_ref, w2o_ref = refs[:3]
        del refs[:3]
    (u_ref,) = refs
    f = pl.program_id(1)

    @pl.when(f == 0)
    def _():
        def body(s, carry):
            rows = pl.ds(pl.multiple_of(s * sub, sub), sub)
            u_ref[rows, :] = _rms_mod(x_ref[rows, :], n_ref[...], _mod_val(sh_ref, per_row),
                                      _mod_val(sc_ref, per_row)).astype(BF16)
            o_ref[rows, :] = jnp.zeros((sub, D_MODEL), F32)
            return carry
        lax.fori_loop(0, tm // sub, body, 0)

    w1, w3, w2 = w1_ref[...], w3_ref[...], w2_ref[...]
    if cast_out:
        w1, w3, w2 = w1.astype(BF16), w3.astype(BF16), w2.astype(BF16)
        w1o_ref[...], w3o_ref[...], w2o_ref[...] = w1, w3, w2
    for s in range(tm // FFN_ROWS):
        rows = slice(s * FFN_ROWS, (s + 1) * FFN_ROWS)
        u = u_ref[rows, :]
        h1 = _dot(u, w1)
        h3 = _dot(u, w3)
        a = (jax.nn.silu(h1) * h3).astype(BF16)
        o_ref[rows, :] += _dot(a, w2)

    @pl.when(f == n_f - 1)
    def _():
        def body(s, carry):
            rows = pl.ds(pl.multiple_of(s * sub, sub), sub)
            h = x_ref[rows, :] + 0.5 * _mod_val(g_ref, per_row) * o_ref[rows, :]
            if final:
                h = _rms_mod(h, gf_ref[...], _mod_val(shf_ref, per_row), _mod_val(scf_ref, per_row))
            o_ref[rows, :] = h
            return carry
        lax.fori_loop(0, tm // sub, body, 0)


def _ffn(x, norm_g, mod, chunk0, w1, w3, w2, *, per_row, tm, tf, final=None, cast_out=False):
    rows = x.shape[0]
    n_f = D_FF // tf
    sub = DEC_BATCH
    row_spec = pl.BlockSpec((tm, D_MODEL), lambda i, f: (i, 0))
    vec_spec = pl.BlockSpec((1, D_MODEL), lambda i, f: (0, 0))
    w_specs = [pl.BlockSpec((D_MODEL, tf), lambda i, f: (0, f)),
               pl.BlockSpec((D_MODEL, tf), lambda i, f: (0, f)),
               pl.BlockSpec((tf, D_MODEL), lambda i, f: (f, 0))]
    in_specs = [row_spec, vec_spec,
                _mod_spec(chunk0, per_row), _mod_spec(chunk0 + 1, per_row), _mod_spec(chunk0 + 2, per_row)]
    in_specs += w_specs
    args = [x, norm_g, mod, mod, mod, w1, w3, w2]
    if final is not None:
        gf, mod_f = final
        in_specs += [vec_spec, _mod_spec(0, per_row), _mod_spec(1, per_row)]
        args += [gf, mod_f, mod_f]
    out_shape = [jax.ShapeDtypeStruct((rows, D_MODEL), F32)]
    out_specs = [row_spec]
    if cast_out:
        assert rows == tm, "every weight block must be visited exactly once"
        out_shape += [jax.ShapeDtypeStruct(w.shape, BF16) for w in (w1, w3, w2)]
        out_specs += w_specs
    outs = pl.pallas_call(
        functools.partial(_ffn_kernel, tm=tm, sub=sub, per_row=per_row, n_f=n_f,
                          final=final is not None, cast_out=cast_out),
        out_shape=out_shape,
        grid=(rows // tm, n_f),
        in_specs=in_specs,
        out_specs=out_specs,
        scratch_shapes=[pltpu.VMEM((tm, D_MODEL), BF16)],
        compiler_params=_cparams(2),
        name="ffn_final" if final is not None else "ffn",
    )(*args)
    return (outs[0], tuple(outs[1:])) if cast_out else outs[0]


PROJ_CHUNK = 256
PROJ_CW = 512


def _proj_layout():
    cw = PROJ_CW
    n_q = ATTN_WIDTH // (2 * cw)
    n_kv = 2 * KV_WIDTH // (2 * cw)
    n_glu = C_CONV // cw
    n_gate = D_MODEL // (2 * cw)
    pre = n_q + n_kv
    steps = pre + n_glu + 2 * n_gate

    def block_a(j):
        return jnp.where((j >= pre) & (j < pre + n_glu), j + pre, 2 * j)

    def block_b(j):
        return jnp.where((j >= pre) & (j < pre + n_glu), j + pre + n_glu, 2 * j + 1)

    return n_q, n_kv, n_glu, n_gate, steps, block_a, block_b


def _proj_kernel(*refs, tm, sub, per_row, cast_out):
    refs = list(refs)
    h_ref, n_ref, sh_ref, sc_ref, wa_ref, wb_ref, q_ref, kv_ref, z_ref, ga_ref, gc_ref = refs[:11]
    del refs[:11]
    if cast_out:
        wao_ref, wbo_ref = refs[:2]
        del refs[:2]
    (u_ref,) = refs
    cw = PROJ_CW
    n_q, n_kv, n_glu, n_gate, _, _, _ = _proj_layout()
    j = pl.program_id(1)

    @pl.when(j == 0)
    def _():
        def body(s, carry):
            rows = pl.ds(pl.multiple_of(s * sub, sub), sub)
            u_ref[rows, :] = _rms_mod(h_ref[rows, :], n_ref[...], _mod_val(sh_ref, per_row),
                                      _mod_val(sc_ref, per_row)).astype(BF16)
            return carry
        lax.fori_loop(0, tm // sub, body, 0)

    if cast_out:
        wao_ref[...] = wa_ref[...].astype(BF16)
        wbo_ref[...] = wb_ref[...].astype(BF16)
        wa_ref, wb_ref = wao_ref, wbo_ref

    n_chunks = cw // PROJ_CHUNK

    def chunks(w_ref):
        for c in range(n_chunks):
            cols = slice(c * PROJ_CHUNK, (c + 1) * PROJ_CHUNK)
            yield c * PROJ_CHUNK, _dot(u_ref[...], w_ref[:, cols])

    def pair_store(o_ref, fn):
        for half, w_ref in enumerate((wa_ref, wb_ref)):
            for c0, acc in chunks(w_ref):
                o_ref[:, half * cw + c0:half * cw + c0 + PROJ_CHUNK] = fn(acc).astype(o_ref.dtype)

    lo = 0

    @pl.when(j < n_q)
    def _():
        pair_store(q_ref, lambda acc: acc)

    lo += n_q

    @pl.when((j >= lo) & (j < lo + n_kv))
    def _():
        pair_store(kv_ref, lambda acc: acc)

    lo += n_kv

    @pl.when((j >= lo) & (j < lo + n_glu))
    def _():
        for (c0, a), (_, g) in zip(chunks(wa_ref), chunks(wb_ref)):
            z_ref[:, c0:c0 + PROJ_CHUNK] = a * jax.nn.sigmoid(g)

    lo += n_glu

    @pl.when((j >= lo) & (j < lo + n_gate))
    def _():
        pair_store(ga_ref, jax.nn.sigmoid)

    lo += n_gate

    @pl.when(j >= lo)
    def _():
        pair_store(gc_ref, jax.nn.sigmoid)


def _proj(h, norm_g, mod, w_a, w_b, *, per_row, tm, q_dtype, cast_out=False):
    rows = h.shape[0]
    sub = DEC_BATCH
    cw = PROJ_CW
    n_q, n_kv, n_glu, n_gate, steps, block_a, block_b = _proj_layout()

    def out_spec(width, first, count):
        return pl.BlockSpec((tm, width), lambda i, j: (i, jnp.clip(j - first, 0, count - 1)))

    step_spec = pl.BlockSpec((D_MODEL, cw), lambda i, j: (0, j))
    if cast_out:
        wa_spec = pl.BlockSpec((D_MODEL, cw), lambda i, j: (0, block_a(j)))
        wb_spec = pl.BlockSpec((D_MODEL, cw), lambda i, j: (0, block_b(j)))
    else:
        wa_spec = wb_spec = step_spec
    out_shape = [jax.ShapeDtypeStruct((rows, ATTN_WIDTH), q_dtype),
                 jax.ShapeDtypeStruct((rows, 2 * KV_WIDTH), F32),
                 jax.ShapeDtypeStruct((rows, C_CONV), F32),
                 jax.ShapeDtypeStruct((rows, D_MODEL), BF16),
                 jax.ShapeDtypeStruct((rows, D_MODEL), BF16)]
    out_specs = [out_spec(2 * cw, 0, n_q),
                 out_spec(2 * cw, n_q, n_kv),
                 out_spec(cw, n_q + n_kv, n_glu),
                 out_spec(2 * cw, n_q + n_kv + n_glu, n_gate),
                 out_spec(2 * cw, n_q + n_kv + n_glu + n_gate, n_gate)]
    if cast_out:
        assert rows == tm, "every weight block must be visited exactly once"
        out_shape += [jax.ShapeDtypeStruct((D_MODEL, steps * cw), BF16)] * 2
        out_specs += [step_spec, step_spec]
    return pl.pallas_call(
        functools.partial(_proj_kernel, tm=tm, sub=sub, per_row=per_row, cast_out=cast_out),
        out_shape=out_shape,
        grid=(rows // tm, steps),
        in_specs=[pl.BlockSpec((tm, D_MODEL), lambda i, j: (i, 0)),
                  pl.BlockSpec((1, D_MODEL), lambda i, j: (0, 0)),
                  _mod_spec(3, per_row), _mod_spec(4, per_row), wa_spec, wb_spec],
        out_specs=out_specs,
        scratch_shapes=[pltpu.VMEM((tm, D_MODEL), BF16)],
        compiler_params=_cparams(2),
        name="proj",
    )(h, norm_g, mod, mod, w_a, w_b)


def _t5_bucket_np(dist):
    exact = N_BUCKETS // 2
    d = np.maximum(dist, 0)
    df = np.maximum(d, 1).astype(np.float32)
    large = exact + (np.log(df / np.float32(exact)) / np.float32(np.log(MAX_DISTANCE / exact))
                     * np.float32(N_BUCKETS - exact)).astype(np.int32)
    large = np.minimum(large, N_BUCKETS - 1)
    return np.where(d < exact, d, large).astype(np.int32)


def _prompt_codes():
    qi = np.arange(Q_BLOCK)[:, None]
    kj = np.arange(2 * Q_BLOCK)[None, :]
    dist = qi + Q_BLOCK - kj
    valid = (dist >= 0) & (dist <= WINDOW)
    return np.where(valid, _t5_bucket_np(dist), -1).astype(np.int32)


def _sample_codes():
    t = np.repeat(np.arange(DEC_SEQ), GROUP * N_KV_HEADS)[:, None]
    j = np.arange(S_KEYS)[None, :]
    dist = t + WBUF - j
    valid = (dist >= 0) & (dist <= WINDOW) & (j < WBUF + DEC_SEQ)
    code = np.where(valid, _t5_bucket_np(dist), -1)
    code = np.where(j == SINK_COL, N_BUCKETS, code)
    return code.astype(np.int32)


BIAS_ROWS = 64


def _attn_p_kernel(tab_ref, sink_ref, code_ref, q_ref, kvc_ref, kvp_ref, o_ref, bias_ref):
    n = pl.program_id(0)
    n_keys = 2 * Q_BLOCK
    wide = GROUP * Q_BLOCK

    @pl.when(n == 0)
    def _():
        def kv_body(kv, carry):
            for g in range(GROUP):
                for r in range(n_keys // BIAS_ROWS):
                    rows = slice(r * BIAS_ROWS, (r + 1) * BIAS_ROWS)
                    code = code_ref[rows, :]
                    acc = jnp.full((BIAS_ROWS, Q_BLOCK), NEG_INF, F32)
                    for bk in range(N_BUCKETS):
                        acc = jnp.where(code == bk, tab_ref[bk * N_HEADS + kv * GROUP + g], acc)
                    bias_ref[kv, rows, g * Q_BLOCK:(g + 1) * Q_BLOCK] = acc
            return carry
        lax.fori_loop(0, N_KV_HEADS, kv_body, 0)

    q_t = q_ref[...].astype(F32).T
    ones = jnp.ones((16, n_keys), BF16)
    zeros = jnp.zeros((HEAD_DIM, wide), BF16)
    tile = (n_keys, LANES)
    key = lax.broadcasted_iota(jnp.int32, tile, 0)
    lane = lax.broadcasted_iota(jnp.int32, tile, 1)
    pen = jnp.where((lane == 0) & (key < Q_BLOCK) & (n == 0), NEG_INF, 0.0).astype(BF16)
    pen_rows = (lax.broadcasted_iota(jnp.int32, (LANES, wide), 0) == 0).astype(BF16)

    def pair_kv(pair):
        lanes = slice(pair * LANES, (pair + 1) * LANES)
        v_lanes = slice(KV_WIDTH + pair * LANES, KV_WIDTH + (pair + 1) * LANES)
        k2 = jnp.concatenate([kvp_ref[:, lanes], kvc_ref[:, lanes]], axis=0).astype(BF16)
        k2 = jnp.concatenate([k2, pen], axis=1)
        v2_t = jnp.concatenate([kvp_ref[:, v_lanes], kvc_ref[:, v_lanes]], axis=0).T.astype(BF16)
        return k2, v2_t

    def scores(kv, k2):
        q4 = jnp.concatenate(
            [q_t[(kv * GROUP + g) * HEAD_DIM:(kv * GROUP + g + 1) * HEAD_DIM, :] for g in range(GROUP)],
            axis=1)
        q4 = (q4 * ATTN_SCALE).astype(BF16)
        q4 = jnp.concatenate(([q4, zeros] if kv % 2 == 0 else [zeros, q4]) + [pen_rows], axis=0)
        return _dot(k2, q4)

    kvs = [pair_kv(pair) for pair in range(N_KV_HEADS // 2)]
    s_next = scores(0, kvs[0][0])
    for kv in range(N_KV_HEADS):
        pair, parity = divmod(kv, 2)
        s = s_next
        if kv + 1 < N_KV_HEADS:
            s_next = scores(kv + 1, kvs[(kv + 1) // 2][0])
        s = s + bias_ref[kv]
        sink = jnp.concatenate(
            [jnp.full((1, Q_BLOCK), sink_ref[kv * GROUP + g], F32) for g in range(GROUP)], axis=1)
        m = jnp.maximum(jnp.max(s, axis=0, keepdims=True), sink)
        p = jnp.exp(s - m).astype(BF16)
        lhs = jnp.concatenate([kvs[pair][1][parity * HEAD_DIM:(parity + 1) * HEAD_DIM, :], ones], axis=0)
        o_t = _dot(lhs, p)
        l = o_t[HEAD_DIM:HEAD_DIM + 1, :] + jnp.exp(sink - m)
        o_n = o_t[0:HEAD_DIM, :] * (1.0 / l)
        o_kv = jnp.concatenate([o_n[:, g * Q_BLOCK:(g + 1) * Q_BLOCK] for g in range(GROUP)], axis=0)
        o_ref[:, kv * GROUP * HEAD_DIM:(kv + 1) * GROUP * HEAD_DIM] = o_kv.T.astype(o_ref.dtype)


def _attn_prompt(q, kv, tab_flat, sinks):
    nb = SEQ // Q_BLOCK
    code_t = jnp.asarray(np.ascontiguousarray(_prompt_codes().T))
    smem = pl.BlockSpec(memory_space=pltpu.SMEM)
    return pl.pallas_call(
        _attn_p_kernel,
        out_shape=jax.ShapeDtypeStruct((SEQ, ATTN_WIDTH), BF16),
        grid=(nb,),
        in_specs=[smem, smem,
                  pl.BlockSpec((2 * Q_BLOCK, Q_BLOCK), lambda n: (0, 0)),
                  pl.BlockSpec((Q_BLOCK, ATTN_WIDTH), lambda n: (n, 0)),
                  pl.BlockSpec((Q_BLOCK, 2 * KV_WIDTH), lambda n: (n, 0)),
                  pl.BlockSpec((Q_BLOCK, 2 * KV_WIDTH), lambda n: (jnp.maximum(n - 1, 0), 0))],
        out_specs=pl.BlockSpec((Q_BLOCK, ATTN_WIDTH), lambda n: (n, 0)),
        scratch_shapes=[pltpu.VMEM((N_KV_HEADS, 2 * Q_BLOCK, GROUP * Q_BLOCK), F32)],
        compiler_params=_cparams(1),
        name="attn_prompt",
    )(tab_flat, sinks, code_t, q, kv, kv)


S_BB = 8


def _regroup_heads(x, to_group_major):
    half = HEAD_DIM
    assert LANES == 2 * half
    low = lax.broadcasted_iota(jnp.int32, (x.shape[0], LANES), 1) < half
    n_cols = ATTN_WIDTH // LANES
    outer, inner = (GROUP, N_KV_HEADS) if to_group_major else (N_KV_HEADS, GROUP)
    cols = []
    for c in range(n_cols):
        o, i = divmod(2 * c, inner)
        src = [(i + d) * outer + o for d in range(2)]
        a = x[:, (src[0] // 2) * LANES:(src[0] // 2 + 1) * LANES]
        b = x[:, (src[1] // 2) * LANES:(src[1] // 2 + 1) * LANES]
        assert src[0] % 2 == src[1] % 2
        if src[0] % 2 == 0:
            cols.append(jnp.where(low, a, pltpu.roll(b, half, axis=1)))
        else:
            cols.append(jnp.where(low, pltpu.roll(a, half, axis=1), b))
    return jnp.concatenate(cols, axis=1)


def _attn_s_kernel(th_ref, code_ref, q_ref, kv_ref, ck_ref, cv_ref, o_ref, kw_ref, vw_ref,
                   bias_ref, kf_ref, vf_ref, qe_ref, qg_ref, og_ref):
    for t in range(DEC_SEQ):
        qg_ref[t] = _regroup_heads(q_ref[t], True)

    @pl.when(pl.program_id(0) == 0)
    def _():
        code = code_ref[...]
        acc = jnp.full((DEC_SEQ * N_HEADS, S_KEYS), NEG_INF, F32)
        for bk in range(N_BUCKETS + 1):
            acc = jnp.where(code == bk, th_ref[:, bk:bk + 1], acc)
        bias_ref[...] = acc
        kf_ref[...] = jnp.zeros_like(kf_ref)
        vf_ref[...] = jnp.zeros_like(vf_ref)

    lane_head = lax.broadcasted_iota(jnp.int32, (N_KV_HEADS, KV_WIDTH), 1) // HEAD_DIM
    row_head = lax.broadcasted_iota(jnp.int32, (N_KV_HEADS, KV_WIDTH), 0)
    diag = lane_head == row_head
    rows_all = DEC_SEQ * N_HEADS
    out_mask = (lax.broadcasted_iota(jnp.int32, (rows_all, KV_WIDTH), 0) % N_KV_HEADS
                == lax.broadcasted_iota(jnp.int32, (rows_all, KV_WIDTH), 1) // HEAD_DIM)
    bias = bias_ref[...]

    for b in range(S_BB):
        kf_ref[0:WBUF, :] = ck_ref[b]
        vf_ref[0:WBUF, :] = cv_ref[b]
        kw_ref[b, 0:WBUF - DEC_SEQ, :] = ck_ref[b, DEC_SEQ:WBUF, :]
        vw_ref[b, 0:WBUF - DEC_SEQ, :] = cv_ref[b, DEC_SEQ:WBUF, :]
        for t in range(DEC_SEQ):
            k_new = kv_ref[t, b:b + 1, 0:KV_WIDTH]
            v_new = kv_ref[t, b:b + 1, KV_WIDTH:2 * KV_WIDTH]
            kf_ref[WBUF + t:WBUF + t + 1, :] = k_new
            vf_ref[WBUF + t:WBUF + t + 1, :] = v_new
            kw_ref[b, WBUF - DEC_SEQ + t:WBUF - DEC_SEQ + t + 1, :] = k_new
            vw_ref[b, WBUF - DEC_SEQ + t:WBUF - DEC_SEQ + t + 1, :] = v_new
            for g in range(GROUP):
                q_row = qg_ref[t, b:b + 1, g * KV_WIDTH:(g + 1) * KV_WIDTH]
                piece = jnp.where(diag, jnp.broadcast_to(q_row, (N_KV_HEADS, KV_WIDTH)), 0.0)
                r0 = (t * GROUP + g) * N_KV_HEADS
                qe_ref[r0:r0 + N_KV_HEADS, :] = piece
        s = lax.dot_general(qe_ref[...].astype(BF16), kf_ref[...].astype(BF16),
                            (((1,), (1,)), ((), ())), preferred_element_type=F32)
        s = s * ATTN_SCALE + bias
        m = jnp.max(s, axis=-1, keepdims=True)
        p = jnp.exp(s - m)
        l = jnp.sum(p, axis=-1, keepdims=True)
        o = _dot(p.astype(BF16), vf_ref[...].astype(BF16)) / l
        o = jnp.where(out_mask, o, 0.0)
        for t in range(DEC_SEQ):
            for g in range(GROUP):
                r0 = (t * GROUP + g) * N_KV_HEADS
                row = jnp.sum(o[r0:r0 + N_KV_HEADS, :], axis=0, keepdims=True)
                og_ref[t, b:b + 1, g * KV_WIDTH:(g + 1) * KV_WIDTH] = row
    for t in range(DEC_SEQ):
        o_ref[t] = _regroup_heads(og_ref[t], False)


def _attn_sample(q, kv, cache_k, cache_v, th):
    code = jnp.asarray(_sample_codes())
    rows_all = DEC_SEQ * N_HEADS
    tb_spec = lambda w: pl.BlockSpec((DEC_SEQ, S_BB, w), lambda i: (0, i, 0))
    cache_spec = pl.BlockSpec((S_BB, WBUF, KV_WIDTH), lambda i: (i, 0, 0))
    cache_shape = jax.ShapeDtypeStruct((DEC_BATCH, WBUF, KV_WIDTH), F32)
    return pl.pallas_call(
        _attn_s_kernel,
        out_shape=(jax.ShapeDtypeStruct((DEC_SEQ, DEC_BATCH, ATTN_WIDTH), F32), cache_shape, cache_shape),
        grid=(DEC_BATCH // S_BB,),
        in_specs=[pl.BlockSpec((rows_all, N_BUCKETS + 1), lambda i: (0, 0)),
                  pl.BlockSpec((rows_all, S_KEYS), lambda i: (0, 0)),
                  tb_spec(ATTN_WIDTH), tb_spec(2 * KV_WIDTH), cache_spec, cache_spec],
        out_specs=(tb_spec(ATTN_WIDTH), cache_spec, cache_spec),
        scratch_shapes=[pltpu.VMEM((rows_all, S_KEYS), F32),
                        pltpu.VMEM((S_KEYS, KV_WIDTH), F32),
                        pltpu.VMEM((S_KEYS, KV_WIDTH), F32),
                        pltpu.VMEM((rows_all, KV_WIDTH), F32),
                        pltpu.VMEM((DEC_SEQ, S_BB, ATTN_WIDTH), F32),
                        pltpu.VMEM((DEC_SEQ, S_BB, ATTN_WIDTH), F32)],
        compiler_params=_cparams(1),
        name="attn_sample",
    )(th, code, q, kv, cache_k, cache_v)


CONV_RC = 32
CONV_ROWS = 128
CONV_HALO = 32


def _ln_silu(acc, lg, lb):
    mu = jnp.mean(acc, axis=-1, keepdims=True)
    xc = acc - mu
    var = jnp.mean(xc * xc, axis=-1, keepdims=True)
    return jax.nn.silu(xc * lax.rsqrt(var + NORM_EPS) * lg + lb)


def _conv_tile(first, zc_ref, zh_ref, w_ref, b_ref, lg_ref, lb_ref, y_ref, s_ref, c_ref, tc):
    s_ref[0:CONV_HALO, :] = jnp.where(first, 0.0, zh_ref[...])
    s_ref[CONV_HALO:, :] = zc_ref[...]
    off = CONV_HALO - HIST
    groups = [[j for j in range(CONV_WIDTH) if (j + off) % SUBLANES == r] for r in range(SUBLANES)]
    for rc in range(tc // CONV_ROWS):
        t0 = rc * CONV_ROWS
        for lc in range(C_CONV // LANES):
            lanes = slice(lc * LANES, (lc + 1) * LANES)
            out = jnp.broadcast_to(b_ref[:, lanes], (CONV_ROWS, LANES))
            for r, taps in enumerate(groups):
                n_rows = CONV_ROWS + (SUBLANES if r else 0)
                part = None
                for j in taps:
                    base = t0 + (j + off) - r
                    term = w_ref[j:j + 1, lanes] * s_ref[base:base + n_rows, lanes]
                    part = term if part is None else part + term
                out = out + part[r:r + CONV_ROWS, :]
            c_ref[t0:t0 + CONV_ROWS, lanes] = out
    for r in range(tc // CONV_RC):
        rows = slice(r * CONV_RC, (r + 1) * CONV_RC)
        y_ref[rows, :] = _ln_silu(c_ref[rows, :], lg_ref[...], lb_ref[...]).astype(BF16)


def _mix_conv_kernel(zc_ref, zh_ref, w_ref, b_ref, lg_ref, lb_ref,
                     at_ref, ga_ref, gc_ref, h_ref, g2_ref, wao_ref, wco_ref, wout_ref,
                     o_ref, s_ref, c_ref, y_ref, *, tm):
    _conv_tile(pl.program_id(0) == 0, zc_ref, zh_ref, w_ref, b_ref, lg_ref, lb_ref, y_ref, s_ref, c_ref, tm)
    a = _dot(at_ref[...], wao_ref[...])
    c = _dot(y_ref[...], wco_ref[...])
    merged = (ga_ref[...].astype(F32) * a + gc_ref[...].astype(F32) * c).astype(BF16)
    r = _dot(merged, wout_ref[...])
    o_ref[...] = h_ref[...] + g2_ref[0:1, :] * r


def _mix_conv_prompt(z, dw_w, dw_b, ln_g, ln_b, attn, ga, gc, h, mod, w_ao, w_co, w_out, tm=256):
    rows = h.shape[0]
    n = rows // tm
    ratio = tm // CONV_HALO
    vec = pl.BlockSpec((1, C_CONV), lambda s: (0, 0))
    row = lambda w: pl.BlockSpec((tm, w), lambda s: (s, 0))
    resident = lambda shape: pl.BlockSpec(shape, lambda s: (0, 0), pipeline_mode=pl.Buffered(1))
    return pl.pallas_call(
        functools.partial(_mix_conv_kernel, tm=tm),
        out_shape=jax.ShapeDtypeStruct((rows, D_MODEL), F32),
        grid=(n,),
        in_specs=[row(C_CONV),
                  pl.BlockSpec((CONV_HALO, C_CONV), lambda s: (jnp.maximum(s * ratio - 1, 0), 0)),
                  pl.BlockSpec((CONV_WIDTH, C_CONV), lambda s: (0, 0)), vec, vec, vec,
                  row(ATTN_WIDTH), row(D_MODEL), row(D_MODEL), row(D_MODEL),
                  _mod_spec(5, False),
                  resident((ATTN_WIDTH, D_MODEL)), resident((C_CONV, D_MODEL)), resident((D_MODEL, D_MODEL))],
        out_specs=row(D_MODEL),
        scratch_shapes=[pltpu.VMEM((tm + CONV_HALO, C_CONV), F32), pltpu.VMEM((tm, C_CONV), F32),
                        pltpu.VMEM((tm, C_CONV), BF16)],
        compiler_params=_cparams(1),
        name="mix_conv",
    )(z, z, dw_w, dw_b, ln_g, ln_b, attn, ga, gc, h, mod, w_ao, w_co, w_out)


CONV_S_BB = 32


def _conv_s_kernel(z_ref, hist_ref, w_ref, b_ref, lg_ref, lb_ref, y_ref, ns_ref):
    for t in range(DEC_SEQ):
        acc = jnp.broadcast_to(b_ref[...], (CONV_S_BB, C_CONV))
        for j in range(CONV_WIDTH):
            i = t + j
            src = hist_ref[0, :, i, :] if i < HIST else z_ref[i - HIST]
            acc = acc + w_ref[j:j + 1, :] * src
        y_ref[t] = _ln_silu(acc, lg_ref[...], lb_ref[...]).astype(BF16)
    keep = HIST - DEC_SEQ
    ns_ref[0, :, 0:keep, :] = hist_ref[0, :, DEC_SEQ:HIST, :]
    for t in range(DEC_SEQ):
        ns_ref[0, :, keep + t, :] = z_ref[t]


def _conv_sample(z, hist, dw_w, dw_b, ln_g, ln_b):
    vec = pl.BlockSpec((1, C_CONV), lambda i: (0, 0))
    tb = pl.BlockSpec((DEC_SEQ, CONV_S_BB, C_CONV), lambda i: (0, i, 0))
    st = pl.BlockSpec((1, CONV_S_BB, HIST, C_CONV), lambda i: (0, i, 0, 0))
    return pl.pallas_call(
        _conv_s_kernel,
        out_shape=(jax.ShapeDtypeStruct((DEC_SEQ, DEC_BATCH, C_CONV), BF16),
                   jax.ShapeDtypeStruct(hist.shape, F32)),
        grid=(DEC_BATCH // CONV_S_BB,),
        in_specs=[tb, st, pl.BlockSpec((CONV_WIDTH, C_CONV), lambda i: (0, 0)), vec, vec, vec],
        out_specs=(tb, st),
        compiler_params=_cparams(1),
        name="conv_sample",
    )(z, hist, dw_w, dw_b, ln_g, ln_b)


MIX_TN = 512
MIX_NB = D_MODEL // MIX_TN


def _mix_s_kernel(at_ref, y_ref, ga_ref, gc_ref, h_ref, g2_ref, wao_ref, wco_ref, wout_ref,
                  o_ref, waoo_ref, wcoo_ref, wouto_ref, ab_ref, m_ref):
    j = pl.program_id(0)

    @pl.when(j == 0)
    def _():
        ab_ref[...] = at_ref[...].astype(BF16)

    @pl.when(j < MIX_NB)
    def _():
        waoo_ref[...] = wao_ref[...].astype(BF16)
        wcoo_ref[...] = wco_ref[...].astype(BF16)
        a = _dot(ab_ref[...], waoo_ref[...])
        c = _dot(y_ref[...], wcoo_ref[...])
        m_ref[j] = (ga_ref[...].astype(F32) * a + gc_ref[...].astype(F32) * c).astype(BF16)

    @pl.when(j >= MIX_NB)
    def _():
        wouto_ref[...] = wout_ref[...].astype(BF16)
        r = _dot(m_ref[0], wouto_ref[0:MIX_TN, :])
        for k in range(1, MIX_NB):
            r = r + _dot(m_ref[k], wouto_ref[k * MIX_TN:(k + 1) * MIX_TN, :])
        for s in range(DEC_SEQ):
            rows = slice(s * DEC_BATCH, (s + 1) * DEC_BATCH)
            o_ref[rows, :] = h_ref[rows, :] + g2_ref[...] * r[rows, :]


def _mix_sample(attn, y, ga, gc, h, mod, w_ao, w_co, w_out):
    col_a = lambda j: jnp.minimum(j, MIX_NB - 1)
    col_b = lambda j: jnp.maximum(j - MIX_NB, 0)
    whole = lambda w: pl.BlockSpec((S_ROWS, w), lambda j: (0, 0))
    blk = lambda rows, col: pl.BlockSpec((rows, MIX_TN), lambda j: (0, col(j)))
    mod_chunks = D_MODEL // MIX_TN
    return pl.pallas_call(
        _mix_s_kernel,
        out_shape=(jax.ShapeDtypeStruct((S_ROWS, D_MODEL), F32),
                   jax.ShapeDtypeStruct(w_ao.shape, BF16), jax.ShapeDtypeStruct(w_co.shape, BF16),
                   jax.ShapeDtypeStruct(w_out.shape, BF16)),
        grid=(2 * MIX_NB,),
        in_specs=[whole(ATTN_WIDTH), whole(C_CONV), blk(S_ROWS, col_a), blk(S_ROWS, col_a), blk(S_ROWS, col_b),
                  pl.BlockSpec((DEC_BATCH, MIX_TN), lambda j: (0, 5 * mod_chunks + col_b(j))),
                  blk(ATTN_WIDTH, col_a), blk(C_CONV, col_a), blk(D_MODEL, col_b)],
        out_specs=(blk(S_ROWS, col_b), blk(ATTN_WIDTH, col_a), blk(C_CONV, col_a), blk(D_MODEL, col_b)),
        scratch_shapes=[pltpu.VMEM((S_ROWS, ATTN_WIDTH), BF16), pltpu.VMEM((MIX_NB, S_ROWS, MIX_TN), BF16)],
        compiler_params=_cparams(1),
        name="mix_sample",
    )(attn, y, ga, gc, h, mod, w_ao, w_co, w_out)


def kernel(x_prompt, x_sample, cache_k, cache_v, state_conv, c_prompt, c_sample, rel_bias_table, norm1_g, ffn1_w1, ffn1_w3, ffn1_w2, norm2_g, w_in, attn_sinks, conv_dw_w, conv_dw_b, conv_ln_g, conv_ln_b, w_conv_out, w_attn_out, w_out, norm3_g, ffn2_w1, ffn2_w3, ffn2_w2, w_ada, b_ada, final_norm_g, w_ada_final, b_ada_final):
    c_all = jnp.concatenate([c_sample, c_prompt, jnp.zeros((N_COND - DEC_BATCH - 1, D_MODEL), F32)], axis=0)
    mod = _adaln(c_all, w_ada[0], b_ada)
    mod_f = _adaln(c_all, w_ada_final, b_ada_final[None, :])

    sinks = attn_sinks[0]
    tab_flat = rel_bias_table.reshape(-1)
    tab_ext = jnp.concatenate([rel_bias_table, sinks[None, :]], axis=0)
    th = tab_ext.T.reshape(N_KV_HEADS, GROUP, N_BUCKETS + 1).transpose(1, 0, 2)
    th = jnp.tile(th.reshape(N_HEADS, N_BUCKETS + 1), (DEC_SEQ, 1))

    xp = x_prompt[0]
    xs = x_sample.transpose(1, 0, 2).reshape(S_ROWS, D_MODEL)

    conv_w = (conv_dw_w[0], conv_dw_b, conv_ln_g, conv_ln_b)

    final = (final_norm_g[None, :], mod_f)

    def mixers_p(q, kv, z, ga, gc, h, mix_w):
        attn = _attn_prompt(q, kv, tab_flat, sinks)
        return _mix_conv_prompt(z, *conv_w, attn, ga, gc, h, mod, *mix_w)

    def mixers_s(q, kv, z, ga, gc, h):
        o, kw, vw = _attn_sample(q.reshape(DEC_SEQ, DEC_BATCH, ATTN_WIDTH),
                                 kv.reshape(DEC_SEQ, DEC_BATCH, 2 * KV_WIDTH),
                                 cache_k.reshape(DEC_BATCH, WBUF, KV_WIDTH),
                                 cache_v.reshape(DEC_BATCH, WBUF, KV_WIDTH), th)
        y, ns = _conv_sample(z.reshape(DEC_SEQ, DEC_BATCH, C_CONV), state_conv, *conv_w)
        h, *mix_w = _mix_sample(o.reshape(S_ROWS, ATTN_WIDTH), y.reshape(S_ROWS, C_CONV), ga, gc, h, mod,
                                w_attn_out[0], w_conv_out[0], w_out[0])
        return h, (kw, vw, ns), mix_w

    hs, f1 = _ffn(xs, norm1_g, mod, 0, ffn1_w1[0], ffn1_w3[0], ffn1_w2[0], per_row=True,
                  tm=S_ROWS, tf=FFN_TF_CAST, cast_out=True)
    hp = _ffn(xp, norm1_g, mod, 0, *f1, per_row=False, tm=FFN_TM, tf=FFN_TF)

    qs, kv_s, z_s, ga_s, gc_s, w_a, w_b = _proj(hs, norm2_g, mod, w_in[0], w_in[0], per_row=True,
                                                tm=S_ROWS, q_dtype=F32, cast_out=True)
    qp, kv_p, z_p, ga_p, gc_p = _proj(hp, norm2_g, mod, w_a, w_b, per_row=False, tm=PROJ_TM, q_dtype=BF16)

    hs, (kw, vw, ns), mix_w = mixers_s(qs, kv_s, z_s, ga_s, gc_s, hs)
    hp = mixers_p(qp, kv_p, z_p, ga_p, gc_p, hp, mix_w)

    ys, f2 = _ffn(hs, norm3_g, mod, 6, ffn2_w1[0], ffn2_w3[0], ffn2_w2[0], per_row=True,
                  tm=S_ROWS, tf=FFN_TF_CAST, final=final, cast_out=True)
    yp = _ffn(hp, norm3_g, mod, 6, *f2, per_row=False, tm=FFN_TM, tf=FFN_TF, final=final)

    w = min(WINDOW, SEQ)
    kv_shape = (1, 1, w, N_KV_HEADS, HEAD_DIM)
    k_win_p = kv_p[SEQ - w:, :KV_WIDTH].reshape(kv_shape)
    v_win_p = kv_p[SEQ - w:, KV_WIDTH:].reshape(kv_shape)
    conv_p_state = z_p[SEQ - HIST:].reshape(1, 1, HIST, C_CONV)
    s_shape = (1, DEC_BATCH, WBUF, N_KV_HEADS, HEAD_DIM)
    y_prompt = yp[None]
    y_sample = ys.reshape(DEC_SEQ, DEC_BATCH, D_MODEL).transpose(1, 0, 2)
    return (y_prompt, y_sample, k_win_p, v_win_p, conv_p_state,
            kw.reshape(s_shape), vw.reshape(s_shape), ns)
```

```python
import functools

import numpy as np
import jax
import jax.numpy as jnp
from jax import lax
from jax.experimental import pallas as pl
from jax.experimental.pallas import tpu as pltpu

D_MODEL = 2048
SEQ = 8192
DEC_BATCH = 128
DEC_SEQ = 4
PAST_LEN = 16384
N_HEADS = 32
N_KV_HEADS = 8
HEAD_DIM = 64
GROUP = N_HEADS // N_KV_HEADS
ATTN_WIDTH = N_HEADS * HEAD_DIM
KV_WIDTH = N_KV_HEADS * HEAD_DIM
WINDOW = 128
Q_BLOCK = 128
ATTN_SCALE = HEAD_DIM ** -0.5
N_BUCKETS = 32
MAX_DISTANCE = 128
C_CONV = D_MODEL // 2
CONV_WIDTH = 31
HIST = CONV_WIDTH - 1
D_FF = 5632
NORM_EPS = 1e-6
NEG_INF = -1e30
IN_WIDTH = ATTN_WIDTH + 2 * KV_WIDTH + 2 * C_CONV + 2 * D_MODEL
WBUF = min(WINDOW, PAST_LEN)

S_ROWS = DEC_BATCH * DEC_SEQ
N_COND = 136
PROMPT_MOD_BLOCK = DEC_BATCH // 8
S_EXTRA = 8

VMEM_LIMIT = 60 * 1024 * 1024
SUBLANES = 8
LANES = 128

F32 = jnp.float32
BF16 = jnp.bfloat16


def _cparams(n_axes, flags=None):
    return pltpu.CompilerParams(dimension_semantics=("arbitrary",) * n_axes,
                                vmem_limit_bytes=VMEM_LIMIT, flags=flags)


def _dot(a, b):
    return jnp.dot(a, b, preferred_element_type=F32)


def _rms_mod(x, g, sh, sc):
    ms = jnp.mean(x * x, axis=-1, keepdims=True)
    y = x * lax.rsqrt(ms + NORM_EPS) * g
    return y * (1.0 + sc) + sh


def _mod_val(ref, per_row):
    return ref[...] if per_row else ref[0:1, :]


def _mod_spec(chunk, per_row, width=D_MODEL):
    if per_row:
        return pl.BlockSpec((DEC_BATCH, width), lambda *_: (0, chunk))
    return pl.BlockSpec((8, width), lambda *_: (PROMPT_MOD_BLOCK, chunk))


def _adaln_kernel(c_ref, w_ref, b_ref, o_ref, s_ref):
    @pl.when(pl.program_id(0) == 0)
    def _():
        s_ref[...] = jax.nn.silu(c_ref[...]).astype(BF16)

    o_ref[...] = _dot(s_ref[...], w_ref[...].astype(BF16)) + b_ref[...]


def _adaln(c_all, w, b, tn=1024):
    n = w.shape[1]
    return pl.pallas_call(
        _adaln_kernel,
        out_shape=jax.ShapeDtypeStruct((N_COND, n), F32),
        grid=(n // tn,),
        in_specs=[pl.BlockSpec((N_COND, D_MODEL), lambda j: (0, 0)),
                  pl.BlockSpec((D_MODEL, tn), lambda j: (0, j)),
                  pl.BlockSpec((1, tn), lambda j: (0, j))],
        out_specs=pl.BlockSpec((N_COND, tn), lambda j: (0, j)),
        scratch_shapes=[pltpu.VMEM((N_COND, D_MODEL), BF16)],
        compiler_params=_cparams(1),
        name="adaln",
    )(c_all, w, b)


FFN_ROWS = 512
FFN_TM = 1024
FFN_TF = 512
FFN_TF_CAST = 256
PROJ_TM = 1024


def _ffn_kernel(*refs, tm, sub, per_row, n_f, final, cast_out):
    refs = list(refs)
    x_ref, n_ref, sh_ref, sc_ref, g_ref, w1_ref, w3_ref, w2_ref = refs[:8]
    del refs[:8]
    if final:
        gf_ref, shf_ref, scf_ref = refs[:3]
        del refs[:3]
    o_ref = refs.pop(0)
    if cast_out:
        w1o_ref, w3o_ref, w2o_ref = refs[:3]
        del refs[:3]
    (u_ref,) = refs
    f = pl.program_id(1)

    @pl.when(f == 0)
    def _():
        def body(s, carry):
            rows = pl.ds(pl.multiple_of(s * sub, sub), sub)
            u_ref[rows, :] = _rms_mod(x_ref[rows, :], n_ref[...], _mod_val(sh_ref, per_row),
                                      _mod_val(sc_ref, per_row)).astype(BF16)
            o_ref[rows, :] = jnp.zeros((sub, D_MODEL), F32)
            return carry
        lax.fori_loop(0, tm // sub, body, 0)

    w1, w3, w2 = w1_ref[...], w3_ref[...], w2_ref[...]
    if cast_out:
        w1, w3, w2 = w1.astype(BF16), w3.astype(BF16), w2.astype(BF16)
        w1o_ref[...], w3o_ref[...], w2o_ref[...] = w1, w3, w2
    for s in range(tm // FFN_ROWS):
        rows = slice(s * FFN_ROWS, (s + 1) * FFN_ROWS)
        u = u_ref[rows, :]
        h1 = _dot(u, w1)
        h3 = _dot(u, w3)
        a = (jax.nn.silu(h1) * h3).astype(BF16)
        o_ref[rows, :] += _dot(a, w2)

    @pl.when(f == n_f - 1)
    def _():
        def body(s, carry):
            rows = pl.ds(pl.multiple_of(s * sub, sub), sub)
            h = x_ref[rows, :] + 0.5 * _mod_val(g_ref, per_row) * o_ref[rows, :]
            if final:
                h = _rms_mod(h, gf_ref[...], _mod_val(shf_ref, per_row), _mod_val(scf_ref, per_row))
            o_ref[rows, :] = h
            return carry
        lax.fori_loop(0, tm // sub, body, 0)


def _ffn(x, norm_g, mod, chunk0, w1, w3, w2, *, per_row, tm, tf, final=None, cast_out=False):
    rows = x.shape[0]
    n_f = D_FF // tf
    sub = DEC_BATCH
    row_spec = pl.BlockSpec((tm, D_MODEL), lambda i, f: (i, 0))
    vec_spec = pl.BlockSpec((1, D_MODEL), lambda i, f: (0, 0))
    w_specs = [pl.BlockSpec((D_MODEL, tf), lambda i, f: (0, f)),
               pl.BlockSpec((D_MODEL, tf), lambda i, f: (0, f)),
               pl.BlockSpec((tf, D_MODEL), lambda i, f: (f, 0))]
    in_specs = [row_spec, vec_spec,
                _mod_spec(chunk0, per_row), _mod_spec(chunk0 + 1, per_row), _mod_spec(chunk0 + 2, per_row)]
    in_specs += w_specs
    args = [x, norm_g, mod, mod, mod, w1, w3, w2]
    if final is not None:
        gf, mod_f = final
        in_specs += [vec_spec, _mod_spec(0, per_row), _mod_spec(1, per_row)]
        args += [gf, mod_f, mod_f]
    out_shape = [jax.ShapeDtypeStruct((rows, D_MODEL), F32)]
    out_specs = [row_spec]
    if cast_out:
        assert rows == tm, "every weight block must be visited exactly once"
        out_shape += [jax.ShapeDtypeStruct(w.shape, BF16) for w in (w1, w3, w2)]
        out_specs += w_specs
    outs = pl.pallas_call(
        functools.partial(_ffn_kernel, tm=tm, sub=sub, per_row=per_row, n_f=n_f,
                          final=final is not None, cast_out=cast_out),
        out_shape=out_shape,
        grid=(rows // tm, n_f),
        in_specs=in_specs,
        out_specs=out_specs,
        scratch_shapes=[pltpu.VMEM((tm, D_MODEL), BF16)],
        compiler_params=_cparams(2),
        name="ffn_final" if final is not None else "ffn",
    )(*args)
    return (outs[0], tuple(outs[1:])) if cast_out else outs[0]


PROJ_CHUNK = 256
PROJ_CW = 512


def _proj_layout():
    cw = PROJ_CW
    n_q = ATTN_WIDTH // (2 * cw)
    n_kv = 2 * KV_WIDTH // (2 * cw)
    n_glu = C_CONV // cw
    n_gate = D_MODEL // (2 * cw)
    pre = n_q + n_kv
    steps = pre + n_glu + 2 * n_gate

    def block_a(j):
        return jnp.where((j >= pre) & (j < pre + n_glu), j + pre, 2 * j)

    def block_b(j):
        return jnp.where((j >= pre) & (j < pre + n_glu), j + pre + n_glu, 2 * j + 1)

    return n_q, n_kv, n_glu, n_gate, steps, block_a, block_b


def _proj_kernel(*refs, tm, sub, per_row, cast_out):
    refs = list(refs)
    h_ref, n_ref, sh_ref, sc_ref, wa_ref, wb_ref, q_ref, kv_ref, z_ref, ga_ref, gc_ref = refs[:11]
    del refs[:11]
    if cast_out:
        wao_ref, wbo_ref = refs[:2]
        del refs[:2]
    (u_ref,) = refs
    cw = PROJ_CW
    n_q, n_kv, n_glu, n_gate, _, _, _ = _proj_layout()
    j = pl.program_id(1)

    @pl.when(j == 0)
    def _():
        def body(s, carry):
            rows = pl.ds(pl.multiple_of(s * sub, sub), sub)
            u_ref[rows, :] = _rms_mod(h_ref[rows, :], n_ref[...], _mod_val(sh_ref, per_row),
                                      _mod_val(sc_ref, per_row)).astype(BF16)
            return carry
        lax.fori_loop(0, tm // sub, body, 0)

    if cast_out:
        wao_ref[...] = wa_ref[...].astype(BF16)
        wbo_ref[...] = wb_ref[...].astype(BF16)
        wa_ref, wb_ref = wao_ref, wbo_ref

    n_chunks = cw // PROJ_CHUNK

    def chunks(w_ref):
        for c in range(n_chunks):
            cols = slice(c * PROJ_CHUNK, (c + 1) * PROJ_CHUNK)
            yield c * PROJ_CHUNK, _dot(u_ref[...], w_ref[:, cols])

    def pair_store(o_ref, fn):
        for half, w_ref in enumerate((wa_ref, wb_ref)):
            for c0, acc in chunks(w_ref):
                o_ref[:, half * cw + c0:half * cw + c0 + PROJ_CHUNK] = fn(acc).astype(o_ref.dtype)

    lo = 0

    @pl.when(j < n_q)
    def _():
        pair_store(q_ref, lambda acc: acc)

    lo += n_q

    @pl.when((j >= lo) & (j < lo + n_kv))
    def _():
        pair_store(kv_ref, lambda acc: acc)

    lo += n_kv

    @pl.when((j >= lo) & (j < lo + n_glu))
    def _():
        for (c0, a), (_, g) in zip(chunks(wa_ref), chunks(wb_ref)):
            z_ref[:, c0:c0 + PROJ_CHUNK] = a * jax.nn.sigmoid(g)

    lo += n_glu

    @pl.when((j >= lo) & (j < lo + n_gate))
    def _():
        pair_store(ga_ref, jax.nn.sigmoid)

    lo += n_gate

    @pl.when(j >= lo)
    def _():
        pair_store(gc_ref, jax.nn.sigmoid)


def _proj(h, norm_g, mod, w_a, w_b, *, per_row, tm, q_dtype, cast_out=False):
    rows = h.shape[0]
    sub = DEC_BATCH
    cw = PROJ_CW
    n_q, n_kv, n_glu, n_gate, steps, block_a, block_b = _proj_layout()

    def out_spec(width, first, count):
        return pl.BlockSpec((tm, width), lambda i, j: (i, jnp.clip(j - first, 0, count - 1)))

    step_spec = pl.BlockSpec((D_MODEL, cw), lambda i, j: (0, j))
    if cast_out:
        wa_spec = pl.BlockSpec((D_MODEL, cw), lambda i, j: (0, block_a(j)))
        wb_spec = pl.BlockSpec((D_MODEL, cw), lambda i, j: (0, block_b(j)))
    else:
        wa_spec = wb_spec = step_spec
    out_shape = [jax.ShapeDtypeStruct((rows, ATTN_WIDTH), q_dtype),
                 jax.ShapeDtypeStruct((rows, 2 * KV_WIDTH), F32),
                 jax.ShapeDtypeStruct((rows, C_CONV), F32),
                 jax.ShapeDtypeStruct((rows, D_MODEL), BF16),
                 jax.ShapeDtypeStruct((rows, D_MODEL), BF16)]
    out_specs = [out_spec(2 * cw, 0, n_q),
                 out_spec(2 * cw, n_q, n_kv),
                 out_spec(cw, n_q + n_kv, n_glu),
                 out_spec(2 * cw, n_q + n_kv + n_glu, n_gate),
                 out_spec(2 * cw, n_q + n_kv + n_glu + n_gate, n_gate)]
    if cast_out:
        assert rows == tm, "every weight block must be visited exactly once"
        out_shape += [jax.ShapeDtypeStruct((D_MODEL, steps * cw), BF16)] * 2
        out_specs += [step_spec, step_spec]
    return pl.pallas_call(
        functools.partial(_proj_kernel, tm=tm, sub=sub, per_row=per_row, cast_out=cast_out),
        out_shape=out_shape,
        grid=(rows // tm, steps),
        in_specs=[pl.BlockSpec((tm, D_MODEL), lambda i, j: (i, 0)),
                  pl.BlockSpec((1, D_MODEL), lambda i, j: (0, 0)),
                  _mod_spec(3, per_row), _mod_spec(4, per_row), wa_spec, wb_spec],
        out_specs=out_specs,
        scratch_shapes=[pltpu.VMEM((tm, D_MODEL), BF16)],
        compiler_params=_cparams(2),
        name="proj",
    )(h, norm_g, mod, mod, w_a, w_b)


def _t5_bucket_np(dist):
    exact = N_BUCKETS // 2
    d = np.maximum(dist, 0)
    df = np.maximum(d, 1).astype(np.float32)
    large = exact + (np.log(df / np.float32(exact)) / np.float32(np.log(MAX_DISTANCE / exact))
                     * np.float32(N_BUCKETS - exact)).astype(np.int32)
    large = np.minimum(large, N_BUCKETS - 1)
    return np.where(d < exact, d, large).astype(np.int32)


def _prompt_codes():
    qi = np.arange(Q_BLOCK)[:, None]
    kj = np.arange(2 * Q_BLOCK)[None, :]
    dist = qi + Q_BLOCK - kj
    valid = (dist >= 0) & (dist <= WINDOW)
    return np.where(valid, _t5_bucket_np(dist), -1).astype(np.int32)


def _sample_codes():
    t = np.repeat(np.arange(DEC_SEQ), GROUP * N_KV_HEADS)[:, None]
    j = np.arange(WBUF)[None, :]
    dist = t + WBUF - j
    cache = np.where((dist >= 0) & (dist <= WINDOW), _t5_bucket_np(dist), -1)
    c = np.arange(S_EXTRA)[None, :]
    t_new = c - (S_EXTRA - DEC_SEQ)
    dist = t - t_new
    extra = np.where((t_new >= 0) & (dist >= 0) & (dist <= WINDOW), _t5_bucket_np(dist), -1)
    extra = np.where(c == 0, N_BUCKETS, extra)
    return cache.astype(np.int32), extra.astype(np.int32)


BIAS_ROWS = 64


def _attn_p_kernel(tab_ref, sink_ref, code_ref, q_ref, kvc_ref, kvp_ref, o_ref, bias_ref):
    n = pl.program_id(0)
    n_keys = 2 * Q_BLOCK
    wide = GROUP * Q_BLOCK

    @pl.when(n == 0)
    def _():
        def kv_body(kv, carry):
            for g in range(GROUP):
                for r in range(n_keys // BIAS_ROWS):
                    rows = slice(r * BIAS_ROWS, (r + 1) * BIAS_ROWS)
                    code = code_ref[rows, :]
                    acc = jnp.full((BIAS_ROWS, Q_BLOCK), NEG_INF, F32)
                    for bk in range(N_BUCKETS):
                        acc = jnp.where(code == bk, tab_ref[bk * N_HEADS + kv * GROUP + g], acc)
                    bias_ref[kv, rows, g * Q_BLOCK:(g + 1) * Q_BLOCK] = acc
            return carry
        lax.fori_loop(0, N_KV_HEADS, kv_body, 0)

    q_t = q_ref[...].astype(F32).T
    ones = jnp.ones((16, n_keys), BF16)
    zeros = jnp.zeros((HEAD_DIM, wide), BF16)
    tile = (n_keys, LANES)
    key = lax.broadcasted_iota(jnp.int32, tile, 0)
    lane = lax.broadcasted_iota(jnp.int32, tile, 1)
    pen = jnp.where((lane == 0) & (key < Q_BLOCK) & (n == 0), NEG_INF, 0.0).astype(BF16)
    pen_rows = (lax.broadcasted_iota(jnp.int32, (LANES, wide), 0) == 0).astype(BF16)

    def pair_kv(pair):
        lanes = slice(pair * LANES, (pair + 1) * LANES)
        v_lanes = slice(KV_WIDTH + pair * LANES, KV_WIDTH + (pair + 1) * LANES)
        k2 = jnp.concatenate([kvp_ref[:, lanes], kvc_ref[:, lanes]], axis=0).astype(BF16)
        k2 = jnp.concatenate([k2, pen], axis=1)
        v2_t = jnp.concatenate([kvp_ref[:, v_lanes], kvc_ref[:, v_lanes]], axis=0).T.astype(BF16)
        return k2, v2_t

    def scores(kv, k2):
        q4 = jnp.concatenate(
            [q_t[(kv * GROUP + g) * HEAD_DIM:(kv * GROUP + g + 1) * HEAD_DIM, :] for g in range(GROUP)],
            axis=1)
        q4 = (q4 * ATTN_SCALE).astype(BF16)
        q4 = jnp.concatenate(([q4, zeros] if kv % 2 == 0 else [zeros, q4]) + [pen_rows], axis=0)
        return _dot(k2, q4)

    kvs = [pair_kv(pair) for pair in range(N_KV_HEADS // 2)]
    s_next = scores(0, kvs[0][0])
    for kv in range(N_KV_HEADS):
        pair, parity = divmod(kv, 2)
        s = s_next
        if kv + 1 < N_KV_HEADS:
            s_next = scores(kv + 1, kvs[(kv + 1) // 2][0])
        s = s + bias_ref[kv]
        sink = jnp.concatenate(
            [jnp.full((1, Q_BLOCK), sink_ref[kv * GROUP + g], F32) for g in range(GROUP)], axis=1)
        m = jnp.maximum(jnp.max(s, axis=0, keepdims=True), sink)
        p = jnp.exp(s - m).astype(BF16)
        lhs = jnp.concatenate([kvs[pair][1][parity * HEAD_DIM:(parity + 1) * HEAD_DIM, :], ones], axis=0)
        o_t = _dot(lhs, p)
        l = o_t[HEAD_DIM:HEAD_DIM + 1, :] + jnp.exp(sink - m)
        o_n = o_t[0:HEAD_DIM, :] * (1.0 / l)
        o_kv = jnp.concatenate([o_n[:, g * Q_BLOCK:(g + 1) * Q_BLOCK] for g in range(GROUP)], axis=0)
        o_ref[:, kv * GROUP * HEAD_DIM:(kv + 1) * GROUP * HEAD_DIM] = o_kv.T.astype(o_ref.dtype)


def _attn_prompt(q, kv, tab_flat, sinks):
    nb = SEQ // Q_BLOCK
    code_t = jnp.asarray(np.ascontiguousarray(_prompt_codes().T))
    smem = pl.BlockSpec(memory_space=pltpu.SMEM)
    return pl.pallas_call(
        _attn_p_kernel,
        out_shape=jax.ShapeDtypeStruct((SEQ, ATTN_WIDTH), BF16),
        grid=(nb,),
        in_specs=[smem, smem,
                  pl.BlockSpec((2 * Q_BLOCK, Q_BLOCK), lambda n: (0, 0)),
                  pl.BlockSpec((Q_BLOCK, ATTN_WIDTH), lambda n: (n, 0)),
                  pl.BlockSpec((Q_BLOCK, 2 * KV_WIDTH), lambda n: (n, 0)),
                  pl.BlockSpec((Q_BLOCK, 2 * KV_WIDTH), lambda n: (jnp.maximum(n - 1, 0), 0))],
        out_specs=pl.BlockSpec((Q_BLOCK, ATTN_WIDTH), lambda n: (n, 0)),
        scratch_shapes=[pltpu.VMEM((N_KV_HEADS, 2 * Q_BLOCK, GROUP * Q_BLOCK), F32)],
        compiler_params=_cparams(1),
        name="attn_prompt",
    )(tab_flat, sinks, code_t, q, kv, kv)


S_BB = 8


def _regroup_heads(x, to_group_major):
    half = HEAD_DIM
    assert LANES == 2 * half
    low = lax.broadcasted_iota(jnp.int32, (x.shape[0], LANES), 1) < half
    n_cols = ATTN_WIDTH // LANES
    outer, inner = (GROUP, N_KV_HEADS) if to_group_major else (N_KV_HEADS, GROUP)
    cols = []
    for c in range(n_cols):
        o, i = divmod(2 * c, inner)
        src = [(i + d) * outer + o for d in range(2)]
        a = x[:, (src[0] // 2) * LANES:(src[0] // 2 + 1) * LANES]
        b = x[:, (src[1] // 2) * LANES:(src[1] // 2 + 1) * LANES]
        assert src[0] % 2 == src[1] % 2
        if src[0] % 2 == 0:
            cols.append(jnp.where(low, a, pltpu.roll(b, half, axis=1)))
        else:
            cols.append(jnp.where(low, pltpu.roll(a, half, axis=1), b))
    return jnp.concatenate(cols, axis=1)


def _attn_s_kernel(th_ref, code_c_ref, code_x_ref, q_ref, kv_ref, ck_ref, cv_ref, o_ref, kw_ref, vw_ref,
                   bias_c_ref, bias_x_ref, nk_ref, nv_ref, qe_ref, qg_ref, og_ref):
    rows_all = DEC_SEQ * N_HEADS
    for t in range(DEC_SEQ):
        qg_ref[t] = _regroup_heads(q_ref[t], True)

    @pl.when(pl.program_id(0) == 0)
    def _():
        for code_ref, bias_ref in ((code_c_ref, bias_c_ref), (code_x_ref, bias_x_ref)):
            code = code_ref[...]
            acc = jnp.full(code.shape, NEG_INF, F32)
            for bk in range(N_BUCKETS + 1):
                acc = jnp.where(code == bk, th_ref[:, bk:bk + 1], acc)
            bias_ref[...] = acc
        nk_ref[...] = jnp.zeros_like(nk_ref)
        nv_ref[...] = jnp.zeros_like(nv_ref)

    lane_head = lax.broadcasted_iota(jnp.int32, (N_KV_HEADS, KV_WIDTH), 1) // HEAD_DIM
    row_head = lax.broadcasted_iota(jnp.int32, (N_KV_HEADS, KV_WIDTH), 0)
    diag = lane_head == row_head
    out_mask = (lax.broadcasted_iota(jnp.int32, (rows_all, KV_WIDTH), 0) % N_KV_HEADS
                == lax.broadcasted_iota(jnp.int32, (rows_all, KV_WIDTH), 1) // HEAD_DIM)
    new_lanes = lax.broadcasted_iota(jnp.int32, (KV_WIDTH, WBUF), 1) >= WBUF - DEC_SEQ
    bias_c = bias_c_ref[...]
    bias_x = bias_x_ref[...]
    nt_dims = (((1,), (1,)), ((), ()))
    x0 = WBUF - S_EXTRA

    for b in range(S_BB):
        for t in range(DEC_SEQ):
            nk_ref[WBUF - DEC_SEQ + t:WBUF - DEC_SEQ + t + 1, :] = kv_ref[t, b:b + 1, 0:KV_WIDTH]
            nv_ref[WBUF - DEC_SEQ + t:WBUF - DEC_SEQ + t + 1, :] = kv_ref[t, b:b + 1, KV_WIDTH:2 * KV_WIDTH]
            for g in range(GROUP):
                q_row = qg_ref[t, b:b + 1, g * KV_WIDTH:(g + 1) * KV_WIDTH]
                piece = jnp.where(diag, jnp.broadcast_to(q_row, (N_KV_HEADS, KV_WIDTH)), 0.0)
                r0 = (t * GROUP + g) * N_KV_HEADS
                qe_ref[r0:r0 + N_KV_HEADS, :] = piece
        k_t = ck_ref[b]
        v_t = cv_ref[b]
        kw_ref[b] = jnp.where(new_lanes, nk_ref[...].T, pltpu.roll(k_t, WBUF - DEC_SEQ, axis=1))
        vw_ref[b] = jnp.where(new_lanes, nv_ref[...].T, pltpu.roll(v_t, WBUF - DEC_SEQ, axis=1))

        qe = (qe_ref[...] * ATTN_SCALE).astype(BF16)
        s_c = _dot(qe, k_t.astype(BF16)) + bias_c
        s_x = lax.dot_general(qe, nk_ref[x0:WBUF, :].astype(BF16), nt_dims,
                              preferred_element_type=F32) + bias_x
        m = jnp.maximum(jnp.max(s_c, axis=-1, keepdims=True), jnp.max(s_x, axis=-1, keepdims=True))
        p_c = jnp.exp(s_c - m)
        p_x = jnp.exp(s_x - m)
        l = jnp.sum(p_c, axis=-1, keepdims=True) + jnp.sum(p_x, axis=-1, keepdims=True)
        o = lax.dot_general(p_c.astype(BF16), v_t.astype(BF16), nt_dims, preferred_element_type=F32)
        o = (o + _dot(p_x.astype(BF16), nv_ref[x0:WBUF, :].astype(BF16))) / l
        o = jnp.where(out_mask, o, 0.0)
        for t in range(DEC_SEQ):
            for g in range(GROUP):
                r0 = (t * GROUP + g) * N_KV_HEADS
                row = jnp.sum(o[r0:r0 + N_KV_HEADS, :], axis=0, keepdims=True)
                og_ref[t, b:b + 1, g * KV_WIDTH:(g + 1) * KV_WIDTH] = row
    for t in range(DEC_SEQ):
        o_ref[t] = _regroup_heads(og_ref[t], False)


def _attn_sample(q, kv, cache_k_t, cache_v_t, th):
    code_c, code_x = (jnp.asarray(c) for c in _sample_codes())
    rows_all = DEC_SEQ * N_HEADS
    full = lambda a: pl.BlockSpec(a.shape, lambda i: (0,) * a.ndim)
    tb_spec = lambda w: pl.BlockSpec((DEC_SEQ, S_BB, w), lambda i: (0, i, 0))
    cache_spec = pl.BlockSpec((S_BB, KV_WIDTH, WBUF), lambda i: (i, 0, 0))
    cache_shape = jax.ShapeDtypeStruct((DEC_BATCH, KV_WIDTH, WBUF), F32)
    return pl.pallas_call(
        _attn_s_kernel,
        out_shape=(jax.ShapeDtypeStruct((DEC_SEQ, DEC_BATCH, ATTN_WIDTH), F32), cache_shape, cache_shape),
        grid=(DEC_BATCH // S_BB,),
        in_specs=[full(th), full(code_c), full(code_x),
                  tb_spec(ATTN_WIDTH), tb_spec(2 * KV_WIDTH), cache_spec, cache_spec],
        out_specs=(tb_spec(ATTN_WIDTH), cache_spec, cache_spec),
        scratch_shapes=[pltpu.VMEM((rows_all, WBUF), F32),
                        pltpu.VMEM((rows_all, S_EXTRA), F32),
                        pltpu.VMEM((WBUF, KV_WIDTH), F32),
                        pltpu.VMEM((WBUF, KV_WIDTH), F32),
                        pltpu.VMEM((rows_all, KV_WIDTH), F32),
                        pltpu.VMEM((DEC_SEQ, S_BB, ATTN_WIDTH), F32),
                        pltpu.VMEM((DEC_SEQ, S_BB, ATTN_WIDTH), F32)],
        compiler_params=_cparams(1),
        name="attn_sample",
    )(th, code_c, code_x, q, kv, cache_k_t, cache_v_t)


CONV_RC = 32
CONV_ROWS = 128
CONV_HALO = 32


def _ln_silu(acc, lg, lb):
    mu = jnp.mean(acc, axis=-1, keepdims=True)
    xc = acc - mu
    var = jnp.mean(xc * xc, axis=-1, keepdims=True)
    return jax.nn.silu(xc * lax.rsqrt(var + NORM_EPS) * lg + lb)


def _conv_tile(first, zc_ref, zh_ref, w_ref, b_ref, lg_ref, lb_ref, y_ref, s_ref, c_ref, tc):
    s_ref[0:CONV_HALO, :] = jnp.where(first, 0.0, zh_ref[...])
    s_ref[CONV_HALO:, :] = zc_ref[...]
    off = CONV_HALO - HIST
    groups = [[j for j in range(CONV_WIDTH) if (j + off) % SUBLANES == r] for r in range(SUBLANES)]
    for rc in range(tc // CONV_ROWS):
        t0 = rc * CONV_ROWS
        for lc in range(C_CONV // LANES):
            lanes = slice(lc * LANES, (lc + 1) * LANES)
            out = jnp.broadcast_to(b_ref[:, lanes], (CONV_ROWS, LANES))
            for r, taps in enumerate(groups):
                n_rows = CONV_ROWS + (SUBLANES if r else 0)
                part = None
                for j in taps:
                    base = t0 + (j + off) - r
                    term = w_ref[j:j + 1, lanes] * s_ref[base:base + n_rows, lanes]
                    part = term if part is None else part + term
                out = out + part[r:r + CONV_ROWS, :]
            c_ref[t0:t0 + CONV_ROWS, lanes] = out
    for r in range(tc // CONV_RC):
        rows = slice(r * CONV_RC, (r + 1) * CONV_RC)
        y_ref[rows, :] = _ln_silu(c_ref[rows, :], lg_ref[...], lb_ref[...]).astype(BF16)


def _mix_conv_kernel(zc_ref, zh_ref, w_ref, b_ref, lg_ref, lb_ref,
                     at_ref, ga_ref, gc_ref, h_ref, g2_ref, wao_ref, wco_ref, wout_ref,
                     o_ref, s_ref, c_ref, y_ref, *, tm):
    _conv_tile(pl.program_id(0) == 0, zc_ref, zh_ref, w_ref, b_ref, lg_ref, lb_ref, y_ref, s_ref, c_ref, tm)
    a = _dot(at_ref[...], wao_ref[...])
    c = _dot(y_ref[...], wco_ref[...])
    merged = (ga_ref[...].astype(F32) * a + gc_ref[...].astype(F32) * c).astype(BF16)
    r = _dot(merged, wout_ref[...])
    o_ref[...] = h_ref[...] + g2_ref[0:1, :] * r


def _mix_conv_prompt(z, dw_w, dw_b, ln_g, ln_b, attn, ga, gc, h, mod, w_ao, w_co, w_out, tm=256):
    rows = h.shape[0]
    n = rows // tm
    ratio = tm // CONV_HALO
    vec = pl.BlockSpec((1, C_CONV), lambda s: (0, 0))
    row = lambda w: pl.BlockSpec((tm, w), lambda s: (s, 0))
    resident = lambda shape: pl.BlockSpec(shape, lambda s: (0, 0), pipeline_mode=pl.Buffered(1))
    return pl.pallas_call(
        functools.partial(_mix_conv_kernel, tm=tm),
        out_shape=jax.ShapeDtypeStruct((rows, D_MODEL), F32),
        grid=(n,),
        in_specs=[row(C_CONV),
                  pl.BlockSpec((CONV_HALO, C_CONV), lambda s: (jnp.maximum(s * ratio - 1, 0), 0)),
                  pl.BlockSpec((CONV_WIDTH, C_CONV), lambda s: (0, 0)), vec, vec, vec,
                  row(ATTN_WIDTH), row(D_MODEL), row(D_MODEL), row(D_MODEL),
                  _mod_spec(5, False),
                  resident((ATTN_WIDTH, D_MODEL)), resident((C_CONV, D_MODEL)), resident((D_MODEL, D_MODEL))],
        out_specs=row(D_MODEL),
        scratch_shapes=[pltpu.VMEM((tm + CONV_HALO, C_CONV), F32), pltpu.VMEM((tm, C_CONV), F32),
                        pltpu.VMEM((tm, C_CONV), BF16)],
        compiler_params=_cparams(1),
        name="mix_conv",
    )(z, z, dw_w, dw_b, ln_g, ln_b, attn, ga, gc, h, mod, w_ao, w_co, w_out)


CONV_S_BB = 32


def _conv_s_kernel(z_ref, hist_ref, w_ref, b_ref, lg_ref, lb_ref, y_ref, ns_ref):
    for t in range(DEC_SEQ):
        acc = jnp.broadcast_to(b_ref[...], (CONV_S_BB, C_CONV))
        for j in range(CONV_WIDTH):
            i = t + j
            src = hist_ref[i] if i < HIST else z_ref[i - HIST]
            acc = acc + w_ref[j:j + 1, :] * src
        y_ref[t] = _ln_silu(acc, lg_ref[...], lb_ref[...]).astype(BF16)
    keep = HIST - DEC_SEQ
    ns_ref[0:keep] = hist_ref[DEC_SEQ:HIST]
    ns_ref[keep:HIST] = z_ref[...]


def _conv_sample(z, hist_t, dw_w, dw_b, ln_g, ln_b):
    vec = pl.BlockSpec((1, C_CONV), lambda i: (0, 0))
    tb = pl.BlockSpec((DEC_SEQ, CONV_S_BB, C_CONV), lambda i: (0, i, 0))
    st = pl.BlockSpec((HIST, CONV_S_BB, C_CONV), lambda i: (0, i, 0))
    return pl.pallas_call(
        _conv_s_kernel,
        out_shape=(jax.ShapeDtypeStruct((DEC_SEQ, DEC_BATCH, C_CONV), BF16),
                   jax.ShapeDtypeStruct(hist_t.shape, F32)),
        grid=(DEC_BATCH // CONV_S_BB,),
        in_specs=[tb, st, pl.BlockSpec((CONV_WIDTH, C_CONV), lambda i: (0, 0)), vec, vec, vec],
        out_specs=(tb, st),
        compiler_params=_cparams(1),
        name="conv_sample",
    )(z, hist_t, dw_w, dw_b, ln_g, ln_b)


MIX_TN = 512
MIX_NB = D_MODEL // MIX_TN


def _mix_s_kernel(at_ref, y_ref, ga_ref, gc_ref, h_ref, g2_ref, wao_ref, wco_ref, wout_ref,
                  o_ref, waoo_ref, wcoo_ref, wouto_ref, ab_ref, m_ref):
    j = pl.program_id(0)

    @pl.when(j == 0)
    def _():
        ab_ref[...] = at_ref[...].astype(BF16)

    @pl.when(j < MIX_NB)
    def _():
        waoo_ref[...] = wao_ref[...].astype(BF16)
        wcoo_ref[...] = wco_ref[...].astype(BF16)
        a = _dot(ab_ref[...], waoo_ref[...])
        c = _dot(y_ref[...], wcoo_ref[...])
        m_ref[j] = (ga_ref[...].astype(F32) * a + gc_ref[...].astype(F32) * c).astype(BF16)

    @pl.when(j >= MIX_NB)
    def _():
        wouto_ref[...] = wout_ref[...].astype(BF16)
        r = _dot(m_ref[0], wouto_ref[0:MIX_TN, :])
        for k in range(1, MIX_NB):
            r = r + _dot(m_ref[k], wouto_ref[k * MIX_TN:(k + 1) * MIX_TN, :])
        for s in range(DEC_SEQ):
            rows = slice(s * DEC_BATCH, (s + 1) * DEC_BATCH)
            o_ref[rows, :] = h_ref[rows, :] + g2_ref[...] * r[rows, :]


def _mix_sample(attn, y, ga, gc, h, mod, w_ao, w_co, w_out):
    col_a = lambda j: jnp.minimum(j, MIX_NB - 1)
    col_b = lambda j: jnp.maximum(j - MIX_NB, 0)
    whole = lambda w: pl.BlockSpec((S_ROWS, w), lambda j: (0, 0))
    blk = lambda rows, col: pl.BlockSpec((rows, MIX_TN), lambda j: (0, col(j)))
    mod_chunks = D_MODEL // MIX_TN
    return pl.pallas_call(
        _mix_s_kernel,
        out_shape=(jax.ShapeDtypeStruct((S_ROWS, D_MODEL), F32),
                   jax.ShapeDtypeStruct(w_ao.shape, BF16), jax.ShapeDtypeStruct(w_co.shape, BF16),
                   jax.ShapeDtypeStruct(w_out.shape, BF16)),
        grid=(2 * MIX_NB,),
        in_specs=[whole(ATTN_WIDTH), whole(C_CONV), blk(S_ROWS, col_a), blk(S_ROWS, col_a), blk(S_ROWS, col_b),
                  pl.BlockSpec((DEC_BATCH, MIX_TN), lambda j: (0, 5 * mod_chunks + col_b(j))),
                  blk(ATTN_WIDTH, col_a), blk(C_CONV, col_a), blk(D_MODEL, col_b)],
        out_specs=(blk(S_ROWS, col_b), blk(ATTN_WIDTH, col_a), blk(C_CONV, col_a), blk(D_MODEL, col_b)),
        scratch_shapes=[pltpu.VMEM((S_ROWS, ATTN_WIDTH), BF16), pltpu.VMEM((MIX_NB, S_ROWS, MIX_TN), BF16)],
        compiler_params=_cparams(1),
        name="mix_sample",
    )(attn, y, ga, gc, h, mod, w_ao, w_co, w_out)


def kernel(x_prompt, x_sample, cache_k, cache_v, state_conv, c_prompt, c_sample, rel_bias_table, norm1_g, ffn1_w1, ffn1_w3, ffn1_w2, norm2_g, w_in, attn_sinks, conv_dw_w, conv_dw_b, conv_ln_g, conv_ln_b, w_conv_out, w_attn_out, w_out, norm3_g, ffn2_w1, ffn2_w3, ffn2_w2, w_ada, b_ada, final_norm_g, w_ada_final, b_ada_final):
    c_all = jnp.concatenate([c_sample, c_prompt, jnp.zeros((N_COND - DEC_BATCH - 1, D_MODEL), F32)], axis=0)
    mod = _adaln(c_all, w_ada[0], b_ada)
    mod_f = _adaln(c_all, w_ada_final, b_ada_final[None, :])

    sinks = attn_sinks[0]
    tab_flat = rel_bias_table.reshape(-1)
    tab_ext = jnp.concatenate([rel_bias_table, sinks[None, :]], axis=0)
    th = tab_ext.T.reshape(N_KV_HEADS, GROUP, N_BUCKETS + 1).transpose(1, 0, 2)
    th = jnp.tile(th.reshape(N_HEADS, N_BUCKETS + 1), (DEC_SEQ, 1))

    xp = x_prompt[0]
    xs = x_sample.transpose(1, 0, 2).reshape(S_ROWS, D_MODEL)

    conv_w = (conv_dw_w[0], conv_dw_b, conv_ln_g, conv_ln_b)

    final = (final_norm_g[None, :], mod_f)

    def keys_on_lanes(cache):
        return cache[0].transpose(0, 2, 3, 1).reshape(DEC_BATCH, KV_WIDTH, WBUF)

    def keys_on_rows(win):
        return win.reshape(DEC_BATCH, N_KV_HEADS, HEAD_DIM, WBUF).transpose(0, 3, 1, 2)[None]

    def mixers_p(q, kv, z, ga, gc, h, mix_w):
        attn = _attn_prompt(q, kv, tab_flat, sinks)
        return _mix_conv_prompt(z, *conv_w, attn, ga, gc, h, mod, *mix_w)

    def mixers_s(q, kv, z, ga, gc, h):
        o, kw, vw = _attn_sample(q.reshape(DEC_SEQ, DEC_BATCH, ATTN_WIDTH),
                                 kv.reshape(DEC_SEQ, DEC_BATCH, 2 * KV_WIDTH),
                                 keys_on_lanes(cache_k), keys_on_lanes(cache_v), th)
        y, ns = _conv_sample(z.reshape(DEC_SEQ, DEC_BATCH, C_CONV), state_conv[0].transpose(1, 0, 2), *conv_w)
        h, *mix_w = _mix_sample(o.reshape(S_ROWS, ATTN_WIDTH), y.reshape(S_ROWS, C_CONV), ga, gc, h, mod,
                                w_attn_out[0], w_conv_out[0], w_out[0])
        return h, (kw, vw, ns), mix_w

    hs, f1 = _ffn(xs, norm1_g, mod, 0, ffn1_w1[0], ffn1_w3[0], ffn1_w2[0], per_row=True,
                  tm=S_ROWS, tf=FFN_TF_CAST, cast_out=True)
    hp = _ffn(xp, norm1_g, mod, 0, *f1, per_row=False, tm=FFN_TM, tf=FFN_TF)

    qs, kv_s, z_s, ga_s, gc_s, w_a, w_b = _proj(hs, norm2_g, mod, w_in[0], w_in[0], per_row=True,
                                                tm=S_ROWS, q_dtype=F32, cast_out=True)
    qp, kv_p, z_p, ga_p, gc_p = _proj(hp, norm2_g, mod, w_a, w_b, per_row=False, tm=PROJ_TM, q_dtype=BF16)

    hs, (kw, vw, ns), mix_w = mixers_s(qs, kv_s, z_s, ga_s, gc_s, hs)
    hp = mixers_p(qp, kv_p, z_p, ga_p, gc_p, hp, mix_w)

    ys, f2 = _ffn(hs, norm3_g, mod, 6, ffn2_w1[0], ffn2_w3[0], ffn2_w2[0], per_row=True,
                  tm=S_ROWS, tf=FFN_TF_CAST, final=final, cast_out=True)
    yp = _ffn(hp, norm3_g, mod, 6, *f2, per_row=False, tm=FFN_TM, tf=FFN_TF, final=final)

    w = min(WINDOW, SEQ)
    kv_shape = (1, 1, w, N_KV_HEADS, HEAD_DIM)
    k_win_p = kv_p[SEQ - w:, :KV_WIDTH].reshape(kv_shape)
    v_win_p = kv_p[SEQ - w:, KV_WIDTH:].reshape(kv_shape)
    conv_p_state = z_p[SEQ - HIST:].reshape(1, 1, HIST, C_CONV)
    s_shape = (1, DEC_BATCH, WBUF, N_KV_HEADS, HEAD_DIM)
    y_prompt = yp[None]
    y_sample = ys.reshape(DEC_SEQ, DEC_BATCH, D_MODEL).transpose(1, 0, 2)
    return (y_prompt, y_sample, k_win_p, v_win_p, conv_p_state,
            keys_on_rows(kw), keys_on_rows(vw), ns.transpose(1, 0, 2)[None])
```

```python
import functools

import numpy as np
import jax
import jax.numpy as jnp
from jax import lax
from jax.experimental import pallas as pl
from jax.experimental.pallas import tpu as pltpu

D_MODEL = 2048
SEQ = 8192
DEC_BATCH = 128
DEC_SEQ = 4
PAST_LEN = 16384
N_HEADS = 32
N_KV_HEADS = 8
HEAD_DIM = 64
GROUP = N_HEADS // N_KV_HEADS
ATTN_WIDTH = N_HEADS * HEAD_DIM
KV_WIDTH = N_KV_HEADS * HEAD_DIM
WINDOW = 128
Q_BLOCK = 128
ATTN_SCALE = HEAD_DIM ** -0.5
N_BUCKETS = 32
MAX_DISTANCE = 128
C_CONV = D_MODEL // 2
CONV_WIDTH = 31
HIST = CONV_WIDTH - 1
D_FF = 5632
NORM_EPS = 1e-6
NEG_INF = -1e30
IN_WIDTH = ATTN_WIDTH + 2 * KV_WIDTH + 2 * C_CONV + 2 * D_MODEL
WBUF = min(WINDOW, PAST_LEN)

S_ROWS = DEC_BATCH * DEC_SEQ
N_COND = 136
PROMPT_MOD_BLOCK = DEC_BATCH // 8
S_EXTRA = 8

VMEM_LIMIT = 60 * 1024 * 1024
SUBLANES = 8
LANES = 128

F32 = jnp.float32
BF16 = jnp.bfloat16


def _cparams(n_axes, flags=None):
    return pltpu.CompilerParams(dimension_semantics=("arbitrary",) * n_axes,
                                vmem_limit_bytes=VMEM_LIMIT, flags=flags)


def _dot(a, b):
    return jnp.dot(a, b, preferred_element_type=F32)


def _rms_mod(x, g, sh, sc):
    ms = jnp.mean(x * x, axis=-1, keepdims=True)
    y = x * lax.rsqrt(ms + NORM_EPS) * g
    return y * (1.0 + sc) + sh


def _mod_val(ref, per_row):
    return ref[...] if per_row else ref[0:1, :]


def _mod_spec(chunk, per_row, width=D_MODEL):
    if per_row:
        return pl.BlockSpec((DEC_BATCH, width), lambda *_: (0, chunk))
    return pl.BlockSpec((8, width), lambda *_: (PROMPT_MOD_BLOCK, chunk))


def _adaln_kernel(c_ref, w_ref, b_ref, o_ref, s_ref):
    @pl.when(pl.program_id(0) == 0)
    def _():
        s_ref[...] = jax.nn.silu(c_ref[...]).astype(BF16)

    o_ref[...] = _dot(s_ref[...], w_ref[...].astype(BF16)) + b_ref[...]


def _adaln(c_all, w, b, tn=1024):
    n = w.shape[1]
    return pl.pallas_call(
        _adaln_kernel,
        out_shape=jax.ShapeDtypeStruct((N_COND, n), F32),
        grid=(n // tn,),
        in_specs=[pl.BlockSpec((N_COND, D_MODEL), lambda j: (0, 0)),
                  pl.BlockSpec((D_MODEL, tn), lambda j: (0, j)),
                  pl.BlockSpec((1, tn), lambda j: (0, j))],
        out_specs=pl.BlockSpec((N_COND, tn), lambda j: (0, j)),
        scratch_shapes=[pltpu.VMEM((N_COND, D_MODEL), BF16)],
        compiler_params=_cparams(1),
        name="adaln",
    )(c_all, w, b)


FFN_ROWS = 512
FFN_TM = 1024
FFN_TF = 512
FFN_TF_CAST = 512
PROJ_TM = 1024


def _ffn_kernel(*refs, tm, sub, per_row, n_f, final, cast_out):
    refs = list(refs)
    x_ref, n_ref, sh_ref, sc_ref, g_ref, w1_ref, w3_ref, w2_ref = refs[:8]
    del refs[:8]
    if final:
        gf_ref, shf_ref, scf_ref = refs[:3]
        del refs[:3]
    o_ref = refs.pop(0)
    if cast_out:
        w1o_ref, w3o_ref, w2o_ref = refs[:3]
        del refs[:3]
    (u_ref,) = refs
    f = pl.program_id(1)

    if cast_out:
        w1o_ref[...] = w1_ref[...].astype(BF16)
        w3o_ref[...] = w3_ref[...].astype(BF16)
        w2o_ref[...] = w2_ref[...].astype(BF16)
        w1_ref, w3_ref, w2_ref = w1o_ref, w3o_ref, w2o_ref

    def accumulate(first):
        for s in range(tm // FFN_ROWS):
            rows = slice(s * FFN_ROWS, (s + 1) * FFN_ROWS)
            u = u_ref[rows, :]
            h1 = _dot(u, w1_ref[...])
            h3 = _dot(u, w3_ref[...])
            a = (jax.nn.silu(h1) * h3).astype(BF16)
            d = _dot(a, w2_ref[...])
            if first:
                o_ref[rows, :] = d
            else:
                o_ref[rows, :] += d

    @pl.when(f == 0)
    def _():
        def body(s, carry):
            rows = pl.ds(pl.multiple_of(s * sub, sub), sub)
            u_ref[rows, :] = _rms_mod(x_ref[rows, :], n_ref[...], _mod_val(sh_ref, per_row),
                                      _mod_val(sc_ref, per_row)).astype(BF16)
            return carry
        lax.fori_loop(0, tm // sub, body, 0)
        accumulate(True)

    @pl.when(f > 0)
    def _():
        accumulate(False)

    @pl.when(f == n_f - 1)
    def _():
        def body(s, carry):
            rows = pl.ds(pl.multiple_of(s * sub, sub), sub)
            h = x_ref[rows, :] + 0.5 * _mod_val(g_ref, per_row) * o_ref[rows, :]
            if final:
                h = _rms_mod(h, gf_ref[...], _mod_val(shf_ref, per_row), _mod_val(scf_ref, per_row))
            o_ref[rows, :] = h
            return carry
        lax.fori_loop(0, tm // sub, body, 0)


def _ffn(x, norm_g, mod, chunk0, w1, w3, w2, *, per_row, tm, tf, final=None, cast_out=False):
    rows = x.shape[0]
    n_f = D_FF // tf
    sub = DEC_BATCH
    row_mode = dict(pipeline_mode=pl.Buffered(1)) if rows == tm else {}
    row_spec = pl.BlockSpec((tm, D_MODEL), lambda i, f: (i, 0), **row_mode)
    vec_spec = pl.BlockSpec((1, D_MODEL), lambda i, f: (0, 0))
    w_specs = [pl.BlockSpec((D_MODEL, tf), lambda i, f: (0, f)),
               pl.BlockSpec((D_MODEL, tf), lambda i, f: (0, f)),
               pl.BlockSpec((tf, D_MODEL), lambda i, f: (f, 0))]
    in_specs = [row_spec, vec_spec,
                _mod_spec(chunk0, per_row), _mod_spec(chunk0 + 1, per_row), _mod_spec(chunk0 + 2, per_row)]
    in_specs += w_specs
    args = [x, norm_g, mod, mod, mod, w1, w3, w2]
    if final is not None:
        gf, mod_f = final
        in_specs += [vec_spec, _mod_spec(0, per_row), _mod_spec(1, per_row)]
        args += [gf, mod_f, mod_f]
    out_shape = [jax.ShapeDtypeStruct((rows, D_MODEL), F32)]
    out_specs = [row_spec]
    if cast_out:
        assert rows == tm, "every weight block must be visited exactly once"
        out_shape += [jax.ShapeDtypeStruct(w.shape, BF16) for w in (w1, w3, w2)]
        out_specs += w_specs
    outs = pl.pallas_call(
        functools.partial(_ffn_kernel, tm=tm, sub=sub, per_row=per_row, n_f=n_f,
                          final=final is not None, cast_out=cast_out),
        out_shape=out_shape,
        grid=(rows // tm, n_f),
        in_specs=in_specs,
        out_specs=out_specs,
        scratch_shapes=[pltpu.VMEM((tm, D_MODEL), BF16)],
        compiler_params=_cparams(2),
        name="ffn_final" if final is not None else "ffn",
    )(*args)
    return (outs[0], tuple(outs[1:])) if cast_out else outs[0]


PROJ_CHUNK = 256
PROJ_CW = 512


def _proj_layout():
    cw = PROJ_CW
    n_q = ATTN_WIDTH // (2 * cw)
    n_kv = 2 * KV_WIDTH // (2 * cw)
    n_glu = C_CONV // cw
    n_gate = D_MODEL // (2 * cw)
    pre = n_q + n_kv
    steps = pre + n_glu + 2 * n_gate

    def block_a(j):
        return jnp.where((j >= pre) & (j < pre + n_glu), j + pre, 2 * j)

    def block_b(j):
        return jnp.where((j >= pre) & (j < pre + n_glu), j + pre + n_glu, 2 * j + 1)

    return n_q, n_kv, n_glu, n_gate, steps, block_a, block_b


def _proj_kernel(*refs, tm, sub, per_row, cast_out):
    refs = list(refs)
    h_ref, n_ref, sh_ref, sc_ref, wa_ref, wb_ref, q_ref, kv_ref, z_ref, ga_ref, gc_ref = refs[:11]
    del refs[:11]
    if cast_out:
        wao_ref, wbo_ref = refs[:2]
        del refs[:2]
    (u_ref,) = refs
    cw = PROJ_CW
    n_q, n_kv, n_glu, n_gate, _, _, _ = _proj_layout()
    j = pl.program_id(1)

    @pl.when(j == 0)
    def _():
        def body(s, carry):
            rows = pl.ds(pl.multiple_of(s * sub, sub), sub)
            u_ref[rows, :] = _rms_mod(h_ref[rows, :], n_ref[...], _mod_val(sh_ref, per_row),
                                      _mod_val(sc_ref, per_row)).astype(BF16)
            return carry
        lax.fori_loop(0, tm // sub, body, 0)

    if cast_out:
        wao_ref[...] = wa_ref[...].astype(BF16)
        wbo_ref[...] = wb_ref[...].astype(BF16)
        wa_ref, wb_ref = wao_ref, wbo_ref

    n_chunks = cw // PROJ_CHUNK

    def chunks(w_ref):
        for c in range(n_chunks):
            cols = slice(c * PROJ_CHUNK, (c + 1) * PROJ_CHUNK)
            yield c * PROJ_CHUNK, _dot(u_ref[...], w_ref[:, cols])

    def pair_store(o_ref, fn):
        for half, w_ref in enumerate((wa_ref, wb_ref)):
            for c0, acc in chunks(w_ref):
                o_ref[:, half * cw + c0:half * cw + c0 + PROJ_CHUNK] = fn(acc).astype(o_ref.dtype)

    lo = 0

    @pl.when(j < n_q)
    def _():
        pair_store(q_ref, lambda acc: acc)

    lo += n_q

    @pl.when((j >= lo) & (j < lo + n_kv))
    def _():
        pair_store(kv_ref, lambda acc: acc)

    lo += n_kv

    @pl.when((j >= lo) & (j < lo + n_glu))
    def _():
        for (c0, a), (_, g) in zip(chunks(wa_ref), chunks(wb_ref)):
            z_ref[:, c0:c0 + PROJ_CHUNK] = a * jax.nn.sigmoid(g)

    lo += n_glu

    @pl.when((j >= lo) & (j < lo + n_gate))
    def _():
        pair_store(ga_ref, jax.nn.sigmoid)

    lo += n_gate

    @pl.when(j >= lo)
    def _():
        pair_store(gc_ref, jax.nn.sigmoid)


def _proj(h, norm_g, mod, w_a, w_b, *, per_row, tm, q_dtype, cast_out=False):
    rows = h.shape[0]
    sub = DEC_BATCH
    cw = PROJ_CW
    n_q, n_kv, n_glu, n_gate, steps, block_a, block_b = _proj_layout()

    def out_spec(width, first, count):
        return pl.BlockSpec((tm, width), lambda i, j: (i, jnp.clip(j - first, 0, count - 1)))

    step_spec = pl.BlockSpec((D_MODEL, cw), lambda i, j: (0, j))
    if cast_out:
        wa_spec = pl.BlockSpec((D_MODEL, cw), lambda i, j: (0, block_a(j)))
        wb_spec = pl.BlockSpec((D_MODEL, cw), lambda i, j: (0, block_b(j)))
    else:
        wa_spec = wb_spec = step_spec
    out_shape = [jax.ShapeDtypeStruct((rows, ATTN_WIDTH), q_dtype),
                 jax.ShapeDtypeStruct((rows, 2 * KV_WIDTH), F32),
                 jax.ShapeDtypeStruct((rows, C_CONV), F32),
                 jax.ShapeDtypeStruct((rows, D_MODEL), BF16),
                 jax.ShapeDtypeStruct((rows, D_MODEL), BF16)]
    out_specs = [out_spec(2 * cw, 0, n_q),
                 out_spec(2 * cw, n_q, n_kv),
                 out_spec(cw, n_q + n_kv, n_glu),
                 out_spec(2 * cw, n_q + n_kv + n_glu, n_gate),
                 out_spec(2 * cw, n_q + n_kv + n_glu + n_gate, n_gate)]
    if cast_out:
        assert rows == tm, "every weight block must be visited exactly once"
        out_shape += [jax.ShapeDtypeStruct((D_MODEL, steps * cw), BF16)] * 2
        out_specs += [step_spec, step_spec]
    return pl.pallas_call(
        functools.partial(_proj_kernel, tm=tm, sub=sub, per_row=per_row, cast_out=cast_out),
        out_shape=out_shape,
        grid=(rows // tm, steps),
        in_specs=[pl.BlockSpec((tm, D_MODEL), lambda i, j: (i, 0)),
                  pl.BlockSpec((1, D_MODEL), lambda i, j: (0, 0)),
                  _mod_spec(3, per_row), _mod_spec(4, per_row), wa_spec, wb_spec],
        out_specs=out_specs,
        scratch_shapes=[pltpu.VMEM((tm, D_MODEL), BF16)],
        compiler_params=_cparams(2),
        name="proj",
    )(h, norm_g, mod, mod, w_a, w_b)


def _t5_bucket_np(dist):
    exact = N_BUCKETS // 2
    d = np.maximum(dist, 0)
    df = np.maximum(d, 1).astype(np.float32)
    large = exact + (np.log(df / np.float32(exact)) / np.float32(np.log(MAX_DISTANCE / exact))
                     * np.float32(N_BUCKETS - exact)).astype(np.int32)
    large = np.minimum(large, N_BUCKETS - 1)
    return np.where(d < exact, d, large).astype(np.int32)


def _prompt_codes():
    qi = np.arange(Q_BLOCK)[:, None]
    kj = np.arange(2 * Q_BLOCK)[None, :]
    dist = qi + Q_BLOCK - kj
    valid = (dist >= 0) & (dist <= WINDOW)
    return np.where(valid, _t5_bucket_np(dist), -1).astype(np.int32)


def _sample_codes():
    t = np.repeat(np.arange(DEC_SEQ), GROUP * N_KV_HEADS)[:, None]
    j = np.arange(WBUF)[None, :]
    dist = t + WBUF - j
    cache = np.where((dist >= 0) & (dist <= WINDOW), _t5_bucket_np(dist), -1)
    c = np.arange(S_EXTRA)[None, :]
    t_new = c - (S_EXTRA - DEC_SEQ)
    dist = t - t_new
    extra = np.where((t_new >= 0) & (dist >= 0) & (dist <= WINDOW), _t5_bucket_np(dist), -1)
    extra = np.where(c == 0, N_BUCKETS, extra)
    return cache.astype(np.int32), extra.astype(np.int32)


BIAS_ROWS = 64
ATTN_QB = 2


def _attn_p_kernel(tab_ref, sink_ref, code_ref, q_ref, kvc_ref, kvp_ref, o_ref, bias_ref):
    n = pl.program_id(0)
    n_keys = 2 * Q_BLOCK
    wide = GROUP * Q_BLOCK

    @pl.when(n == 0)
    def _():
        def kv_body(kv, carry):
            for g in range(GROUP):
                for r in range(n_keys // BIAS_ROWS):
                    rows = slice(r * BIAS_ROWS, (r + 1) * BIAS_ROWS)
                    code = code_ref[rows, :]
                    acc = jnp.full((BIAS_ROWS, Q_BLOCK), NEG_INF, F32)
                    for bk in range(N_BUCKETS):
                        acc = jnp.where(code == bk, tab_ref[bk * N_HEADS + kv * GROUP + g], acc)
                    bias_ref[kv, rows, g * Q_BLOCK:(g + 1) * Q_BLOCK] = acc
            return carry
        lax.fori_loop(0, N_KV_HEADS, kv_body, 0)

    ones = jnp.ones((16, n_keys), BF16)
    zeros = jnp.zeros((HEAD_DIM, wide), BF16)
    tile = (n_keys, LANES)
    key = lax.broadcasted_iota(jnp.int32, tile, 0)
    lane = lax.broadcasted_iota(jnp.int32, tile, 1)
    pen = jnp.where((lane == 0) & (key < Q_BLOCK) & (n == 0), NEG_INF, 0.0).astype(BF16)
    pen_rows = (lax.broadcasted_iota(jnp.int32, (LANES, wide), 0) == 0).astype(BF16)

    def query_block(a):
        rows = slice(a * Q_BLOCK, (a + 1) * Q_BLOCK)
        q_t = q_ref[rows, :].astype(F32).T

        def keys(cols):
            prev = kvp_ref[:, cols] if a == 0 else kvc_ref[(a - 1) * Q_BLOCK:a * Q_BLOCK, cols]
            return jnp.concatenate([prev, kvc_ref[rows, cols]], axis=0)

        def pair_kv(pair):
            k2 = keys(slice(pair * LANES, (pair + 1) * LANES)).astype(BF16)
            if a == 0:
                k2 = jnp.concatenate([k2, pen], axis=1)
            v2_t = keys(slice(KV_WIDTH + pair * LANES, KV_WIDTH + (pair + 1) * LANES)).T.astype(BF16)
            return k2, v2_t

        def scores(kv, k2):
            q4 = jnp.concatenate(
                [q_t[(kv * GROUP + g) * HEAD_DIM:(kv * GROUP + g + 1) * HEAD_DIM, :] for g in range(GROUP)],
                axis=1)
            q4 = (q4 * ATTN_SCALE).astype(BF16)
            parts = [q4, zeros] if kv % 2 == 0 else [zeros, q4]
            return _dot(k2, jnp.concatenate(parts + ([pen_rows] if a == 0 else []), axis=0))

        kvs = [pair_kv(pair) for pair in range(N_KV_HEADS // 2)]
        for kv in range(N_KV_HEADS):
            pair, parity = divmod(kv, 2)
            s = scores(kv, kvs[pair][0]) + bias_ref[kv]
            sink = jnp.concatenate(
                [jnp.full((1, Q_BLOCK), sink_ref[kv * GROUP + g], F32) for g in range(GROUP)], axis=1)
            m = jnp.maximum(jnp.max(s, axis=0, keepdims=True), sink)
            p = jnp.exp(s - m).astype(BF16)
            lhs = jnp.concatenate([kvs[pair][1][parity * HEAD_DIM:(parity + 1) * HEAD_DIM, :], ones], axis=0)
            o_t = _dot(lhs, p)
            l = o_t[HEAD_DIM:HEAD_DIM + 1, :] + jnp.exp(sink - m)
            o_n = o_t[0:HEAD_DIM, :] * (1.0 / l)
            o_kv = jnp.concatenate([o_n[:, g * Q_BLOCK:(g + 1) * Q_BLOCK] for g in range(GROUP)], axis=0)
            o_ref[rows, kv * GROUP * HEAD_DIM:(kv + 1) * GROUP * HEAD_DIM] = o_kv.T.astype(o_ref.dtype)

    for a in range(ATTN_QB):
        query_block(a)


def _attn_prompt(q, kv, tab_flat, sinks):
    step_rows = ATTN_QB * Q_BLOCK
    code_t = jnp.asarray(np.ascontiguousarray(_prompt_codes().T))
    smem = pl.BlockSpec(memory_space=pltpu.SMEM)
    return pl.pallas_call(
        _attn_p_kernel,
        out_shape=jax.ShapeDtypeStruct((SEQ, ATTN_WIDTH), BF16),
        grid=(SEQ // step_rows,),
        in_specs=[smem, smem,
                  pl.BlockSpec((2 * Q_BLOCK, Q_BLOCK), lambda n: (0, 0)),
                  pl.BlockSpec((step_rows, ATTN_WIDTH), lambda n: (n, 0)),
                  pl.BlockSpec((step_rows, 2 * KV_WIDTH), lambda n: (n, 0)),
                  pl.BlockSpec((Q_BLOCK, 2 * KV_WIDTH), lambda n: (jnp.maximum(ATTN_QB * n - 1, 0), 0))],
        out_specs=pl.BlockSpec((step_rows, ATTN_WIDTH), lambda n: (n, 0)),
        scratch_shapes=[pltpu.VMEM((N_KV_HEADS, 2 * Q_BLOCK, GROUP * Q_BLOCK), F32)],
        compiler_params=_cparams(1),
        name="attn_prompt",
    )(tab_flat, sinks, code_t, q, kv, kv)


S_BB = 8


def _regroup_heads(x, to_group_major):
    half = HEAD_DIM
    assert LANES == 2 * half
    low = lax.broadcasted_iota(jnp.int32, (x.shape[0], LANES), 1) < half
    n_cols = ATTN_WIDTH // LANES
    outer, inner = (GROUP, N_KV_HEADS) if to_group_major else (N_KV_HEADS, GROUP)
    cols = []
    for c in range(n_cols):
        o, i = divmod(2 * c, inner)
        src = [(i + d) * outer + o for d in range(2)]
        a = x[:, (src[0] // 2) * LANES:(src[0] // 2 + 1) * LANES]
        b = x[:, (src[1] // 2) * LANES:(src[1] // 2 + 1) * LANES]
        assert src[0] % 2 == src[1] % 2
        if src[0] % 2 == 0:
            cols.append(jnp.where(low, a, pltpu.roll(b, half, axis=1)))
        else:
            cols.append(jnp.where(low, pltpu.roll(a, half, axis=1), b))
    return jnp.concatenate(cols, axis=1)


def _attn_s_kernel(th_ref, code_c_ref, code_x_ref, q_ref, kv_ref, ck_ref, cv_ref, o_ref, kw_ref, vw_ref,
                   bias_c_ref, bias_x_ref, nk_ref, nv_ref, qe_ref, qg_ref, og_ref):
    rows_all = DEC_SEQ * N_HEADS
    for t in range(DEC_SEQ):
        qg_ref[t] = _regroup_heads(q_ref[t], True)

    @pl.when(pl.program_id(0) == 0)
    def _():
        for code_ref, bias_ref in ((code_c_ref, bias_c_ref), (code_x_ref, bias_x_ref)):
            code = code_ref[...]
            acc = jnp.full(code.shape, NEG_INF, F32)
            for bk in range(N_BUCKETS + 1):
                acc = jnp.where(code == bk, th_ref[:, bk:bk + 1], acc)
            bias_ref[...] = acc
        nk_ref[...] = jnp.zeros_like(nk_ref)
        nv_ref[...] = jnp.zeros_like(nv_ref)

    lane_head = lax.broadcasted_iota(jnp.int32, (N_KV_HEADS, KV_WIDTH), 1) // HEAD_DIM
    row_head = lax.broadcasted_iota(jnp.int32, (N_KV_HEADS, KV_WIDTH), 0)
    diag = lane_head == row_head
    out_mask = (lax.broadcasted_iota(jnp.int32, (rows_all, KV_WIDTH), 0) % N_KV_HEADS
                == lax.broadcasted_iota(jnp.int32, (rows_all, KV_WIDTH), 1) // HEAD_DIM)
    new_lanes = lax.broadcasted_iota(jnp.int32, (KV_WIDTH, WBUF), 1) >= WBUF - DEC_SEQ
    bias_c = bias_c_ref[...]
    bias_x = bias_x_ref[...]
    nt_dims = (((1,), (1,)), ((), ()))
    x0 = WBUF - S_EXTRA

    for b in range(S_BB):
        for t in range(DEC_SEQ):
            nk_ref[WBUF - DEC_SEQ + t:WBUF - DEC_SEQ + t + 1, :] = kv_ref[t, b:b + 1, 0:KV_WIDTH]
            nv_ref[WBUF - DEC_SEQ + t:WBUF - DEC_SEQ + t + 1, :] = kv_ref[t, b:b + 1, KV_WIDTH:2 * KV_WIDTH]
            for g in range(GROUP):
                q_row = qg_ref[t, b:b + 1, g * KV_WIDTH:(g + 1) * KV_WIDTH]
                piece = jnp.where(diag, jnp.broadcast_to(q_row, (N_KV_HEADS, KV_WIDTH)), 0.0)
                r0 = (t * GROUP + g) * N_KV_HEADS
                qe_ref[r0:r0 + N_KV_HEADS, :] = piece
        k_t = ck_ref[b]
        v_t = cv_ref[b]
        kw_ref[b] = jnp.where(new_lanes, nk_ref[...].T, pltpu.roll(k_t, WBUF - DEC_SEQ, axis=1))
        vw_ref[b] = jnp.where(new_lanes, nv_ref[...].T, pltpu.roll(v_t, WBUF - DEC_SEQ, axis=1))

        qe = (qe_ref[...] * ATTN_SCALE).astype(BF16)
        s_c = _dot(qe, k_t.astype(BF16)) + bias_c
        s_x = lax.dot_general(qe, nk_ref[x0:WBUF, :].astype(BF16), nt_dims,
                              preferred_element_type=F32) + bias_x
        m = jnp.maximum(jnp.max(s_c, axis=-1, keepdims=True), jnp.max(s_x, axis=-1, keepdims=True))
        p_c = jnp.exp(s_c - m)
        p_x = jnp.exp(s_x - m)
        l = jnp.sum(p_c, axis=-1, keepdims=True) + jnp.sum(p_x, axis=-1, keepdims=True)
        o = lax.dot_general(p_c.astype(BF16), v_t.astype(BF16), nt_dims, preferred_element_type=F32)
        o = (o + _dot(p_x.astype(BF16), nv_ref[x0:WBUF, :].astype(BF16))) / l
        o = jnp.where(out_mask, o, 0.0)
        for t in range(DEC_SEQ):
            for g in range(GROUP):
                r0 = (t * GROUP + g) * N_KV_HEADS
                row = jnp.sum(o[r0:r0 + N_KV_HEADS, :], axis=0, keepdims=True)
                og_ref[t, b:b + 1, g * KV_WIDTH:(g + 1) * KV_WIDTH] = row
    for t in range(DEC_SEQ):
        o_ref[t] = _regroup_heads(og_ref[t], False)


def _attn_sample(q, kv, cache_k_t, cache_v_t, th):
    code_c, code_x = (jnp.asarray(c) for c in _sample_codes())
    rows_all = DEC_SEQ * N_HEADS
    full = lambda a: pl.BlockSpec(a.shape, lambda i: (0,) * a.ndim)
    tb_spec = lambda w: pl.BlockSpec((DEC_SEQ, S_BB, w), lambda i: (0, i, 0))
    cache_spec = pl.BlockSpec((S_BB, KV_WIDTH, WBUF), lambda i: (i, 0, 0))
    cache_shape = jax.ShapeDtypeStruct((DEC_BATCH, KV_WIDTH, WBUF), F32)
    return pl.pallas_call(
        _attn_s_kernel,
        out_shape=(jax.ShapeDtypeStruct((DEC_SEQ, DEC_BATCH, ATTN_WIDTH), F32), cache_shape, cache_shape),
        grid=(DEC_BATCH // S_BB,),
        in_specs=[full(th), full(code_c), full(code_x),
                  tb_spec(ATTN_WIDTH), tb_spec(2 * KV_WIDTH), cache_spec, cache_spec],
        out_specs=(tb_spec(ATTN_WIDTH), cache_spec, cache_spec),
        scratch_shapes=[pltpu.VMEM((rows_all, WBUF), F32),
                        pltpu.VMEM((rows_all, S_EXTRA), F32),
                        pltpu.VMEM((WBUF, KV_WIDTH), F32),
                        pltpu.VMEM((WBUF, KV_WIDTH), F32),
                        pltpu.VMEM((rows_all, KV_WIDTH), F32),
                        pltpu.VMEM((DEC_SEQ, S_BB, ATTN_WIDTH), F32),
                        pltpu.VMEM((DEC_SEQ, S_BB, ATTN_WIDTH), F32)],
        compiler_params=_cparams(1),
        name="attn_sample",
    )(th, code_c, code_x, q, kv, cache_k_t, cache_v_t)


CONV_RC = 32
CONV_ROWS = 128
CONV_HALO = 32


def _ln_silu(acc, lg, lb):
    mu = jnp.mean(acc, axis=-1, keepdims=True)
    xc = acc - mu
    var = jnp.mean(xc * xc, axis=-1, keepdims=True)
    return jax.nn.silu(xc * lax.rsqrt(var + NORM_EPS) * lg + lb)


def _conv_stage(first, zc_ref, zh_ref, s_ref):
    s_ref[0:CONV_HALO, :] = jnp.where(first, 0.0, zh_ref[...])
    s_ref[CONV_HALO:, :] = zc_ref[...]


def _conv_rows(rc, w_ref, b_ref, lg_ref, lb_ref, y_ref, s_ref, c_ref):
    off = CONV_HALO - HIST
    groups = [[j for j in range(CONV_WIDTH) if (j + off) % SUBLANES == r] for r in range(SUBLANES)]
    t0 = rc * CONV_ROWS
    for lc in range(C_CONV // LANES):
        lanes = slice(lc * LANES, (lc + 1) * LANES)
        out = jnp.broadcast_to(b_ref[:, lanes], (CONV_ROWS, LANES))
        for r, taps in enumerate(groups):
            n_rows = CONV_ROWS + (SUBLANES if r else 0)
            part = None
            for j in taps:
                base = t0 + (j + off) - r
                term = w_ref[j:j + 1, lanes] * s_ref[base:base + n_rows, lanes]
                part = term if part is None else part + term
            out = out + part[r:r + CONV_ROWS, :]
        c_ref[t0:t0 + CONV_ROWS, lanes] = out
    for r in range(CONV_ROWS // CONV_RC):
        rows = slice(t0 + r * CONV_RC, t0 + (r + 1) * CONV_RC)
        y_ref[rows, :] = _ln_silu(c_ref[rows, :], lg_ref[...], lb_ref[...]).astype(BF16)


def _mix_conv_kernel(zc_ref, zh_ref, w_ref, b_ref, lg_ref, lb_ref,
                     at_ref, ga_ref, gc_ref, h_ref, g2_ref, wao_ref, wco_ref, wout_ref,
                     o_ref, s_ref, c_ref, y_ref, *, tm):
    _conv_stage(pl.program_id(0) == 0, zc_ref, zh_ref, s_ref)
    for rc in range(tm // CONV_ROWS):
        _conv_rows(rc, w_ref, b_ref, lg_ref, lb_ref, y_ref, s_ref, c_ref)
    a = _dot(at_ref[...], wao_ref[...])
    c = _dot(y_ref[...], wco_ref[...])
    merged = (ga_ref[...].astype(F32) * a + gc_ref[...].astype(F32) * c).astype(BF16)
    r = _dot(merged, wout_ref[...])
    o_ref[...] = h_ref[...] + g2_ref[0:1, :] * r


def _mix_conv_prompt(z, dw_w, dw_b, ln_g, ln_b, attn, ga, gc, h, mod, w_ao, w_co, w_out, tm=256):
    rows = h.shape[0]
    n = rows // tm
    ratio = tm // CONV_HALO
    vec = pl.BlockSpec((1, C_CONV), lambda s: (0, 0))
    row = lambda w: pl.BlockSpec((tm, w), lambda s: (s, 0))
    resident = lambda shape: pl.BlockSpec(shape, lambda s: (0, 0), pipeline_mode=pl.Buffered(1))
    return pl.pallas_call(
        functools.partial(_mix_conv_kernel, tm=tm),
        out_shape=jax.ShapeDtypeStruct((rows, D_MODEL), F32),
        grid=(n,),
        in_specs=[row(C_CONV),
                  pl.BlockSpec((CONV_HALO, C_CONV), lambda s: (jnp.maximum(s * ratio - 1, 0), 0)),
                  pl.BlockSpec((CONV_WIDTH, C_CONV), lambda s: (0, 0)), vec, vec, vec,
                  row(ATTN_WIDTH), row(D_MODEL), row(D_MODEL), row(D_MODEL),
                  _mod_spec(5, False),
                  resident((ATTN_WIDTH, D_MODEL)), resident((C_CONV, D_MODEL)), resident((D_MODEL, D_MODEL))],
        out_specs=row(D_MODEL),
        scratch_shapes=[pltpu.VMEM((tm + CONV_HALO, C_CONV), F32), pltpu.VMEM((tm, C_CONV), F32),
                        pltpu.VMEM((tm, C_CONV), BF16)],
        compiler_params=_cparams(1),
        name="mix_conv",
    )(z, z, dw_w, dw_b, ln_g, ln_b, attn, ga, gc, h, mod, w_ao, w_co, w_out)


CONV_S_BB = 32


def _conv_s_kernel(z_ref, hist_ref, w_ref, b_ref, lg_ref, lb_ref, y_ref, ns_ref):
    for t in range(DEC_SEQ):
        acc = jnp.broadcast_to(b_ref[...], (CONV_S_BB, C_CONV))
        for j in range(CONV_WIDTH):
            i = t + j
            src = hist_ref[i] if i < HIST else z_ref[i - HIST]
            acc = acc + w_ref[j:j + 1, :] * src
        y_ref[t] = _ln_silu(acc, lg_ref[...], lb_ref[...]).astype(BF16)
    keep = HIST - DEC_SEQ
    ns_ref[0:keep] = hist_ref[DEC_SEQ:HIST]
    ns_ref[keep:HIST] = z_ref[...]


def _conv_sample(z, hist_t, dw_w, dw_b, ln_g, ln_b):
    vec = pl.BlockSpec((1, C_CONV), lambda i: (0, 0))
    tb = pl.BlockSpec((DEC_SEQ, CONV_S_BB, C_CONV), lambda i: (0, i, 0))
    st = pl.BlockSpec((HIST, CONV_S_BB, C_CONV), lambda i: (0, i, 0))
    return pl.pallas_call(
        _conv_s_kernel,
        out_shape=(jax.ShapeDtypeStruct((DEC_SEQ, DEC_BATCH, C_CONV), BF16),
                   jax.ShapeDtypeStruct(hist_t.shape, F32)),
        grid=(DEC_BATCH // CONV_S_BB,),
        in_specs=[tb, st, pl.BlockSpec((CONV_WIDTH, C_CONV), lambda i: (0, 0)), vec, vec, vec],
        out_specs=(tb, st),
        compiler_params=_cparams(1),
        name="conv_sample",
    )(z, hist_t, dw_w, dw_b, ln_g, ln_b)


MIX_TN = 512
MIX_NB = D_MODEL // MIX_TN


def _mix_s_kernel(at_ref, y_ref, ga_ref, gc_ref, h_ref, g2_ref, wao_ref, wco_ref, wout_ref,
                  o_ref, waoo_ref, wcoo_ref, wouto_ref, ab_ref, m_ref):
    j = pl.program_id(0)

    @pl.when(j == 0)
    def _():
        ab_ref[...] = at_ref[...].astype(BF16)

    @pl.when(j < MIX_NB)
    def _():
        waoo_ref[...] = wao_ref[...].astype(BF16)
        wcoo_ref[...] = wco_ref[...].astype(BF16)
        a = _dot(ab_ref[...], waoo_ref[...])
        c = _dot(y_ref[...], wcoo_ref[...])
        m_ref[j] = (ga_ref[...].astype(F32) * a + gc_ref[...].astype(F32) * c).astype(BF16)

    @pl.when(j >= MIX_NB)
    def _():
        wouto_ref[...] = wout_ref[...].astype(BF16)
        r = _dot(m_ref[0], wouto_ref[0:MIX_TN, :])
        for k in range(1, MIX_NB):
            r = r + _dot(m_ref[k], wouto_ref[k * MIX_TN:(k + 1) * MIX_TN, :])
        for s in range(DEC_SEQ):
            rows = slice(s * DEC_BATCH, (s + 1) * DEC_BATCH)
            o_ref[rows, :] = h_ref[rows, :] + g2_ref[...] * r[rows, :]


def _mix_sample(attn, y, ga, gc, h, mod, w_ao, w_co, w_out):
    col_a = lambda j: jnp.minimum(j, MIX_NB - 1)
    col_b = lambda j: jnp.maximum(j - MIX_NB, 0)
    whole = lambda w: pl.BlockSpec((S_ROWS, w), lambda j: (0, 0))
    blk = lambda rows, col: pl.BlockSpec((rows, MIX_TN), lambda j: (0, col(j)))
    mod_chunks = D_MODEL // MIX_TN
    return pl.pallas_call(
        _mix_s_kernel,
        out_shape=(jax.ShapeDtypeStruct((S_ROWS, D_MODEL), F32),
                   jax.ShapeDtypeStruct(w_ao.shape, BF16), jax.ShapeDtypeStruct(w_co.shape, BF16),
                   jax.ShapeDtypeStruct(w_out.shape, BF16)),
        grid=(2 * MIX_NB,),
        in_specs=[whole(ATTN_WIDTH), whole(C_CONV), blk(S_ROWS, col_a), blk(S_ROWS, col_a), blk(S_ROWS, col_b),
                  pl.BlockSpec((DEC_BATCH, MIX_TN), lambda j: (0, 5 * mod_chunks + col_b(j))),
                  blk(ATTN_WIDTH, col_a), blk(C_CONV, col_a), blk(D_MODEL, col_b)],
        out_specs=(blk(S_ROWS, col_b), blk(ATTN_WIDTH, col_a), blk(C_CONV, col_a), blk(D_MODEL, col_b)),
        scratch_shapes=[pltpu.VMEM((S_ROWS, ATTN_WIDTH), BF16), pltpu.VMEM((MIX_NB, S_ROWS, MIX_TN), BF16)],
        compiler_params=_cparams(1),
        name="mix_sample",
    )(attn, y, ga, gc, h, mod, w_ao, w_co, w_out)


def kernel(x_prompt, x_sample, cache_k, cache_v, state_conv, c_prompt, c_sample, rel_bias_table, norm1_g, ffn1_w1, ffn1_w3, ffn1_w2, norm2_g, w_in, attn_sinks, conv_dw_w, conv_dw_b, conv_ln_g, conv_ln_b, w_conv_out, w_attn_out, w_out, norm3_g, ffn2_w1, ffn2_w3, ffn2_w2, w_ada, b_ada, final_norm_g, w_ada_final, b_ada_final):
    c_all = jnp.concatenate([c_sample, c_prompt, jnp.zeros((N_COND - DEC_BATCH - 1, D_MODEL), F32)], axis=0)
    mod = _adaln(c_all, w_ada[0], b_ada)
    mod_f = _adaln(c_all, w_ada_final, b_ada_final[None, :])

    sinks = attn_sinks[0]
    tab_flat = rel_bias_table.reshape(-1)
    tab_ext = jnp.concatenate([rel_bias_table, sinks[None, :]], axis=0)
    th = tab_ext.T.reshape(N_KV_HEADS, GROUP, N_BUCKETS + 1).transpose(1, 0, 2)
    th = jnp.tile(th.reshape(N_HEADS, N_BUCKETS + 1), (DEC_SEQ, 1))

    xp = x_prompt[0]
    xs = x_sample.transpose(1, 0, 2).reshape(S_ROWS, D_MODEL)

    conv_w = (conv_dw_w[0], conv_dw_b, conv_ln_g, conv_ln_b)

    final = (final_norm_g[None, :], mod_f)

    def keys_on_lanes(cache):
        return cache[0].transpose(0, 2, 3, 1).reshape(DEC_BATCH, KV_WIDTH, WBUF)

    def keys_on_rows(win):
        return win.reshape(DEC_BATCH, N_KV_HEADS, HEAD_DIM, WBUF).transpose(0, 3, 1, 2)[None]

    def mixers_p(q, kv, z, ga, gc, h, mix_w):
        attn = _attn_prompt(q, kv, tab_flat, sinks)
        return _mix_conv_prompt(z, *conv_w, attn, ga, gc, h, mod, *mix_w)

    def mixers_s(q, kv, z, ga, gc, h):
        o, kw, vw = _attn_sample(q.reshape(DEC_SEQ, DEC_BATCH, ATTN_WIDTH),
                                 kv.reshape(DEC_SEQ, DEC_BATCH, 2 * KV_WIDTH),
                                 keys_on_lanes(cache_k), keys_on_lanes(cache_v), th)
        y, ns = _conv_sample(z.reshape(DEC_SEQ, DEC_BATCH, C_CONV), state_conv[0].transpose(1, 0, 2), *conv_w)
        h, *mix_w = _mix_sample(o.reshape(S_ROWS, ATTN_WIDTH), y.reshape(S_ROWS, C_CONV), ga, gc, h, mod,
                                w_attn_out[0], w_conv_out[0], w_out[0])
        return h, (kw, vw, ns), mix_w

    hs, f1 = _ffn(xs, norm1_g, mod, 0, ffn1_w1[0], ffn1_w3[0], ffn1_w2[0], per_row=True,
                  tm=S_ROWS, tf=FFN_TF_CAST, cast_out=True)
    hp = _ffn(xp, norm1_g, mod, 0, *f1, per_row=False, tm=FFN_TM, tf=FFN_TF)

    qs, kv_s, z_s, ga_s, gc_s, w_a, w_b = _proj(hs, norm2_g, mod, w_in[0], w_in[0], per_row=True,
                                                tm=S_ROWS, q_dtype=F32, cast_out=True)
    qp, kv_p, z_p, ga_p, gc_p = _proj(hp, norm2_g, mod, w_a, w_b, per_row=False, tm=PROJ_TM, q_dtype=BF16)

    hs, (kw, vw, ns), mix_w = mixers_s(qs, kv_s, z_s, ga_s, gc_s, hs)
    hp = mixers_p(qp, kv_p, z_p, ga_p, gc_p, hp, mix_w)

    ys, f2 = _ffn(hs, norm3_g, mod, 6, ffn2_w1[0], ffn2_w3[0], ffn2_w2[0], per_row=True,
                  tm=S_ROWS, tf=FFN_TF_CAST, final=final, cast_out=True)
    yp = _ffn(hp, norm3_g, mod, 6, *f2, per_row=False, tm=FFN_TM, tf=FFN_TF, final=final)

    w = min(WINDOW, SEQ)
    kv_shape = (1, 1, w, N_KV_HEADS, HEAD_DIM)
    k_win_p = kv_p[SEQ - w:, :KV_WIDTH].reshape(kv_shape)
    v_win_p = kv_p[SEQ - w:, KV_WIDTH:].reshape(kv_shape)
    conv_p_state = z_p[SEQ - HIST:].reshape(1, 1, HIST, C_CONV)
    s_shape = (1, DEC_BATCH, WBUF, N_KV_HEADS, HEAD_DIM)
    y_prompt = yp[None]
    y_sample = ys.reshape(DEC_SEQ, DEC_BATCH, D_MODEL).transpose(1, 0, 2)
    return (y_prompt, y_sample, k_win_p, v_win_p, conv_p_state,
            keys_on_rows(kw), keys_on_rows(vw), ns.transpose(1, 0, 2)[None])
```

```python
import functools

import numpy as np
import jax
import jax.numpy as jnp
from jax import lax
from jax.experimental import pallas as pl
from jax.experimental.pallas import tpu as pltpu

D_MODEL = 2048
SEQ = 8192
DEC_BATCH = 128
DEC_SEQ = 4
PAST_LEN = 16384
N_HEADS = 32
N_KV_HEADS = 8
HEAD_DIM = 64
GROUP = N_HEADS // N_KV_HEADS
ATTN_WIDTH = N_HEADS * HEAD_DIM
KV_WIDTH = N_KV_HEADS * HEAD_DIM
WINDOW = 128
Q_BLOCK = 128
ATTN_SCALE = HEAD_DIM ** -0.5
N_BUCKETS = 32
MAX_DISTANCE = 128
C_CONV = D_MODEL // 2
CONV_WIDTH = 31
HIST = CONV_WIDTH - 1
D_FF = 5632
NORM_EPS = 1e-6
NEG_INF = -1e30
IN_WIDTH = ATTN_WIDTH + 2 * KV_WIDTH + 2 * C_CONV + 2 * D_MODEL
WBUF = min(WINDOW, PAST_LEN)

S_ROWS = DEC_BATCH * DEC_SEQ
N_COND = 136
PROMPT_MOD_BLOCK = DEC_BATCH // 8
S_EXTRA = 8

VMEM_LIMIT = 60 * 1024 * 1024
SUBLANES = 8
LANES = 128

F32 = jnp.float32
BF16 = jnp.bfloat16


def _cparams(n_axes, flags=None):
    return pltpu.CompilerParams(dimension_semantics=("arbitrary",) * n_axes,
                                vmem_limit_bytes=VMEM_LIMIT, flags=flags)


def _dot(a, b):
    return jnp.dot(a, b, preferred_element_type=F32)


def _rms_mod(x, g, sh, sc):
    ms = jnp.mean(x * x, axis=-1, keepdims=True)
    y = x * lax.rsqrt(ms + NORM_EPS) * g
    return y * (1.0 + sc) + sh


def _mod_val(ref, per_row):
    return ref[...] if per_row else ref[0:1, :]


def _mod_spec(chunk, per_row, width=D_MODEL):
    if per_row:
        return pl.BlockSpec((DEC_BATCH, width), lambda *_: (0, chunk))
    return pl.BlockSpec((8, width), lambda *_: (PROMPT_MOD_BLOCK, chunk))


def _adaln_kernel(c_ref, w_ref, b_ref, o_ref, s_ref):
    @pl.when(pl.program_id(0) == 0)
    def _():
        s_ref[...] = jax.nn.silu(c_ref[...]).astype(BF16)

    o_ref[...] = _dot(s_ref[...], w_ref[...].astype(BF16)) + b_ref[...]


def _adaln(c_all, w, b, tn=1024):
    n = w.shape[1]
    return pl.pallas_call(
        _adaln_kernel,
        out_shape=jax.ShapeDtypeStruct((N_COND, n), F32),
        grid=(n // tn,),
        in_specs=[pl.BlockSpec((N_COND, D_MODEL), lambda j: (0, 0)),
                  pl.BlockSpec((D_MODEL, tn), lambda j: (0, j)),
                  pl.BlockSpec((1, tn), lambda j: (0, j))],
        out_specs=pl.BlockSpec((N_COND, tn), lambda j: (0, j)),
        scratch_shapes=[pltpu.VMEM((N_COND, D_MODEL), BF16)],
        compiler_params=_cparams(1),
        name="adaln",
    )(c_all, w, b)


FFN_ROWS = 512
FFN_TM = 1024
FFN_TF = 512
FFN_TF_CAST = 256
PROJ_TM = 1024
W_IN_SIDE_COLS = 1024


def _side_cast(side_in, side_out):
    for i_ref, o_ref in zip(side_in, side_out):
        o_ref[...] = i_ref[...].astype(BF16)


def _side_specs(side):
    args = [a for a, _, _ in side]
    specs = [pl.BlockSpec(block, index_map) for _, block, index_map in side]
    shapes = [jax.ShapeDtypeStruct(a.shape, BF16) for a in args]
    return args, specs, shapes


def _ffn_kernel(*refs, tm, sub, per_row, n_f, final, cast_out, n_side):
    refs = list(refs)
    x_ref, n_ref, sh_ref, sc_ref, g_ref, w1_ref, w3_ref, w2_ref = refs[:8]
    del refs[:8]
    if final:
        gf_ref, shf_ref, scf_ref = refs[:3]
        del refs[:3]
    side_in = refs[:n_side]
    del refs[:n_side]
    o_ref = refs.pop(0)
    if cast_out:
        w1o_ref, w3o_ref, w2o_ref = refs[:3]
        del refs[:3]
    side_out = refs[:n_side]
    del refs[:n_side]
    (u_ref,) = refs
    f = pl.program_id(1)
    _side_cast(side_in, side_out)

    if cast_out:
        w1o_ref[...] = w1_ref[...].astype(BF16)
        w3o_ref[...] = w3_ref[...].astype(BF16)
        w2o_ref[...] = w2_ref[...].astype(BF16)
        w1_ref, w3_ref, w2_ref = w1o_ref, w3o_ref, w2o_ref

    def accumulate(first):
        for s in range(tm // FFN_ROWS):
            rows = slice(s * FFN_ROWS, (s + 1) * FFN_ROWS)
            u = u_ref[rows, :]
            h1 = _dot(u, w1_ref[...])
            h3 = _dot(u, w3_ref[...])
            a = (jax.nn.silu(h1) * h3).astype(BF16)
            d = _dot(a, w2_ref[...])
            if first:
                o_ref[rows, :] = d
            else:
                o_ref[rows, :] += d

    @pl.when(f == 0)
    def _():
        def body(s, carry):
            rows = pl.ds(pl.multiple_of(s * sub, sub), sub)
            u_ref[rows, :] = _rms_mod(x_ref[rows, :], n_ref[...], _mod_val(sh_ref, per_row),
                                      _mod_val(sc_ref, per_row)).astype(BF16)
            return carry
        lax.fori_loop(0, tm // sub, body, 0)
        accumulate(True)

    @pl.when(f > 0)
    def _():
        accumulate(False)

    @pl.when(f == n_f - 1)
    def _():
        def body(s, carry):
            rows = pl.ds(pl.multiple_of(s * sub, sub), sub)
            h = x_ref[rows, :] + 0.5 * _mod_val(g_ref, per_row) * o_ref[rows, :]
            if final:
                h = _rms_mod(h, gf_ref[...], _mod_val(shf_ref, per_row), _mod_val(scf_ref, per_row))
            o_ref[rows, :] = h
            return carry
        lax.fori_loop(0, tm // sub, body, 0)


def _ffn(x, norm_g, mod, chunk0, w1, w3, w2, *, per_row, tm, tf, final=None, cast_out=False, side=()):
    rows = x.shape[0]
    n_f = D_FF // tf
    sub = DEC_BATCH
    row_spec = pl.BlockSpec((tm, D_MODEL), lambda i, f: (i, 0))
    vec_spec = pl.BlockSpec((1, D_MODEL), lambda i, f: (0, 0))
    w_specs = [pl.BlockSpec((D_MODEL, tf), lambda i, f: (0, f)),
               pl.BlockSpec((D_MODEL, tf), lambda i, f: (0, f)),
               pl.BlockSpec((tf, D_MODEL), lambda i, f: (f, 0))]
    in_specs = [row_spec, vec_spec,
                _mod_spec(chunk0, per_row), _mod_spec(chunk0 + 1, per_row), _mod_spec(chunk0 + 2, per_row)]
    in_specs += w_specs
    args = [x, norm_g, mod, mod, mod, w1, w3, w2]
    if final is not None:
        gf, mod_f = final
        in_specs += [vec_spec, _mod_spec(0, per_row), _mod_spec(1, per_row)]
        args += [gf, mod_f, mod_f]
    side_args, side_specs, side_shapes = _side_specs(side)
    in_specs += side_specs
    args += side_args
    out_shape = [jax.ShapeDtypeStruct((rows, D_MODEL), F32)]
    out_specs = [row_spec]
    if cast_out:
        assert rows == tm, "every weight block must be visited exactly once"
        out_shape += [jax.ShapeDtypeStruct(w.shape, BF16) for w in (w1, w3, w2)]
        out_specs += w_specs
    out_shape += side_shapes
    out_specs += side_specs
    outs = pl.pallas_call(
        functools.partial(_ffn_kernel, tm=tm, sub=sub, per_row=per_row, n_f=n_f,
                          final=final is not None, cast_out=cast_out, n_side=len(side)),
        out_shape=out_shape,
        grid=(rows // tm, n_f),
        in_specs=in_specs,
        out_specs=out_specs,
        scratch_shapes=[pltpu.VMEM((tm, D_MODEL), BF16)],
        compiler_params=_cparams(2),
        name="ffn_final" if final is not None else "ffn",
    )(*args)
    return (outs[0], tuple(outs[1:])) if (cast_out or side) else outs[0]


PROJ_CHUNK = 256
PROJ_CW = 512


def _proj_layout():
    cw = PROJ_CW
    n_q = ATTN_WIDTH // (2 * cw)
    n_kv = 2 * KV_WIDTH // (2 * cw)
    n_glu = C_CONV // cw
    n_gate = D_MODEL // (2 * cw)
    pre = n_q + n_kv
    steps = pre + n_glu + 2 * n_gate

    def block_a(j):
        return jnp.where((j >= pre) & (j < pre + n_glu), j + pre, 2 * j)

    def block_b(j):
        return jnp.where((j >= pre) & (j < pre + n_glu), j + pre + n_glu, 2 * j + 1)

    return n_q, n_kv, n_glu, n_gate, steps, block_a, block_b


def _proj_kernel(h_ref, n_ref, sh_ref, sc_ref, wa_ref, wb_ref, q_ref, kv_ref, z_ref, ga_ref, gc_ref, u_ref,
                 *, tm, sub, per_row):
    cw = PROJ_CW
    n_q, n_kv, n_glu, n_gate, _, _, _ = _proj_layout()
    j = pl.program_id(1)

    @pl.when(j == 0)
    def _():
        def body(s, carry):
            rows = pl.ds(pl.multiple_of(s * sub, sub), sub)
            u_ref[rows, :] = _rms_mod(h_ref[rows, :], n_ref[...], _mod_val(sh_ref, per_row),
                                      _mod_val(sc_ref, per_row)).astype(BF16)
            return carry
        lax.fori_loop(0, tm // sub, body, 0)

    n_chunks = cw // PROJ_CHUNK

    def chunks(w_ref):
        for c in range(n_chunks):
            cols = slice(c * PROJ_CHUNK, (c + 1) * PROJ_CHUNK)
            yield c * PROJ_CHUNK, _dot(u_ref[...], w_ref[:, cols])

    def pair_store(o_ref, fn):
        for half, w_ref in enumerate((wa_ref, wb_ref)):
            for c0, acc in chunks(w_ref):
                o_ref[:, half * cw + c0:half * cw + c0 + PROJ_CHUNK] = fn(acc).astype(o_ref.dtype)

    lo = 0

    @pl.when(j < n_q)
    def _():
        pair_store(q_ref, lambda acc: acc)

    lo += n_q

    @pl.when((j >= lo) & (j < lo + n_kv))
    def _():
        pair_store(kv_ref, lambda acc: acc)

    lo += n_kv

    @pl.when((j >= lo) & (j < lo + n_glu))
    def _():
        for (c0, a), (_, g) in zip(chunks(wa_ref), chunks(wb_ref)):
            z_ref[:, c0:c0 + PROJ_CHUNK] = a * jax.nn.sigmoid(g)

    lo += n_glu

    @pl.when((j >= lo) & (j < lo + n_gate))
    def _():
        pair_store(ga_ref, jax.nn.sigmoid)

    lo += n_gate

    @pl.when(j >= lo)
    def _():
        pair_store(gc_ref, jax.nn.sigmoid)


def _proj(h, norm_g, mod, w, *, per_row, tm, q_dtype):
    rows = h.shape[0]
    sub = DEC_BATCH
    cw = PROJ_CW
    n_q, n_kv, n_glu, n_gate, steps, block_a, block_b = _proj_layout()

    def out_spec(width, first, count):
        return pl.BlockSpec((tm, width), lambda i, j: (i, jnp.clip(j - first, 0, count - 1)))

    wa_spec = pl.BlockSpec((D_MODEL, cw), lambda i, j: (0, block_a(j)))
    wb_spec = pl.BlockSpec((D_MODEL, cw), lambda i, j: (0, block_b(j)))
    out_shape = [jax.ShapeDtypeStruct((rows, ATTN_WIDTH), q_dtype),
                 jax.ShapeDtypeStruct((rows, 2 * KV_WIDTH), F32),
                 jax.ShapeDtypeStruct((rows, C_CONV), F32),
                 jax.ShapeDtypeStruct((rows, D_MODEL), BF16),
                 jax.ShapeDtypeStruct((rows, D_MODEL), BF16)]
    out_specs = [out_spec(2 * cw, 0, n_q),
                 out_spec(2 * cw, n_q, n_kv),
                 out_spec(cw, n_q + n_kv, n_glu),
                 out_spec(2 * cw, n_q + n_kv + n_glu, n_gate),
                 out_spec(2 * cw, n_q + n_kv + n_glu + n_gate, n_gate)]
    return pl.pallas_call(
        functools.partial(_proj_kernel, tm=tm, sub=sub, per_row=per_row),
        out_shape=out_shape,
        grid=(rows // tm, steps),
        in_specs=[pl.BlockSpec((tm, D_MODEL), lambda i, j: (i, 0)),
                  pl.BlockSpec((1, D_MODEL), lambda i, j: (0, 0)),
                  _mod_spec(3, per_row), _mod_spec(4, per_row), wa_spec, wb_spec],
        out_specs=out_specs,
        scratch_shapes=[pltpu.VMEM((tm, D_MODEL), BF16)],
        compiler_params=_cparams(2),
        name="proj",
    )(h, norm_g, mod, mod, w, w)


def _t5_bucket_np(dist):
    exact = N_BUCKETS // 2
    d = np.maximum(dist, 0)
    df = np.maximum(d, 1).astype(np.float32)
    large = exact + (np.log(df / np.float32(exact)) / np.float32(np.log(MAX_DISTANCE / exact))
                     * np.float32(N_BUCKETS - exact)).astype(np.int32)
    large = np.minimum(large, N_BUCKETS - 1)
    return np.where(d < exact, d, large).astype(np.int32)


def _prompt_codes():
    qi = np.arange(Q_BLOCK)[:, None]
    kj = np.arange(2 * Q_BLOCK)[None, :]
    dist = qi + Q_BLOCK - kj
    valid = (dist >= 0) & (dist <= WINDOW)
    return np.where(valid, _t5_bucket_np(dist), -1).astype(np.int32)


def _sample_codes():
    t = np.repeat(np.arange(DEC_SEQ), GROUP * N_KV_HEADS)[:, None]
    j = np.arange(WBUF)[None, :]
    dist = t + WBUF - j
    cache = np.where((dist >= 0) & (dist <= WINDOW), _t5_bucket_np(dist), -1)
    c = np.arange(S_EXTRA)[None, :]
    t_new = c - (S_EXTRA - DEC_SEQ)
    dist = t - t_new
    extra = np.where((t_new >= 0) & (dist >= 0) & (dist <= WINDOW), _t5_bucket_np(dist), -1)
    extra = np.where(c == 0, N_BUCKETS, extra)
    return cache.astype(np.int32), extra.astype(np.int32)


BIAS_ROWS = 64
ATTN_QB = 2


def _attn_p_kernel(*refs, n_side):
    refs = list(refs)
    tab_ref, sink_ref, code_ref, q_ref, kvc_ref, kvp_ref = refs[:6]
    side_in = refs[6:6 + n_side]
    o_ref = refs[6 + n_side]
    side_out = refs[7 + n_side:7 + 2 * n_side]
    (bias_ref,) = refs[7 + 2 * n_side:]
    _side_cast(side_in, side_out)
    n = pl.program_id(0)
    n_keys = 2 * Q_BLOCK
    wide = GROUP * Q_BLOCK

    @pl.when(n == 0)
    def _():
        def kv_body(kv, carry):
            for g in range(GROUP):
                for r in range(n_keys // BIAS_ROWS):
                    rows = slice(r * BIAS_ROWS, (r + 1) * BIAS_ROWS)
                    code = code_ref[rows, :]
                    acc = jnp.full((BIAS_ROWS, Q_BLOCK), NEG_INF, F32)
                    for bk in range(N_BUCKETS):
                        acc = jnp.where(code == bk, tab_ref[bk * N_HEADS + kv * GROUP + g], acc)
                    bias_ref[kv, rows, g * Q_BLOCK:(g + 1) * Q_BLOCK] = acc
            return carry
        lax.fori_loop(0, N_KV_HEADS, kv_body, 0)

    ones = jnp.ones((16, n_keys), BF16)
    zeros = jnp.zeros((HEAD_DIM, wide), BF16)
    tile = (n_keys, LANES)
    key = lax.broadcasted_iota(jnp.int32, tile, 0)
    lane = lax.broadcasted_iota(jnp.int32, tile, 1)
    pen = jnp.where((lane == 0) & (key < Q_BLOCK) & (n == 0), NEG_INF, 0.0).astype(BF16)
    pen_rows = (lax.broadcasted_iota(jnp.int32, (LANES, wide), 0) == 0).astype(BF16)

    def query_block(a):
        rows = slice(a * Q_BLOCK, (a + 1) * Q_BLOCK)
        q_t = q_ref[rows, :].astype(F32).T

        def keys(cols):
            prev = kvp_ref[:, cols] if a == 0 else kvc_ref[(a - 1) * Q_BLOCK:a * Q_BLOCK, cols]
            return jnp.concatenate([prev, kvc_ref[rows, cols]], axis=0)

        def pair_kv(pair):
            k2 = keys(slice(pair * LANES, (pair + 1) * LANES)).astype(BF16)
            if a == 0:
                k2 = jnp.concatenate([k2, pen], axis=1)
            v2_t = keys(slice(KV_WIDTH + pair * LANES, KV_WIDTH + (pair + 1) * LANES)).T.astype(BF16)
            return k2, v2_t

        def scores(kv, k2):
            q4 = jnp.concatenate(
                [q_t[(kv * GROUP + g) * HEAD_DIM:(kv * GROUP + g + 1) * HEAD_DIM, :] for g in range(GROUP)],
                axis=1)
            q4 = (q4 * ATTN_SCALE).astype(BF16)
            parts = [q4, zeros] if kv % 2 == 0 else [zeros, q4]
            return _dot(k2, jnp.concatenate(parts + ([pen_rows] if a == 0 else []), axis=0))

        kvs = [pair_kv(pair) for pair in range(N_KV_HEADS // 2)]
        for kv in range(N_KV_HEADS):
            pair, parity = divmod(kv, 2)
            s = scores(kv, kvs[pair][0]) + bias_ref[kv]
            sink = jnp.concatenate(
                [jnp.full((1, Q_BLOCK), sink_ref[kv * GROUP + g], F32) for g in range(GROUP)], axis=1)
            m = jnp.maximum(jnp.max(s, axis=0, keepdims=True), sink)
            p = jnp.exp(s - m).astype(BF16)
            lhs = jnp.concatenate([kvs[pair][1][parity * HEAD_DIM:(parity + 1) * HEAD_DIM, :], ones], axis=0)
            o_t = _dot(lhs, p)
            l = o_t[HEAD_DIM:HEAD_DIM + 1, :] + jnp.exp(sink - m)
            o_n = o_t[0:HEAD_DIM, :] * (1.0 / l)
            o_kv = jnp.concatenate([o_n[:, g * Q_BLOCK:(g + 1) * Q_BLOCK] for g in range(GROUP)], axis=0)
            o_ref[rows, kv * GROUP * HEAD_DIM:(kv + 1) * GROUP * HEAD_DIM] = o_kv.T.astype(o_ref.dtype)

    for a in range(ATTN_QB):
        query_block(a)


ATTN_STEPS = SEQ // (ATTN_QB * Q_BLOCK)


def _row_slab_side(w):
    slab = w.shape[0] // ATTN_STEPS
    assert slab * ATTN_STEPS == w.shape[0] and slab % 16 == 0
    return (w, (slab, w.shape[1]), lambda n: (n, 0))


def _attn_prompt(q, kv, tab_flat, sinks, side=()):
    step_rows = ATTN_QB * Q_BLOCK
    code_t = jnp.asarray(np.ascontiguousarray(_prompt_codes().T))
    smem = pl.BlockSpec(memory_space=pltpu.SMEM)
    side_args, side_specs, side_shapes = _side_specs(side)
    out_spec = pl.BlockSpec((step_rows, ATTN_WIDTH), lambda n: (n, 0))
    outs = pl.pallas_call(
        functools.partial(_attn_p_kernel, n_side=len(side)),
        out_shape=[jax.ShapeDtypeStruct((SEQ, ATTN_WIDTH), BF16)] + side_shapes,
        grid=(ATTN_STEPS,),
        in_specs=[smem, smem,
                  pl.BlockSpec((2 * Q_BLOCK, Q_BLOCK), lambda n: (0, 0)),
                  pl.BlockSpec((step_rows, ATTN_WIDTH), lambda n: (n, 0)),
                  pl.BlockSpec((step_rows, 2 * KV_WIDTH), lambda n: (n, 0)),
                  pl.BlockSpec((Q_BLOCK, 2 * KV_WIDTH), lambda n: (jnp.maximum(ATTN_QB * n - 1, 0), 0))]
        + side_specs,
        out_specs=[out_spec] + side_specs,
        scratch_shapes=[pltpu.VMEM((N_KV_HEADS, 2 * Q_BLOCK, GROUP * Q_BLOCK), F32)],
        compiler_params=_cparams(1),
        name="attn_prompt",
    )(tab_flat, sinks, code_t, q, kv, kv, *side_args)
    return outs[0], tuple(outs[1:])


S_BB = 8


def _regroup_heads(x, to_group_major):
    half = HEAD_DIM
    assert LANES == 2 * half
    low = lax.broadcasted_iota(jnp.int32, (x.shape[0], LANES), 1) < half
    n_cols = ATTN_WIDTH // LANES
    outer, inner = (GROUP, N_KV_HEADS) if to_group_major else (N_KV_HEADS, GROUP)
    cols = []
    for c in range(n_cols):
        o, i = divmod(2 * c, inner)
        src = [(i + d) * outer + o for d in range(2)]
        a = x[:, (src[0] // 2) * LANES:(src[0] // 2 + 1) * LANES]
        b = x[:, (src[1] // 2) * LANES:(src[1] // 2 + 1) * LANES]
        assert src[0] % 2 == src[1] % 2
        if src[0] % 2 == 0:
            cols.append(jnp.where(low, a, pltpu.roll(b, half, axis=1)))
        else:
            cols.append(jnp.where(low, pltpu.roll(a, half, axis=1), b))
    return jnp.concatenate(cols, axis=1)


def _attn_s_kernel(th_ref, code_c_ref, code_x_ref, q_ref, kv_ref, ck_ref, cv_ref, o_ref, kw_ref, vw_ref,
                   bias_c_ref, bias_x_ref, nk_ref, nv_ref, qe_ref, qg_ref, og_ref):
    rows_all = DEC_SEQ * N_HEADS
    for t in range(DEC_SEQ):
        qg_ref[t] = _regroup_heads(q_ref[t], True)

    @pl.when(pl.program_id(0) == 0)
    def _():
        for code_ref, bias_ref in ((code_c_ref, bias_c_ref), (code_x_ref, bias_x_ref)):
            code = code_ref[...]
            acc = jnp.full(code.shape, NEG_INF, F32)
            for bk in range(N_BUCKETS + 1):
                acc = jnp.where(code == bk, th_ref[:, bk:bk + 1], acc)
            bias_ref[...] = acc
        nk_ref[...] = jnp.zeros_like(nk_ref)
        nv_ref[...] = jnp.zeros_like(nv_ref)

    lane_head = lax.broadcasted_iota(jnp.int32, (N_KV_HEADS, KV_WIDTH), 1) // HEAD_DIM
    row_head = lax.broadcasted_iota(jnp.int32, (N_KV_HEADS, KV_WIDTH), 0)
    diag = lane_head == row_head
    out_mask = (lax.broadcasted_iota(jnp.int32, (rows_all, KV_WIDTH), 0) % N_KV_HEADS
                == lax.broadcasted_iota(jnp.int32, (rows_all, KV_WIDTH), 1) // HEAD_DIM)
    new_lanes = lax.broadcasted_iota(jnp.int32, (KV_WIDTH, WBUF), 1) >= WBUF - DEC_SEQ
    bias_c = bias_c_ref[...]
    bias_x = bias_x_ref[...]
    nt_dims = (((1,), (1,)), ((), ()))
    x0 = WBUF - S_EXTRA

    for b in range(S_BB):
        for t in range(DEC_SEQ):
            nk_ref[WBUF - DEC_SEQ + t:WBUF - DEC_SEQ + t + 1, :] = kv_ref[t, b:b + 1, 0:KV_WIDTH]
            nv_ref[WBUF - DEC_SEQ + t:WBUF - DEC_SEQ + t + 1, :] = kv_ref[t, b:b + 1, KV_WIDTH:2 * KV_WIDTH]
            for g in range(GROUP):
                q_row = qg_ref[t, b:b + 1, g * KV_WIDTH:(g + 1) * KV_WIDTH]
                piece = jnp.where(diag, jnp.broadcast_to(q_row, (N_KV_HEADS, KV_WIDTH)), 0.0)
                r0 = (t * GROUP + g) * N_KV_HEADS
                qe_ref[r0:r0 + N_KV_HEADS, :] = piece
        k_t = ck_ref[b]
        v_t = cv_ref[b]
        kw_ref[b] = jnp.where(new_lanes, nk_ref[...].T, pltpu.roll(k_t, WBUF - DEC_SEQ, axis=1))
        vw_ref[b] = jnp.where(new_lanes, nv_ref[...].T, pltpu.roll(v_t, WBUF - DEC_SEQ, axis=1))

        qe = (qe_ref[...] * ATTN_SCALE).astype(BF16)
        s_c = _dot(qe, k_t.astype(BF16)) + bias_c
        s_x = lax.dot_general(qe, nk_ref[x0:WBUF, :].astype(BF16), nt_dims,
                              preferred_element_type=F32) + bias_x
        m = jnp.maximum(jnp.max(s_c, axis=-1, keepdims=True), jnp.max(s_x, axis=-1, keepdims=True))
        p_c = jnp.exp(s_c - m)
        p_x = jnp.exp(s_x - m)
        l = jnp.sum(p_c, axis=-1, keepdims=True) + jnp.sum(p_x, axis=-1, keepdims=True)
        o = lax.dot_general(p_c.astype(BF16), v_t.astype(BF16), nt_dims, preferred_element_type=F32)
        o = (o + _dot(p_x.astype(BF16), nv_ref[x0:WBUF, :].astype(BF16))) / l
        o = jnp.where(out_mask, o, 0.0)
        for t in range(DEC_SEQ):
            for g in range(GROUP):
                r0 = (t * GROUP + g) * N_KV_HEADS
                row = jnp.sum(o[r0:r0 + N_KV_HEADS, :], axis=0, keepdims=True)
                og_ref[t, b:b + 1, g * KV_WIDTH:(g + 1) * KV_WIDTH] = row
    for t in range(DEC_SEQ):
        o_ref[t] = _regroup_heads(og_ref[t], False)


def _attn_sample(q, kv, cache_k_t, cache_v_t, th):
    code_c, code_x = (jnp.asarray(c) for c in _sample_codes())
    rows_all = DEC_SEQ * N_HEADS
    full = lambda a: pl.BlockSpec(a.shape, lambda i: (0,) * a.ndim)
    tb_spec = lambda w: pl.BlockSpec((DEC_SEQ, S_BB, w), lambda i: (0, i, 0))
    cache_spec = pl.BlockSpec((S_BB, KV_WIDTH, WBUF), lambda i: (i, 0, 0))
    cache_shape = jax.ShapeDtypeStruct((DEC_BATCH, KV_WIDTH, WBUF), F32)
    return pl.pallas_call(
        _attn_s_kernel,
        out_shape=(jax.ShapeDtypeStruct((DEC_SEQ, DEC_BATCH, ATTN_WIDTH), F32), cache_shape, cache_shape),
        grid=(DEC_BATCH // S_BB,),
        in_specs=[full(th), full(code_c), full(code_x),
                  tb_spec(ATTN_WIDTH), tb_spec(2 * KV_WIDTH), cache_spec, cache_spec],
        out_specs=(tb_spec(ATTN_WIDTH), cache_spec, cache_spec),
        scratch_shapes=[pltpu.VMEM((rows_all, WBUF), F32),
                        pltpu.VMEM((rows_all, S_EXTRA), F32),
                        pltpu.VMEM((WBUF, KV_WIDTH), F32),
                        pltpu.VMEM((WBUF, KV_WIDTH), F32),
                        pltpu.VMEM((rows_all, KV_WIDTH), F32),
                        pltpu.VMEM((DEC_SEQ, S_BB, ATTN_WIDTH), F32),
                        pltpu.VMEM((DEC_SEQ, S_BB, ATTN_WIDTH), F32)],
        compiler_params=_cparams(1),
        name="attn_sample",
    )(th, code_c, code_x, q, kv, cache_k_t, cache_v_t)


CONV_RC = 32
CONV_ROWS = 128
CONV_HALO = 32


def _ln_silu(acc, lg, lb):
    mu = jnp.mean(acc, axis=-1, keepdims=True)
    xc = acc - mu
    var = jnp.mean(xc * xc, axis=-1, keepdims=True)
    return jax.nn.silu(xc * lax.rsqrt(var + NORM_EPS) * lg + lb)


def _conv_stage(first, zc_ref, zh_ref, s_ref):
    s_ref[0:CONV_HALO, :] = jnp.where(first, 0.0, zh_ref[...])
    s_ref[CONV_HALO:, :] = zc_ref[...]


def _conv_rows(rc, w_ref, b_ref, lg_ref, lb_ref, y_ref, s_ref, c_ref):
    off = CONV_HALO - HIST
    groups = [[j for j in range(CONV_WIDTH) if (j + off) % SUBLANES == r] for r in range(SUBLANES)]
    t0 = rc * CONV_ROWS
    for lc in range(C_CONV // LANES):
        lanes = slice(lc * LANES, (lc + 1) * LANES)
        out = jnp.broadcast_to(b_ref[:, lanes], (CONV_ROWS, LANES))
        for r, taps in enumerate(groups):
            n_rows = CONV_ROWS + (SUBLANES if r else 0)
            part = None
            for j in taps:
                base = t0 + (j + off) - r
                term = w_ref[j:j + 1, lanes] * s_ref[base:base + n_rows, lanes]
                part = term if part is None else part + term
            out = out + part[r:r + CONV_ROWS, :]
        c_ref[t0:t0 + CONV_ROWS, lanes] = out
    for r in range(CONV_ROWS // CONV_RC):
        rows = slice(t0 + r * CONV_RC, t0 + (r + 1) * CONV_RC)
        y_ref[rows, :] = _ln_silu(c_ref[rows, :], lg_ref[...], lb_ref[...]).astype(BF16)


def _mix_conv_kernel(zc_ref, zh_ref, w_ref, b_ref, lg_ref, lb_ref,
                     at_ref, ga_ref, gc_ref, h_ref, g2_ref, wao_ref, wco_ref, wout_ref,
                     o_ref, s_ref, c_ref, y_ref, *, tm):
    _conv_stage(pl.program_id(0) == 0, zc_ref, zh_ref, s_ref)
    for rc in range(tm // CONV_ROWS):
        _conv_rows(rc, w_ref, b_ref, lg_ref, lb_ref, y_ref, s_ref, c_ref)
    a = _dot(at_ref[...], wao_ref[...])
    c = _dot(y_ref[...], wco_ref[...])
    merged = (ga_ref[...].astype(F32) * a + gc_ref[...].astype(F32) * c).astype(BF16)
    r = _dot(merged, wout_ref[...])
    o_ref[...] = h_ref[...] + g2_ref[0:1, :] * r


def _mix_conv_prompt(z, dw_w, dw_b, ln_g, ln_b, attn, ga, gc, h, mod, w_ao, w_co, w_out, tm=256):
    rows = h.shape[0]
    n = rows // tm
    ratio = tm // CONV_HALO
    vec = pl.BlockSpec((1, C_CONV), lambda s: (0, 0))
    row = lambda w: pl.BlockSpec((tm, w), lambda s: (s, 0))
    resident = lambda shape: pl.BlockSpec(shape, lambda s: (0, 0), pipeline_mode=pl.Buffered(1))
    return pl.pallas_call(
        functools.partial(_mix_conv_kernel, tm=tm),
        out_shape=jax.ShapeDtypeStruct((rows, D_MODEL), F32),
        grid=(n,),
        in_specs=[row(C_CONV),
                  pl.BlockSpec((CONV_HALO, C_CONV), lambda s: (jnp.maximum(s * ratio - 1, 0), 0)),
                  pl.BlockSpec((CONV_WIDTH, C_CONV), lambda s: (0, 0)), vec, vec, vec,
                  row(ATTN_WIDTH), row(D_MODEL), row(D_MODEL), row(D_MODEL),
                  _mod_spec(5, False),
                  resident((ATTN_WIDTH, D_MODEL)), resident((C_CONV, D_MODEL)), resident((D_MODEL, D_MODEL))],
        out_specs=row(D_MODEL),
        scratch_shapes=[pltpu.VMEM((tm + CONV_HALO, C_CONV), F32), pltpu.VMEM((tm, C_CONV), F32),
                        pltpu.VMEM((tm, C_CONV), BF16)],
        compiler_params=_cparams(1),
        name="mix_conv",
    )(z, z, dw_w, dw_b, ln_g, ln_b, attn, ga, gc, h, mod, w_ao, w_co, w_out)


CONV_S_BB = 32


def _conv_s_kernel(z_ref, hist_ref, w_ref, b_ref, lg_ref, lb_ref, y_ref, ns_ref):
    for t in range(DEC_SEQ):
        acc = jnp.broadcast_to(b_ref[...], (CONV_S_BB, C_CONV))
        for j in range(CONV_WIDTH):
            i = t + j
            src = hist_ref[i] if i < HIST else z_ref[i - HIST]
            acc = acc + w_ref[j:j + 1, :] * src
        y_ref[t] = _ln_silu(acc, lg_ref[...], lb_ref[...]).astype(BF16)
    keep = HIST - DEC_SEQ
    ns_ref[0:keep] = hist_ref[DEC_SEQ:HIST]
    ns_ref[keep:HIST] = z_ref[...]


def _conv_sample(z, hist_t, dw_w, dw_b, ln_g, ln_b):
    vec = pl.BlockSpec((1, C_CONV), lambda i: (0, 0))
    tb = pl.BlockSpec((DEC_SEQ, CONV_S_BB, C_CONV), lambda i: (0, i, 0))
    st = pl.BlockSpec((HIST, CONV_S_BB, C_CONV), lambda i: (0, i, 0))
    return pl.pallas_call(
        _conv_s_kernel,
        out_shape=(jax.ShapeDtypeStruct((DEC_SEQ, DEC_BATCH, C_CONV), BF16),
                   jax.ShapeDtypeStruct(hist_t.shape, F32)),
        grid=(DEC_BATCH // CONV_S_BB,),
        in_specs=[tb, st, pl.BlockSpec((CONV_WIDTH, C_CONV), lambda i: (0, 0)), vec, vec, vec],
        out_specs=(tb, st),
        compiler_params=_cparams(1),
        name="conv_sample",
    )(z, hist_t, dw_w, dw_b, ln_g, ln_b)


def _mix_s_kernel(at_ref, y_ref, ga_ref, gc_ref, h_ref, g2_ref, wao_ref, wco_ref, wout_ref, o_ref):
    a = _dot(at_ref[...].astype(BF16), wao_ref[...])
    c = _dot(y_ref[...], wco_ref[...])
    merged = (ga_ref[...].astype(F32) * a + gc_ref[...].astype(F32) * c).astype(BF16)
    r = _dot(merged, wout_ref[...])
    o_ref[...] = h_ref[...] + g2_ref[...] * r


def _mix_sample(attn, y, ga, gc, h, mod, w_ao, w_co, w_out):
    row = lambda w: pl.BlockSpec((DEC_BATCH, w), lambda t: (t, 0))
    resident = lambda shape: pl.BlockSpec(shape, lambda t: (0, 0), pipeline_mode=pl.Buffered(1))
    return pl.pallas_call(
        _mix_s_kernel,
        out_shape=jax.ShapeDtypeStruct((S_ROWS, D_MODEL), F32),
        grid=(DEC_SEQ,),
        in_specs=[row(ATTN_WIDTH), row(C_CONV), row(D_MODEL), row(D_MODEL), row(D_MODEL),
                  _mod_spec(5, True),
                  resident((ATTN_WIDTH, D_MODEL)), resident((C_CONV, D_MODEL)), resident((D_MODEL, D_MODEL))],
        out_specs=row(D_MODEL),
        compiler_params=_cparams(1),
        name="mix_sample",
    )(attn, y, ga, gc, h, mod, w_ao, w_co, w_out)


def kernel(x_prompt, x_sample, cache_k, cache_v, state_conv, c_prompt, c_sample, rel_bias_table, norm1_g, ffn1_w1, ffn1_w3, ffn1_w2, norm2_g, w_in, attn_sinks, conv_dw_w, conv_dw_b, conv_ln_g, conv_ln_b, w_conv_out, w_attn_out, w_out, norm3_g, ffn2_w1, ffn2_w3, ffn2_w2, w_ada, b_ada, final_norm_g, w_ada_final, b_ada_final):
    c_all = jnp.concatenate([c_sample, c_prompt, jnp.zeros((N_COND - DEC_BATCH - 1, D_MODEL), F32)], axis=0)
    mod = _adaln(c_all, w_ada[0], b_ada)
    mod_f = _adaln(c_all, w_ada_final, b_ada_final[None, :])

    sinks = attn_sinks[0]
    tab_flat = rel_bias_table.reshape(-1)
    tab_ext = jnp.concatenate([rel_bias_table, sinks[None, :]], axis=0)
    th = tab_ext.T.reshape(N_KV_HEADS, GROUP, N_BUCKETS + 1).transpose(1, 0, 2)
    th = jnp.tile(th.reshape(N_HEADS, N_BUCKETS + 1), (DEC_SEQ, 1))

    xp = x_prompt[0]
    xs = x_sample.transpose(1, 0, 2).reshape(S_ROWS, D_MODEL)

    conv_w = (conv_dw_w[0], conv_dw_b, conv_ln_g, conv_ln_b)

    final = (final_norm_g[None, :], mod_f)

    def keys_on_lanes(cache):
        return cache[0].transpose(0, 2, 3, 1).reshape(DEC_BATCH, KV_WIDTH, WBUF)

    def keys_on_rows(win):
        return win.reshape(DEC_BATCH, N_KV_HEADS, HEAD_DIM, WBUF).transpose(0, 3, 1, 2)[None]

    def mixers_s(q, kv, z, ga, gc, h, mix_w):
        o, kw, vw = _attn_sample(q.reshape(DEC_SEQ, DEC_BATCH, ATTN_WIDTH),
                                 kv.reshape(DEC_SEQ, DEC_BATCH, 2 * KV_WIDTH),
                                 keys_on_lanes(cache_k), keys_on_lanes(cache_v), th)
        y, ns = _conv_sample(z.reshape(DEC_SEQ, DEC_BATCH, C_CONV), state_conv[0].transpose(1, 0, 2), *conv_w)
        h = _mix_sample(o.reshape(S_ROWS, ATTN_WIDTH), y.reshape(S_ROWS, C_CONV), ga, gc, h, mod, *mix_w)
        return h, (kw, vw, ns)

    hs, f1 = _ffn(xs, norm1_g, mod, 0, ffn1_w1[0], ffn1_w3[0], ffn1_w2[0], per_row=True,
                  tm=S_ROWS, tf=FFN_TF_CAST, cast_out=True)
    w_in_rows = D_MODEL // (SEQ // FFN_TM)
    w_in_side = (w_in[0], (w_in_rows, W_IN_SIDE_COLS),
                 lambda i, f: (i, jnp.minimum(f, IN_WIDTH // W_IN_SIDE_COLS - 1)))
    hp, (w_in_b,) = _ffn(xp, norm1_g, mod, 0, *f1, per_row=False, tm=FFN_TM, tf=FFN_TF, side=(w_in_side,))

    qs, kv_s, z_s, ga_s, gc_s = _proj(hs, norm2_g, mod, w_in_b, per_row=True, tm=S_ROWS, q_dtype=F32)
    qp, kv_p, z_p, ga_p, gc_p = _proj(hp, norm2_g, mod, w_in_b, per_row=False, tm=PROJ_TM, q_dtype=BF16)

    side = [_row_slab_side(w[0]) for w in (ffn2_w1, ffn2_w3, ffn2_w2, w_attn_out, w_conv_out, w_out)]
    attn_p, side_b = _attn_prompt(qp, kv_p, tab_flat, sinks, side)
    f2, mix_w = side_b[:3], side_b[3:]

    hs, (kw, vw, ns) = mixers_s(qs, kv_s, z_s, ga_s, gc_s, hs, mix_w)
    hp = _mix_conv_prompt(z_p, *conv_w, attn_p, ga_p, gc_p, hp, mod, *mix_w)

    ys = _ffn(hs, norm3_g, mod, 6, *f2, per_row=True, tm=S_ROWS, tf=FFN_TF, final=final)
    yp = _ffn(hp, norm3_g, mod, 6, *f2, per_row=False, tm=FFN_TM, tf=FFN_TF, final=final)

    w = min(WINDOW, SEQ)
    kv_shape = (1, 1, w, N_KV_HEADS, HEAD_DIM)
    k_win_p = kv_p[SEQ - w:, :KV_WIDTH].reshape(kv_shape)
    v_win_p = kv_p[SEQ - w:, KV_WIDTH:].reshape(kv_shape)
    conv_p_state = z_p[SEQ - HIST:].reshape(1, 1, HIST, C_CONV)
    s_shape = (1, DEC_BATCH, WBUF, N_KV_HEADS, HEAD_DIM)
    y_prompt = yp[None]
    y_sample = ys.reshape(DEC_SEQ, DEC_BATCH, D_MODEL).transpose(1, 0, 2)
    return (y_prompt, y_sample, k_win_p, v_win_p, conv_p_state,
            keys_on_rows(kw), keys_on_rows(vw), ns.transpose(1, 0, 2)[None])
```

```python
import functools

import numpy as np
import jax
import jax.numpy as jnp
from jax import lax
from jax.experimental import pallas as pl
from jax.experimental.pallas import tpu as pltpu

D_MODEL = 2048
SEQ = 8192
DEC_BATCH = 128
DEC_SEQ = 4
PAST_LEN = 16384
N_HEADS = 32
N_KV_HEADS = 8
HEAD_DIM = 64
GROUP = N_HEADS // N_KV_HEADS
ATTN_WIDTH = N_HEADS * HEAD_DIM
KV_WIDTH = N_KV_HEADS * HEAD_DIM
WINDOW = 128
Q_BLOCK = 128
ATTN_SCALE = HEAD_DIM ** -0.5
N_BUCKETS = 32
MAX_DISTANCE = 128
C_CONV = D_MODEL // 2
CONV_WIDTH = 31
HIST = CONV_WIDTH - 1
D_FF = 5632
NORM_EPS = 1e-6
NEG_INF = -1e30
IN_WIDTH = ATTN_WIDTH + 2 * KV_WIDTH + 2 * C_CONV + 2 * D_MODEL
WBUF = min(WINDOW, PAST_LEN)

S_ROWS = DEC_BATCH * DEC_SEQ
N_COND = 136
PROMPT_MOD_BLOCK = DEC_BATCH // 8
S_EXTRA = 8

VMEM_LIMIT = 60 * 1024 * 1024
SUBLANES = 8
LANES = 128

F32 = jnp.float32
BF16 = jnp.bfloat16


def _cparams(n_axes, flags=None):
    return pltpu.CompilerParams(dimension_semantics=("arbitrary",) * n_axes,
                                vmem_limit_bytes=VMEM_LIMIT, flags=flags)


def _dot(a, b):
    return jnp.dot(a, b, preferred_element_type=F32)


def _rms_mod(x, g, sh, sc):
    ms = jnp.mean(x * x, axis=-1, keepdims=True)
    y = x * lax.rsqrt(ms + NORM_EPS) * g
    return y * (1.0 + sc) + sh


def _mod_val(ref, per_row):
    return ref[...] if per_row else ref[0:1, :]


def _mod_spec(chunk, per_row, width=D_MODEL):
    if per_row:
        return pl.BlockSpec((DEC_BATCH, width), lambda *_: (0, chunk))
    return pl.BlockSpec((8, width), lambda *_: (PROMPT_MOD_BLOCK, chunk))


def _adaln_kernel(c_ref, w_ref, b_ref, o_ref, s_ref):
    @pl.when(pl.program_id(0) == 0)
    def _():
        s_ref[...] = jax.nn.silu(c_ref[...]).astype(BF16)

    o_ref[...] = _dot(s_ref[...], w_ref[...].astype(BF16)) + b_ref[...]


def _adaln(c_all, w, b, n, tn=1024):
    return pl.pallas_call(
        _adaln_kernel,
        out_shape=jax.ShapeDtypeStruct((N_COND, n), F32),
        grid=(n // tn,),
        in_specs=[pl.BlockSpec((N_COND, D_MODEL), lambda j: (0, 0)),
                  pl.BlockSpec((D_MODEL, tn), lambda j: (0, j)),
                  pl.BlockSpec((1, tn), lambda j: (0, j))],
        out_specs=pl.BlockSpec((N_COND, tn), lambda j: (0, j)),
        scratch_shapes=[pltpu.VMEM((N_COND, D_MODEL), BF16)],
        compiler_params=_cparams(1),
        name="adaln",
    )(c_all, w, b)


FFN_ROWS = 512
FFN_TM = 1024
FFN_TF = 512
FFN_TF_CAST = 256
PROJ_TM = 1024
W_IN_SIDE_COLS = 1024


def _side_cast(side_in, side_out):
    for i_ref, o_ref in zip(side_in, side_out):
        o_ref[...] = i_ref[...].astype(BF16)


def _side_specs(side):
    args = [a for a, _, _ in side]
    specs = [pl.BlockSpec(block, index_map) for _, block, index_map in side]
    shapes = [jax.ShapeDtypeStruct(a.shape, BF16) for a in args]
    return args, specs, shapes


def _ffn_kernel(*refs, tm, sub, per_row, n_f, final, cast_out, n_side):
    refs = list(refs)
    x_ref, n_ref, sh_ref, sc_ref, g_ref, w1_ref, w3_ref, w2_ref = refs[:8]
    del refs[:8]
    if final:
        gf_ref, shf_ref, scf_ref = refs[:3]
        del refs[:3]
    side_in = refs[:n_side]
    del refs[:n_side]
    o_ref = refs.pop(0)
    if cast_out:
        w1o_ref, w3o_ref, w2o_ref = refs[:3]
        del refs[:3]
    side_out = refs[:n_side]
    del refs[:n_side]
    (u_ref,) = refs
    f = pl.program_id(1)
    _side_cast(side_in, side_out)

    if cast_out:
        w1o_ref[...] = w1_ref[...].astype(BF16)
        w3o_ref[...] = w3_ref[...].astype(BF16)
        w2o_ref[...] = w2_ref[...].astype(BF16)
        w1_ref, w3_ref, w2_ref = w1o_ref, w3o_ref, w2o_ref

    def accumulate(first):
        for s in range(tm // FFN_ROWS):
            rows = slice(s * FFN_ROWS, (s + 1) * FFN_ROWS)
            u = u_ref[rows, :]
            h1 = _dot(u, w1_ref[...])
            h3 = _dot(u, w3_ref[...])
            a = (jax.nn.silu(h1) * h3).astype(BF16)
            d = _dot(a, w2_ref[...])
            if first:
                o_ref[rows, :] = d
            else:
                o_ref[rows, :] += d

    @pl.when(f == 0)
    def _():
        def body(s, carry):
            rows = pl.ds(pl.multiple_of(s * sub, sub), sub)
            u_ref[rows, :] = _rms_mod(x_ref[rows, :], n_ref[...], _mod_val(sh_ref, per_row),
                                      _mod_val(sc_ref, per_row)).astype(BF16)
            return carry
        lax.fori_loop(0, tm // sub, body, 0)
        accumulate(True)

    @pl.when(f > 0)
    def _():
        accumulate(False)

    @pl.when(f == n_f - 1)
    def _():
        def body(s, carry):
            rows = pl.ds(pl.multiple_of(s * sub, sub), sub)
            h = x_ref[rows, :] + 0.5 * _mod_val(g_ref, per_row) * o_ref[rows, :]
            if final:
                h = _rms_mod(h, gf_ref[...], _mod_val(shf_ref, per_row), _mod_val(scf_ref, per_row))
            o_ref[rows, :] = h
            return carry
        lax.fori_loop(0, tm // sub, body, 0)


def _ffn(x, norm_g, mod, chunk0, w1, w3, w2, *, per_row, tm, tf, final=None, cast_out=False, side=()):
    rows = x.shape[0]
    n_f = D_FF // tf
    sub = DEC_BATCH
    row_spec = pl.BlockSpec((tm, D_MODEL), lambda i, f: (i, 0))
    vec_spec = pl.BlockSpec((1, D_MODEL), lambda i, f: (0, 0))
    w_specs = [pl.BlockSpec((D_MODEL, tf), lambda i, f: (0, f)),
               pl.BlockSpec((D_MODEL, tf), lambda i, f: (0, f)),
               pl.BlockSpec((tf, D_MODEL), lambda i, f: (f, 0))]
    in_specs = [row_spec, vec_spec,
                _mod_spec(chunk0, per_row), _mod_spec(chunk0 + 1, per_row), _mod_spec(chunk0 + 2, per_row)]
    in_specs += w_specs
    args = [x, norm_g, mod, mod, mod, w1, w3, w2]
    if final is not None:
        gf, mod_f = final
        in_specs += [vec_spec, _mod_spec(0, per_row), _mod_spec(1, per_row)]
        args += [gf, mod_f, mod_f]
    side_args, side_specs, side_shapes = _side_specs(side)
    in_specs += side_specs
    args += side_args
    out_shape = [jax.ShapeDtypeStruct((rows, D_MODEL), F32)]
    out_specs = [row_spec]
    if cast_out:
        assert rows == tm, "every weight block must be visited exactly once"
        out_shape += [jax.ShapeDtypeStruct(w.shape, BF16) for w in (w1, w3, w2)]
        out_specs += w_specs
    out_shape += side_shapes
    out_specs += side_specs
    outs = pl.pallas_call(
        functools.partial(_ffn_kernel, tm=tm, sub=sub, per_row=per_row, n_f=n_f,
                          final=final is not None, cast_out=cast_out, n_side=len(side)),
        out_shape=out_shape,
        grid=(rows // tm, n_f),
        in_specs=in_specs,
        out_specs=out_specs,
        scratch_shapes=[pltpu.VMEM((tm, D_MODEL), BF16)],
        compiler_params=_cparams(2),
        name="ffn_final" if final is not None else "ffn",
    )(*args)
    return (outs[0], tuple(outs[1:])) if (cast_out or side) else outs[0]


PROJ_CHUNK = 256
PROJ_CW = 512


def _proj_layout():
    cw = PROJ_CW
    n_q = ATTN_WIDTH // (2 * cw)
    n_kv = 2 * KV_WIDTH // (2 * cw)
    n_glu = C_CONV // cw
    n_gate = D_MODEL // (2 * cw)
    pre = n_q + n_kv
    steps = pre + n_glu + 2 * n_gate

    def block_a(j):
        return jnp.where((j >= pre) & (j < pre + n_glu), j + pre, 2 * j)

    def block_b(j):
        return jnp.where((j >= pre) & (j < pre + n_glu), j + pre + n_glu, 2 * j + 1)

    return n_q, n_kv, n_glu, n_gate, steps, block_a, block_b


def _proj_kernel(h_ref, n_ref, sh_ref, sc_ref, wa_ref, wb_ref, q_ref, kv_ref, z_ref, ga_ref, gc_ref, u_ref,
                 *, tm, sub, per_row):
    cw = PROJ_CW
    n_q, n_kv, n_glu, n_gate, _, _, _ = _proj_layout()
    j = pl.program_id(1)

    @pl.when(j == 0)
    def _():
        def body(s, carry):
            rows = pl.ds(pl.multiple_of(s * sub, sub), sub)
            u_ref[rows, :] = _rms_mod(h_ref[rows, :], n_ref[...], _mod_val(sh_ref, per_row),
                                      _mod_val(sc_ref, per_row)).astype(BF16)
            return carry
        lax.fori_loop(0, tm // sub, body, 0)

    n_chunks = cw // PROJ_CHUNK

    def chunks(w_ref):
        for c in range(n_chunks):
            cols = slice(c * PROJ_CHUNK, (c + 1) * PROJ_CHUNK)
            yield c * PROJ_CHUNK, _dot(u_ref[...], w_ref[:, cols])

    def pair_store(o_ref, fn):
        for half, w_ref in enumerate((wa_ref, wb_ref)):
            for c0, acc in chunks(w_ref):
                o_ref[:, half * cw + c0:half * cw + c0 + PROJ_CHUNK] = fn(acc).astype(o_ref.dtype)

    lo = 0

    @pl.when(j < n_q)
    def _():
        pair_store(q_ref, lambda acc: acc)

    lo += n_q

    @pl.when((j >= lo) & (j < lo + n_kv))
    def _():
        pair_store(kv_ref, lambda acc: acc)

    lo += n_kv

    @pl.when((j >= lo) & (j < lo + n_glu))
    def _():
        for (c0, a), (_, g) in zip(chunks(wa_ref), chunks(wb_ref)):
            z_ref[:, c0:c0 + PROJ_CHUNK] = a * jax.nn.sigmoid(g)

    lo += n_glu

    @pl.when((j >= lo) & (j < lo + n_gate))
    def _():
        pair_store(ga_ref, jax.nn.sigmoid)

    lo += n_gate

    @pl.when(j >= lo)
    def _():
        pair_store(gc_ref, jax.nn.sigmoid)


def _proj(h, norm_g, mod, w, *, per_row, tm, q_dtype):
    rows = h.shape[0]
    sub = DEC_BATCH
    cw = PROJ_CW
    n_q, n_kv, n_glu, n_gate, steps, block_a, block_b = _proj_layout()

    def out_spec(width, first, count):
        return pl.BlockSpec((tm, width), lambda i, j: (i, jnp.clip(j - first, 0, count - 1)))

    wa_spec = pl.BlockSpec((D_MODEL, cw), lambda i, j: (0, block_a(j)))
    wb_spec = pl.BlockSpec((D_MODEL, cw), lambda i, j: (0, block_b(j)))
    out_shape = [jax.ShapeDtypeStruct((rows, ATTN_WIDTH), q_dtype),
                 jax.ShapeDtypeStruct((rows, 2 * KV_WIDTH), F32),
                 jax.ShapeDtypeStruct((rows, C_CONV), F32),
                 jax.ShapeDtypeStruct((rows, D_MODEL), BF16),
                 jax.ShapeDtypeStruct((rows, D_MODEL), BF16)]
    out_specs = [out_spec(2 * cw, 0, n_q),
                 out_spec(2 * cw, n_q, n_kv),
                 out_spec(cw, n_q + n_kv, n_glu),
                 out_spec(2 * cw, n_q + n_kv + n_glu, n_gate),
                 out_spec(2 * cw, n_q + n_kv + n_glu + n_gate, n_gate)]
    return pl.pallas_call(
        functools.partial(_proj_kernel, tm=tm, sub=sub, per_row=per_row),
        out_shape=out_shape,
        grid=(rows // tm, steps),
        in_specs=[pl.BlockSpec((tm, D_MODEL), lambda i, j: (i, 0)),
                  pl.BlockSpec((1, D_MODEL), lambda i, j: (0, 0)),
                  _mod_spec(3, per_row), _mod_spec(4, per_row), wa_spec, wb_spec],
        out_specs=out_specs,
        scratch_shapes=[pltpu.VMEM((tm, D_MODEL), BF16)],
        compiler_params=_cparams(2),
        name="proj",
    )(h, norm_g, mod, mod, w, w)


def _t5_bucket_np(dist):
    exact = N_BUCKETS // 2
    d = np.maximum(dist, 0)
    df = np.maximum(d, 1).astype(np.float32)
    large = exact + (np.log(df / np.float32(exact)) / np.float32(np.log(MAX_DISTANCE / exact))
                     * np.float32(N_BUCKETS - exact)).astype(np.int32)
    large = np.minimum(large, N_BUCKETS - 1)
    return np.where(d < exact, d, large).astype(np.int32)


def _prompt_codes():
    qi = np.arange(Q_BLOCK)[:, None]
    kj = np.arange(2 * Q_BLOCK)[None, :]
    dist = qi + Q_BLOCK - kj
    valid = (dist >= 0) & (dist <= WINDOW)
    return np.where(valid, _t5_bucket_np(dist), -1).astype(np.int32)


def _sample_codes():
    t = np.repeat(np.arange(DEC_SEQ), GROUP * N_KV_HEADS)[:, None]
    j = np.arange(WBUF)[None, :]
    dist = t + WBUF - j
    cache = np.where((dist >= 0) & (dist <= WINDOW), _t5_bucket_np(dist), -1)
    c = np.arange(S_EXTRA)[None, :]
    t_new = c - (S_EXTRA - DEC_SEQ)
    dist = t - t_new
    extra = np.where((t_new >= 0) & (dist >= 0) & (dist <= WINDOW), _t5_bucket_np(dist), -1)
    extra = np.where(c == 0, N_BUCKETS, extra)
    return cache.astype(np.int32), extra.astype(np.int32)


BIAS_ROWS = 64
ATTN_QB = 2


def _attn_p_kernel(*refs, n_side, n_ada):
    refs = list(refs)
    tab_ref, sink_ref, code_ref, q_ref, kvc_ref, kvp_ref = refs[:6]
    del refs[:6]
    side_in = refs[:n_side]
    del refs[:n_side]
    c_ref = refs.pop(0)
    ada_in = [(refs[2 * k], refs[2 * k + 1]) for k in range(n_ada)]
    del refs[:2 * n_ada]
    o_ref = refs.pop(0)
    side_out = refs[:n_side]
    del refs[:n_side]
    ada_out = refs[:n_ada]
    del refs[:n_ada]
    bias_ref, silu_c_ref = refs
    n = pl.program_id(0)
    _side_cast(side_in, side_out)

    @pl.when(n == 0)
    def _():
        silu_c_ref[...] = jax.nn.silu(c_ref[...]).astype(BF16)

    for (w_ref, b_ref), m_ref in zip(ada_in, ada_out):
        m_ref[...] = _dot(silu_c_ref[...], w_ref[...].astype(BF16)) + b_ref[...]

    n_keys = 2 * Q_BLOCK
    wide = GROUP * Q_BLOCK

    @pl.when(n == 0)
    def _():
        def kv_body(kv, carry):
            for g in range(GROUP):
                for r in range(n_keys // BIAS_ROWS):
                    rows = slice(r * BIAS_ROWS, (r + 1) * BIAS_ROWS)
                    code = code_ref[rows, :]
                    acc = jnp.full((BIAS_ROWS, Q_BLOCK), NEG_INF, F32)
                    for bk in range(N_BUCKETS):
                        acc = jnp.where(code == bk, tab_ref[bk * N_HEADS + kv * GROUP + g], acc)
                    bias_ref[kv, rows, g * Q_BLOCK:(g + 1) * Q_BLOCK] = acc
            return carry
        lax.fori_loop(0, N_KV_HEADS, kv_body, 0)

    ones = jnp.ones((16, n_keys), BF16)
    zeros = jnp.zeros((HEAD_DIM, wide), BF16)
    tile = (n_keys, LANES)
    key = lax.broadcasted_iota(jnp.int32, tile, 0)
    lane = lax.broadcasted_iota(jnp.int32, tile, 1)
    pen = jnp.where((lane == 0) & (key < Q_BLOCK) & (n == 0), NEG_INF, 0.0).astype(BF16)
    pen_rows = (lax.broadcasted_iota(jnp.int32, (LANES, wide), 0) == 0).astype(BF16)

    def query_block(a):
        rows = slice(a * Q_BLOCK, (a + 1) * Q_BLOCK)
        q_t = q_ref[rows, :].astype(F32).T

        def keys(cols):
            prev = kvp_ref[:, cols] if a == 0 else kvc_ref[(a - 1) * Q_BLOCK:a * Q_BLOCK, cols]
            return jnp.concatenate([prev, kvc_ref[rows, cols]], axis=0)

        def pair_kv(pair):
            k2 = keys(slice(pair * LANES, (pair + 1) * LANES)).astype(BF16)
            if a == 0:
                k2 = jnp.concatenate([k2, pen], axis=1)
            v2_t = keys(slice(KV_WIDTH + pair * LANES, KV_WIDTH + (pair + 1) * LANES)).T.astype(BF16)
            return k2, v2_t

        def scores(kv, k2):
            q4 = jnp.concatenate(
                [q_t[(kv * GROUP + g) * HEAD_DIM:(kv * GROUP + g + 1) * HEAD_DIM, :] for g in range(GROUP)],
                axis=1)
            q4 = (q4 * ATTN_SCALE).astype(BF16)
            parts = [q4, zeros] if kv % 2 == 0 else [zeros, q4]
            return _dot(k2, jnp.concatenate(parts + ([pen_rows] if a == 0 else []), axis=0))

        kvs = [pair_kv(pair) for pair in range(N_KV_HEADS // 2)]
        for kv in range(N_KV_HEADS):
            pair, parity = divmod(kv, 2)
            s = scores(kv, kvs[pair][0]) + bias_ref[kv]
            sink = jnp.concatenate(
                [jnp.full((1, Q_BLOCK), sink_ref[kv * GROUP + g], F32) for g in range(GROUP)], axis=1)
            m = jnp.maximum(jnp.max(s, axis=0, keepdims=True), sink)
            p = jnp.exp(s - m).astype(BF16)
            lhs = jnp.concatenate([kvs[pair][1][parity * HEAD_DIM:(parity + 1) * HEAD_DIM, :], ones], axis=0)
            o_t = _dot(lhs, p)
            l = o_t[HEAD_DIM:HEAD_DIM + 1, :] + jnp.exp(sink - m)
            o_n = o_t[0:HEAD_DIM, :] * (1.0 / l)
            o_kv = jnp.concatenate([o_n[:, g * Q_BLOCK:(g + 1) * Q_BLOCK] for g in range(GROUP)], axis=0)
            o_ref[rows, kv * GROUP * HEAD_DIM:(kv + 1) * GROUP * HEAD_DIM] = o_kv.T.astype(o_ref.dtype)

    for a in range(ATTN_QB):
        query_block(a)


ATTN_STEPS = SEQ // (ATTN_QB * Q_BLOCK)


def _row_slab_side(w):
    slab = w.shape[0] // ATTN_STEPS
    assert slab * ATTN_STEPS == w.shape[0] and slab % 16 == 0
    return (w, (slab, w.shape[1]), lambda n: (n, 0))


def _attn_prompt(q, kv, tab_flat, sinks, side, c_all, ada):
    step_rows = ATTN_QB * Q_BLOCK
    code_t = jnp.asarray(np.ascontiguousarray(_prompt_codes().T))
    smem = pl.BlockSpec(memory_space=pltpu.SMEM)
    side_args, side_specs, side_shapes = _side_specs(side)
    ada_args, ada_in_specs, ada_out_specs, ada_shapes = [], [], [], []
    for w, b, col0, n_cols in ada:
        tn = n_cols // ATTN_STEPS
        assert tn * ATTN_STEPS == n_cols and tn % LANES == 0 and col0 % tn == 0
        first = col0 // tn
        ada_args += [w, b]
        ada_in_specs += [pl.BlockSpec((D_MODEL, tn), lambda n, first=first: (0, first + n)),
                         pl.BlockSpec((1, tn), lambda n, first=first: (0, first + n))]
        ada_out_specs.append(pl.BlockSpec((N_COND, tn), lambda n: (0, n)))
        ada_shapes.append(jax.ShapeDtypeStruct((N_COND, n_cols), F32))
    out_spec = pl.BlockSpec((step_rows, ATTN_WIDTH), lambda n: (n, 0))
    outs = pl.pallas_call(
        functools.partial(_attn_p_kernel, n_side=len(side), n_ada=len(ada)),
        out_shape=[jax.ShapeDtypeStruct((SEQ, ATTN_WIDTH), BF16)] + side_shapes + ada_shapes,
        grid=(ATTN_STEPS,),
        in_specs=[smem, smem,
                  pl.BlockSpec((2 * Q_BLOCK, Q_BLOCK), lambda n: (0, 0)),
                  pl.BlockSpec((step_rows, ATTN_WIDTH), lambda n: (n, 0)),
                  pl.BlockSpec((step_rows, 2 * KV_WIDTH), lambda n: (n, 0)),
                  pl.BlockSpec((Q_BLOCK, 2 * KV_WIDTH), lambda n: (jnp.maximum(ATTN_QB * n - 1, 0), 0))]
        + side_specs + [pl.BlockSpec((N_COND, D_MODEL), lambda n: (0, 0))] + ada_in_specs,
        out_specs=[out_spec] + side_specs + ada_out_specs,
        scratch_shapes=[pltpu.VMEM((N_KV_HEADS, 2 * Q_BLOCK, GROUP * Q_BLOCK), F32),
                        pltpu.VMEM((N_COND, D_MODEL), BF16)],
        compiler_params=_cparams(1),
        name="attn_prompt",
    )(tab_flat, sinks, code_t, q, kv, kv, *side_args, c_all, *ada_args)
    n_side = len(side)
    return outs[0], tuple(outs[1:1 + n_side]), tuple(outs[1 + n_side:])


S_BB = 8


def _regroup_heads(x, to_group_major):
    half = HEAD_DIM
    assert LANES == 2 * half
    low = lax.broadcasted_iota(jnp.int32, (x.shape[0], LANES), 1) < half
    n_cols = ATTN_WIDTH // LANES
    outer, inner = (GROUP, N_KV_HEADS) if to_group_major else (N_KV_HEADS, GROUP)
    cols = []
    for c in range(n_cols):
        o, i = divmod(2 * c, inner)
        src = [(i + d) * outer + o for d in range(2)]
        a = x[:, (src[0] // 2) * LANES:(src[0] // 2 + 1) * LANES]
        b = x[:, (src[1] // 2) * LANES:(src[1] // 2 + 1) * LANES]
        assert src[0] % 2 == src[1] % 2
        if src[0] % 2 == 0:
            cols.append(jnp.where(low, a, pltpu.roll(b, half, axis=1)))
        else:
            cols.append(jnp.where(low, pltpu.roll(a, half, axis=1), b))
    return jnp.concatenate(cols, axis=1)


def _attn_s_kernel(th_ref, code_c_ref, code_x_ref, q_ref, kv_ref, ck_ref, cv_ref, o_ref, kw_ref, vw_ref,
                   bias_c_ref, bias_x_ref, nk_ref, nv_ref, qe_ref, qg_ref, og_ref):
    rows_all = DEC_SEQ * N_HEADS
    for t in range(DEC_SEQ):
        qg_ref[t] = _regroup_heads(q_ref[t], True)

    @pl.when(pl.program_id(0) == 0)
    def _():
        for code_ref, bias_ref in ((code_c_ref, bias_c_ref), (code_x_ref, bias_x_ref)):
            code = code_ref[...]
            acc = jnp.full(code.shape, NEG_INF, F32)
            for bk in range(N_BUCKETS + 1):
                acc = jnp.where(code == bk, th_ref[:, bk:bk + 1], acc)
            bias_ref[...] = acc
        nk_ref[...] = jnp.zeros_like(nk_ref)
        nv_ref[...] = jnp.zeros_like(nv_ref)

    lane_head = lax.broadcasted_iota(jnp.int32, (N_KV_HEADS, KV_WIDTH), 1) // HEAD_DIM
    row_head = lax.broadcasted_iota(jnp.int32, (N_KV_HEADS, KV_WIDTH), 0)
    diag = lane_head == row_head
    out_mask = (lax.broadcasted_iota(jnp.int32, (rows_all, KV_WIDTH), 0) % N_KV_HEADS
                == lax.broadcasted_iota(jnp.int32, (rows_all, KV_WIDTH), 1) // HEAD_DIM)
    new_lanes = lax.broadcasted_iota(jnp.int32, (KV_WIDTH, WBUF), 1) >= WBUF - DEC_SEQ
    bias_c = bias_c_ref[...]
    bias_x = bias_x_ref[...]
    nt_dims = (((1,), (1,)), ((), ()))
    x0 = WBUF - S_EXTRA

    for b in range(S_BB):
        for t in range(DEC_SEQ):
            nk_ref[WBUF - DEC_SEQ + t:WBUF - DEC_SEQ + t + 1, :] = kv_ref[t, b:b + 1, 0:KV_WIDTH]
            nv_ref[WBUF - DEC_SEQ + t:WBUF - DEC_SEQ + t + 1, :] = kv_ref[t, b:b + 1, KV_WIDTH:2 * KV_WIDTH]
            for g in range(GROUP):
                q_row = qg_ref[t, b:b + 1, g * KV_WIDTH:(g + 1) * KV_WIDTH]
                piece = jnp.where(diag, jnp.broadcast_to(q_row, (N_KV_HEADS, KV_WIDTH)), 0.0)
                r0 = (t * GROUP + g) * N_KV_HEADS
                qe_ref[r0:r0 + N_KV_HEADS, :] = piece
        k_t = ck_ref[b]
        v_t = cv_ref[b]
        kw_ref[b] = jnp.where(new_lanes, nk_ref[...].T, pltpu.roll(k_t, WBUF - DEC_SEQ, axis=1))
        vw_ref[b] = jnp.where(new_lanes, nv_ref[...].T, pltpu.roll(v_t, WBUF - DEC_SEQ, axis=1))

        qe = (qe_ref[...] * ATTN_SCALE).astype(BF16)
        s_c = _dot(qe, k_t.astype(BF16)) + bias_c
        s_x = lax.dot_general(qe, nk_ref[x0:WBUF, :].astype(BF16), nt_dims,
                              preferred_element_type=F32) + bias_x
        m = jnp.maximum(jnp.max(s_c, axis=-1, keepdims=True), jnp.max(s_x, axis=-1, keepdims=True))
        p_c = jnp.exp(s_c - m)
        p_x = jnp.exp(s_x - m)
        l = jnp.sum(p_c, axis=-1, keepdims=True) + jnp.sum(p_x, axis=-1, keepdims=True)
        o = lax.dot_general(p_c.astype(BF16), v_t.astype(BF16), nt_dims, preferred_element_type=F32)
        o = (o + _dot(p_x.astype(BF16), nv_ref[x0:WBUF, :].astype(BF16))) / l
        o = jnp.where(out_mask, o, 0.0)
        for t in range(DEC_SEQ):
            for g in range(GROUP):
                r0 = (t * GROUP + g) * N_KV_HEADS
                row = jnp.sum(o[r0:r0 + N_KV_HEADS, :], axis=0, keepdims=True)
                og_ref[t, b:b + 1, g * KV_WIDTH:(g + 1) * KV_WIDTH] = row
    for t in range(DEC_SEQ):
        o_ref[t] = _regroup_heads(og_ref[t], False)


def _attn_sample(q, kv, cache_k_t, cache_v_t, th):
    code_c, code_x = (jnp.asarray(c) for c in _sample_codes())
    rows_all = DEC_SEQ * N_HEADS
    full = lambda a: pl.BlockSpec(a.shape, lambda i: (0,) * a.ndim)
    tb_spec = lambda w: pl.BlockSpec((DEC_SEQ, S_BB, w), lambda i: (0, i, 0))
    cache_spec = pl.BlockSpec((S_BB, KV_WIDTH, WBUF), lambda i: (i, 0, 0))
    cache_shape = jax.ShapeDtypeStruct((DEC_BATCH, KV_WIDTH, WBUF), F32)
    return pl.pallas_call(
        _attn_s_kernel,
        out_shape=(jax.ShapeDtypeStruct((DEC_SEQ, DEC_BATCH, ATTN_WIDTH), F32), cache_shape, cache_shape),
        grid=(DEC_BATCH // S_BB,),
        in_specs=[full(th), full(code_c), full(code_x),
                  tb_spec(ATTN_WIDTH), tb_spec(2 * KV_WIDTH), cache_spec, cache_spec],
        out_specs=(tb_spec(ATTN_WIDTH), cache_spec, cache_spec),
        scratch_shapes=[pltpu.VMEM((rows_all, WBUF), F32),
                        pltpu.VMEM((rows_all, S_EXTRA), F32),
                        pltpu.VMEM((WBUF, KV_WIDTH), F32),
                        pltpu.VMEM((WBUF, KV_WIDTH), F32),
                        pltpu.VMEM((rows_all, KV_WIDTH), F32),
                        pltpu.VMEM((DEC_SEQ, S_BB, ATTN_WIDTH), F32),
                        pltpu.VMEM((DEC_SEQ, S_BB, ATTN_WIDTH), F32)],
        compiler_params=_cparams(1),
        name="attn_sample",
    )(th, code_c, code_x, q, kv, cache_k_t, cache_v_t)


CONV_RC = 32
CONV_ROWS = 128
CONV_HALO = 32


def _ln_silu(acc, lg, lb):
    mu = jnp.mean(acc, axis=-1, keepdims=True)
    xc = acc - mu
    var = jnp.mean(xc * xc, axis=-1, keepdims=True)
    return jax.nn.silu(xc * lax.rsqrt(var + NORM_EPS) * lg + lb)


def _conv_stage(first, zc_ref, zh_ref, s_ref):
    s_ref[0:CONV_HALO, :] = jnp.where(first, 0.0, zh_ref[...])
    s_ref[CONV_HALO:, :] = zc_ref[...]


def _conv_rows(rc, w_ref, b_ref, lg_ref, lb_ref, y_ref, s_ref, c_ref):
    off = CONV_HALO - HIST
    groups = [[j for j in range(CONV_WIDTH) if (j + off) % SUBLANES == r] for r in range(SUBLANES)]
    t0 = rc * CONV_ROWS
    for lc in range(C_CONV // LANES):
        lanes = slice(lc * LANES, (lc + 1) * LANES)
        out = jnp.broadcast_to(b_ref[:, lanes], (CONV_ROWS, LANES))
        for r, taps in enumerate(groups):
            n_rows = CONV_ROWS + (SUBLANES if r else 0)
            part = None
            for j in taps:
                base = t0 + (j + off) - r
                term = w_ref[j:j + 1, lanes] * s_ref[base:base + n_rows, lanes]
                part = term if part is None else part + term
            out = out + part[r:r + CONV_ROWS, :]
        c_ref[t0:t0 + CONV_ROWS, lanes] = out
    for r in range(CONV_ROWS // CONV_RC):
        rows = slice(t0 + r * CONV_RC, t0 + (r + 1) * CONV_RC)
        y_ref[rows, :] = _ln_silu(c_ref[rows, :], lg_ref[...], lb_ref[...]).astype(BF16)


def _mix_conv_kernel(zc_ref, zh_ref, w_ref, b_ref, lg_ref, lb_ref,
                     at_ref, ga_ref, gc_ref, h_ref, g2_ref, wao_ref, wco_ref, wout_ref,
                     o_ref, s_ref, c_ref, y_ref, *, tm):
    _conv_stage(pl.program_id(0) == 0, zc_ref, zh_ref, s_ref)
    for rc in range(tm // CONV_ROWS):
        _conv_rows(rc, w_ref, b_ref, lg_ref, lb_ref, y_ref, s_ref, c_ref)
    a = _dot(at_ref[...], wao_ref[...])
    c = _dot(y_ref[...], wco_ref[...])
    merged = (ga_ref[...].astype(F32) * a + gc_ref[...].astype(F32) * c).astype(BF16)
    r = _dot(merged, wout_ref[...])
    o_ref[...] = h_ref[...] + g2_ref[0:1, :] * r


def _mix_conv_prompt(z, dw_w, dw_b, ln_g, ln_b, attn, ga, gc, h, mod, w_ao, w_co, w_out, tm=256):
    rows = h.shape[0]
    n = rows // tm
    ratio = tm // CONV_HALO
    vec = pl.BlockSpec((1, C_CONV), lambda s: (0, 0))
    row = lambda w: pl.BlockSpec((tm, w), lambda s: (s, 0))
    resident = lambda shape: pl.BlockSpec(shape, lambda s: (0, 0), pipeline_mode=pl.Buffered(1))
    return pl.pallas_call(
        functools.partial(_mix_conv_kernel, tm=tm),
        out_shape=jax.ShapeDtypeStruct((rows, D_MODEL), F32),
        grid=(n,),
        in_specs=[row(C_CONV),
                  pl.BlockSpec((CONV_HALO, C_CONV), lambda s: (jnp.maximum(s * ratio - 1, 0), 0)),
                  pl.BlockSpec((CONV_WIDTH, C_CONV), lambda s: (0, 0)), vec, vec, vec,
                  row(ATTN_WIDTH), row(D_MODEL), row(D_MODEL), row(D_MODEL),
                  _mod_spec(0, False),
                  resident((ATTN_WIDTH, D_MODEL)), resident((C_CONV, D_MODEL)), resident((D_MODEL, D_MODEL))],
        out_specs=row(D_MODEL),
        scratch_shapes=[pltpu.VMEM((tm + CONV_HALO, C_CONV), F32), pltpu.VMEM((tm, C_CONV), F32),
                        pltpu.VMEM((tm, C_CONV), BF16)],
        compiler_params=_cparams(1),
        name="mix_conv",
    )(z, z, dw_w, dw_b, ln_g, ln_b, attn, ga, gc, h, mod, w_ao, w_co, w_out)


CONV_S_BB = 32


def _conv_s_kernel(z_ref, hist_ref, w_ref, b_ref, lg_ref, lb_ref, y_ref, ns_ref):
    for t in range(DEC_SEQ):
        acc = jnp.broadcast_to(b_ref[...], (CONV_S_BB, C_CONV))
        for j in range(CONV_WIDTH):
            i = t + j
            src = hist_ref[i] if i < HIST else z_ref[i - HIST]
            acc = acc + w_ref[j:j + 1, :] * src
        y_ref[t] = _ln_silu(acc, lg_ref[...], lb_ref[...]).astype(BF16)
    keep = HIST - DEC_SEQ
    ns_ref[0:keep] = hist_ref[DEC_SEQ:HIST]
    ns_ref[keep:HIST] = z_ref[...]


def _conv_sample(z, hist_t, dw_w, dw_b, ln_g, ln_b):
    vec = pl.BlockSpec((1, C_CONV), lambda i: (0, 0))
    tb = pl.BlockSpec((DEC_SEQ, CONV_S_BB, C_CONV), lambda i: (0, i, 0))
    st = pl.BlockSpec((HIST, CONV_S_BB, C_CONV), lambda i: (0, i, 0))
    return pl.pallas_call(
        _conv_s_kernel,
        out_shape=(jax.ShapeDtypeStruct((DEC_SEQ, DEC_BATCH, C_CONV), BF16),
                   jax.ShapeDtypeStruct(hist_t.shape, F32)),
        grid=(DEC_BATCH // CONV_S_BB,),
        in_specs=[tb, st, pl.BlockSpec((CONV_WIDTH, C_CONV), lambda i: (0, 0)), vec, vec, vec],
        out_specs=(tb, st),
        compiler_params=_cparams(1),
        name="conv_sample",
    )(z, hist_t, dw_w, dw_b, ln_g, ln_b)


def _mix_s_kernel(at_ref, y_ref, ga_ref, gc_ref, h_ref, g2_ref, wao_ref, wco_ref, wout_ref, o_ref):
    a = _dot(at_ref[...].astype(BF16), wao_ref[...])
    c = _dot(y_ref[...], wco_ref[...])
    merged = (ga_ref[...].astype(F32) * a + gc_ref[...].astype(F32) * c).astype(BF16)
    r = _dot(merged, wout_ref[...])
    o_ref[...] = h_ref[...] + g2_ref[...] * r


def _mix_sample(attn, y, ga, gc, h, mod, w_ao, w_co, w_out):
    row = lambda w: pl.BlockSpec((DEC_BATCH, w), lambda t: (t, 0))
    resident = lambda shape: pl.BlockSpec(shape, lambda t: (0, 0), pipeline_mode=pl.Buffered(1))
    return pl.pallas_call(
        _mix_s_kernel,
        out_shape=jax.ShapeDtypeStruct((S_ROWS, D_MODEL), F32),
        grid=(DEC_SEQ,),
        in_specs=[row(ATTN_WIDTH), row(C_CONV), row(D_MODEL), row(D_MODEL), row(D_MODEL),
                  _mod_spec(0, True),
                  resident((ATTN_WIDTH, D_MODEL)), resident((C_CONV, D_MODEL)), resident((D_MODEL, D_MODEL))],
        out_specs=row(D_MODEL),
        compiler_params=_cparams(1),
        name="mix_sample",
    )(attn, y, ga, gc, h, mod, w_ao, w_co, w_out)


def kernel(x_prompt, x_sample, cache_k, cache_v, state_conv, c_prompt, c_sample, rel_bias_table, norm1_g, ffn1_w1, ffn1_w3, ffn1_w2, norm2_g, w_in, attn_sinks, conv_dw_w, conv_dw_b, conv_ln_g, conv_ln_b, w_conv_out, w_attn_out, w_out, norm3_g, ffn2_w1, ffn2_w3, ffn2_w2, w_ada, b_ada, final_norm_g, w_ada_final, b_ada_final):
    c_all = jnp.concatenate([c_sample, c_prompt, jnp.zeros((N_COND - DEC_BATCH - 1, D_MODEL), F32)], axis=0)
    early = 5 * D_MODEL
    mod = _adaln(c_all, w_ada[0], b_ada, early)
    late_ada = [(w_ada[0], b_ada, early, 4 * D_MODEL), (w_ada_final, b_ada_final[None, :], 0, 2 * D_MODEL)]

    sinks = attn_sinks[0]
    tab_flat = rel_bias_table.reshape(-1)
    tab_ext = jnp.concatenate([rel_bias_table, sinks[None, :]], axis=0)
    th = tab_ext.T.reshape(N_KV_HEADS, GROUP, N_BUCKETS + 1).transpose(1, 0, 2)
    th = jnp.tile(th.reshape(N_HEADS, N_BUCKETS + 1), (DEC_SEQ, 1))

    xp = x_prompt[0]
    xs = x_sample.transpose(1, 0, 2).reshape(S_ROWS, D_MODEL)

    conv_w = (conv_dw_w[0], conv_dw_b, conv_ln_g, conv_ln_b)


    def keys_on_lanes(cache):
        return cache[0].transpose(0, 2, 3, 1).reshape(DEC_BATCH, KV_WIDTH, WBUF)

    def keys_on_rows(win):
        return win.reshape(DEC_BATCH, N_KV_HEADS, HEAD_DIM, WBUF).transpose(0, 3, 1, 2)[None]

    def mixers_s(q, kv, z, ga, gc, h, mod_late, mix_w):
        o, kw, vw = _attn_sample(q.reshape(DEC_SEQ, DEC_BATCH, ATTN_WIDTH),
                                 kv.reshape(DEC_SEQ, DEC_BATCH, 2 * KV_WIDTH),
                                 keys_on_lanes(cache_k), keys_on_lanes(cache_v), th)
        y, ns = _conv_sample(z.reshape(DEC_SEQ, DEC_BATCH, C_CONV), state_conv[0].transpose(1, 0, 2), *conv_w)
        h = _mix_sample(o.reshape(S_ROWS, ATTN_WIDTH), y.reshape(S_ROWS, C_CONV), ga, gc, h, mod_late, *mix_w)
        return h, (kw, vw, ns)

    hs, f1 = _ffn(xs, norm1_g, mod, 0, ffn1_w1[0], ffn1_w3[0], ffn1_w2[0], per_row=True,
                  tm=S_ROWS, tf=FFN_TF_CAST, cast_out=True)
    w_in_rows = D_MODEL // (SEQ // FFN_TM)
    w_in_side = (w_in[0], (w_in_rows, W_IN_SIDE_COLS),
                 lambda i, f: (i, jnp.minimum(f, IN_WIDTH // W_IN_SIDE_COLS - 1)))
    hp, (w_in_b,) = _ffn(xp, norm1_g, mod, 0, *f1, per_row=False, tm=FFN_TM, tf=FFN_TF, side=(w_in_side,))

    qs, kv_s, z_s, ga_s, gc_s = _proj(hs, norm2_g, mod, w_in_b, per_row=True, tm=S_ROWS, q_dtype=F32)
    qp, kv_p, z_p, ga_p, gc_p = _proj(hp, norm2_g, mod, w_in_b, per_row=False, tm=PROJ_TM, q_dtype=BF16)

    side = [_row_slab_side(w[0]) for w in (ffn2_w1, ffn2_w3, ffn2_w2, w_attn_out, w_conv_out, w_out)]
    attn_p, side_b, (mod_late, mod_f) = _attn_prompt(qp, kv_p, tab_flat, sinks, side, c_all, late_ada)
    f2, mix_w = side_b[:3], side_b[3:]
    final = (final_norm_g[None, :], mod_f)

    hs, (kw, vw, ns) = mixers_s(qs, kv_s, z_s, ga_s, gc_s, hs, mod_late, mix_w)
    hp = _mix_conv_prompt(z_p, *conv_w, attn_p, ga_p, gc_p, hp, mod_late, *mix_w)

    ys = _ffn(hs, norm3_g, mod_late, 1, *f2, per_row=True, tm=S_ROWS, tf=FFN_TF, final=final)
    yp = _ffn(hp, norm3_g, mod_late, 1, *f2, per_row=False, tm=FFN_TM, tf=FFN_TF, final=final)

    w = min(WINDOW, SEQ)
    kv_shape = (1, 1, w, N_KV_HEADS, HEAD_DIM)
    k_win_p = kv_p[SEQ - w:, :KV_WIDTH].reshape(kv_shape)
    v_win_p = kv_p[SEQ - w:, KV_WIDTH:].reshape(kv_shape)
    conv_p_state = z_p[SEQ - HIST:].reshape(1, 1, HIST, C_CONV)
    s_shape = (1, DEC_BATCH, WBUF, N_KV_HEADS, HEAD_DIM)
    y_prompt = yp[None]
    y_sample = ys.reshape(DEC_SEQ, DEC_BATCH, D_MODEL).transpose(1, 0, 2)
    return (y_prompt, y_sample, k_win_p, v_win_p, conv_p_state,
            keys_on_rows(kw), keys_on_rows(vw), ns.transpose(1, 0, 2)[None])
```

```python
import functools

import numpy as np
import jax
import jax.numpy as jnp
from jax import lax
from jax.experimental import pallas as pl
from jax.experimental.pallas import tpu as pltpu

D_MODEL = 2048
SEQ = 8192
DEC_BATCH = 128
DEC_SEQ = 4
PAST_LEN = 16384
N_HEADS = 32
N_KV_HEADS = 8
HEAD_DIM = 64
GROUP = N_HEADS // N_KV_HEADS
ATTN_WIDTH = N_HEADS * HEAD_DIM
KV_WIDTH = N_KV_HEADS * HEAD_DIM
WINDOW = 128
Q_BLOCK = 128
ATTN_SCALE = HEAD_DIM ** -0.5
N_BUCKETS = 32
MAX_DISTANCE = 128
C_CONV = D_MODEL // 2
CONV_WIDTH = 31
HIST = CONV_WIDTH - 1
D_FF = 5632
NORM_EPS = 1e-6
NEG_INF = -1e30
IN_WIDTH = ATTN_WIDTH + 2 * KV_WIDTH + 2 * C_CONV + 2 * D_MODEL
WBUF = min(WINDOW, PAST_LEN)

S_ROWS = DEC_BATCH * DEC_SEQ
N_COND = 136
PROMPT_MOD_BLOCK = DEC_BATCH // 8
S_EXTRA = 8

VMEM_LIMIT = 60 * 1024 * 1024
SUBLANES = 8
LANES = 128

F32 = jnp.float32
BF16 = jnp.bfloat16


def _cparams(n_axes, flags=None):
    return pltpu.CompilerParams(dimension_semantics=("arbitrary",) * n_axes,
                                vmem_limit_bytes=VMEM_LIMIT, flags=flags)


def _dot(a, b):
    return jnp.dot(a, b, preferred_element_type=F32)


def _rms_mod(x, g, sh, sc):
    ms = jnp.mean(x * x, axis=-1, keepdims=True)
    y = x * lax.rsqrt(ms + NORM_EPS) * g
    return y * (1.0 + sc) + sh


def _mod_val(ref, per_row):
    return ref[...] if per_row else ref[0:1, :]


def _mod_spec(chunk, per_row, width=D_MODEL):
    if per_row:
        return pl.BlockSpec((DEC_BATCH, width), lambda *_: (0, chunk))
    return pl.BlockSpec((8, width), lambda *_: (PROMPT_MOD_BLOCK, chunk))


def _adaln_kernel(c_ref, w_ref, b_ref, o_ref, s_ref):
    @pl.when(pl.program_id(0) == 0)
    def _():
        s_ref[...] = jax.nn.silu(c_ref[...]).astype(BF16)

    o_ref[...] = _dot(s_ref[...], w_ref[...].astype(BF16)) + b_ref[...]


def _adaln(c_all, w, b, n, tn=1024):
    return pl.pallas_call(
        _adaln_kernel,
        out_shape=jax.ShapeDtypeStruct((N_COND, n), F32),
        grid=(n // tn,),
        in_specs=[pl.BlockSpec((N_COND, D_MODEL), lambda j: (0, 0)),
                  pl.BlockSpec((D_MODEL, tn), lambda j: (0, j)),
                  pl.BlockSpec((1, tn), lambda j: (0, j))],
        out_specs=pl.BlockSpec((N_COND, tn), lambda j: (0, j)),
        scratch_shapes=[pltpu.VMEM((N_COND, D_MODEL), BF16)],
        compiler_params=_cparams(1),
        name="adaln",
    )(c_all, w, b)


FFN_ROWS = 512
FFN_TM = 1024
FFN_TF = 512
FFN_TF_CAST = 256
PROJ_TM = 1024
W_IN_SIDE_COLS = 1024


def _side_cast(side_in, side_out):
    for i_ref, o_ref in zip(side_in, side_out):
        o_ref[...] = i_ref[...].astype(BF16)


def _side_specs(side):
    args = [a for a, _, _ in side]
    specs = [pl.BlockSpec(block, index_map) for _, block, index_map in side]
    shapes = [jax.ShapeDtypeStruct(a.shape, BF16) for a in args]
    return args, specs, shapes


def _ffn_kernel(*refs, tm, sub, per_row, n_f, final, cast_out, n_side):
    refs = list(refs)
    x_ref, n_ref, sh_ref, sc_ref, g_ref, w1_ref, w3_ref, w2_ref = refs[:8]
    del refs[:8]
    if final:
        gf_ref, shf_ref, scf_ref = refs[:3]
        del refs[:3]
    side_in = refs[:n_side]
    del refs[:n_side]
    o_ref = refs.pop(0)
    if cast_out:
        w1o_ref, w3o_ref, w2o_ref = refs[:3]
        del refs[:3]
    side_out = refs[:n_side]
    del refs[:n_side]
    (u_ref,) = refs
    f = pl.program_id(1)
    _side_cast(side_in, side_out)

    if cast_out:
        w1o_ref[...] = w1_ref[...].astype(BF16)
        w3o_ref[...] = w3_ref[...].astype(BF16)
        w2o_ref[...] = w2_ref[...].astype(BF16)
        w1_ref, w3_ref, w2_ref = w1o_ref, w3o_ref, w2o_ref

    def accumulate(first, last):
        for s in range(tm // FFN_ROWS):
            rows = slice(s * FFN_ROWS, (s + 1) * FFN_ROWS)
            subs = [slice(r0, r0 + sub) for r0 in range(s * FFN_ROWS, (s + 1) * FFN_ROWS, sub)]
            if first:
                for r in subs:
                    u_ref[r, :] = _rms_mod(x_ref[r, :], n_ref[...], _mod_val(sh_ref, per_row),
                                           _mod_val(sc_ref, per_row)).astype(BF16)
            u = u_ref[rows, :]
            h1 = _dot(u, w1_ref[...])
            h3 = _dot(u, w3_ref[...])
            a = (jax.nn.silu(h1) * h3).astype(BF16)
            d = _dot(a, w2_ref[...])
            if first:
                o_ref[rows, :] = d
            else:
                o_ref[rows, :] += d
            if last:
                for r in subs:
                    h = x_ref[r, :] + 0.5 * _mod_val(g_ref, per_row) * o_ref[r, :]
                    if final:
                        h = _rms_mod(h, gf_ref[...], _mod_val(shf_ref, per_row), _mod_val(scf_ref, per_row))
                    o_ref[r, :] = h

    assert n_f > 2

    @pl.when(f == 0)
    def _():
        accumulate(True, False)

    @pl.when((f > 0) & (f < n_f - 1))
    def _():
        accumulate(False, False)

    @pl.when(f == n_f - 1)
    def _():
        accumulate(False, True)


def _ffn(x, norm_g, mod, chunk0, w1, w3, w2, *, per_row, tm, tf, final=None, cast_out=False, side=()):
    rows = x.shape[0]
    n_f = D_FF // tf
    sub = DEC_BATCH
    row_spec = pl.BlockSpec((tm, D_MODEL), lambda i, f: (i, 0))
    vec_spec = pl.BlockSpec((1, D_MODEL), lambda i, f: (0, 0))
    w_specs = [pl.BlockSpec((D_MODEL, tf), lambda i, f: (0, f)),
               pl.BlockSpec((D_MODEL, tf), lambda i, f: (0, f)),
               pl.BlockSpec((tf, D_MODEL), lambda i, f: (f, 0))]
    in_specs = [row_spec, vec_spec,
                _mod_spec(chunk0, per_row), _mod_spec(chunk0 + 1, per_row), _mod_spec(chunk0 + 2, per_row)]
    in_specs += w_specs
    args = [x, norm_g, mod, mod, mod, w1, w3, w2]
    if final is not None:
        gf, mod_f = final
        in_specs += [vec_spec, _mod_spec(0, per_row), _mod_spec(1, per_row)]
        args += [gf, mod_f, mod_f]
    side_args, side_specs, side_shapes = _side_specs(side)
    in_specs += side_specs
    args += side_args
    out_shape = [jax.ShapeDtypeStruct((rows, D_MODEL), F32)]
    out_specs = [row_spec]
    if cast_out:
        assert rows == tm, "every weight block must be visited exactly once"
        out_shape += [jax.ShapeDtypeStruct(w.shape, BF16) for w in (w1, w3, w2)]
        out_specs += w_specs
    out_shape += side_shapes
    out_specs += side_specs
    outs = pl.pallas_call(
        functools.partial(_ffn_kernel, tm=tm, sub=sub, per_row=per_row, n_f=n_f,
                          final=final is not None, cast_out=cast_out, n_side=len(side)),
        out_shape=out_shape,
        grid=(rows // tm, n_f),
        in_specs=in_specs,
        out_specs=out_specs,
        scratch_shapes=[pltpu.VMEM((tm, D_MODEL), BF16)],
        compiler_params=_cparams(2),
        name="ffn_final" if final is not None else "ffn",
    )(*args)
    return (outs[0], tuple(outs[1:])) if (cast_out or side) else outs[0]


PROJ_CHUNK = 256
PROJ_NORM_ROWS = 512
PROJ_CW = 512


def _proj_layout():
    cw = PROJ_CW
    n_q = ATTN_WIDTH // (2 * cw)
    n_kv = 2 * KV_WIDTH // (2 * cw)
    n_glu = C_CONV // cw
    n_gate = D_MODEL // (2 * cw)
    pre = n_q + n_kv
    steps = pre + n_glu + 2 * n_gate

    def block_a(j):
        return jnp.where((j >= pre) & (j < pre + n_glu), j + pre, 2 * j)

    def block_b(j):
        return jnp.where((j >= pre) & (j < pre + n_glu), j + pre + n_glu, 2 * j + 1)

    return n_q, n_kv, n_glu, n_gate, steps, block_a, block_b


def _proj_kernel(h_ref, n_ref, sh_ref, sc_ref, wa_ref, wb_ref, q_ref, kv_ref, z_ref, ga_ref, gc_ref, u_ref,
                 *, tm, sub, per_row):
    cw = PROJ_CW
    n_q, n_kv, n_glu, n_gate, _, _, _ = _proj_layout()
    j = pl.program_id(1)

    n_chunks = cw // PROJ_CHUNK

    def chunks(w_ref, rows=slice(None)):
        for c in range(n_chunks):
            cols = slice(c * PROJ_CHUNK, (c + 1) * PROJ_CHUNK)
            yield c * PROJ_CHUNK, _dot(u_ref[rows, :], w_ref[:, cols])

    def pair_store(o_ref, fn, rows=slice(None)):
        for half, w_ref in enumerate((wa_ref, wb_ref)):
            for c0, acc in chunks(w_ref, rows):
                o_ref[rows, half * cw + c0:half * cw + c0 + PROJ_CHUNK] = fn(acc).astype(o_ref.dtype)

    @pl.when(j == 0)
    def _():
        for g0 in range(0, tm, PROJ_NORM_ROWS):
            for r0 in range(g0, g0 + PROJ_NORM_ROWS, sub):
                r = slice(r0, r0 + sub)
                u_ref[r, :] = _rms_mod(h_ref[r, :], n_ref[...], _mod_val(sh_ref, per_row),
                                       _mod_val(sc_ref, per_row)).astype(BF16)
            pair_store(q_ref, lambda acc: acc, slice(g0, g0 + PROJ_NORM_ROWS))

    lo = 0

    @pl.when((j > 0) & (j < n_q))
    def _():
        pair_store(q_ref, lambda acc: acc)

    lo += n_q

    @pl.when((j >= lo) & (j < lo + n_kv))
    def _():
        pair_store(kv_ref, lambda acc: acc)

    lo += n_kv

    @pl.when((j >= lo) & (j < lo + n_glu))
    def _():
        for (c0, a), (_, g) in zip(chunks(wa_ref), chunks(wb_ref)):
            z_ref[:, c0:c0 + PROJ_CHUNK] = a * jax.nn.sigmoid(g)

    lo += n_glu

    @pl.when((j >= lo) & (j < lo + n_gate))
    def _():
        pair_store(ga_ref, jax.nn.sigmoid)

    lo += n_gate

    @pl.when(j >= lo)
    def _():
        pair_store(gc_ref, jax.nn.sigmoid)


def _proj(h, norm_g, mod, w, *, per_row, tm, q_dtype):
    rows = h.shape[0]
    sub = DEC_BATCH
    cw = PROJ_CW
    n_q, n_kv, n_glu, n_gate, steps, block_a, block_b = _proj_layout()

    def out_spec(width, first, count):
        return pl.BlockSpec((tm, width), lambda i, j: (i, jnp.clip(j - first, 0, count - 1)))

    wa_spec = pl.BlockSpec((D_MODEL, cw), lambda i, j: (0, block_a(j)))
    wb_spec = pl.BlockSpec((D_MODEL, cw), lambda i, j: (0, block_b(j)))
    out_shape = [jax.ShapeDtypeStruct((rows, ATTN_WIDTH), q_dtype),
                 jax.ShapeDtypeStruct((rows, 2 * KV_WIDTH), F32),
                 jax.ShapeDtypeStruct((rows, C_CONV), F32),
                 jax.ShapeDtypeStruct((rows, D_MODEL), BF16),
                 jax.ShapeDtypeStruct((rows, D_MODEL), BF16)]
    out_specs = [out_spec(2 * cw, 0, n_q),
                 out_spec(2 * cw, n_q, n_kv),
                 out_spec(cw, n_q + n_kv, n_glu),
                 out_spec(2 * cw, n_q + n_kv + n_glu, n_gate),
                 out_spec(2 * cw, n_q + n_kv + n_glu + n_gate, n_gate)]
    return pl.pallas_call(
        functools.partial(_proj_kernel, tm=tm, sub=sub, per_row=per_row),
        out_shape=out_shape,
        grid=(rows // tm, steps),
        in_specs=[pl.BlockSpec((tm, D_MODEL), lambda i, j: (i, 0)),
                  pl.BlockSpec((1, D_MODEL), lambda i, j: (0, 0)),
                  _mod_spec(3, per_row), _mod_spec(4, per_row), wa_spec, wb_spec],
        out_specs=out_specs,
        scratch_shapes=[pltpu.VMEM((tm, D_MODEL), BF16)],
        compiler_params=_cparams(2),
        name="proj",
    )(h, norm_g, mod, mod, w, w)


def _t5_bucket_np(dist):
    exact = N_BUCKETS // 2
    d = np.maximum(dist, 0)
    df = np.maximum(d, 1).astype(np.float32)
    large = exact + (np.log(df / np.float32(exact)) / np.float32(np.log(MAX_DISTANCE / exact))
                     * np.float32(N_BUCKETS - exact)).astype(np.int32)
    large = np.minimum(large, N_BUCKETS - 1)
    return np.where(d < exact, d, large).astype(np.int32)


def _prompt_codes():
    qi = np.arange(Q_BLOCK)[:, None]
    kj = np.arange(2 * Q_BLOCK)[None, :]
    dist = qi + Q_BLOCK - kj
    valid = (dist >= 0) & (dist <= WINDOW)
    return np.where(valid, _t5_bucket_np(dist), -1).astype(np.int32)


def _sample_codes():
    t = np.repeat(np.arange(DEC_SEQ), GROUP * N_KV_HEADS)[:, None]
    j = np.arange(WBUF)[None, :]
    dist = t + WBUF - j
    cache = np.where((dist >= 0) & (dist <= WINDOW), _t5_bucket_np(dist), -1)
    c = np.arange(S_EXTRA)[None, :]
    t_new = c - (S_EXTRA - DEC_SEQ)
    dist = t - t_new
    extra = np.where((t_new >= 0) & (dist >= 0) & (dist <= WINDOW), _t5_bucket_np(dist), -1)
    extra = np.where(c == 0, N_BUCKETS, extra)
    return cache.astype(np.int32), extra.astype(np.int32)


BIAS_ROWS = 64
ATTN_QB = 2


def _attn_p_kernel(*refs, n_side, n_ada):
    refs = list(refs)
    tab_ref, sink_ref, code_ref, q_ref, kvc_ref, kvp_ref = refs[:6]
    del refs[:6]
    side_in = refs[:n_side]
    del refs[:n_side]
    c_ref = refs.pop(0)
    ada_in = [(refs[2 * k], refs[2 * k + 1]) for k in range(n_ada)]
    del refs[:2 * n_ada]
    o_ref = refs.pop(0)
    side_out = refs[:n_side]
    del refs[:n_side]
    ada_out = refs[:n_ada]
    del refs[:n_ada]
    bias_ref, silu_c_ref = refs
    n = pl.program_id(0)
    _side_cast(side_in, side_out)

    @pl.when(n == 0)
    def _():
        silu_c_ref[...] = jax.nn.silu(c_ref[...]).astype(BF16)

    for (w_ref, b_ref), m_ref in zip(ada_in, ada_out):
        m_ref[...] = _dot(silu_c_ref[...], w_ref[...].astype(BF16)) + b_ref[...]

    n_keys = 2 * Q_BLOCK
    wide = GROUP * Q_BLOCK

    @pl.when(n == 0)
    def _():
        def kv_body(kv, carry):
            for g in range(GROUP):
                for r in range(n_keys // BIAS_ROWS):
                    rows = slice(r * BIAS_ROWS, (r + 1) * BIAS_ROWS)
                    code = code_ref[rows, :]
                    acc = jnp.full((BIAS_ROWS, Q_BLOCK), NEG_INF, F32)
                    for bk in range(N_BUCKETS):
                        acc = jnp.where(code == bk, tab_ref[bk * N_HEADS + kv * GROUP + g], acc)
                    bias_ref[kv, rows, g * Q_BLOCK:(g + 1) * Q_BLOCK] = acc
            return carry
        lax.fori_loop(0, N_KV_HEADS, kv_body, 0)

    ones = jnp.ones((16, n_keys), BF16)
    zeros = jnp.zeros((HEAD_DIM, wide), BF16)
    tile = (n_keys, LANES)
    key = lax.broadcasted_iota(jnp.int32, tile, 0)
    lane = lax.broadcasted_iota(jnp.int32, tile, 1)
    pen = jnp.where((lane == 0) & (key < Q_BLOCK) & (n == 0), NEG_INF, 0.0).astype(BF16)
    pen_rows = (lax.broadcasted_iota(jnp.int32, (LANES, wide), 0) == 0).astype(BF16)

    def query_block(a):
        rows = slice(a * Q_BLOCK, (a + 1) * Q_BLOCK)
        q_t = q_ref[rows, :].astype(F32).T

        def keys(cols):
            prev = kvp_ref[:, cols] if a == 0 else kvc_ref[(a - 1) * Q_BLOCK:a * Q_BLOCK, cols]
            return jnp.concatenate([prev, kvc_ref[rows, cols]], axis=0)

        def pair_kv(pair):
            k2 = keys(slice(pair * LANES, (pair + 1) * LANES)).astype(BF16)
            if a == 0:
                k2 = jnp.concatenate([k2, pen], axis=1)
            v2_t = keys(slice(KV_WIDTH + pair * LANES, KV_WIDTH + (pair + 1) * LANES)).T.astype(BF16)
            return k2, v2_t

        def scores(kv, k2):
            q4 = jnp.concatenate(
                [q_t[(kv * GROUP + g) * HEAD_DIM:(kv * GROUP + g + 1) * HEAD_DIM, :] for g in range(GROUP)],
                axis=1)
            q4 = (q4 * ATTN_SCALE).astype(BF16)
            parts = [q4, zeros] if kv % 2 == 0 else [zeros, q4]
            return _dot(k2, jnp.concatenate(parts + ([pen_rows] if a == 0 else []), axis=0))

        kvs = [pair_kv(pair) for pair in range(N_KV_HEADS // 2)]
        for kv in range(N_KV_HEADS):
            pair, parity = divmod(kv, 2)
            s = scores(kv, kvs[pair][0]) + bias_ref[kv]
            sink = jnp.concatenate(
                [jnp.full((1, Q_BLOCK), sink_ref[kv * GROUP + g], F32) for g in range(GROUP)], axis=1)
            m = jnp.maximum(jnp.max(s, axis=0, keepdims=True), sink)
            p = jnp.exp(s - m).astype(BF16)
            lhs = jnp.concatenate([kvs[pair][1][parity * HEAD_DIM:(parity + 1) * HEAD_DIM, :], ones], axis=0)
            o_t = _dot(lhs, p)
            l = o_t[HEAD_DIM:HEAD_DIM + 1, :] + jnp.exp(sink - m)
            o_n = o_t[0:HEAD_DIM, :] * (1.0 / l)
            o_kv = jnp.concatenate([o_n[:, g * Q_BLOCK:(g + 1) * Q_BLOCK] for g in range(GROUP)], axis=0)
            o_ref[rows, kv * GROUP * HEAD_DIM:(kv + 1) * GROUP * HEAD_DIM] = o_kv.T.astype(o_ref.dtype)

    for a in range(ATTN_QB):
        query_block(a)


ATTN_STEPS = SEQ // (ATTN_QB * Q_BLOCK)


def _row_slab_side(w):
    slab = w.shape[0] // ATTN_STEPS
    assert slab * ATTN_STEPS == w.shape[0] and slab % 16 == 0
    return (w, (slab, w.shape[1]), lambda n: (n, 0))


def _attn_prompt(q, kv, tab_flat, sinks, side, c_all, ada):
    step_rows = ATTN_QB * Q_BLOCK
    code_t = jnp.asarray(np.ascontiguousarray(_prompt_codes().T))
    smem = pl.BlockSpec(memory_space=pltpu.SMEM)
    side_args, side_specs, side_shapes = _side_specs(side)
    ada_args, ada_in_specs, ada_out_specs, ada_shapes = [], [], [], []
    for w, b, col0, n_cols in ada:
        tn = n_cols // ATTN_STEPS
        assert tn * ATTN_STEPS == n_cols and tn % LANES == 0 and col0 % tn == 0
        first = col0 // tn
        ada_args += [w, b]
        ada_in_specs += [pl.BlockSpec((D_MODEL, tn), lambda n, first=first: (0, first + n)),
                         pl.BlockSpec((1, tn), lambda n, first=first: (0, first + n))]
        ada_out_specs.append(pl.BlockSpec((N_COND, tn), lambda n: (0, n)))
        ada_shapes.append(jax.ShapeDtypeStruct((N_COND, n_cols), F32))
    out_spec = pl.BlockSpec((step_rows, ATTN_WIDTH), lambda n: (n, 0))
    outs = pl.pallas_call(
        functools.partial(_attn_p_kernel, n_side=len(side), n_ada=len(ada)),
        out_shape=[jax.ShapeDtypeStruct((SEQ, ATTN_WIDTH), BF16)] + side_shapes + ada_shapes,
        grid=(ATTN_STEPS,),
        in_specs=[smem, smem,
                  pl.BlockSpec((2 * Q_BLOCK, Q_BLOCK), lambda n: (0, 0)),
                  pl.BlockSpec((step_rows, ATTN_WIDTH), lambda n: (n, 0)),
                  pl.BlockSpec((step_rows, 2 * KV_WIDTH), lambda n: (n, 0)),
                  pl.BlockSpec((Q_BLOCK, 2 * KV_WIDTH), lambda n: (jnp.maximum(ATTN_QB * n - 1, 0), 0))]
        + side_specs + [pl.BlockSpec((N_COND, D_MODEL), lambda n: (0, 0))] + ada_in_specs,
        out_specs=[out_spec] + side_specs + ada_out_specs,
        scratch_shapes=[pltpu.VMEM((N_KV_HEADS, 2 * Q_BLOCK, GROUP * Q_BLOCK), F32),
                        pltpu.VMEM((N_COND, D_MODEL), BF16)],
        compiler_params=_cparams(1),
        name="attn_prompt",
    )(tab_flat, sinks, code_t, q, kv, kv, *side_args, c_all, *ada_args)
    n_side = len(side)
    return outs[0], tuple(outs[1:1 + n_side]), tuple(outs[1 + n_side:])


S_BB = 8


def _regroup_heads(x, to_group_major):
    half = HEAD_DIM
    assert LANES == 2 * half
    low = lax.broadcasted_iota(jnp.int32, (x.shape[0], LANES), 1) < half
    n_cols = ATTN_WIDTH // LANES
    outer, inner = (GROUP, N_KV_HEADS) if to_group_major else (N_KV_HEADS, GROUP)
    cols = []
    for c in range(n_cols):
        o, i = divmod(2 * c, inner)
        src = [(i + d) * outer + o for d in range(2)]
        a = x[:, (src[0] // 2) * LANES:(src[0] // 2 + 1) * LANES]
        b = x[:, (src[1] // 2) * LANES:(src[1] // 2 + 1) * LANES]
        assert src[0] % 2 == src[1] % 2
        if src[0] % 2 == 0:
            cols.append(jnp.where(low, a, pltpu.roll(b, half, axis=1)))
        else:
            cols.append(jnp.where(low, pltpu.roll(a, half, axis=1), b))
    return jnp.concatenate(cols, axis=1)


def _attn_s_kernel(th_ref, code_c_ref, code_x_ref, q_ref, kv_ref, ck_ref, cv_ref, o_ref, kw_ref, vw_ref,
                   bias_c_ref, bias_x_ref, nk_ref, nv_ref, qe_ref, qg_ref, og_ref):
    rows_all = DEC_SEQ * N_HEADS
    for t in range(DEC_SEQ):
        qg_ref[t] = _regroup_heads(q_ref[t], True)

    @pl.when(pl.program_id(0) == 0)
    def _():
        for code_ref, bias_ref in ((code_c_ref, bias_c_ref), (code_x_ref, bias_x_ref)):
            code = code_ref[...]
            acc = jnp.full(code.shape, NEG_INF, F32)
            for bk in range(N_BUCKETS + 1):
                acc = jnp.where(code == bk, th_ref[:, bk:bk + 1], acc)
            bias_ref[...] = acc
        nk_ref[...] = jnp.zeros_like(nk_ref)
        nv_ref[...] = jnp.zeros_like(nv_ref)

    lane_head = lax.broadcasted_iota(jnp.int32, (N_KV_HEADS, KV_WIDTH), 1) // HEAD_DIM
    row_head = lax.broadcasted_iota(jnp.int32, (N_KV_HEADS, KV_WIDTH), 0)
    diag = lane_head == row_head
    out_mask = (lax.broadcasted_iota(jnp.int32, (rows_all, KV_WIDTH), 0) % N_KV_HEADS
                == lax.broadcasted_iota(jnp.int32, (rows_all, KV_WIDTH), 1) // HEAD_DIM)
    new_lanes = lax.broadcasted_iota(jnp.int32, (KV_WIDTH, WBUF), 1) >= WBUF - DEC_SEQ
    bias_c = bias_c_ref[...]
    bias_x = bias_x_ref[...]
    nt_dims = (((1,), (1,)), ((), ()))
    x0 = WBUF - S_EXTRA

    for b in range(S_BB):
        for t in range(DEC_SEQ):
            nk_ref[WBUF - DEC_SEQ + t:WBUF - DEC_SEQ + t + 1, :] = kv_ref[t, b:b + 1, 0:KV_WIDTH]
            nv_ref[WBUF - DEC_SEQ + t:WBUF - DEC_SEQ + t + 1, :] = kv_ref[t, b:b + 1, KV_WIDTH:2 * KV_WIDTH]
            for g in range(GROUP):
                q_row = qg_ref[t, b:b + 1, g * KV_WIDTH:(g + 1) * KV_WIDTH]
                piece = jnp.where(diag, jnp.broadcast_to(q_row, (N_KV_HEADS, KV_WIDTH)), 0.0)
                r0 = (t * GROUP + g) * N_KV_HEADS
                qe_ref[r0:r0 + N_KV_HEADS, :] = piece
        k_t = ck_ref[b]
        v_t = cv_ref[b]
        kw_ref[b] = jnp.where(new_lanes, nk_ref[...].T, pltpu.roll(k_t, WBUF - DEC_SEQ, axis=1))
        vw_ref[b] = jnp.where(new_lanes, nv_ref[...].T, pltpu.roll(v_t, WBUF - DEC_SEQ, axis=1))

        qe = (qe_ref[...] * ATTN_SCALE).astype(BF16)
        s_c = _dot(qe, k_t.astype(BF16)) + bias_c
        s_x = lax.dot_general(qe, nk_ref[x0:WBUF, :].astype(BF16), nt_dims,
                              preferred_element_type=F32) + bias_x
        m = jnp.maximum(jnp.max(s_c, axis=-1, keepdims=True), jnp.max(s_x, axis=-1, keepdims=True))
        p_c = jnp.exp(s_c - m)
        p_x = jnp.exp(s_x - m)
        l = jnp.sum(p_c, axis=-1, keepdims=True) + jnp.sum(p_x, axis=-1, keepdims=True)
        o = lax.dot_general(p_c.astype(BF16), v_t.astype(BF16), nt_dims, preferred_element_type=F32)
        o = (o + _dot(p_x.astype(BF16), nv_ref[x0:WBUF, :].astype(BF16))) / l
        o = jnp.where(out_mask, o, 0.0)
        for t in range(DEC_SEQ):
            for g in range(GROUP):
                r0 = (t * GROUP + g) * N_KV_HEADS
                row = jnp.sum(o[r0:r0 + N_KV_HEADS, :], axis=0, keepdims=True)
                og_ref[t, b:b + 1, g * KV_WIDTH:(g + 1) * KV_WIDTH] = row
    for t in range(DEC_SEQ):
        o_ref[t] = _regroup_heads(og_ref[t], False)


def _attn_sample(q, kv, cache_k_t, cache_v_t, th):
    code_c, code_x = (jnp.asarray(c) for c in _sample_codes())
    rows_all = DEC_SEQ * N_HEADS
    full = lambda a: pl.BlockSpec(a.shape, lambda i: (0,) * a.ndim)
    tb_spec = lambda w: pl.BlockSpec((DEC_SEQ, S_BB, w), lambda i: (0, i, 0))
    cache_spec = pl.BlockSpec((S_BB, KV_WIDTH, WBUF), lambda i: (i, 0, 0))
    cache_shape = jax.ShapeDtypeStruct((DEC_BATCH, KV_WIDTH, WBUF), F32)
    return pl.pallas_call(
        _attn_s_kernel,
        out_shape=(jax.ShapeDtypeStruct((DEC_SEQ, DEC_BATCH, ATTN_WIDTH), F32), cache_shape, cache_shape),
        grid=(DEC_BATCH // S_BB,),
        in_specs=[full(th), full(code_c), full(code_x),
                  tb_spec(ATTN_WIDTH), tb_spec(2 * KV_WIDTH), cache_spec, cache_spec],
        out_specs=(tb_spec(ATTN_WIDTH), cache_spec, cache_spec),
        scratch_shapes=[pltpu.VMEM((rows_all, WBUF), F32),
                        pltpu.VMEM((rows_all, S_EXTRA), F32),
                        pltpu.VMEM((WBUF, KV_WIDTH), F32),
                        pltpu.VMEM((WBUF, KV_WIDTH), F32),
                        pltpu.VMEM((rows_all, KV_WIDTH), F32),
                        pltpu.VMEM((DEC_SEQ, S_BB, ATTN_WIDTH), F32),
                        pltpu.VMEM((DEC_SEQ, S_BB, ATTN_WIDTH), F32)],
        compiler_params=_cparams(1),
        name="attn_sample",
    )(th, code_c, code_x, q, kv, cache_k_t, cache_v_t)


CONV_RC = 32
CONV_ROWS = 128
CONV_HALO = 32


def _ln_silu(acc, lg, lb):
    mu = jnp.mean(acc, axis=-1, keepdims=True)
    xc = acc - mu
    var = jnp.mean(xc * xc, axis=-1, keepdims=True)
    return jax.nn.silu(xc * lax.rsqrt(var + NORM_EPS) * lg + lb)


def _conv_stage(first, zc_ref, zh_ref, s_ref):
    s_ref[0:CONV_HALO, :] = jnp.where(first, 0.0, zh_ref[...])
    s_ref[CONV_HALO:, :] = zc_ref[...]


def _conv_rows(rc, w_ref, b_ref, lg_ref, lb_ref, y_ref, s_ref, c_ref):
    off = CONV_HALO - HIST
    groups = [[j for j in range(CONV_WIDTH) if (j + off) % SUBLANES == r] for r in range(SUBLANES)]
    t0 = rc * CONV_ROWS
    for lc in range(C_CONV // LANES):
        lanes = slice(lc * LANES, (lc + 1) * LANES)
        out = jnp.broadcast_to(b_ref[:, lanes], (CONV_ROWS, LANES))
        for r, taps in enumerate(groups):
            n_rows = CONV_ROWS + (SUBLANES if r else 0)
            part = None
            for j in taps:
                base = t0 + (j + off) - r
                term = w_ref[j:j + 1, lanes] * s_ref[base:base + n_rows, lanes]
                part = term if part is None else part + term
            out = out + part[r:r + CONV_ROWS, :]
        c_ref[t0:t0 + CONV_ROWS, lanes] = out
    for r in range(CONV_ROWS // CONV_RC):
        rows = slice(t0 + r * CONV_RC, t0 + (r + 1) * CONV_RC)
        y_ref[rows, :] = _ln_silu(c_ref[rows, :], lg_ref[...], lb_ref[...]).astype(BF16)


def _mix_conv_kernel(zc_ref, zh_ref, w_ref, b_ref, lg_ref, lb_ref,
                     at_ref, ga_ref, gc_ref, h_ref, g2_ref, wao_ref, wco_ref, wout_ref,
                     o_ref, s_ref, c_ref, y_ref, *, tm):
    _conv_stage(pl.program_id(0) == 0, zc_ref, zh_ref, s_ref)
    for rc in range(tm // CONV_ROWS):
        _conv_rows(rc, w_ref, b_ref, lg_ref, lb_ref, y_ref, s_ref, c_ref)
    a = _dot(at_ref[...], wao_ref[...])
    c = _dot(y_ref[...], wco_ref[...])
    merged = (ga_ref[...].astype(F32) * a + gc_ref[...].astype(F32) * c).astype(BF16)
    r = _dot(merged, wout_ref[...])
    o_ref[...] = h_ref[...] + g2_ref[0:1, :] * r


def _mix_conv_prompt(z, dw_w, dw_b, ln_g, ln_b, attn, ga, gc, h, mod, w_ao, w_co, w_out, tm=256):
    rows = h.shape[0]
    n = rows // tm
    ratio = tm // CONV_HALO
    vec = pl.BlockSpec((1, C_CONV), lambda s: (0, 0))
    row = lambda w: pl.BlockSpec((tm, w), lambda s: (s, 0))
    resident = lambda shape: pl.BlockSpec(shape, lambda s: (0, 0), pipeline_mode=pl.Buffered(1))
    return pl.pallas_call(
        functools.partial(_mix_conv_kernel, tm=tm),
        out_shape=jax.ShapeDtypeStruct((rows, D_MODEL), F32),
        grid=(n,),
        in_specs=[row(C_CONV),
                  pl.BlockSpec((CONV_HALO, C_CONV), lambda s: (jnp.maximum(s * ratio - 1, 0), 0)),
                  pl.BlockSpec((CONV_WIDTH, C_CONV), lambda s: (0, 0)), vec, vec, vec,
                  row(ATTN_WIDTH), row(D_MODEL), row(D_MODEL), row(D_MODEL),
                  _mod_spec(0, False),
                  resident((ATTN_WIDTH, D_MODEL)), resident((C_CONV, D_MODEL)), resident((D_MODEL, D_MODEL))],
        out_specs=row(D_MODEL),
        scratch_shapes=[pltpu.VMEM((tm + CONV_HALO, C_CONV), F32), pltpu.VMEM((tm, C_CONV), F32),
                        pltpu.VMEM((tm, C_CONV), BF16)],
        compiler_params=_cparams(1),
        name="mix_conv",
    )(z, z, dw_w, dw_b, ln_g, ln_b, attn, ga, gc, h, mod, w_ao, w_co, w_out)


CONV_S_BB = 32


def _conv_s_kernel(z_ref, hist_ref, w_ref, b_ref, lg_ref, lb_ref, y_ref, ns_ref):
    for t in range(DEC_SEQ):
        acc = jnp.broadcast_to(b_ref[...], (CONV_S_BB, C_CONV))
        for j in range(CONV_WIDTH):
            i = t + j
            src = hist_ref[i] if i < HIST else z_ref[i - HIST]
            acc = acc + w_ref[j:j + 1, :] * src
        y_ref[t] = _ln_silu(acc, lg_ref[...], lb_ref[...]).astype(BF16)
    keep = HIST - DEC_SEQ
    ns_ref[0:keep] = hist_ref[DEC_SEQ:HIST]
    ns_ref[keep:HIST] = z_ref[...]


def _conv_sample(z, hist_t, dw_w, dw_b, ln_g, ln_b):
    vec = pl.BlockSpec((1, C_CONV), lambda i: (0, 0))
    tb = pl.BlockSpec((DEC_SEQ, CONV_S_BB, C_CONV), lambda i: (0, i, 0))
    st = pl.BlockSpec((HIST, CONV_S_BB, C_CONV), lambda i: (0, i, 0))
    return pl.pallas_call(
        _conv_s_kernel,
        out_shape=(jax.ShapeDtypeStruct((DEC_SEQ, DEC_BATCH, C_CONV), BF16),
                   jax.ShapeDtypeStruct(hist_t.shape, F32)),
        grid=(DEC_BATCH // CONV_S_BB,),
        in_specs=[tb, st, pl.BlockSpec((CONV_WIDTH, C_CONV), lambda i: (0, 0)), vec, vec, vec],
        out_specs=(tb, st),
        compiler_params=_cparams(1),
        name="conv_sample",
    )(z, hist_t, dw_w, dw_b, ln_g, ln_b)


def _mix_s_kernel(at_ref, y_ref, ga_ref, gc_ref, h_ref, g2_ref, wao_ref, wco_ref, wout_ref, o_ref):
    a = _dot(at_ref[...].astype(BF16), wao_ref[...])
    c = _dot(y_ref[...], wco_ref[...])
    merged = (ga_ref[...].astype(F32) * a + gc_ref[...].astype(F32) * c).astype(BF16)
    r = _dot(merged, wout_ref[...])
    o_ref[...] = h_ref[...] + g2_ref[...] * r


def _mix_sample(attn, y, ga, gc, h, mod, w_ao, w_co, w_out):
    row = lambda w: pl.BlockSpec((DEC_BATCH, w), lambda t: (t, 0))
    resident = lambda shape: pl.BlockSpec(shape, lambda t: (0, 0), pipeline_mode=pl.Buffered(1))
    return pl.pallas_call(
        _mix_s_kernel,
        out_shape=jax.ShapeDtypeStruct((S_ROWS, D_MODEL), F32),
        grid=(DEC_SEQ,),
        in_specs=[row(ATTN_WIDTH), row(C_CONV), row(D_MODEL), row(D_MODEL), row(D_MODEL),
                  _mod_spec(0, True),
                  resident((ATTN_WIDTH, D_MODEL)), resident((C_CONV, D_MODEL)), resident((D_MODEL, D_MODEL))],
        out_specs=row(D_MODEL),
        compiler_params=_cparams(1),
        name="mix_sample",
    )(attn, y, ga, gc, h, mod, w_ao, w_co, w_out)


def kernel(x_prompt, x_sample, cache_k, cache_v, state_conv, c_prompt, c_sample, rel_bias_table, norm1_g, ffn1_w1, ffn1_w3, ffn1_w2, norm2_g, w_in, attn_sinks, conv_dw_w, conv_dw_b, conv_ln_g, conv_ln_b, w_conv_out, w_attn_out, w_out, norm3_g, ffn2_w1, ffn2_w3, ffn2_w2, w_ada, b_ada, final_norm_g, w_ada_final, b_ada_final):
    c_all = jnp.concatenate([c_sample, c_prompt, jnp.zeros((N_COND - DEC_BATCH - 1, D_MODEL), F32)], axis=0)
    early = 5 * D_MODEL
    mod = _adaln(c_all, w_ada[0], b_ada, early)
    late_ada = [(w_ada[0], b_ada, early, 4 * D_MODEL), (w_ada_final, b_ada_final[None, :], 0, 2 * D_MODEL)]

    sinks = attn_sinks[0]
    tab_flat = rel_bias_table.reshape(-1)
    tab_ext = jnp.concatenate([rel_bias_table, sinks[None, :]], axis=0)
    th = tab_ext.T.reshape(N_KV_HEADS, GROUP, N_BUCKETS + 1).transpose(1, 0, 2)
    th = jnp.tile(th.reshape(N_HEADS, N_BUCKETS + 1), (DEC_SEQ, 1))

    xp = x_prompt[0]
    xs = x_sample.transpose(1, 0, 2).reshape(S_ROWS, D_MODEL)

    conv_w = (conv_dw_w[0], conv_dw_b, conv_ln_g, conv_ln_b)


    def keys_on_lanes(cache):
        return cache[0].transpose(0, 2, 3, 1).reshape(DEC_BATCH, KV_WIDTH, WBUF)

    def keys_on_rows(win):
        return win.reshape(DEC_BATCH, N_KV_HEADS, HEAD_DIM, WBUF).transpose(0, 3, 1, 2)[None]

    def mixers_s(q, kv, z, ga, gc, h, mod_late, mix_w):
        o, kw, vw = _attn_sample(q.reshape(DEC_SEQ, DEC_BATCH, ATTN_WIDTH),
                                 kv.reshape(DEC_SEQ, DEC_BATCH, 2 * KV_WIDTH),
                                 keys_on_lanes(cache_k), keys_on_lanes(cache_v), th)
        y, ns = _conv_sample(z.reshape(DEC_SEQ, DEC_BATCH, C_CONV), state_conv[0].transpose(1, 0, 2), *conv_w)
        h = _mix_sample(o.reshape(S_ROWS, ATTN_WIDTH), y.reshape(S_ROWS, C_CONV), ga, gc, h, mod_late, *mix_w)
        return h, (kw, vw, ns)

    hs, f1 = _ffn(xs, norm1_g, mod, 0, ffn1_w1[0], ffn1_w3[0], ffn1_w2[0], per_row=True,
                  tm=S_ROWS, tf=FFN_TF_CAST, cast_out=True)
    w_in_rows = D_MODEL // (SEQ // FFN_TM)
    w_in_side = (w_in[0], (w_in_rows, W_IN_SIDE_COLS),
                 lambda i, f: (i, jnp.minimum(f, IN_WIDTH // W_IN_SIDE_COLS - 1)))
    hp, (w_in_b,) = _ffn(xp, norm1_g, mod, 0, *f1, per_row=False, tm=FFN_TM, tf=FFN_TF, side=(w_in_side,))

    qs, kv_s, z_s, ga_s, gc_s = _proj(hs, norm2_g, mod, w_in_b, per_row=True, tm=S_ROWS, q_dtype=F32)
    qp, kv_p, z_p, ga_p, gc_p = _proj(hp, norm2_g, mod, w_in_b, per_row=False, tm=PROJ_TM, q_dtype=BF16)

    side = [_row_slab_side(w[0]) for w in (ffn2_w1, ffn2_w3, ffn2_w2, w_attn_out, w_conv_out, w_out)]
    attn_p, side_b, (mod_late, mod_f) = _attn_prompt(qp, kv_p, tab_flat, sinks, side, c_all, late_ada)
    f2, mix_w = side_b[:3], side_b[3:]
    final = (final_norm_g[None, :], mod_f)

    hs, (kw, vw, ns) = mixers_s(qs, kv_s, z_s, ga_s, gc_s, hs, mod_late, mix_w)
    hp = _mix_conv_prompt(z_p, *conv_w, attn_p, ga_p, gc_p, hp, mod_late, *mix_w)

    ys = _ffn(hs, norm3_g, mod_late, 1, *f2, per_row=True, tm=S_ROWS, tf=FFN_TF, final=final)
    yp = _ffn(hp, norm3_g, mod_late, 1, *f2, per_row=False, tm=FFN_TM, tf=FFN_TF, final=final)

    w = min(WINDOW, SEQ)
    kv_shape = (1, 1, w, N_KV_HEADS, HEAD_DIM)
    k_win_p = kv_p[SEQ - w:, :KV_WIDTH].reshape(kv_shape)
    v_win_p = kv_p[SEQ - w:, KV_WIDTH:].reshape(kv_shape)
    conv_p_state = z_p[SEQ - HIST:].reshape(1, 1, HIST, C_CONV)
    s_shape = (1, DEC_BATCH, WBUF, N_KV_HEADS, HEAD_DIM)
    y_prompt = yp[None]
    y_sample = ys.reshape(DEC_SEQ, DEC_BATCH, D_MODEL).transpose(1, 0, 2)
    return (y_prompt, y_sample, k_win_p, v_win_p, conv_p_state,
            keys_on_rows(kw), keys_on_rows(vw), ns.transpose(1, 0, 2)[None])
```

```python
import functools

import numpy as np
import jax
import jax.numpy as jnp
from jax import lax
from jax.experimental import pallas as pl
from jax.experimental.pallas import tpu as pltpu

D_MODEL = 2048
SEQ = 8192
DEC_BATCH = 128
DEC_SEQ = 4
PAST_LEN = 16384
N_HEADS = 32
N_KV_HEADS = 8
HEAD_DIM = 64
GROUP = N_HEADS // N_KV_HEADS
ATTN_WIDTH = N_HEADS * HEAD_DIM
KV_WIDTH = N_KV_HEADS * HEAD_DIM
WINDOW = 128
Q_BLOCK = 128
ATTN_SCALE = HEAD_DIM ** -0.5
N_BUCKETS = 32
MAX_DISTANCE = 128
C_CONV = D_MODEL // 2
CONV_WIDTH = 31
HIST = CONV_WIDTH - 1
D_FF = 5632
NORM_EPS = 1e-6
NEG_INF = -1e30
IN_WIDTH = ATTN_WIDTH + 2 * KV_WIDTH + 2 * C_CONV + 2 * D_MODEL
WBUF = min(WINDOW, PAST_LEN)

S_ROWS = DEC_BATCH * DEC_SEQ
N_COND = 136
PROMPT_MOD_BLOCK = DEC_BATCH // 8
S_EXTRA = 8

VMEM_LIMIT = 60 * 1024 * 1024
SUBLANES = 8
LANES = 128
BF16_SUBLANES = 16

F32 = jnp.float32
BF16 = jnp.bfloat16


def _cparams(n_axes, flags=None):
    return pltpu.CompilerParams(dimension_semantics=("arbitrary",) * n_axes,
                                vmem_limit_bytes=VMEM_LIMIT, flags=flags)


def _dot(a, b):
    return jnp.dot(a, b, preferred_element_type=F32)


def _rms_mod(x, g, sh, sc):
    ms = jnp.mean(x * x, axis=-1, keepdims=True)
    y = x * lax.rsqrt(ms + NORM_EPS) * g
    return y * (1.0 + sc) + sh


def _mod_val(ref, per_row):
    return ref[...] if per_row else ref[0:1, :]


def _mod_spec(chunk, per_row, width=D_MODEL):
    if per_row:
        return pl.BlockSpec((DEC_BATCH, width), lambda *_: (0, chunk))
    return pl.BlockSpec((8, width), lambda *_: (PROMPT_MOD_BLOCK, chunk))


def _adaln_kernel(c_ref, w_ref, b_ref, o_ref, s_ref):
    @pl.when(pl.program_id(0) == 0)
    def _():
        s_ref[...] = jax.nn.silu(c_ref[...]).astype(BF16)

    o_ref[...] = _dot(s_ref[...], w_ref[...].astype(BF16)) + b_ref[...]


def _adaln(c_all, w, b, n, tn=1024):
    return pl.pallas_call(
        _adaln_kernel,
        out_shape=jax.ShapeDtypeStruct((N_COND, n), F32),
        grid=(n // tn,),
        in_specs=[pl.BlockSpec((N_COND, D_MODEL), lambda j: (0, 0)),
                  pl.BlockSpec((D_MODEL, tn), lambda j: (0, j)),
                  pl.BlockSpec((1, tn), lambda j: (0, j))],
        out_specs=pl.BlockSpec((N_COND, tn), lambda j: (0, j)),
        scratch_shapes=[pltpu.VMEM((N_COND, D_MODEL), BF16)],
        compiler_params=_cparams(1),
        name="adaln",
    )(c_all, w, b)


FFN_ROWS = 512
FFN_TM = 1024
FFN_TF = 512
FFN_TF_CAST = 256
PROJ_TM = 1024
W_IN_SIDE_COLS = 1024


def _side_cast(side_in, side_out):
    for i_ref, o_ref in zip(side_in, side_out):
        o_ref[...] = i_ref[...].astype(BF16)


def _side_specs(side):
    args = [a for a, _, _ in side]
    specs = [pl.BlockSpec(block, index_map) for _, block, index_map in side]
    shapes = [jax.ShapeDtypeStruct(a.shape, BF16) for a in args]
    return args, specs, shapes


def _ffn_kernel(*refs, tm, sub, per_row, n_f, final, cast_out, n_side):
    refs = list(refs)
    x_ref, n_ref, sh_ref, sc_ref, g_ref, w1_ref, w3_ref, w2_ref = refs[:8]
    del refs[:8]
    if final:
        gf_ref, shf_ref, scf_ref = refs[:3]
        del refs[:3]
    side_in = refs[:n_side]
    del refs[:n_side]
    o_ref = refs.pop(0)
    if cast_out:
        w1o_ref, w3o_ref, w2o_ref = refs[:3]
        del refs[:3]
    side_out = refs[:n_side]
    del refs[:n_side]
    (u_ref,) = refs
    f = pl.program_id(1)
    _side_cast(side_in, side_out)

    if cast_out:
        w1o_ref[...] = w1_ref[...].astype(BF16)
        w3o_ref[...] = w3_ref[...].astype(BF16)
        w2o_ref[...] = w2_ref[...].astype(BF16)
        w1_ref, w3_ref, w2_ref = w1o_ref, w3o_ref, w2o_ref

    def accumulate(first, last):
        for s in range(tm // FFN_ROWS):
            rows = slice(s * FFN_ROWS, (s + 1) * FFN_ROWS)
            subs = [slice(r0, r0 + sub) for r0 in range(s * FFN_ROWS, (s + 1) * FFN_ROWS, sub)]
            if first:
                for r in subs:
                    u_ref[r, :] = _rms_mod(x_ref[r, :], n_ref[...], _mod_val(sh_ref, per_row),
                                           _mod_val(sc_ref, per_row)).astype(BF16)
            u = u_ref[rows, :]
            h1 = _dot(u, w1_ref[...])
            h3 = _dot(u, w3_ref[...])
            a = (jax.nn.silu(h1) * h3).astype(BF16)
            d = _dot(a, w2_ref[...])
            if first:
                o_ref[rows, :] = d
            else:
                o_ref[rows, :] += d
            if last:
                for r in subs:
                    h = x_ref[r, :] + 0.5 * _mod_val(g_ref, per_row) * o_ref[r, :]
                    if final:
                        h = _rms_mod(h, gf_ref[...], _mod_val(shf_ref, per_row), _mod_val(scf_ref, per_row))
                    o_ref[r, :] = h

    assert n_f > 2

    @pl.when(f == 0)
    def _():
        accumulate(True, False)

    @pl.when((f > 0) & (f < n_f - 1))
    def _():
        accumulate(False, False)

    @pl.when(f == n_f - 1)
    def _():
        accumulate(False, True)


def _ffn(x, norm_g, mod, chunk0, w1, w3, w2, *, per_row, tm, tf, final=None, cast_out=False, side=()):
    rows = x.shape[0]
    n_f = D_FF // tf
    sub = DEC_BATCH
    row_spec = pl.BlockSpec((tm, D_MODEL), lambda i, f: (i, 0))
    vec_spec = pl.BlockSpec((1, D_MODEL), lambda i, f: (0, 0))
    w_specs = [pl.BlockSpec((D_MODEL, tf), lambda i, f: (0, f)),
               pl.BlockSpec((D_MODEL, tf), lambda i, f: (0, f)),
               pl.BlockSpec((tf, D_MODEL), lambda i, f: (f, 0))]
    in_specs = [row_spec, vec_spec,
                _mod_spec(chunk0, per_row), _mod_spec(chunk0 + 1, per_row), _mod_spec(chunk0 + 2, per_row)]
    in_specs += w_specs
    args = [x, norm_g, mod, mod, mod, w1, w3, w2]
    if final is not None:
        gf, mod_f = final
        in_specs += [vec_spec, _mod_spec(0, per_row), _mod_spec(1, per_row)]
        args += [gf, mod_f, mod_f]
    side_args, side_specs, side_shapes = _side_specs(side)
    in_specs += side_specs
    args += side_args
    out_shape = [jax.ShapeDtypeStruct((rows, D_MODEL), F32)]
    out_specs = [row_spec]
    if cast_out:
        assert rows == tm, "every weight block must be visited exactly once"
        out_shape += [jax.ShapeDtypeStruct(w.shape, BF16) for w in (w1, w3, w2)]
        out_specs += w_specs
    out_shape += side_shapes
    out_specs += side_specs
    outs = pl.pallas_call(
        functools.partial(_ffn_kernel, tm=tm, sub=sub, per_row=per_row, n_f=n_f,
                          final=final is not None, cast_out=cast_out, n_side=len(side)),
        out_shape=out_shape,
        grid=(rows // tm, n_f),
        in_specs=in_specs,
        out_specs=out_specs,
        scratch_shapes=[pltpu.VMEM((tm, D_MODEL), BF16)],
        compiler_params=_cparams(2),
        name="ffn_final" if final is not None else "ffn",
    )(*args)
    return (outs[0], tuple(outs[1:])) if (cast_out or side) else outs[0]


PROJ_CHUNK = 256
PROJ_NORM_ROWS = 512
PROJ_CW = 512


def _proj_layout():
    cw = PROJ_CW
    n_q = ATTN_WIDTH // (2 * cw)
    n_kv = 2 * KV_WIDTH // (2 * cw)
    n_glu = C_CONV // cw
    n_gate = D_MODEL // (2 * cw)
    pre = n_q + n_kv
    steps = pre + n_glu + 2 * n_gate

    def block_a(j):
        return jnp.where((j >= pre) & (j < pre + n_glu), j + pre, 2 * j)

    def block_b(j):
        return jnp.where((j >= pre) & (j < pre + n_glu), j + pre + n_glu, 2 * j + 1)

    return n_q, n_kv, n_glu, n_gate, steps, block_a, block_b


def _proj_kernel(h_ref, n_ref, sh_ref, sc_ref, wa_ref, wb_ref, q_ref, kv_ref, z_ref, ga_ref, gc_ref, u_ref,
                 *, tm, sub, per_row):
    cw = PROJ_CW
    n_q, n_kv, n_glu, n_gate, _, _, _ = _proj_layout()
    j = pl.program_id(1)

    n_chunks = cw // PROJ_CHUNK

    def chunks(w_ref, rows=slice(None)):
        for c in range(n_chunks):
            cols = slice(c * PROJ_CHUNK, (c + 1) * PROJ_CHUNK)
            yield c * PROJ_CHUNK, _dot(u_ref[rows, :], w_ref[:, cols])

    def pair_store(o_ref, fn, rows=slice(None)):
        for half, w_ref in enumerate((wa_ref, wb_ref)):
            for c0, acc in chunks(w_ref, rows):
                o_ref[rows, half * cw + c0:half * cw + c0 + PROJ_CHUNK] = fn(acc).astype(o_ref.dtype)

    @pl.when(j == 0)
    def _():
        for g0 in range(0, tm, PROJ_NORM_ROWS):
            for r0 in range(g0, g0 + PROJ_NORM_ROWS, sub):
                r = slice(r0, r0 + sub)
                u_ref[r, :] = _rms_mod(h_ref[r, :], n_ref[...], _mod_val(sh_ref, per_row),
                                       _mod_val(sc_ref, per_row)).astype(BF16)
            pair_store(q_ref, lambda acc: acc, slice(g0, g0 + PROJ_NORM_ROWS))

    lo = 0

    @pl.when((j > 0) & (j < n_q))
    def _():
        pair_store(q_ref, lambda acc: acc)

    lo += n_q

    @pl.when((j >= lo) & (j < lo + n_kv))
    def _():
        pair_store(kv_ref, lambda acc: acc)

    lo += n_kv

    @pl.when((j >= lo) & (j < lo + n_glu))
    def _():
        for (c0, a), (_, g) in zip(chunks(wa_ref), chunks(wb_ref)):
            z_ref[:, c0:c0 + PROJ_CHUNK] = a * jax.nn.sigmoid(g)

    lo += n_glu

    @pl.when((j >= lo) & (j < lo + n_gate))
    def _():
        pair_store(ga_ref, jax.nn.sigmoid)

    lo += n_gate

    @pl.when(j >= lo)
    def _():
        pair_store(gc_ref, jax.nn.sigmoid)


def _proj(h, norm_g, mod, w, *, per_row, tm, q_dtype):
    rows = h.shape[0]
    sub = DEC_BATCH
    cw = PROJ_CW
    n_q, n_kv, n_glu, n_gate, steps, block_a, block_b = _proj_layout()

    def out_spec(width, first, count):
        return pl.BlockSpec((tm, width), lambda i, j: (i, jnp.clip(j - first, 0, count - 1)))

    wa_spec = pl.BlockSpec((D_MODEL, cw), lambda i, j: (0, block_a(j)))
    wb_spec = pl.BlockSpec((D_MODEL, cw), lambda i, j: (0, block_b(j)))
    out_shape = [jax.ShapeDtypeStruct((rows, ATTN_WIDTH), q_dtype),
                 jax.ShapeDtypeStruct((rows, 2 * KV_WIDTH), F32),
                 jax.ShapeDtypeStruct((rows, C_CONV), F32),
                 jax.ShapeDtypeStruct((rows, D_MODEL), BF16),
                 jax.ShapeDtypeStruct((rows, D_MODEL), BF16)]
    out_specs = [out_spec(2 * cw, 0, n_q),
                 out_spec(2 * cw, n_q, n_kv),
                 out_spec(cw, n_q + n_kv, n_glu),
                 out_spec(2 * cw, n_q + n_kv + n_glu, n_gate),
                 out_spec(2 * cw, n_q + n_kv + n_glu + n_gate, n_gate)]
    return pl.pallas_call(
        functools.partial(_proj_kernel, tm=tm, sub=sub, per_row=per_row),
        out_shape=out_shape,
        grid=(rows // tm, steps),
        in_specs=[pl.BlockSpec((tm, D_MODEL), lambda i, j: (i, 0)),
                  pl.BlockSpec((1, D_MODEL), lambda i, j: (0, 0)),
                  _mod_spec(3, per_row), _mod_spec(4, per_row), wa_spec, wb_spec],
        out_specs=out_specs,
        scratch_shapes=[pltpu.VMEM((tm, D_MODEL), BF16)],
        compiler_params=_cparams(2),
        name="proj",
    )(h, norm_g, mod, mod, w, w)


def _t5_bucket_np(dist):
    exact = N_BUCKETS // 2
    d = np.maximum(dist, 0)
    df = np.maximum(d, 1).astype(np.float32)
    large = exact + (np.log(df / np.float32(exact)) / np.float32(np.log(MAX_DISTANCE / exact))
                     * np.float32(N_BUCKETS - exact)).astype(np.int32)
    large = np.minimum(large, N_BUCKETS - 1)
    return np.where(d < exact, d, large).astype(np.int32)


def _prompt_codes():
    qi = np.arange(Q_BLOCK)[:, None]
    kj = np.arange(2 * Q_BLOCK)[None, :]
    dist = qi + Q_BLOCK - kj
    valid = (dist >= 0) & (dist <= WINDOW)
    return np.where(valid, _t5_bucket_np(dist), -1).astype(np.int32)


def _sample_codes():
    t = np.repeat(np.arange(DEC_SEQ), GROUP * N_KV_HEADS)[:, None]
    j = np.arange(WBUF)[None, :]
    dist = t + WBUF - j
    cache = np.where((dist >= 0) & (dist <= WINDOW), _t5_bucket_np(dist), -1)
    c = np.arange(S_EXTRA)[None, :]
    t_new = c - (S_EXTRA - DEC_SEQ)
    dist = t - t_new
    extra = np.where((t_new >= 0) & (dist >= 0) & (dist <= WINDOW), _t5_bucket_np(dist), -1)
    extra = np.where(c == 0, N_BUCKETS, extra)
    return cache.astype(np.int32), extra.astype(np.int32)


BIAS_ROWS = 64
ATTN_QB = 2


def _attn_p_kernel(*refs, n_side, n_ada):
    refs = list(refs)
    tab_ref, sink_ref, code_ref, q_ref, kvc_ref, kvp_ref = refs[:6]
    del refs[:6]
    side_in = refs[:n_side]
    del refs[:n_side]
    c_ref = refs.pop(0)
    ada_in = [(refs[2 * k], refs[2 * k + 1]) for k in range(n_ada)]
    del refs[:2 * n_ada]
    o_ref = refs.pop(0)
    side_out = refs[:n_side]
    del refs[:n_side]
    ada_out = refs[:n_ada]
    del refs[:n_ada]
    bias_ref, silu_c_ref = refs
    n = pl.program_id(0)
    _side_cast(side_in, side_out)

    @pl.when(n == 0)
    def _():
        silu_c_ref[...] = jax.nn.silu(c_ref[...]).astype(BF16)

    for (w_ref, b_ref), m_ref in zip(ada_in, ada_out):
        m_ref[...] = _dot(silu_c_ref[...], w_ref[...].astype(BF16)) + b_ref[...]

    n_keys = 2 * Q_BLOCK
    wide = GROUP * Q_BLOCK

    @pl.when(n == 0)
    def _():
        def kv_body(kv, carry):
            for g in range(GROUP):
                for r in range(n_keys // BIAS_ROWS):
                    rows = slice(r * BIAS_ROWS, (r + 1) * BIAS_ROWS)
                    code = code_ref[rows, :]
                    acc = jnp.full((BIAS_ROWS, Q_BLOCK), NEG_INF, F32)
                    for bk in range(N_BUCKETS):
                        acc = jnp.where(code == bk, tab_ref[bk * N_HEADS + kv * GROUP + g], acc)
                    bias_ref[kv, rows, g * Q_BLOCK:(g + 1) * Q_BLOCK] = acc
            return carry
        lax.fori_loop(0, N_KV_HEADS, kv_body, 0)

    ones = jnp.ones((BF16_SUBLANES, n_keys), BF16)
    zeros = jnp.zeros((HEAD_DIM, wide), BF16)
    tile = (n_keys, LANES)
    key = lax.broadcasted_iota(jnp.int32, tile, 0)
    lane = lax.broadcasted_iota(jnp.int32, tile, 1)
    pen = jnp.where((lane == 0) & (key < Q_BLOCK) & (n == 0), NEG_INF, 0.0).astype(BF16)
    pen_rows = (lax.broadcasted_iota(jnp.int32, (LANES, wide), 0) == 0).astype(BF16)

    def query_block(a):
        rows = slice(a * Q_BLOCK, (a + 1) * Q_BLOCK)
        q_t = q_ref[rows, :].astype(F32).T

        def keys(cols):
            prev = kvp_ref[:, cols] if a == 0 else kvc_ref[(a - 1) * Q_BLOCK:a * Q_BLOCK, cols]
            return jnp.concatenate([prev, kvc_ref[rows, cols]], axis=0)

        def pair_kv(pair):
            k2 = keys(slice(pair * LANES, (pair + 1) * LANES)).astype(BF16)
            if a == 0:
                k2 = jnp.concatenate([k2, pen], axis=1)
            v2_t = keys(slice(KV_WIDTH + pair * LANES, KV_WIDTH + (pair + 1) * LANES)).T.astype(BF16)
            return k2, v2_t

        def scores(kv, k2):
            q4 = jnp.concatenate(
                [q_t[(kv * GROUP + g) * HEAD_DIM:(kv * GROUP + g + 1) * HEAD_DIM, :] for g in range(GROUP)],
                axis=1)
            q4 = (q4 * ATTN_SCALE).astype(BF16)
            parts = [q4, zeros] if kv % 2 == 0 else [zeros, q4]
            return _dot(k2, jnp.concatenate(parts + ([pen_rows] if a == 0 else []), axis=0))

        kvs = [pair_kv(pair) for pair in range(N_KV_HEADS // 2)]
        for kv in range(N_KV_HEADS):
            pair, parity = divmod(kv, 2)
            s = scores(kv, kvs[pair][0]) + bias_ref[kv]
            sink = jnp.concatenate(
                [jnp.full((1, Q_BLOCK), sink_ref[kv * GROUP + g], F32) for g in range(GROUP)], axis=1)
            m = jnp.maximum(jnp.max(s, axis=0, keepdims=True), sink)
            p = jnp.exp(s - m).astype(BF16)
            lhs = jnp.concatenate([kvs[pair][1][parity * HEAD_DIM:(parity + 1) * HEAD_DIM, :], ones], axis=0)
            o_t = _dot(lhs, p)
            l = o_t[HEAD_DIM:HEAD_DIM + 1, :] + jnp.exp(sink - m)
            o_n = o_t[0:HEAD_DIM, :] * (1.0 / l)
            o_kv = jnp.concatenate([o_n[:, g * Q_BLOCK:(g + 1) * Q_BLOCK] for g in range(GROUP)], axis=0)
            o_ref[rows, kv * GROUP * HEAD_DIM:(kv + 1) * GROUP * HEAD_DIM] = o_kv.T.astype(o_ref.dtype)

    for a in range(ATTN_QB):
        query_block(a)


ATTN_STEPS = SEQ // (ATTN_QB * Q_BLOCK)


def _row_slab_side(w):
    slab = w.shape[0] // ATTN_STEPS
    assert slab * ATTN_STEPS == w.shape[0] and slab % BF16_SUBLANES == 0
    return (w, (slab, w.shape[1]), lambda n: (n, 0))


def _attn_prompt(q, kv, tab_flat, sinks, side, c_all, ada):
    step_rows = ATTN_QB * Q_BLOCK
    code_t = jnp.asarray(np.ascontiguousarray(_prompt_codes().T))
    smem = pl.BlockSpec(memory_space=pltpu.SMEM)
    side_args, side_specs, side_shapes = _side_specs(side)
    ada_args, ada_in_specs, ada_out_specs, ada_shapes = [], [], [], []
    for w, b, col0, n_cols in ada:
        tn = n_cols // ATTN_STEPS
        assert tn * ATTN_STEPS == n_cols and tn % LANES == 0 and col0 % tn == 0
        first = col0 // tn
        ada_args += [w, b]
        ada_in_specs += [pl.BlockSpec((D_MODEL, tn), lambda n, first=first: (0, first + n)),
                         pl.BlockSpec((1, tn), lambda n, first=first: (0, first + n))]
        ada_out_specs.append(pl.BlockSpec((N_COND, tn), lambda n: (0, n)))
        ada_shapes.append(jax.ShapeDtypeStruct((N_COND, n_cols), F32))
    out_spec = pl.BlockSpec((step_rows, ATTN_WIDTH), lambda n: (n, 0))
    outs = pl.pallas_call(
        functools.partial(_attn_p_kernel, n_side=len(side), n_ada=len(ada)),
        out_shape=[jax.ShapeDtypeStruct((SEQ, ATTN_WIDTH), BF16)] + side_shapes + ada_shapes,
        grid=(ATTN_STEPS,),
        in_specs=[smem, smem,
                  pl.BlockSpec((2 * Q_BLOCK, Q_BLOCK), lambda n: (0, 0)),
                  pl.BlockSpec((step_rows, ATTN_WIDTH), lambda n: (n, 0)),
                  pl.BlockSpec((step_rows, 2 * KV_WIDTH), lambda n: (n, 0)),
                  pl.BlockSpec((Q_BLOCK, 2 * KV_WIDTH), lambda n: (jnp.maximum(ATTN_QB * n - 1, 0), 0))]
        + side_specs + [pl.BlockSpec((N_COND, D_MODEL), lambda n: (0, 0))] + ada_in_specs,
        out_specs=[out_spec] + side_specs + ada_out_specs,
        scratch_shapes=[pltpu.VMEM((N_KV_HEADS, 2 * Q_BLOCK, GROUP * Q_BLOCK), F32),
                        pltpu.VMEM((N_COND, D_MODEL), BF16)],
        compiler_params=_cparams(1),
        name="attn_prompt",
    )(tab_flat, sinks, code_t, q, kv, kv, *side_args, c_all, *ada_args)
    n_side = len(side)
    return outs[0], tuple(outs[1:1 + n_side]), tuple(outs[1 + n_side:])


S_BB = 8


def _regroup_heads(x, to_group_major):
    half = HEAD_DIM
    assert LANES == 2 * half
    low = lax.broadcasted_iota(jnp.int32, (x.shape[0], LANES), 1) < half
    n_cols = ATTN_WIDTH // LANES
    outer, inner = (GROUP, N_KV_HEADS) if to_group_major else (N_KV_HEADS, GROUP)
    cols = []
    for c in range(n_cols):
        o, i = divmod(2 * c, inner)
        src = [(i + d) * outer + o for d in range(2)]
        a = x[:, (src[0] // 2) * LANES:(src[0] // 2 + 1) * LANES]
        b = x[:, (src[1] // 2) * LANES:(src[1] // 2 + 1) * LANES]
        assert src[0] % 2 == src[1] % 2
        if src[0] % 2 == 0:
            cols.append(jnp.where(low, a, pltpu.roll(b, half, axis=1)))
        else:
            cols.append(jnp.where(low, pltpu.roll(a, half, axis=1), b))
    return jnp.concatenate(cols, axis=1)


def _attn_s_kernel(th_ref, code_c_ref, code_x_ref, q_ref, kv_ref, ck_ref, cv_ref, o_ref, kw_ref, vw_ref,
                   bias_c_ref, bias_x_ref, nk_ref, nv_ref, qe_ref, qg_ref, og_ref):
    rows_all = DEC_SEQ * N_HEADS
    for t in range(DEC_SEQ):
        qg_ref[t] = _regroup_heads(q_ref[t], True)

    @pl.when(pl.program_id(0) == 0)
    def _():
        for code_ref, bias_ref in ((code_c_ref, bias_c_ref), (code_x_ref, bias_x_ref)):
            code = code_ref[...]
            acc = jnp.full(code.shape, NEG_INF, F32)
            for bk in range(N_BUCKETS + 1):
                acc = jnp.where(code == bk, th_ref[:, bk:bk + 1], acc)
            bias_ref[...] = acc
        nk_ref[...] = jnp.zeros_like(nk_ref)
        nv_ref[...] = jnp.zeros_like(nv_ref)

    lane_head = lax.broadcasted_iota(jnp.int32, (N_KV_HEADS, KV_WIDTH), 1) // HEAD_DIM
    row_head = lax.broadcasted_iota(jnp.int32, (N_KV_HEADS, KV_WIDTH), 0)
    diag = lane_head == row_head
    out_mask = (lax.broadcasted_iota(jnp.int32, (rows_all, KV_WIDTH), 0) % N_KV_HEADS
                == lax.broadcasted_iota(jnp.int32, (rows_all, KV_WIDTH), 1) // HEAD_DIM)
    new_lanes = lax.broadcasted_iota(jnp.int32, (KV_WIDTH, WBUF), 1) >= WBUF - DEC_SEQ
    bias_c = bias_c_ref[...]
    bias_x = bias_x_ref[...]
    nt_dims = (((1,), (1,)), ((), ()))
    x0 = WBUF - S_EXTRA

    for b in range(S_BB):
        for t in range(DEC_SEQ):
            nk_ref[WBUF - DEC_SEQ + t:WBUF - DEC_SEQ + t + 1, :] = kv_ref[t, b:b + 1, 0:KV_WIDTH]
            nv_ref[WBUF - DEC_SEQ + t:WBUF - DEC_SEQ + t + 1, :] = kv_ref[t, b:b + 1, KV_WIDTH:2 * KV_WIDTH]
            for g in range(GROUP):
                q_row = qg_ref[t, b:b + 1, g * KV_WIDTH:(g + 1) * KV_WIDTH]
                piece = jnp.where(diag, jnp.broadcast_to(q_row, (N_KV_HEADS, KV_WIDTH)), 0.0)
                r0 = (t * GROUP + g) * N_KV_HEADS
                qe_ref[r0:r0 + N_KV_HEADS, :] = piece
        k_t = ck_ref[b]
        v_t = cv_ref[b]
        kw_ref[b] = jnp.where(new_lanes, nk_ref[...].T, pltpu.roll(k_t, WBUF - DEC_SEQ, axis=1))
        vw_ref[b] = jnp.where(new_lanes, nv_ref[...].T, pltpu.roll(v_t, WBUF - DEC_SEQ, axis=1))

        qe = (qe_ref[...] * ATTN_SCALE).astype(BF16)
        s_c = _dot(qe, k_t.astype(BF16)) + bias_c
        s_x = lax.dot_general(qe, nk_ref[x0:WBUF, :].astype(BF16), nt_dims,
                              preferred_element_type=F32) + bias_x
        m = jnp.maximum(jnp.max(s_c, axis=-1, keepdims=True), jnp.max(s_x, axis=-1, keepdims=True))
        p_c = jnp.exp(s_c - m)
        p_x = jnp.exp(s_x - m)
        l = jnp.sum(p_c, axis=-1, keepdims=True) + jnp.sum(p_x, axis=-1, keepdims=True)
        o = lax.dot_general(p_c.astype(BF16), v_t.astype(BF16), nt_dims, preferred_element_type=F32)
        o = (o + _dot(p_x.astype(BF16), nv_ref[x0:WBUF, :].astype(BF16))) / l
        o = jnp.where(out_mask, o, 0.0)
        for t in range(DEC_SEQ):
            for g in range(GROUP):
                r0 = (t * GROUP + g) * N_KV_HEADS
                row = jnp.sum(o[r0:r0 + N_KV_HEADS, :], axis=0, keepdims=True)
                og_ref[t, b:b + 1, g * KV_WIDTH:(g + 1) * KV_WIDTH] = row
    for t in range(DEC_SEQ):
        o_ref[t] = _regroup_heads(og_ref[t], False)


def _attn_sample(q, kv, cache_k_t, cache_v_t, th):
    code_c, code_x = (jnp.asarray(c) for c in _sample_codes())
    rows_all = DEC_SEQ * N_HEADS
    full = lambda a: pl.BlockSpec(a.shape, lambda i: (0,) * a.ndim)
    tb_spec = lambda w: pl.BlockSpec((DEC_SEQ, S_BB, w), lambda i: (0, i, 0))
    cache_spec = pl.BlockSpec((S_BB, KV_WIDTH, WBUF), lambda i: (i, 0, 0))
    cache_shape = jax.ShapeDtypeStruct((DEC_BATCH, KV_WIDTH, WBUF), F32)
    return pl.pallas_call(
        _attn_s_kernel,
        out_shape=(jax.ShapeDtypeStruct((DEC_SEQ, DEC_BATCH, ATTN_WIDTH), F32), cache_shape, cache_shape),
        grid=(DEC_BATCH // S_BB,),
        in_specs=[full(th), full(code_c), full(code_x),
                  tb_spec(ATTN_WIDTH), tb_spec(2 * KV_WIDTH), cache_spec, cache_spec],
        out_specs=(tb_spec(ATTN_WIDTH), cache_spec, cache_spec),
        scratch_shapes=[pltpu.VMEM((rows_all, WBUF), F32),
                        pltpu.VMEM((rows_all, S_EXTRA), F32),
                        pltpu.VMEM((WBUF, KV_WIDTH), F32),
                        pltpu.VMEM((WBUF, KV_WIDTH), F32),
                        pltpu.VMEM((rows_all, KV_WIDTH), F32),
                        pltpu.VMEM((DEC_SEQ, S_BB, ATTN_WIDTH), F32),
                        pltpu.VMEM((DEC_SEQ, S_BB, ATTN_WIDTH), F32)],
        compiler_params=_cparams(1),
        name="attn_sample",
    )(th, code_c, code_x, q, kv, cache_k_t, cache_v_t)


CONV_RC = 32
CONV_ROWS = 128
CONV_HALO = 32


def _ln_silu(acc, lg, lb):
    mu = jnp.mean(acc, axis=-1, keepdims=True)
    xc = acc - mu
    var = jnp.mean(xc * xc, axis=-1, keepdims=True)
    return jax.nn.silu(xc * lax.rsqrt(var + NORM_EPS) * lg + lb)


def _conv_stage(first, zc_ref, zh_ref, s_ref):
    s_ref[0:CONV_HALO, :] = jnp.where(first, 0.0, zh_ref[...])
    s_ref[CONV_HALO:, :] = zc_ref[...]


def _conv_rows(rc, w_ref, b_ref, lg_ref, lb_ref, y_ref, s_ref, c_ref):
    off = CONV_HALO - HIST
    groups = [[j for j in range(CONV_WIDTH) if (j + off) % SUBLANES == r] for r in range(SUBLANES)]
    t0 = rc * CONV_ROWS
    for lc in range(C_CONV // LANES):
        lanes = slice(lc * LANES, (lc + 1) * LANES)
        out = jnp.broadcast_to(b_ref[:, lanes], (CONV_ROWS, LANES))
        for r, taps in enumerate(groups):
            n_rows = CONV_ROWS + (SUBLANES if r else 0)
            part = None
            for j in taps:
                base = t0 + (j + off) - r
                term = w_ref[j:j + 1, lanes] * s_ref[base:base + n_rows, lanes]
                part = term if part is None else part + term
            out = out + part[r:r + CONV_ROWS, :]
        c_ref[t0:t0 + CONV_ROWS, lanes] = out
    for r in range(CONV_ROWS // CONV_RC):
        rows = slice(t0 + r * CONV_RC, t0 + (r + 1) * CONV_RC)
        y_ref[rows, :] = _ln_silu(c_ref[rows, :], lg_ref[...], lb_ref[...]).astype(BF16)


def _mix_conv_kernel(zc_ref, zh_ref, w_ref, b_ref, lg_ref, lb_ref,
                     at_ref, ga_ref, gc_ref, h_ref, g2_ref, wao_ref, wco_ref, wout_ref,
                     o_ref, s_ref, c_ref, y_ref, *, tm):
    _conv_stage(pl.program_id(0) == 0, zc_ref, zh_ref, s_ref)
    for rc in range(tm // CONV_ROWS):
        _conv_rows(rc, w_ref, b_ref, lg_ref, lb_ref, y_ref, s_ref, c_ref)
    a = _dot(at_ref[...], wao_ref[...])
    c = _dot(y_ref[...], wco_ref[...])
    merged = (ga_ref[...].astype(F32) * a + gc_ref[...].astype(F32) * c).astype(BF16)
    r = _dot(merged, wout_ref[...])
    o_ref[...] = h_ref[...] + g2_ref[0:1, :] * r


def _mix_conv_prompt(z, dw_w, dw_b, ln_g, ln_b, attn, ga, gc, h, mod, w_ao, w_co, w_out, tm=256):
    rows = h.shape[0]
    n = rows // tm
    ratio = tm // CONV_HALO
    vec = pl.BlockSpec((1, C_CONV), lambda s: (0, 0))
    row = lambda w: pl.BlockSpec((tm, w), lambda s: (s, 0))
    resident = lambda shape: pl.BlockSpec(shape, lambda s: (0, 0), pipeline_mode=pl.Buffered(1))
    return pl.pallas_call(
        functools.partial(_mix_conv_kernel, tm=tm),
        out_shape=jax.ShapeDtypeStruct((rows, D_MODEL), F32),
        grid=(n,),
        in_specs=[row(C_CONV),
                  pl.BlockSpec((CONV_HALO, C_CONV), lambda s: (jnp.maximum(s * ratio - 1, 0), 0)),
                  pl.BlockSpec((CONV_WIDTH, C_CONV), lambda s: (0, 0)), vec, vec, vec,
                  row(ATTN_WIDTH), row(D_MODEL), row(D_MODEL), row(D_MODEL),
                  _mod_spec(0, False),
                  resident((ATTN_WIDTH, D_MODEL)), resident((C_CONV, D_MODEL)), resident((D_MODEL, D_MODEL))],
        out_specs=row(D_MODEL),
        scratch_shapes=[pltpu.VMEM((tm + CONV_HALO, C_CONV), F32), pltpu.VMEM((tm, C_CONV), F32),
                        pltpu.VMEM((tm, C_CONV), BF16)],
        compiler_params=_cparams(1),
        name="mix_conv",
    )(z, z, dw_w, dw_b, ln_g, ln_b, attn, ga, gc, h, mod, w_ao, w_co, w_out)


CONV_S_BB = 32


def _conv_s_kernel(z_ref, hist_ref, w_ref, b_ref, lg_ref, lb_ref, y_ref, ns_ref):
    for t in range(DEC_SEQ):
        acc = jnp.broadcast_to(b_ref[...], (CONV_S_BB, C_CONV))
        for j in range(CONV_WIDTH):
            i = t + j
            src = hist_ref[i] if i < HIST else z_ref[i - HIST]
            acc = acc + w_ref[j:j + 1, :] * src
        y_ref[t] = _ln_silu(acc, lg_ref[...], lb_ref[...]).astype(BF16)
    keep = HIST - DEC_SEQ
    ns_ref[0:keep] = hist_ref[DEC_SEQ:HIST]
    ns_ref[keep:HIST] = z_ref[...]


def _conv_sample(z, hist_t, dw_w, dw_b, ln_g, ln_b):
    vec = pl.BlockSpec((1, C_CONV), lambda i: (0, 0))
    tb = pl.BlockSpec((DEC_SEQ, CONV_S_BB, C_CONV), lambda i: (0, i, 0))
    st = pl.BlockSpec((HIST, CONV_S_BB, C_CONV), lambda i: (0, i, 0))
    return pl.pallas_call(
        _conv_s_kernel,
        out_shape=(jax.ShapeDtypeStruct((DEC_SEQ, DEC_BATCH, C_CONV), BF16),
                   jax.ShapeDtypeStruct(hist_t.shape, F32)),
        grid=(DEC_BATCH // CONV_S_BB,),
        in_specs=[tb, st, pl.BlockSpec((CONV_WIDTH, C_CONV), lambda i: (0, 0)), vec, vec, vec],
        out_specs=(tb, st),
        compiler_params=_cparams(1),
        name="conv_sample",
    )(z, hist_t, dw_w, dw_b, ln_g, ln_b)


def _mix_s_kernel(at_ref, y_ref, ga_ref, gc_ref, h_ref, g2_ref, wao_ref, wco_ref, wout_ref, o_ref):
    a = _dot(at_ref[...].astype(BF16), wao_ref[...])
    c = _dot(y_ref[...], wco_ref[...])
    merged = (ga_ref[...].astype(F32) * a + gc_ref[...].astype(F32) * c).astype(BF16)
    r = _dot(merged, wout_ref[...])
    o_ref[...] = h_ref[...] + g2_ref[...] * r


def _mix_sample(attn, y, ga, gc, h, mod, w_ao, w_co, w_out):
    row = lambda w: pl.BlockSpec((DEC_BATCH, w), lambda t: (t, 0))
    resident = lambda shape: pl.BlockSpec(shape, lambda t: (0, 0), pipeline_mode=pl.Buffered(1))
    return pl.pallas_call(
        _mix_s_kernel,
        out_shape=jax.ShapeDtypeStruct((S_ROWS, D_MODEL), F32),
        grid=(DEC_SEQ,),
        in_specs=[row(ATTN_WIDTH), row(C_CONV), row(D_MODEL), row(D_MODEL), row(D_MODEL),
                  _mod_spec(0, True),
                  resident((ATTN_WIDTH, D_MODEL)), resident((C_CONV, D_MODEL)), resident((D_MODEL, D_MODEL))],
        out_specs=row(D_MODEL),
        compiler_params=_cparams(1),
        name="mix_sample",
    )(attn, y, ga, gc, h, mod, w_ao, w_co, w_out)


def kernel(x_prompt, x_sample, cache_k, cache_v, state_conv, c_prompt, c_sample, rel_bias_table, norm1_g, ffn1_w1, ffn1_w3, ffn1_w2, norm2_g, w_in, attn_sinks, conv_dw_w, conv_dw_b, conv_ln_g, conv_ln_b, w_conv_out, w_attn_out, w_out, norm3_g, ffn2_w1, ffn2_w3, ffn2_w2, w_ada, b_ada, final_norm_g, w_ada_final, b_ada_final):
    c_all = jnp.concatenate([c_sample, c_prompt, jnp.zeros((N_COND - DEC_BATCH - 1, D_MODEL), F32)], axis=0)
    early = 5 * D_MODEL
    mod = _adaln(c_all, w_ada[0], b_ada, early)
    late_ada = [(w_ada[0], b_ada, early, 4 * D_MODEL), (w_ada_final, b_ada_final[None, :], 0, 2 * D_MODEL)]

    sinks = attn_sinks[0]
    tab_flat = rel_bias_table.reshape(-1)
    tab_ext = jnp.concatenate([rel_bias_table, sinks[None, :]], axis=0)
    th = tab_ext.T.reshape(N_KV_HEADS, GROUP, N_BUCKETS + 1).transpose(1, 0, 2)
    th = jnp.tile(th.reshape(N_HEADS, N_BUCKETS + 1), (DEC_SEQ, 1))

    xp = x_prompt[0]
    xs = x_sample.transpose(1, 0, 2).reshape(S_ROWS, D_MODEL)

    conv_w = (conv_dw_w[0], conv_dw_b, conv_ln_g, conv_ln_b)


    def keys_on_lanes(cache):
        return cache[0].transpose(0, 2, 3, 1).reshape(DEC_BATCH, KV_WIDTH, WBUF)

    def keys_on_rows(win):
        return win.reshape(DEC_BATCH, N_KV_HEADS, HEAD_DIM, WBUF).transpose(0, 3, 1, 2)[None]

    def mixers_s(q, kv, z, ga, gc, h, mod_late, mix_w):
        o, kw, vw = _attn_sample(q.reshape(DEC_SEQ, DEC_BATCH, ATTN_WIDTH),
                                 kv.reshape(DEC_SEQ, DEC_BATCH, 2 * KV_WIDTH),
                                 keys_on_lanes(cache_k), keys_on_lanes(cache_v), th)
        y, ns = _conv_sample(z.reshape(DEC_SEQ, DEC_BATCH, C_CONV), state_conv[0].transpose(1, 0, 2), *conv_w)
        h = _mix_sample(o.reshape(S_ROWS, ATTN_WIDTH), y.reshape(S_ROWS, C_CONV), ga, gc, h, mod_late, *mix_w)
        return h, (kw, vw, ns)

    hs, f1 = _ffn(xs, norm1_g, mod, 0, ffn1_w1[0], ffn1_w3[0], ffn1_w2[0], per_row=True,
                  tm=S_ROWS, tf=FFN_TF_CAST, cast_out=True)
    w_in_rows = D_MODEL // (SEQ // FFN_TM)
    w_in_side = (w_in[0], (w_in_rows, W_IN_SIDE_COLS),
                 lambda i, f: (i, jnp.minimum(f, IN_WIDTH // W_IN_SIDE_COLS - 1)))
    hp, (w_in_b,) = _ffn(xp, norm1_g, mod, 0, *f1, per_row=False, tm=FFN_TM, tf=FFN_TF, side=(w_in_side,))

    qs, kv_s, z_s, ga_s, gc_s = _proj(hs, norm2_g, mod, w_in_b, per_row=True, tm=S_ROWS, q_dtype=F32)
    qp, kv_p, z_p, ga_p, gc_p = _proj(hp, norm2_g, mod, w_in_b, per_row=False, tm=PROJ_TM, q_dtype=BF16)

    side = [_row_slab_side(w[0]) for w in (ffn2_w1, ffn2_w3, ffn2_w2, w_attn_out, w_conv_out, w_out)]
    attn_p, side_b, (mod_late, mod_f) = _attn_prompt(qp, kv_p, tab_flat, sinks, side, c_all, late_ada)
    f2, mix_w = side_b[:3], side_b[3:]
    final = (final_norm_g[None, :], mod_f)

    hs, (kw, vw, ns) = mixers_s(qs, kv_s, z_s, ga_s, gc_s, hs, mod_late, mix_w)
    hp = _mix_conv_prompt(z_p, *conv_w, attn_p, ga_p, gc_p, hp, mod_late, *mix_w)

    ys = _ffn(hs, norm3_g, mod_late, 1, *f2, per_row=True, tm=S_ROWS, tf=FFN_TF, final=final)
    yp = _ffn(hp, norm3_g, mod_late, 1, *f2, per_row=False, tm=FFN_TM, tf=FFN_TF, final=final)

    w = min(WINDOW, SEQ)
    kv_shape = (1, 1, w, N_KV_HEADS, HEAD_DIM)
    k_win_p = kv_p[SEQ - w:, :KV_WIDTH].reshape(kv_shape)
    v_win_p = kv_p[SEQ - w:, KV_WIDTH:].reshape(kv_shape)
    conv_p_state = z_p[SEQ - HIST:].reshape(1, 1, HIST, C_CONV)
    s_shape = (1, DEC_BATCH, WBUF, N_KV_HEADS, HEAD_DIM)
    y_prompt = yp[None]
    y_sample = ys.reshape(DEC_SEQ, DEC_BATCH, D_MODEL).transpose(1, 0, 2)
    return (y_prompt, y_sample, k_win_p, v_win_p, conv_p_state,
            keys_on_rows(kw), keys_on_rows(vw), ns.transpose(1, 0, 2)[None])
```

```python
import functools

import numpy as np
import jax
import jax.numpy as jnp
from jax import lax
from jax.experimental import pallas as pl
from jax.experimental.pallas import tpu as pltpu

D_MODEL = 2048
SEQ = 8192
DEC_BATCH = 128
DEC_SEQ = 4
PAST_LEN = 16384
N_HEADS = 32
N_KV_HEADS = 8
HEAD_DIM = 64
GROUP = N_HEADS // N_KV_HEADS
ATTN_WIDTH = N_HEADS * HEAD_DIM
KV_WIDTH = N_KV_HEADS * HEAD_DIM
WINDOW = 128
Q_BLOCK = 128
ATTN_SCALE = HEAD_DIM ** -0.5
N_BUCKETS = 32
MAX_DISTANCE = 128
C_CONV = D_MODEL // 2
CONV_WIDTH = 31
HIST = CONV_WIDTH - 1
D_FF = 5632
NORM_EPS = 1e-6
NEG_INF = -1e30
IN_WIDTH = ATTN_WIDTH + 2 * KV_WIDTH + 2 * C_CONV + 2 * D_MODEL
WBUF = min(WINDOW, PAST_LEN)

S_ROWS = DEC_BATCH * DEC_SEQ
N_COND = 136
PROMPT_MOD_BLOCK = DEC_BATCH // 8
S_EXTRA = 8

VMEM_LIMIT = 60 * 1024 * 1024
SUBLANES = 8
LANES = 128
BF16_SUBLANES = 16

F32 = jnp.float32
BF16 = jnp.bfloat16


def _cparams(n_axes, flags=None):
    return pltpu.CompilerParams(dimension_semantics=("arbitrary",) * n_axes,
                                vmem_limit_bytes=VMEM_LIMIT, flags=flags)


def _dot(a, b):
    return jnp.dot(a, b, preferred_element_type=F32)


def _rms_mod(x, g, sh, sc):
    ms = jnp.mean(x * x, axis=-1, keepdims=True)
    y = x * lax.rsqrt(ms + NORM_EPS) * g
    return y * (1.0 + sc) + sh


def _mod_val(ref, per_row):
    return ref[...] if per_row else ref[0:1, :]


def _mod_spec(chunk, per_row, width=D_MODEL):
    if per_row:
        return pl.BlockSpec((DEC_BATCH, width), lambda *_: (0, chunk))
    return pl.BlockSpec((8, width), lambda *_: (PROMPT_MOD_BLOCK, chunk))


def _adaln_kernel(c_ref, w_ref, b_ref, o_ref, s_ref):
    @pl.when(pl.program_id(0) == 0)
    def _():
        s_ref[...] = jax.nn.silu(c_ref[...]).astype(BF16)

    o_ref[...] = _dot(s_ref[...], w_ref[...].astype(BF16)) + b_ref[...]


def _adaln(c_all, w, b, n, tn=1024):
    return pl.pallas_call(
        _adaln_kernel,
        out_shape=jax.ShapeDtypeStruct((N_COND, n), F32),
        grid=(n // tn,),
        in_specs=[pl.BlockSpec((N_COND, D_MODEL), lambda j: (0, 0)),
                  pl.BlockSpec((D_MODEL, tn), lambda j: (0, j)),
                  pl.BlockSpec((1, tn), lambda j: (0, j))],
        out_specs=pl.BlockSpec((N_COND, tn), lambda j: (0, j)),
        scratch_shapes=[pltpu.VMEM((N_COND, D_MODEL), BF16)],
        compiler_params=_cparams(1),
        name="adaln",
    )(c_all, w, b)


FFN_ROWS = 512
FFN_TM = 1024
FFN_TF = 512
FFN_TF_CAST = 256
FFN_WB = 256
PROJ_TM = 1024
W_IN_SIDE_COLS = 1024


def _to_col_blocks(o_ref, x):
    for b in range(o_ref.shape[0]):
        o_ref[b] = x[:, b * FFN_WB:(b + 1) * FFN_WB]


def _side_cast(side_in, side_out):
    for i_ref, o_ref in zip(side_in, side_out):
        if len(o_ref.shape) == 3:
            _to_col_blocks(o_ref, i_ref[...].astype(BF16))
        else:
            o_ref[...] = i_ref[...].astype(BF16)


def _side_specs(side):
    args, in_specs, out_shapes, out_specs = [], [], [], []
    for a, block, index_map, col_blocked in side:
        args.append(a)
        in_specs.append(pl.BlockSpec(block, index_map))
        if col_blocked:
            assert block[1] == a.shape[1]
            n = a.shape[1] // FFN_WB
            out_shapes.append(jax.ShapeDtypeStruct((n, a.shape[0], FFN_WB), BF16))
            out_specs.append(pl.BlockSpec((n, block[0], FFN_WB),
                                          lambda *idx, index_map=index_map: (0, index_map(*idx)[0], 0)))
        else:
            out_shapes.append(jax.ShapeDtypeStruct(a.shape, BF16))
            out_specs.append(pl.BlockSpec(block, index_map))
    return args, in_specs, out_shapes, out_specs


def _ffn_kernel(*refs, tm, sub, per_row, n_f, final, cast_out, n_side):
    refs = list(refs)
    x_ref, n_ref, sh_ref, sc_ref, g_ref, w1_ref, w3_ref, w2_ref = refs[:8]
    del refs[:8]
    if final:
        gf_ref, shf_ref, scf_ref = refs[:3]
        del refs[:3]
    side_in = refs[:n_side]
    del refs[:n_side]
    o_ref = refs.pop(0)
    if cast_out:
        w1o_ref, w3o_ref, w2o_ref = refs[:3]
        del refs[:3]
    side_out = refs[:n_side]
    del refs[:n_side]
    (u_ref,) = refs
    f = pl.program_id(1)
    _side_cast(side_in, side_out)

    if cast_out:
        _to_col_blocks(w1o_ref, w1_ref[...].astype(BF16))
        _to_col_blocks(w3o_ref, w3_ref[...].astype(BF16))
        w2o_ref[...] = w2_ref[...].astype(BF16)
        w1_ref, w3_ref, w2_ref = w1o_ref, w3o_ref, w2o_ref

    def up_proj(u, w_ref):
        parts = [_dot(u, w_ref[b]) for b in range(w_ref.shape[0])]
        return parts[0] if len(parts) == 1 else jnp.concatenate(parts, axis=1)

    def accumulate(first, last):
        for s in range(tm // FFN_ROWS):
            rows = slice(s * FFN_ROWS, (s + 1) * FFN_ROWS)
            subs = [slice(r0, r0 + sub) for r0 in range(s * FFN_ROWS, (s + 1) * FFN_ROWS, sub)]
            if first:
                for r in subs:
                    u_ref[r, :] = _rms_mod(x_ref[r, :], n_ref[...], _mod_val(sh_ref, per_row),
                                           _mod_val(sc_ref, per_row)).astype(BF16)
            u = u_ref[rows, :]
            h1 = up_proj(u, w1_ref)
            h3 = up_proj(u, w3_ref)
            a = (jax.nn.silu(h1) * h3).astype(BF16)
            d = _dot(a, w2_ref[...])
            if first:
                o_ref[rows, :] = d
            else:
                o_ref[rows, :] += d
            if last:
                for r in subs:
                    h = x_ref[r, :] + 0.5 * _mod_val(g_ref, per_row) * o_ref[r, :]
                    if final:
                        h = _rms_mod(h, gf_ref[...], _mod_val(shf_ref, per_row), _mod_val(scf_ref, per_row))
                    o_ref[r, :] = h

    assert n_f > 2

    @pl.when(f == 0)
    def _():
        accumulate(True, False)

    @pl.when((f > 0) & (f < n_f - 1))
    def _():
        accumulate(False, False)

    @pl.when(f == n_f - 1)
    def _():
        accumulate(False, True)


def _ffn(x, norm_g, mod, chunk0, w1, w3, w2, *, per_row, tm, tf, final=None, cast_out=False, side=()):
    rows = x.shape[0]
    n_f = D_FF // tf
    sub = DEC_BATCH
    row_spec = pl.BlockSpec((tm, D_MODEL), lambda i, f: (i, 0))
    vec_spec = pl.BlockSpec((1, D_MODEL), lambda i, f: (0, 0))
    up_blocked = pl.BlockSpec((tf // FFN_WB, D_MODEL, FFN_WB), lambda i, f: (f, 0, 0))
    up_plain = pl.BlockSpec((D_MODEL, tf), lambda i, f: (0, f))
    down = pl.BlockSpec((tf, D_MODEL), lambda i, f: (f, 0))
    in_specs = [row_spec, vec_spec,
                _mod_spec(chunk0, per_row), _mod_spec(chunk0 + 1, per_row), _mod_spec(chunk0 + 2, per_row)]
    in_specs += [up_plain, up_plain, down] if cast_out else [up_blocked, up_blocked, down]
    args = [x, norm_g, mod, mod, mod, w1, w3, w2]
    if final is not None:
        gf, mod_f = final
        in_specs += [vec_spec, _mod_spec(0, per_row), _mod_spec(1, per_row)]
        args += [gf, mod_f, mod_f]
    side_args, side_in_specs, side_shapes, side_out_specs = _side_specs(side)
    in_specs += side_in_specs
    args += side_args
    out_shape = [jax.ShapeDtypeStruct((rows, D_MODEL), F32)]
    out_specs = [row_spec]
    if cast_out:
        assert rows == tm, "every weight block must be visited exactly once"
        blocked = jax.ShapeDtypeStruct((D_FF // FFN_WB, D_MODEL, FFN_WB), BF16)
        out_shape += [blocked, blocked, jax.ShapeDtypeStruct(w2.shape, BF16)]
        out_specs += [up_blocked, up_blocked, down]
    out_shape += side_shapes
    out_specs += side_out_specs
    outs = pl.pallas_call(
        functools.partial(_ffn_kernel, tm=tm, sub=sub, per_row=per_row, n_f=n_f,
                          final=final is not None, cast_out=cast_out, n_side=len(side)),
        out_shape=out_shape,
        grid=(rows // tm, n_f),
        in_specs=in_specs,
        out_specs=out_specs,
        scratch_shapes=[pltpu.VMEM((tm, D_MODEL), BF16)],
        compiler_params=_cparams(2),
        name="ffn_final" if final is not None else "ffn",
    )(*args)
    return (outs[0], tuple(outs[1:])) if (cast_out or side) else outs[0]


PROJ_CHUNK = 256
PROJ_NORM_ROWS = 512
PROJ_CW = 512


def _proj_layout():
    cw = PROJ_CW
    n_q = ATTN_WIDTH // (2 * cw)
    n_kv = 2 * KV_WIDTH // (2 * cw)
    n_glu = C_CONV // cw
    n_gate = D_MODEL // (2 * cw)
    pre = n_q + n_kv
    steps = pre + n_glu + 2 * n_gate

    def block_a(j):
        return jnp.where((j >= pre) & (j < pre + n_glu), j + pre, 2 * j)

    def block_b(j):
        return jnp.where((j >= pre) & (j < pre + n_glu), j + pre + n_glu, 2 * j + 1)

    return n_q, n_kv, n_glu, n_gate, steps, block_a, block_b


def _proj_kernel(h_ref, n_ref, sh_ref, sc_ref, wa_ref, wb_ref, q_ref, kv_ref, z_ref, ga_ref, gc_ref, u_ref,
                 *, tm, sub, per_row):
    cw = PROJ_CW
    n_q, n_kv, n_glu, n_gate, _, _, _ = _proj_layout()
    j = pl.program_id(1)

    n_chunks = cw // PROJ_CHUNK

    def chunks(w_ref, rows=slice(None)):
        for c in range(n_chunks):
            cols = slice(c * PROJ_CHUNK, (c + 1) * PROJ_CHUNK)
            yield c * PROJ_CHUNK, _dot(u_ref[rows, :], w_ref[:, cols])

    def pair_store(o_ref, fn, rows=slice(None)):
        for half, w_ref in enumerate((wa_ref, wb_ref)):
            for c0, acc in chunks(w_ref, rows):
                o_ref[rows, half * cw + c0:half * cw + c0 + PROJ_CHUNK] = fn(acc).astype(o_ref.dtype)

    @pl.when(j == 0)
    def _():
        for g0 in range(0, tm, PROJ_NORM_ROWS):
            for r0 in range(g0, g0 + PROJ_NORM_ROWS, sub):
                r = slice(r0, r0 + sub)
                u_ref[r, :] = _rms_mod(h_ref[r, :], n_ref[...], _mod_val(sh_ref, per_row),
                                       _mod_val(sc_ref, per_row)).astype(BF16)
            pair_store(q_ref, lambda acc: acc, slice(g0, g0 + PROJ_NORM_ROWS))

    lo = 0

    @pl.when((j > 0) & (j < n_q))
    def _():
        pair_store(q_ref, lambda acc: acc)

    lo += n_q

    @pl.when((j >= lo) & (j < lo + n_kv))
    def _():
        pair_store(kv_ref, lambda acc: acc)

    lo += n_kv

    @pl.when((j >= lo) & (j < lo + n_glu))
    def _():
        for (c0, a), (_, g) in zip(chunks(wa_ref), chunks(wb_ref)):
            z_ref[:, c0:c0 + PROJ_CHUNK] = a * jax.nn.sigmoid(g)

    lo += n_glu

    @pl.when((j >= lo) & (j < lo + n_gate))
    def _():
        pair_store(ga_ref, jax.nn.sigmoid)

    lo += n_gate

    @pl.when(j >= lo)
    def _():
        pair_store(gc_ref, jax.nn.sigmoid)


def _proj(h, norm_g, mod, w, *, per_row, tm, q_dtype):
    rows = h.shape[0]
    sub = DEC_BATCH
    cw = PROJ_CW
    n_q, n_kv, n_glu, n_gate, steps, block_a, block_b = _proj_layout()

    def out_spec(width, first, count):
        return pl.BlockSpec((tm, width), lambda i, j: (i, jnp.clip(j - first, 0, count - 1)))

    wa_spec = pl.BlockSpec((D_MODEL, cw), lambda i, j: (0, block_a(j)))
    wb_spec = pl.BlockSpec((D_MODEL, cw), lambda i, j: (0, block_b(j)))
    out_shape = [jax.ShapeDtypeStruct((rows, ATTN_WIDTH), q_dtype),
                 jax.ShapeDtypeStruct((rows, 2 * KV_WIDTH), F32),
                 jax.ShapeDtypeStruct((rows, C_CONV), F32),
                 jax.ShapeDtypeStruct((rows, D_MODEL), BF16),
                 jax.ShapeDtypeStruct((rows, D_MODEL), BF16)]
    out_specs = [out_spec(2 * cw, 0, n_q),
                 out_spec(2 * cw, n_q, n_kv),
                 out_spec(cw, n_q + n_kv, n_glu),
                 out_spec(2 * cw, n_q + n_kv + n_glu, n_gate),
                 out_spec(2 * cw, n_q + n_kv + n_glu + n_gate, n_gate)]
    return pl.pallas_call(
        functools.partial(_proj_kernel, tm=tm, sub=sub, per_row=per_row),
        out_shape=out_shape,
        grid=(rows // tm, steps),
        in_specs=[pl.BlockSpec((tm, D_MODEL), lambda i, j: (i, 0)),
                  pl.BlockSpec((1, D_MODEL), lambda i, j: (0, 0)),
                  _mod_spec(3, per_row), _mod_spec(4, per_row), wa_spec, wb_spec],
        out_specs=out_specs,
        scratch_shapes=[pltpu.VMEM((tm, D_MODEL), BF16)],
        compiler_params=_cparams(2),
        name="proj",
    )(h, norm_g, mod, mod, w, w)


def _t5_bucket_np(dist):
    exact = N_BUCKETS // 2
    d = np.maximum(dist, 0)
    df = np.maximum(d, 1).astype(np.float32)
    large = exact + (np.log(df / np.float32(exact)) / np.float32(np.log(MAX_DISTANCE / exact))
                     * np.float32(N_BUCKETS - exact)).astype(np.int32)
    large = np.minimum(large, N_BUCKETS - 1)
    return np.where(d < exact, d, large).astype(np.int32)


def _prompt_codes():
    qi = np.arange(Q_BLOCK)[:, None]
    kj = np.arange(2 * Q_BLOCK)[None, :]
    dist = qi + Q_BLOCK - kj
    valid = (dist >= 0) & (dist <= WINDOW)
    return np.where(valid, _t5_bucket_np(dist), -1).astype(np.int32)


def _sample_codes():
    t = np.repeat(np.arange(DEC_SEQ), GROUP * N_KV_HEADS)[:, None]
    j = np.arange(WBUF)[None, :]
    dist = t + WBUF - j
    cache = np.where((dist >= 0) & (dist <= WINDOW), _t5_bucket_np(dist), -1)
    c = np.arange(S_EXTRA)[None, :]
    t_new = c - (S_EXTRA - DEC_SEQ)
    dist = t - t_new
    extra = np.where((t_new >= 0) & (dist >= 0) & (dist <= WINDOW), _t5_bucket_np(dist), -1)
    extra = np.where(c == 0, N_BUCKETS, extra)
    return cache.astype(np.int32), extra.astype(np.int32)


BIAS_ROWS = 64
ATTN_QB = 2


def _attn_p_kernel(*refs, n_side, n_ada):
    refs = list(refs)
    tab_ref, sink_ref, code_ref, q_ref, kvc_ref, kvp_ref = refs[:6]
    del refs[:6]
    side_in = refs[:n_side]
    del refs[:n_side]
    c_ref = refs.pop(0)
    ada_in = [(refs[2 * k], refs[2 * k + 1]) for k in range(n_ada)]
    del refs[:2 * n_ada]
    o_ref = refs.pop(0)
    side_out = refs[:n_side]
    del refs[:n_side]
    ada_out = refs[:n_ada]
    del refs[:n_ada]
    bias_ref, silu_c_ref = refs
    n = pl.program_id(0)
    _side_cast(side_in, side_out)

    @pl.when(n == 0)
    def _():
        silu_c_ref[...] = jax.nn.silu(c_ref[...]).astype(BF16)

    for (w_ref, b_ref), m_ref in zip(ada_in, ada_out):
        m_ref[...] = _dot(silu_c_ref[...], w_ref[...].astype(BF16)) + b_ref[...]

    n_keys = 2 * Q_BLOCK
    wide = GROUP * Q_BLOCK

    @pl.when(n == 0)
    def _():
        def kv_body(kv, carry):
            for g in range(GROUP):
                for r in range(n_keys // BIAS_ROWS):
                    rows = slice(r * BIAS_ROWS, (r + 1) * BIAS_ROWS)
                    code = code_ref[rows, :]
                    acc = jnp.full((BIAS_ROWS, Q_BLOCK), NEG_INF, F32)
                    for bk in range(N_BUCKETS):
                        acc = jnp.where(code == bk, tab_ref[bk * N_HEADS + kv * GROUP + g], acc)
                    bias_ref[kv, rows, g * Q_BLOCK:(g + 1) * Q_BLOCK] = acc
            return carry
        lax.fori_loop(0, N_KV_HEADS, kv_body, 0)

    ones = jnp.ones((BF16_SUBLANES, n_keys), BF16)
    zeros = jnp.zeros((HEAD_DIM, wide), BF16)
    tile = (n_keys, LANES)
    key = lax.broadcasted_iota(jnp.int32, tile, 0)
    lane = lax.broadcasted_iota(jnp.int32, tile, 1)
    pen = jnp.where((lane == 0) & (key < Q_BLOCK) & (n == 0), NEG_INF, 0.0).astype(BF16)
    pen_rows = (lax.broadcasted_iota(jnp.int32, (LANES, wide), 0) == 0).astype(BF16)

    def query_block(a):
        rows = slice(a * Q_BLOCK, (a + 1) * Q_BLOCK)
        q_t = q_ref[rows, :].astype(F32).T

        def keys(cols):
            prev = kvp_ref[:, cols] if a == 0 else kvc_ref[(a - 1) * Q_BLOCK:a * Q_BLOCK, cols]
            return jnp.concatenate([prev, kvc_ref[rows, cols]], axis=0)

        def pair_kv(pair):
            k2 = keys(slice(pair * LANES, (pair + 1) * LANES)).astype(BF16)
            if a == 0:
                k2 = jnp.concatenate([k2, pen], axis=1)
            v2_t = keys(slice(KV_WIDTH + pair * LANES, KV_WIDTH + (pair + 1) * LANES)).T.astype(BF16)
            return k2, v2_t

        def scores(kv, k2):
            q4 = jnp.concatenate(
                [q_t[(kv * GROUP + g) * HEAD_DIM:(kv * GROUP + g + 1) * HEAD_DIM, :] for g in range(GROUP)],
                axis=1)
            q4 = (q4 * ATTN_SCALE).astype(BF16)
            parts = [q4, zeros] if kv % 2 == 0 else [zeros, q4]
            return _dot(k2, jnp.concatenate(parts + ([pen_rows] if a == 0 else []), axis=0))

        kvs = [pair_kv(pair) for pair in range(N_KV_HEADS // 2)]
        for kv in range(N_KV_HEADS):
            pair, parity = divmod(kv, 2)
            s = scores(kv, kvs[pair][0]) + bias_ref[kv]
            sink = jnp.concatenate(
                [jnp.full((1, Q_BLOCK), sink_ref[kv * GROUP + g], F32) for g in range(GROUP)], axis=1)
            m = jnp.maximum(jnp.max(s, axis=0, keepdims=True), sink)
            p = jnp.exp(s - m).astype(BF16)
            lhs = jnp.concatenate([kvs[pair][1][parity * HEAD_DIM:(parity + 1) * HEAD_DIM, :], ones], axis=0)
            o_t = _dot(lhs, p)
            l = o_t[HEAD_DIM:HEAD_DIM + 1, :] + jnp.exp(sink - m)
            o_n = o_t[0:HEAD_DIM, :] * (1.0 / l)
            o_kv = jnp.concatenate([o_n[:, g * Q_BLOCK:(g + 1) * Q_BLOCK] for g in range(GROUP)], axis=0)
            o_ref[rows, kv * GROUP * HEAD_DIM:(kv + 1) * GROUP * HEAD_DIM] = o_kv.T.astype(o_ref.dtype)

    for a in range(ATTN_QB):
        query_block(a)


ATTN_STEPS = SEQ // (ATTN_QB * Q_BLOCK)


def _row_slab_side(w, col_blocked=False):
    slab = w.shape[0] // ATTN_STEPS
    assert slab * ATTN_STEPS == w.shape[0] and slab % BF16_SUBLANES == 0
    return (w, (slab, w.shape[1]), lambda n: (n, 0), col_blocked)


def _attn_prompt(q, kv, tab_flat, sinks, side, c_all, ada):
    step_rows = ATTN_QB * Q_BLOCK
    code_t = jnp.asarray(np.ascontiguousarray(_prompt_codes().T))
    smem = pl.BlockSpec(memory_space=pltpu.SMEM)
    side_args, side_in_specs, side_shapes, side_out_specs = _side_specs(side)
    ada_args, ada_in_specs, ada_out_specs, ada_shapes = [], [], [], []
    for w, b, col0, n_cols in ada:
        tn = n_cols // ATTN_STEPS
        assert tn * ATTN_STEPS == n_cols and tn % LANES == 0 and col0 % tn == 0
        first = col0 // tn
        ada_args += [w, b]
        ada_in_specs += [pl.BlockSpec((D_MODEL, tn), lambda n, first=first: (0, first + n)),
                         pl.BlockSpec((1, tn), lambda n, first=first: (0, first + n))]
        ada_out_specs.append(pl.BlockSpec((N_COND, tn), lambda n: (0, n)))
        ada_shapes.append(jax.ShapeDtypeStruct((N_COND, n_cols), F32))
    out_spec = pl.BlockSpec((step_rows, ATTN_WIDTH), lambda n: (n, 0))
    outs = pl.pallas_call(
        functools.partial(_attn_p_kernel, n_side=len(side), n_ada=len(ada)),
        out_shape=[jax.ShapeDtypeStruct((SEQ, ATTN_WIDTH), BF16)] + side_shapes + ada_shapes,
        grid=(ATTN_STEPS,),
        in_specs=[smem, smem,
                  pl.BlockSpec((2 * Q_BLOCK, Q_BLOCK), lambda n: (0, 0)),
                  pl.BlockSpec((step_rows, ATTN_WIDTH), lambda n: (n, 0)),
                  pl.BlockSpec((step_rows, 2 * KV_WIDTH), lambda n: (n, 0)),
                  pl.BlockSpec((Q_BLOCK, 2 * KV_WIDTH), lambda n: (jnp.maximum(ATTN_QB * n - 1, 0), 0))]
        + side_in_specs + [pl.BlockSpec((N_COND, D_MODEL), lambda n: (0, 0))] + ada_in_specs,
        out_specs=[out_spec] + side_out_specs + ada_out_specs,
        scratch_shapes=[pltpu.VMEM((N_KV_HEADS, 2 * Q_BLOCK, GROUP * Q_BLOCK), F32),
                        pltpu.VMEM((N_COND, D_MODEL), BF16)],
        compiler_params=_cparams(1),
        name="attn_prompt",
    )(tab_flat, sinks, code_t, q, kv, kv, *side_args, c_all, *ada_args)
    n_side = len(side)
    return outs[0], tuple(outs[1:1 + n_side]), tuple(outs[1 + n_side:])


S_BB = 8


def _regroup_heads(x, to_group_major):
    half = HEAD_DIM
    assert LANES == 2 * half
    low = lax.broadcasted_iota(jnp.int32, (x.shape[0], LANES), 1) < half
    n_cols = ATTN_WIDTH // LANES
    outer, inner = (GROUP, N_KV_HEADS) if to_group_major else (N_KV_HEADS, GROUP)
    cols = []
    for c in range(n_cols):
        o, i = divmod(2 * c, inner)
        src = [(i + d) * outer + o for d in range(2)]
        a = x[:, (src[0] // 2) * LANES:(src[0] // 2 + 1) * LANES]
        b = x[:, (src[1] // 2) * LANES:(src[1] // 2 + 1) * LANES]
        assert src[0] % 2 == src[1] % 2
        if src[0] % 2 == 0:
            cols.append(jnp.where(low, a, pltpu.roll(b, half, axis=1)))
        else:
            cols.append(jnp.where(low, pltpu.roll(a, half, axis=1), b))
    return jnp.concatenate(cols, axis=1)


def _attn_s_kernel(th_ref, code_c_ref, code_x_ref, q_ref, kv_ref, ck_ref, cv_ref, o_ref, kw_ref, vw_ref,
                   bias_c_ref, bias_x_ref, nk_ref, nv_ref, qe_ref, qg_ref, og_ref):
    rows_all = DEC_SEQ * N_HEADS
    for t in range(DEC_SEQ):
        qg_ref[t] = _regroup_heads(q_ref[t], True)

    @pl.when(pl.program_id(0) == 0)
    def _():
        for code_ref, bias_ref in ((code_c_ref, bias_c_ref), (code_x_ref, bias_x_ref)):
            code = code_ref[...]
            acc = jnp.full(code.shape, NEG_INF, F32)
            for bk in range(N_BUCKETS + 1):
                acc = jnp.where(code == bk, th_ref[:, bk:bk + 1], acc)
            bias_ref[...] = acc
        nk_ref[...] = jnp.zeros_like(nk_ref)
        nv_ref[...] = jnp.zeros_like(nv_ref)

    lane_head = lax.broadcasted_iota(jnp.int32, (N_KV_HEADS, KV_WIDTH), 1) // HEAD_DIM
    row_head = lax.broadcasted_iota(jnp.int32, (N_KV_HEADS, KV_WIDTH), 0)
    diag = lane_head == row_head
    out_mask = (lax.broadcasted_iota(jnp.int32, (rows_all, KV_WIDTH), 0) % N_KV_HEADS
                == lax.broadcasted_iota(jnp.int32, (rows_all, KV_WIDTH), 1) // HEAD_DIM)
    new_lanes = lax.broadcasted_iota(jnp.int32, (KV_WIDTH, WBUF), 1) >= WBUF - DEC_SEQ
    bias_c = bias_c_ref[...]
    bias_x = bias_x_ref[...]
    nt_dims = (((1,), (1,)), ((), ()))
    x0 = WBUF - S_EXTRA

    for b in range(S_BB):
        for t in range(DEC_SEQ):
            nk_ref[WBUF - DEC_SEQ + t:WBUF - DEC_SEQ + t + 1, :] = kv_ref[t, b:b + 1, 0:KV_WIDTH]
            nv_ref[WBUF - DEC_SEQ + t:WBUF - DEC_SEQ + t + 1, :] = kv_ref[t, b:b + 1, KV_WIDTH:2 * KV_WIDTH]
            for g in range(GROUP):
                q_row = qg_ref[t, b:b + 1, g * KV_WIDTH:(g + 1) * KV_WIDTH]
                piece = jnp.where(diag, jnp.broadcast_to(q_row, (N_KV_HEADS, KV_WIDTH)), 0.0)
                r0 = (t * GROUP + g) * N_KV_HEADS
                qe_ref[r0:r0 + N_KV_HEADS, :] = piece
        k_t = ck_ref[b]
        v_t = cv_ref[b]
        kw_ref[b] = jnp.where(new_lanes, nk_ref[...].T, pltpu.roll(k_t, WBUF - DEC_SEQ, axis=1))
        vw_ref[b] = jnp.where(new_lanes, nv_ref[...].T, pltpu.roll(v_t, WBUF - DEC_SEQ, axis=1))

        qe = (qe_ref[...] * ATTN_SCALE).astype(BF16)
        s_c = _dot(qe, k_t.astype(BF16)) + bias_c
        s_x = lax.dot_general(qe, nk_ref[x0:WBUF, :].astype(BF16), nt_dims,
                              preferred_element_type=F32) + bias_x
        m = jnp.maximum(jnp.max(s_c, axis=-1, keepdims=True), jnp.max(s_x, axis=-1, keepdims=True))
        p_c = jnp.exp(s_c - m)
        p_x = jnp.exp(s_x - m)
        l = jnp.sum(p_c, axis=-1, keepdims=True) + jnp.sum(p_x, axis=-1, keepdims=True)
        o = lax.dot_general(p_c.astype(BF16), v_t.astype(BF16), nt_dims, preferred_element_type=F32)
        o = (o + _dot(p_x.astype(BF16), nv_ref[x0:WBUF, :].astype(BF16))) / l
        o = jnp.where(out_mask, o, 0.0)
        for t in range(DEC_SEQ):
            for g in range(GROUP):
                r0 = (t * GROUP + g) * N_KV_HEADS
                row = jnp.sum(o[r0:r0 + N_KV_HEADS, :], axis=0, keepdims=True)
                og_ref[t, b:b + 1, g * KV_WIDTH:(g + 1) * KV_WIDTH] = row
    for t in range(DEC_SEQ):
        o_ref[t] = _regroup_heads(og_ref[t], False)


def _attn_sample(q, kv, cache_k_t, cache_v_t, th):
    code_c, code_x = (jnp.asarray(c) for c in _sample_codes())
    rows_all = DEC_SEQ * N_HEADS
    full = lambda a: pl.BlockSpec(a.shape, lambda i: (0,) * a.ndim)
    tb_spec = lambda w: pl.BlockSpec((DEC_SEQ, S_BB, w), lambda i: (0, i, 0))
    cache_spec = pl.BlockSpec((S_BB, KV_WIDTH, WBUF), lambda i: (i, 0, 0))
    cache_shape = jax.ShapeDtypeStruct((DEC_BATCH, KV_WIDTH, WBUF), F32)
    return pl.pallas_call(
        _attn_s_kernel,
        out_shape=(jax.ShapeDtypeStruct((DEC_SEQ, DEC_BATCH, ATTN_WIDTH), F32), cache_shape, cache_shape),
        grid=(DEC_BATCH // S_BB,),
        in_specs=[full(th), full(code_c), full(code_x),
                  tb_spec(ATTN_WIDTH), tb_spec(2 * KV_WIDTH), cache_spec, cache_spec],
        out_specs=(tb_spec(ATTN_WIDTH), cache_spec, cache_spec),
        scratch_shapes=[pltpu.VMEM((rows_all, WBUF), F32),
                        pltpu.VMEM((rows_all, S_EXTRA), F32),
                        pltpu.VMEM((WBUF, KV_WIDTH), F32),
                        pltpu.VMEM((WBUF, KV_WIDTH), F32),
                        pltpu.VMEM((rows_all, KV_WIDTH), F32),
                        pltpu.VMEM((DEC_SEQ, S_BB, ATTN_WIDTH), F32),
                        pltpu.VMEM((DEC_SEQ, S_BB, ATTN_WIDTH), F32)],
        compiler_params=_cparams(1),
        name="attn_sample",
    )(th, code_c, code_x, q, kv, cache_k_t, cache_v_t)


CONV_RC = 32
CONV_ROWS = 128
CONV_HALO = 32


def _ln_silu(acc, lg, lb):
    mu = jnp.mean(acc, axis=-1, keepdims=True)
    xc = acc - mu
    var = jnp.mean(xc * xc, axis=-1, keepdims=True)
    return jax.nn.silu(xc * lax.rsqrt(var + NORM_EPS) * lg + lb)


def _conv_stage(first, zc_ref, zh_ref, s_ref):
    s_ref[0:CONV_HALO, :] = jnp.where(first, 0.0, zh_ref[...])
    s_ref[CONV_HALO:, :] = zc_ref[...]


def _conv_rows(rc, w_ref, b_ref, lg_ref, lb_ref, y_ref, s_ref, c_ref):
    off = CONV_HALO - HIST
    groups = [[j for j in range(CONV_WIDTH) if (j + off) % SUBLANES == r] for r in range(SUBLANES)]
    t0 = rc * CONV_ROWS
    for lc in range(C_CONV // LANES):
        lanes = slice(lc * LANES, (lc + 1) * LANES)
        out = jnp.broadcast_to(b_ref[:, lanes], (CONV_ROWS, LANES))
        for r, taps in enumerate(groups):
            n_rows = CONV_ROWS + (SUBLANES if r else 0)
            part = None
            for j in taps:
                base = t0 + (j + off) - r
                term = w_ref[j:j + 1, lanes] * s_ref[base:base + n_rows, lanes]
                part = term if part is None else part + term
            out = out + part[r:r + CONV_ROWS, :]
        c_ref[t0:t0 + CONV_ROWS, lanes] = out
    for r in range(CONV_ROWS // CONV_RC):
        rows = slice(t0 + r * CONV_RC, t0 + (r + 1) * CONV_RC)
        y_ref[rows, :] = _ln_silu(c_ref[rows, :], lg_ref[...], lb_ref[...]).astype(BF16)


def _mix_conv_kernel(zc_ref, zh_ref, w_ref, b_ref, lg_ref, lb_ref,
                     at_ref, ga_ref, gc_ref, h_ref, g2_ref, wao_ref, wco_ref, wout_ref,
                     o_ref, s_ref, c_ref, y_ref, *, tm):
    _conv_stage(pl.program_id(0) == 0, zc_ref, zh_ref, s_ref)
    for rc in range(tm // CONV_ROWS):
        _conv_rows(rc, w_ref, b_ref, lg_ref, lb_ref, y_ref, s_ref, c_ref)
    a = _dot(at_ref[...], wao_ref[...])
    c = _dot(y_ref[...], wco_ref[...])
    merged = (ga_ref[...].astype(F32) * a + gc_ref[...].astype(F32) * c).astype(BF16)
    r = _dot(merged, wout_ref[...])
    o_ref[...] = h_ref[...] + g2_ref[0:1, :] * r


def _mix_conv_prompt(z, dw_w, dw_b, ln_g, ln_b, attn, ga, gc, h, mod, w_ao, w_co, w_out, tm=256):
    rows = h.shape[0]
    n = rows // tm
    ratio = tm // CONV_HALO
    vec = pl.BlockSpec((1, C_CONV), lambda s: (0, 0))
    row = lambda w: pl.BlockSpec((tm, w), lambda s: (s, 0))
    resident = lambda shape: pl.BlockSpec(shape, lambda s: (0, 0), pipeline_mode=pl.Buffered(1))
    return pl.pallas_call(
        functools.partial(_mix_conv_kernel, tm=tm),
        out_shape=jax.ShapeDtypeStruct((rows, D_MODEL), F32),
        grid=(n,),
        in_specs=[row(C_CONV),
                  pl.BlockSpec((CONV_HALO, C_CONV), lambda s: (jnp.maximum(s * ratio - 1, 0), 0)),
                  pl.BlockSpec((CONV_WIDTH, C_CONV), lambda s: (0, 0)), vec, vec, vec,
                  row(ATTN_WIDTH), row(D_MODEL), row(D_MODEL), row(D_MODEL),
                  _mod_spec(0, False),
                  resident((ATTN_WIDTH, D_MODEL)), resident((C_CONV, D_MODEL)), resident((D_MODEL, D_MODEL))],
        out_specs=row(D_MODEL),
        scratch_shapes=[pltpu.VMEM((tm + CONV_HALO, C_CONV), F32), pltpu.VMEM((tm, C_CONV), F32),
                        pltpu.VMEM((tm, C_CONV), BF16)],
        compiler_params=_cparams(1),
        name="mix_conv",
    )(z, z, dw_w, dw_b, ln_g, ln_b, attn, ga, gc, h, mod, w_ao, w_co, w_out)


CONV_S_BB = 32


def _conv_s_kernel(z_ref, hist_ref, w_ref, b_ref, lg_ref, lb_ref, y_ref, ns_ref):
    for t in range(DEC_SEQ):
        acc = jnp.broadcast_to(b_ref[...], (CONV_S_BB, C_CONV))
        for j in range(CONV_WIDTH):
            i = t + j
            src = hist_ref[i] if i < HIST else z_ref[i - HIST]
            acc = acc + w_ref[j:j + 1, :] * src
        y_ref[t] = _ln_silu(acc, lg_ref[...], lb_ref[...]).astype(BF16)
    keep = HIST - DEC_SEQ
    ns_ref[0:keep] = hist_ref[DEC_SEQ:HIST]
    ns_ref[keep:HIST] = z_ref[...]


def _conv_sample(z, hist_t, dw_w, dw_b, ln_g, ln_b):
    vec = pl.BlockSpec((1, C_CONV), lambda i: (0, 0))
    tb = pl.BlockSpec((DEC_SEQ, CONV_S_BB, C_CONV), lambda i: (0, i, 0))
    st = pl.BlockSpec((HIST, CONV_S_BB, C_CONV), lambda i: (0, i, 0))
    return pl.pallas_call(
        _conv_s_kernel,
        out_shape=(jax.ShapeDtypeStruct((DEC_SEQ, DEC_BATCH, C_CONV), BF16),
                   jax.ShapeDtypeStruct(hist_t.shape, F32)),
        grid=(DEC_BATCH // CONV_S_BB,),
        in_specs=[tb, st, pl.BlockSpec((CONV_WIDTH, C_CONV), lambda i: (0, 0)), vec, vec, vec],
        out_specs=(tb, st),
        compiler_params=_cparams(1),
        name="conv_sample",
    )(z, hist_t, dw_w, dw_b, ln_g, ln_b)


def _mix_s_kernel(at_ref, y_ref, ga_ref, gc_ref, h_ref, g2_ref, wao_ref, wco_ref, wout_ref, o_ref):
    a = _dot(at_ref[...].astype(BF16), wao_ref[...])
    c = _dot(y_ref[...], wco_ref[...])
    merged = (ga_ref[...].astype(F32) * a + gc_ref[...].astype(F32) * c).astype(BF16)
    r = _dot(merged, wout_ref[...])
    o_ref[...] = h_ref[...] + g2_ref[...] * r


def _mix_sample(attn, y, ga, gc, h, mod, w_ao, w_co, w_out):
    row = lambda w: pl.BlockSpec((DEC_BATCH, w), lambda t: (t, 0))
    resident = lambda shape: pl.BlockSpec(shape, lambda t: (0, 0), pipeline_mode=pl.Buffered(1))
    return pl.pallas_call(
        _mix_s_kernel,
        out_shape=jax.ShapeDtypeStruct((S_ROWS, D_MODEL), F32),
        grid=(DEC_SEQ,),
        in_specs=[row(ATTN_WIDTH), row(C_CONV), row(D_MODEL), row(D_MODEL), row(D_MODEL),
                  _mod_spec(0, True),
                  resident((ATTN_WIDTH, D_MODEL)), resident((C_CONV, D_MODEL)), resident((D_MODEL, D_MODEL))],
        out_specs=row(D_MODEL),
        compiler_params=_cparams(1),
        name="mix_sample",
    )(attn, y, ga, gc, h, mod, w_ao, w_co, w_out)


def kernel(x_prompt, x_sample, cache_k, cache_v, state_conv, c_prompt, c_sample, rel_bias_table, norm1_g, ffn1_w1, ffn1_w3, ffn1_w2, norm2_g, w_in, attn_sinks, conv_dw_w, conv_dw_b, conv_ln_g, conv_ln_b, w_conv_out, w_attn_out, w_out, norm3_g, ffn2_w1, ffn2_w3, ffn2_w2, w_ada, b_ada, final_norm_g, w_ada_final, b_ada_final):
    c_all = jnp.concatenate([c_sample, c_prompt, jnp.zeros((N_COND - DEC_BATCH - 1, D_MODEL), F32)], axis=0)
    early = 5 * D_MODEL
    mod = _adaln(c_all, w_ada[0], b_ada, early)
    late_ada = [(w_ada[0], b_ada, early, 4 * D_MODEL), (w_ada_final, b_ada_final[None, :], 0, 2 * D_MODEL)]

    sinks = attn_sinks[0]
    tab_flat = rel_bias_table.reshape(-1)
    tab_ext = jnp.concatenate([rel_bias_table, sinks[None, :]], axis=0)
    th = tab_ext.T.reshape(N_KV_HEADS, GROUP, N_BUCKETS + 1).transpose(1, 0, 2)
    th = jnp.tile(th.reshape(N_HEADS, N_BUCKETS + 1), (DEC_SEQ, 1))

    xp = x_prompt[0]
    xs = x_sample.transpose(1, 0, 2).reshape(S_ROWS, D_MODEL)

    conv_w = (conv_dw_w[0], conv_dw_b, conv_ln_g, conv_ln_b)


    def keys_on_lanes(cache):
        return cache[0].transpose(0, 2, 3, 1).reshape(DEC_BATCH, KV_WIDTH, WBUF)

    def keys_on_rows(win):
        return win.reshape(DEC_BATCH, N_KV_HEADS, HEAD_DIM, WBUF).transpose(0, 3, 1, 2)[None]

    def mixers_s(q, kv, z, ga, gc, h, mod_late, mix_w):
        o, kw, vw = _attn_sample(q.reshape(DEC_SEQ, DEC_BATCH, ATTN_WIDTH),
                                 kv.reshape(DEC_SEQ, DEC_BATCH, 2 * KV_WIDTH),
                                 keys_on_lanes(cache_k), keys_on_lanes(cache_v), th)
        y, ns = _conv_sample(z.reshape(DEC_SEQ, DEC_BATCH, C_CONV), state_conv[0].transpose(1, 0, 2), *conv_w)
        h = _mix_sample(o.reshape(S_ROWS, ATTN_WIDTH), y.reshape(S_ROWS, C_CONV), ga, gc, h, mod_late, *mix_w)
        return h, (kw, vw, ns)

    hs, f1 = _ffn(xs, norm1_g, mod, 0, ffn1_w1[0], ffn1_w3[0], ffn1_w2[0], per_row=True,
                  tm=S_ROWS, tf=FFN_TF_CAST, cast_out=True)
    w_in_rows = D_MODEL // (SEQ // FFN_TM)
    w_in_side = (w_in[0], (w_in_rows, W_IN_SIDE_COLS),
                 lambda i, f: (i, jnp.minimum(f, IN_WIDTH // W_IN_SIDE_COLS - 1)), False)
    hp, (w_in_b,) = _ffn(xp, norm1_g, mod, 0, *f1, per_row=False, tm=FFN_TM, tf=FFN_TF, side=(w_in_side,))

    qs, kv_s, z_s, ga_s, gc_s = _proj(hs, norm2_g, mod, w_in_b, per_row=True, tm=S_ROWS, q_dtype=F32)
    qp, kv_p, z_p, ga_p, gc_p = _proj(hp, norm2_g, mod, w_in_b, per_row=False, tm=PROJ_TM, q_dtype=BF16)

    side = [_row_slab_side(ffn2_w1[0], True), _row_slab_side(ffn2_w3[0], True)]
    side += [_row_slab_side(w[0]) for w in (ffn2_w2, w_attn_out, w_conv_out, w_out)]
    attn_p, side_b, (mod_late, mod_f) = _attn_prompt(qp, kv_p, tab_flat, sinks, side, c_all, late_ada)
    f2, mix_w = side_b[:3], side_b[3:]
    final = (final_norm_g[None, :], mod_f)

    hs, (kw, vw, ns) = mixers_s(qs, kv_s, z_s, ga_s, gc_s, hs, mod_late, mix_w)
    hp = _mix_conv_prompt(z_p, *conv_w, attn_p, ga_p, gc_p, hp, mod_late, *mix_w)

    ys = _ffn(hs, norm3_g, mod_late, 1, *f2, per_row=True, tm=S_ROWS, tf=FFN_TF, final=final)
    yp = _ffn(hp, norm3_g, mod_late, 1, *f2, per_row=False, tm=FFN_TM, tf=FFN_TF, final=final)

    w = min(WINDOW, SEQ)
    kv_shape = (1, 1, w, N_KV_HEADS, HEAD_DIM)
    k_win_p = kv_p[SEQ - w:, :KV_WIDTH].reshape(kv_shape)
    v_win_p = kv_p[SEQ - w:, KV_WIDTH:].reshape(kv_shape)
    conv_p_state = z_p[SEQ - HIST:].reshape(1, 1, HIST, C_CONV)
    s_shape = (1, DEC_BATCH, WBUF, N_KV_HEADS, HEAD_DIM)
    y_prompt = yp[None]
    y_sample = ys.reshape(DEC_SEQ, DEC_BATCH, D_MODEL).transpose(1, 0, 2)
    return (y_prompt, y_sample, k_win_p, v_win_p, conv_p_state,
            keys_on_rows(kw), keys_on_rows(vw), ns.transpose(1, 0, 2)[None])
```

```python
import functools

import numpy as np
import jax
import jax.numpy as jnp
from jax import lax
from jax.experimental import pallas as pl
from jax.experimental.pallas import tpu as pltpu

D_MODEL = 2048
SEQ = 8192
DEC_BATCH = 128
DEC_SEQ = 4
PAST_LEN = 16384
N_HEADS = 32
N_KV_HEADS = 8
HEAD_DIM = 64
GROUP = N_HEADS // N_KV_HEADS
ATTN_WIDTH = N_HEADS * HEAD_DIM
KV_WIDTH = N_KV_HEADS * HEAD_DIM
WINDOW = 128
Q_BLOCK = 128
ATTN_SCALE = HEAD_DIM ** -0.5
N_BUCKETS = 32
MAX_DISTANCE = 128
C_CONV = D_MODEL // 2
CONV_WIDTH = 31
HIST = CONV_WIDTH - 1
D_FF = 5632
NORM_EPS = 1e-6
NEG_INF = -1e30
IN_WIDTH = ATTN_WIDTH + 2 * KV_WIDTH + 2 * C_CONV + 2 * D_MODEL
WBUF = min(WINDOW, PAST_LEN)

VMEM_LIMIT = 60 * 1024 * 1024
SUBLANES = 8
LANES = 128
BF16_SUBLANES = 16

S_ROWS = DEC_BATCH * DEC_SEQ
N_COND = DEC_BATCH + SUBLANES
PROMPT_MOD_BLOCK = DEC_BATCH // SUBLANES
S_EXTRA = SUBLANES

F32 = jnp.float32
BF16 = jnp.bfloat16


def _cparams(n_axes):
    return pltpu.CompilerParams(dimension_semantics=("arbitrary",) * n_axes,
                                vmem_limit_bytes=VMEM_LIMIT)


def _dot(a, b):
    return jnp.dot(a, b, preferred_element_type=F32)


def _rms_mod(x, g, sh, sc):
    ms = jnp.mean(x * x, axis=-1, keepdims=True)
    y = x * lax.rsqrt(ms + NORM_EPS) * g
    return y * (1.0 + sc) + sh


def _mod_val(ref, per_row):
    return ref[...] if per_row else ref[0:1, :]


def _mod_spec(chunk, per_row):
    if per_row:
        return pl.BlockSpec((DEC_BATCH, D_MODEL), lambda *_: (0, chunk))
    return pl.BlockSpec((SUBLANES, D_MODEL), lambda *_: (PROMPT_MOD_BLOCK, chunk))


def _adaln_kernel(c_ref, w_ref, b_ref, o_ref, s_ref):
    @pl.when(pl.program_id(0) == 0)
    def _():
        s_ref[...] = jax.nn.silu(c_ref[...]).astype(BF16)

    o_ref[...] = _dot(s_ref[...], w_ref[...].astype(BF16)) + b_ref[...]


ADALN_TN = 1024


def _adaln(c_all, w, b, n):
    tn = ADALN_TN
    return pl.pallas_call(
        _adaln_kernel,
        out_shape=jax.ShapeDtypeStruct((N_COND, n), F32),
        grid=(n // tn,),
        in_specs=[pl.BlockSpec((N_COND, D_MODEL), lambda j: (0, 0)),
                  pl.BlockSpec((D_MODEL, tn), lambda j: (0, j)),
                  pl.BlockSpec((1, tn), lambda j: (0, j))],
        out_specs=pl.BlockSpec((N_COND, tn), lambda j: (0, j)),
        scratch_shapes=[pltpu.VMEM((N_COND, D_MODEL), BF16)],
        compiler_params=_cparams(1),
        name="adaln",
    )(c_all, w, b)


FFN_ROWS = 512
FFN_TM = 1024
FFN_TF = 512
FFN_TF_CAST = 256
PROJ_TM = 1024
W_IN_SIDE_COLS = 1024


def _side_cast(side_in, side_out):
    for i_ref, o_ref in zip(side_in, side_out):
        o_ref[...] = i_ref[...].astype(BF16)


def _side_specs(side):
    args = [a for a, _, _ in side]
    specs = [pl.BlockSpec(block, index_map) for _, block, index_map in side]
    shapes = [jax.ShapeDtypeStruct(a.shape, BF16) for a in args]
    return args, specs, shapes


def _ffn_kernel(*refs, tm, sub, per_row, n_f, final, cast_out, n_side):
    refs = list(refs)
    x_ref, n_ref, sh_ref, sc_ref, g_ref, w1_ref, w3_ref, w2_ref = refs[:8]
    del refs[:8]
    if final:
        gf_ref, shf_ref, scf_ref = refs[:3]
        del refs[:3]
    side_in = refs[:n_side]
    del refs[:n_side]
    o_ref = refs.pop(0)
    if cast_out:
        w1o_ref, w3o_ref, w2o_ref = refs[:3]
        del refs[:3]
    side_out = refs[:n_side]
    del refs[:n_side]
    (u_ref,) = refs
    f = pl.program_id(1)
    _side_cast(side_in, side_out)

    if cast_out:
        w1o_ref[...] = w1_ref[...].astype(BF16)
        w3o_ref[...] = w3_ref[...].astype(BF16)
        w2o_ref[...] = w2_ref[...].astype(BF16)
        w1_ref, w3_ref, w2_ref = w1o_ref, w3o_ref, w2o_ref

    def accumulate(first, last):
        for s in range(tm // FFN_ROWS):
            rows = slice(s * FFN_ROWS, (s + 1) * FFN_ROWS)
            subs = [slice(r0, r0 + sub) for r0 in range(s * FFN_ROWS, (s + 1) * FFN_ROWS, sub)]
            if first:
                for r in subs:
                    u_ref[r, :] = _rms_mod(x_ref[r, :], n_ref[...], _mod_val(sh_ref, per_row),
                                           _mod_val(sc_ref, per_row)).astype(BF16)
            u = u_ref[rows, :]
            h1 = _dot(u, w1_ref[...])
            h3 = _dot(u, w3_ref[...])
            a = (jax.nn.silu(h1) * h3).astype(BF16)
            d = _dot(a, w2_ref[...])
            if first:
                o_ref[rows, :] = d
            else:
                o_ref[rows, :] += d
            if last:
                for r in subs:
                    h = x_ref[r, :] + 0.5 * _mod_val(g_ref, per_row) * o_ref[r, :]
                    if final:
                        h = _rms_mod(h, gf_ref[...], _mod_val(shf_ref, per_row), _mod_val(scf_ref, per_row))
                    o_ref[r, :] = h

    assert n_f > 2

    @pl.when(f == 0)
    def _():
        accumulate(True, False)

    @pl.when((f > 0) & (f < n_f - 1))
    def _():
        accumulate(False, False)

    @pl.when(f == n_f - 1)
    def _():
        accumulate(False, True)


def _ffn(x, norm_g, mod, chunk0, w1, w3, w2, *, per_row, tm, tf, final=None, cast_out=False, side=()):
    rows = x.shape[0]
    n_f = D_FF // tf
    sub = DEC_BATCH
    row_spec = pl.BlockSpec((tm, D_MODEL), lambda i, f: (i, 0))
    vec_spec = pl.BlockSpec((1, D_MODEL), lambda i, f: (0, 0))
    w_specs = [pl.BlockSpec((D_MODEL, tf), lambda i, f: (0, f)),
               pl.BlockSpec((D_MODEL, tf), lambda i, f: (0, f)),
               pl.BlockSpec((tf, D_MODEL), lambda i, f: (f, 0))]
    in_specs = [row_spec, vec_spec,
                _mod_spec(chunk0, per_row), _mod_spec(chunk0 + 1, per_row), _mod_spec(chunk0 + 2, per_row)]
    in_specs += w_specs
    args = [x, norm_g, mod, mod, mod, w1, w3, w2]
    if final is not None:
        gf, mod_f = final
        in_specs += [vec_spec, _mod_spec(0, per_row), _mod_spec(1, per_row)]
        args += [gf, mod_f, mod_f]
    side_args, side_specs, side_shapes = _side_specs(side)
    in_specs += side_specs
    args += side_args
    out_shape = [jax.ShapeDtypeStruct((rows, D_MODEL), F32)]
    out_specs = [row_spec]
    if cast_out:
        assert rows == tm, "every weight block must be visited exactly once"
        out_shape += [jax.ShapeDtypeStruct(w.shape, BF16) for w in (w1, w3, w2)]
        out_specs += w_specs
    out_shape += side_shapes
    out_specs += side_specs
    outs = pl.pallas_call(
        functools.partial(_ffn_kernel, tm=tm, sub=sub, per_row=per_row, n_f=n_f,
                          final=final is not None, cast_out=cast_out, n_side=len(side)),
        out_shape=out_shape,
        grid=(rows // tm, n_f),
        in_specs=in_specs,
        out_specs=out_specs,
        scratch_shapes=[pltpu.VMEM((tm, D_MODEL), BF16)],
        compiler_params=_cparams(2),
        name="ffn_final" if final is not None else "ffn",
    )(*args)
    return (outs[0], tuple(outs[1:])) if (cast_out or side) else outs[0]


PROJ_CHUNK = 256
PROJ_NORM_ROWS = 512
PROJ_CW = 512


def _proj_layout():
    cw = PROJ_CW
    n_q = ATTN_WIDTH // (2 * cw)
    n_kv = 2 * KV_WIDTH // (2 * cw)
    n_glu = C_CONV // cw
    n_gate = D_MODEL // (2 * cw)
    pre = n_q + n_kv
    steps = pre + n_glu + 2 * n_gate

    def block_a(j):
        return jnp.where((j >= pre) & (j < pre + n_glu), j + pre, 2 * j)

    def block_b(j):
        return jnp.where((j >= pre) & (j < pre + n_glu), j + pre + n_glu, 2 * j + 1)

    return n_q, n_kv, n_glu, n_gate, steps, block_a, block_b


def _proj_kernel(h_ref, n_ref, sh_ref, sc_ref, wa_ref, wb_ref, q_ref, kv_ref, z_ref, ga_ref, gc_ref, u_ref,
                 *, tm, sub, per_row):
    cw = PROJ_CW
    n_q, n_kv, n_glu, n_gate, _, _, _ = _proj_layout()
    j = pl.program_id(1)

    n_chunks = cw // PROJ_CHUNK

    def chunks(w_ref, rows=slice(None)):
        for c in range(n_chunks):
            cols = slice(c * PROJ_CHUNK, (c + 1) * PROJ_CHUNK)
            yield c * PROJ_CHUNK, _dot(u_ref[rows, :], w_ref[:, cols])

    def pair_store(o_ref, fn, rows=slice(None)):
        for half, w_ref in enumerate((wa_ref, wb_ref)):
            for c0, acc in chunks(w_ref, rows):
                o_ref[rows, half * cw + c0:half * cw + c0 + PROJ_CHUNK] = fn(acc).astype(o_ref.dtype)

    @pl.when(j == 0)
    def _():
        for g0 in range(0, tm, PROJ_NORM_ROWS):
            for r0 in range(g0, g0 + PROJ_NORM_ROWS, sub):
                r = slice(r0, r0 + sub)
                u_ref[r, :] = _rms_mod(h_ref[r, :], n_ref[...], _mod_val(sh_ref, per_row),
                                       _mod_val(sc_ref, per_row)).astype(BF16)
            pair_store(q_ref, lambda acc: acc, slice(g0, g0 + PROJ_NORM_ROWS))

    lo = 0

    @pl.when((j > 0) & (j < n_q))
    def _():
        pair_store(q_ref, lambda acc: acc)

    lo += n_q

    @pl.when((j >= lo) & (j < lo + n_kv))
    def _():
        pair_store(kv_ref, lambda acc: acc)

    lo += n_kv

    @pl.when((j >= lo) & (j < lo + n_glu))
    def _():
        for (c0, a), (_, g) in zip(chunks(wa_ref), chunks(wb_ref)):
            z_ref[:, c0:c0 + PROJ_CHUNK] = a * jax.nn.sigmoid(g)

    lo += n_glu

    @pl.when((j >= lo) & (j < lo + n_gate))
    def _():
        pair_store(ga_ref, jax.nn.sigmoid)

    lo += n_gate

    @pl.when(j >= lo)
    def _():
        pair_store(gc_ref, jax.nn.sigmoid)


def _proj(h, norm_g, mod, w, *, per_row, tm, q_dtype):
    rows = h.shape[0]
    sub = DEC_BATCH
    cw = PROJ_CW
    n_q, n_kv, n_glu, n_gate, steps, block_a, block_b = _proj_layout()

    def out_spec(width, first, count):
        return pl.BlockSpec((tm, width), lambda i, j: (i, jnp.clip(j - first, 0, count - 1)))

    wa_spec = pl.BlockSpec((D_MODEL, cw), lambda i, j: (0, block_a(j)))
    wb_spec = pl.BlockSpec((D_MODEL, cw), lambda i, j: (0, block_b(j)))
    out_shape = [jax.ShapeDtypeStruct((rows, ATTN_WIDTH), q_dtype),
                 jax.ShapeDtypeStruct((rows, 2 * KV_WIDTH), F32),
                 jax.ShapeDtypeStruct((rows, C_CONV), F32),
                 jax.ShapeDtypeStruct((rows, D_MODEL), BF16),
                 jax.ShapeDtypeStruct((rows, D_MODEL), BF16)]
    out_specs = [out_spec(2 * cw, 0, n_q),
                 out_spec(2 * cw, n_q, n_kv),
                 out_spec(cw, n_q + n_kv, n_glu),
                 out_spec(2 * cw, n_q + n_kv + n_glu, n_gate),
                 out_spec(2 * cw, n_q + n_kv + n_glu + n_gate, n_gate)]
    return pl.pallas_call(
        functools.partial(_proj_kernel, tm=tm, sub=sub, per_row=per_row),
        out_shape=out_shape,
        grid=(rows // tm, steps),
        in_specs=[pl.BlockSpec((tm, D_MODEL), lambda i, j: (i, 0)),
                  pl.BlockSpec((1, D_MODEL), lambda i, j: (0, 0)),
                  _mod_spec(3, per_row), _mod_spec(4, per_row), wa_spec, wb_spec],
        out_specs=out_specs,
        scratch_shapes=[pltpu.VMEM((tm, D_MODEL), BF16)],
        compiler_params=_cparams(2),
        name="proj",
    )(h, norm_g, mod, mod, w, w)


def _t5_bucket_np(dist):
    exact = N_BUCKETS // 2
    d = np.maximum(dist, 0)
    df = np.maximum(d, 1).astype(np.float32)
    large = exact + (np.log(df / np.float32(exact)) / np.float32(np.log(MAX_DISTANCE / exact))
                     * np.float32(N_BUCKETS - exact)).astype(np.int32)
    large = np.minimum(large, N_BUCKETS - 1)
    return np.where(d < exact, d, large).astype(np.int32)


def _prompt_codes():
    qi = np.arange(Q_BLOCK)[:, None]
    kj = np.arange(2 * Q_BLOCK)[None, :]
    dist = qi + Q_BLOCK - kj
    valid = (dist >= 0) & (dist <= WINDOW)
    return np.where(valid, _t5_bucket_np(dist), -1).astype(np.int32)


def _sample_codes():
    t = np.repeat(np.arange(DEC_SEQ), GROUP * N_KV_HEADS)[:, None]
    j = np.arange(WBUF)[None, :]
    dist = t + WBUF - j
    cache = np.where((dist >= 0) & (dist <= WINDOW), _t5_bucket_np(dist), -1)
    c = np.arange(S_EXTRA)[None, :]
    t_new = c - (S_EXTRA - DEC_SEQ)
    dist = t - t_new
    extra = np.where((t_new >= 0) & (dist >= 0) & (dist <= WINDOW), _t5_bucket_np(dist), -1)
    extra = np.where(c == 0, N_BUCKETS, extra)
    return cache.astype(np.int32), extra.astype(np.int32)


BIAS_ROWS = 64
ATTN_QB = 2


def _attn_p_kernel(*refs, n_side, n_ada):
    refs = list(refs)
    tab_ref, sink_ref, code_ref, q_ref, kvc_ref, kvp_ref = refs[:6]
    del refs[:6]
    side_in = refs[:n_side]
    del refs[:n_side]
    c_ref = refs.pop(0)
    ada_in = [(refs[2 * k], refs[2 * k + 1]) for k in range(n_ada)]
    del refs[:2 * n_ada]
    o_ref = refs.pop(0)
    side_out = refs[:n_side]
    del refs[:n_side]
    ada_out = refs[:n_ada]
    del refs[:n_ada]
    bias_ref, silu_c_ref = refs
    n = pl.program_id(0)
    _side_cast(side_in, side_out)

    @pl.when(n == 0)
    def _():
        silu_c_ref[...] = jax.nn.silu(c_ref[...]).astype(BF16)

    for (w_ref, b_ref), m_ref in zip(ada_in, ada_out):
        m_ref[...] = _dot(silu_c_ref[...], w_ref[...].astype(BF16)) + b_ref[...]

    n_keys = 2 * Q_BLOCK
    wide = GROUP * Q_BLOCK

    @pl.when(n == 0)
    def _():
        def kv_body(kv, carry):
            for g in range(GROUP):
                for r in range(n_keys // BIAS_ROWS):
                    rows = slice(r * BIAS_ROWS, (r + 1) * BIAS_ROWS)
                    code = code_ref[rows, :]
                    acc = jnp.full((BIAS_ROWS, Q_BLOCK), NEG_INF, F32)
                    for bk in range(N_BUCKETS):
                        acc = jnp.where(code == bk, tab_ref[bk * N_HEADS + kv * GROUP + g], acc)
                    bias_ref[kv, rows, g * Q_BLOCK:(g + 1) * Q_BLOCK] = acc
            return carry
        lax.fori_loop(0, N_KV_HEADS, kv_body, 0)

    ones = jnp.ones((BF16_SUBLANES, n_keys), BF16)
    zeros = jnp.zeros((HEAD_DIM, wide), BF16)
    tile = (n_keys, LANES)
    key = lax.broadcasted_iota(jnp.int32, tile, 0)
    lane = lax.broadcasted_iota(jnp.int32, tile, 1)
    pen = jnp.where((lane == 0) & (key < Q_BLOCK) & (n == 0), NEG_INF, 0.0).astype(BF16)
    pen_rows = (lax.broadcasted_iota(jnp.int32, (LANES, wide), 0) == 0).astype(BF16)

    def query_block(a):
        rows = slice(a * Q_BLOCK, (a + 1) * Q_BLOCK)
        q_t = q_ref[rows, :].astype(F32).T

        def keys(cols):
            prev = kvp_ref[:, cols] if a == 0 else kvc_ref[(a - 1) * Q_BLOCK:a * Q_BLOCK, cols]
            return jnp.concatenate([prev, kvc_ref[rows, cols]], axis=0)

        def pair_kv(pair):
            k2 = keys(slice(pair * LANES, (pair + 1) * LANES)).astype(BF16)
            if a == 0:
                k2 = jnp.concatenate([k2, pen], axis=1)
            v2_t = keys(slice(KV_WIDTH + pair * LANES, KV_WIDTH + (pair + 1) * LANES)).T.astype(BF16)
            return k2, v2_t

        def scores(kv, k2):
            q4 = jnp.concatenate(
                [q_t[(kv * GROUP + g) * HEAD_DIM:(kv * GROUP + g + 1) * HEAD_DIM, :] for g in range(GROUP)],
                axis=1)
            q4 = (q4 * ATTN_SCALE).astype(BF16)
            parts = [q4, zeros] if kv % 2 == 0 else [zeros, q4]
            return _dot(k2, jnp.concatenate(parts + ([pen_rows] if a == 0 else []), axis=0))

        kvs = [pair_kv(pair) for pair in range(N_KV_HEADS // 2)]
        for kv in range(N_KV_HEADS):
            pair, parity = divmod(kv, 2)
            s = scores(kv, kvs[pair][0]) + bias_ref[kv]
            sink = jnp.concatenate(
                [jnp.full((1, Q_BLOCK), sink_ref[kv * GROUP + g], F32) for g in range(GROUP)], axis=1)
            m = jnp.maximum(jnp.max(s, axis=0, keepdims=True), sink)
            p = jnp.exp(s - m).astype(BF16)
            lhs = jnp.concatenate([kvs[pair][1][parity * HEAD_DIM:(parity + 1) * HEAD_DIM, :], ones], axis=0)
            o_t = _dot(lhs, p)
            l = o_t[HEAD_DIM:HEAD_DIM + 1, :] + jnp.exp(sink - m)
            o_n = o_t[0:HEAD_DIM, :] * (1.0 / l)
            o_kv = jnp.concatenate([o_n[:, g * Q_BLOCK:(g + 1) * Q_BLOCK] for g in range(GROUP)], axis=0)
            o_ref[rows, kv * GROUP * HEAD_DIM:(kv + 1) * GROUP * HEAD_DIM] = o_kv.T.astype(o_ref.dtype)

    for a in range(ATTN_QB):
        query_block(a)


ATTN_STEPS = SEQ // (ATTN_QB * Q_BLOCK)


def _row_slab_side(w):
    slab = w.shape[0] // ATTN_STEPS
    assert slab * ATTN_STEPS == w.shape[0] and slab % BF16_SUBLANES == 0
    return (w, (slab, w.shape[1]), lambda n: (n, 0))


def _attn_prompt(q, kv, tab_flat, sinks, side, c_all, ada):
    step_rows = ATTN_QB * Q_BLOCK
    code_t = jnp.asarray(np.ascontiguousarray(_prompt_codes().T))
    smem = pl.BlockSpec(memory_space=pltpu.SMEM)
    side_args, side_specs, side_shapes = _side_specs(side)
    ada_args, ada_in_specs, ada_out_specs, ada_shapes = [], [], [], []
    for w, b, col0, n_cols in ada:
        tn = n_cols // ATTN_STEPS
        assert tn * ATTN_STEPS == n_cols and tn % LANES == 0 and col0 % tn == 0
        first = col0 // tn
        ada_args += [w, b]
        ada_in_specs += [pl.BlockSpec((D_MODEL, tn), lambda n, first=first: (0, first + n)),
                         pl.BlockSpec((1, tn), lambda n, first=first: (0, first + n))]
        ada_out_specs.append(pl.BlockSpec((N_COND, tn), lambda n: (0, n)))
        ada_shapes.append(jax.ShapeDtypeStruct((N_COND, n_cols), F32))
    out_spec = pl.BlockSpec((step_rows, ATTN_WIDTH), lambda n: (n, 0))
    outs = pl.pallas_call(
        functools.partial(_attn_p_kernel, n_side=len(side), n_ada=len(ada)),
        out_shape=[jax.ShapeDtypeStruct((SEQ, ATTN_WIDTH), BF16)] + side_shapes + ada_shapes,
        grid=(ATTN_STEPS,),
        in_specs=[smem, smem,
                  pl.BlockSpec((2 * Q_BLOCK, Q_BLOCK), lambda n: (0, 0)),
                  pl.BlockSpec((step_rows, ATTN_WIDTH), lambda n: (n, 0)),
                  pl.BlockSpec((step_rows, 2 * KV_WIDTH), lambda n: (n, 0)),
                  pl.BlockSpec((Q_BLOCK, 2 * KV_WIDTH), lambda n: (jnp.maximum(ATTN_QB * n - 1, 0), 0))]
        + side_specs + [pl.BlockSpec((N_COND, D_MODEL), lambda n: (0, 0))] + ada_in_specs,
        out_specs=[out_spec] + side_specs + ada_out_specs,
        scratch_shapes=[pltpu.VMEM((N_KV_HEADS, 2 * Q_BLOCK, GROUP * Q_BLOCK), F32),
                        pltpu.VMEM((N_COND, D_MODEL), BF16)],
        compiler_params=_cparams(1),
        name="attn_prompt",
    )(tab_flat, sinks, code_t, q, kv, kv, *side_args, c_all, *ada_args)
    n_side = len(side)
    return outs[0], tuple(outs[1:1 + n_side]), tuple(outs[1 + n_side:])


S_BB = 8


def _regroup_heads(x, to_group_major):
    half = HEAD_DIM
    assert LANES == 2 * half
    low = lax.broadcasted_iota(jnp.int32, (x.shape[0], LANES), 1) < half
    n_cols = ATTN_WIDTH // LANES
    outer, inner = (GROUP, N_KV_HEADS) if to_group_major else (N_KV_HEADS, GROUP)
    cols = []
    for c in range(n_cols):
        o, i = divmod(2 * c, inner)
        src = [(i + d) * outer + o for d in range(2)]
        a = x[:, (src[0] // 2) * LANES:(src[0] // 2 + 1) * LANES]
        b = x[:, (src[1] // 2) * LANES:(src[1] // 2 + 1) * LANES]
        assert src[0] % 2 == src[1] % 2
        if src[0] % 2 == 0:
            cols.append(jnp.where(low, a, pltpu.roll(b, half, axis=1)))
        else:
            cols.append(jnp.where(low, pltpu.roll(a, half, axis=1), b))
    return jnp.concatenate(cols, axis=1)


def _attn_s_kernel(th_ref, code_c_ref, code_x_ref, q_ref, kv_ref, ck_ref, cv_ref, o_ref, kw_ref, vw_ref,
                   bias_c_ref, bias_x_ref, nk_ref, nv_ref, qe_ref, qg_ref, og_ref):
    rows_all = DEC_SEQ * N_HEADS
    for t in range(DEC_SEQ):
        qg_ref[t] = _regroup_heads(q_ref[t], True)

    @pl.when(pl.program_id(0) == 0)
    def _():
        for code_ref, bias_ref in ((code_c_ref, bias_c_ref), (code_x_ref, bias_x_ref)):
            code = code_ref[...]
            acc = jnp.full(code.shape, NEG_INF, F32)
            for bk in range(N_BUCKETS + 1):
                acc = jnp.where(code == bk, th_ref[:, bk:bk + 1], acc)
            bias_ref[...] = acc
        nk_ref[...] = jnp.zeros_like(nk_ref)
        nv_ref[...] = jnp.zeros_like(nv_ref)

    lane_head = lax.broadcasted_iota(jnp.int32, (N_KV_HEADS, KV_WIDTH), 1) // HEAD_DIM
    row_head = lax.broadcasted_iota(jnp.int32, (N_KV_HEADS, KV_WIDTH), 0)
    diag = lane_head == row_head
    out_mask = (lax.broadcasted_iota(jnp.int32, (rows_all, KV_WIDTH), 0) % N_KV_HEADS
                == lax.broadcasted_iota(jnp.int32, (rows_all, KV_WIDTH), 1) // HEAD_DIM)
    new_lanes = lax.broadcasted_iota(jnp.int32, (KV_WIDTH, WBUF), 1) >= WBUF - DEC_SEQ
    bias_c = bias_c_ref[...]
    bias_x = bias_x_ref[...]
    nt_dims = (((1,), (1,)), ((), ()))
    x0 = WBUF - S_EXTRA

    for b in range(S_BB):
        for t in range(DEC_SEQ):
            nk_ref[WBUF - DEC_SEQ + t:WBUF - DEC_SEQ + t + 1, :] = kv_ref[t, b:b + 1, 0:KV_WIDTH]
            nv_ref[WBUF - DEC_SEQ + t:WBUF - DEC_SEQ + t + 1, :] = kv_ref[t, b:b + 1, KV_WIDTH:2 * KV_WIDTH]
            for g in range(GROUP):
                q_row = qg_ref[t, b:b + 1, g * KV_WIDTH:(g + 1) * KV_WIDTH]
                piece = jnp.where(diag, jnp.broadcast_to(q_row, (N_KV_HEADS, KV_WIDTH)), 0.0)
                r0 = (t * GROUP + g) * N_KV_HEADS
                qe_ref[r0:r0 + N_KV_HEADS, :] = piece
        k_t = ck_ref[b]
        v_t = cv_ref[b]
        kw_ref[b] = jnp.where(new_lanes, nk_ref[...].T, pltpu.roll(k_t, WBUF - DEC_SEQ, axis=1))
        vw_ref[b] = jnp.where(new_lanes, nv_ref[...].T, pltpu.roll(v_t, WBUF - DEC_SEQ, axis=1))

        qe = (qe_ref[...] * ATTN_SCALE).astype(BF16)
        s_c = _dot(qe, k_t.astype(BF16)) + bias_c
        s_x = lax.dot_general(qe, nk_ref[x0:WBUF, :].astype(BF16), nt_dims,
                              preferred_element_type=F32) + bias_x
        m = jnp.maximum(jnp.max(s_c, axis=-1, keepdims=True), jnp.max(s_x, axis=-1, keepdims=True))
        p_c = jnp.exp(s_c - m)
        p_x = jnp.exp(s_x - m)
        l = jnp.sum(p_c, axis=-1, keepdims=True) + jnp.sum(p_x, axis=-1, keepdims=True)
        o = lax.dot_general(p_c.astype(BF16), v_t.astype(BF16), nt_dims, preferred_element_type=F32)
        o = (o + _dot(p_x.astype(BF16), nv_ref[x0:WBUF, :].astype(BF16))) / l
        o = jnp.where(out_mask, o, 0.0)
        for t in range(DEC_SEQ):
            for g in range(GROUP):
                r0 = (t * GROUP + g) * N_KV_HEADS
                row = jnp.sum(o[r0:r0 + N_KV_HEADS, :], axis=0, keepdims=True)
                og_ref[t, b:b + 1, g * KV_WIDTH:(g + 1) * KV_WIDTH] = row
    for t in range(DEC_SEQ):
        o_ref[t] = _regroup_heads(og_ref[t], False)


def _attn_sample(q, kv, cache_k_t, cache_v_t, th):
    code_c, code_x = (jnp.asarray(c) for c in _sample_codes())
    rows_all = DEC_SEQ * N_HEADS
    full = lambda a: pl.BlockSpec(a.shape, lambda i: (0,) * a.ndim)
    tb_spec = lambda w: pl.BlockSpec((DEC_SEQ, S_BB, w), lambda i: (0, i, 0))
    cache_spec = pl.BlockSpec((S_BB, KV_WIDTH, WBUF), lambda i: (i, 0, 0))
    cache_shape = jax.ShapeDtypeStruct((DEC_BATCH, KV_WIDTH, WBUF), F32)
    return pl.pallas_call(
        _attn_s_kernel,
        out_shape=(jax.ShapeDtypeStruct((DEC_SEQ, DEC_BATCH, ATTN_WIDTH), F32), cache_shape, cache_shape),
        grid=(DEC_BATCH // S_BB,),
        in_specs=[full(th), full(code_c), full(code_x),
                  tb_spec(ATTN_WIDTH), tb_spec(2 * KV_WIDTH), cache_spec, cache_spec],
        out_specs=(tb_spec(ATTN_WIDTH), cache_spec, cache_spec),
        scratch_shapes=[pltpu.VMEM((rows_all, WBUF), F32),
                        pltpu.VMEM((rows_all, S_EXTRA), F32),
                        pltpu.VMEM((WBUF, KV_WIDTH), F32),
                        pltpu.VMEM((WBUF, KV_WIDTH), F32),
                        pltpu.VMEM((rows_all, KV_WIDTH), F32),
                        pltpu.VMEM((DEC_SEQ, S_BB, ATTN_WIDTH), F32),
                        pltpu.VMEM((DEC_SEQ, S_BB, ATTN_WIDTH), F32)],
        compiler_params=_cparams(1),
        name="attn_sample",
    )(th, code_c, code_x, q, kv, cache_k_t, cache_v_t)


CONV_RC = 32
CONV_ROWS = 128
CONV_HALO = 32


def _ln_silu(acc, lg, lb):
    mu = jnp.mean(acc, axis=-1, keepdims=True)
    xc = acc - mu
    var = jnp.mean(xc * xc, axis=-1, keepdims=True)
    return jax.nn.silu(xc * lax.rsqrt(var + NORM_EPS) * lg + lb)


def _conv_stage(first, zc_ref, zh_ref, s_ref):
    s_ref[0:CONV_HALO, :] = jnp.where(first, 0.0, zh_ref[...])
    s_ref[CONV_HALO:, :] = zc_ref[...]


def _conv_rows(rc, w_ref, b_ref, lg_ref, lb_ref, y_ref, s_ref, c_ref):
    off = CONV_HALO - HIST
    groups = [[j for j in range(CONV_WIDTH) if (j + off) % SUBLANES == r] for r in range(SUBLANES)]
    t0 = rc * CONV_ROWS
    for lc in range(C_CONV // LANES):
        lanes = slice(lc * LANES, (lc + 1) * LANES)
        out = jnp.broadcast_to(b_ref[:, lanes], (CONV_ROWS, LANES))
        for r, taps in enumerate(groups):
            n_rows = CONV_ROWS + (SUBLANES if r else 0)
            part = None
            for j in taps:
                base = t0 + (j + off) - r
                term = w_ref[j:j + 1, lanes] * s_ref[base:base + n_rows, lanes]
                part = term if part is None else part + term
            out = out + part[r:r + CONV_ROWS, :]
        c_ref[t0:t0 + CONV_ROWS, lanes] = out
    for r in range(CONV_ROWS // CONV_RC):
        rows = slice(t0 + r * CONV_RC, t0 + (r + 1) * CONV_RC)
        y_ref[rows, :] = _ln_silu(c_ref[rows, :], lg_ref[...], lb_ref[...]).astype(BF16)


def _mix_conv_kernel(zc_ref, zh_ref, w_ref, b_ref, lg_ref, lb_ref,
                     at_ref, ga_ref, gc_ref, h_ref, g2_ref, wao_ref, wco_ref, wout_ref,
                     o_ref, s_ref, c_ref, y_ref, *, tm):
    _conv_stage(pl.program_id(0) == 0, zc_ref, zh_ref, s_ref)
    for rc in range(tm // CONV_ROWS):
        _conv_rows(rc, w_ref, b_ref, lg_ref, lb_ref, y_ref, s_ref, c_ref)
    a = _dot(at_ref[...], wao_ref[...])
    c = _dot(y_ref[...], wco_ref[...])
    merged = (ga_ref[...].astype(F32) * a + gc_ref[...].astype(F32) * c).astype(BF16)
    r = _dot(merged, wout_ref[...])
    o_ref[...] = h_ref[...] + g2_ref[0:1, :] * r


MIX_TM = 256


def _mix_conv_prompt(z, dw_w, dw_b, ln_g, ln_b, attn, ga, gc, h, mod, w_ao, w_co, w_out):
    rows = h.shape[0]
    tm = MIX_TM
    n = rows // tm
    ratio = tm // CONV_HALO
    vec = pl.BlockSpec((1, C_CONV), lambda s: (0, 0))
    row = lambda w: pl.BlockSpec((tm, w), lambda s: (s, 0))
    resident = lambda shape: pl.BlockSpec(shape, lambda s: (0, 0), pipeline_mode=pl.Buffered(1))
    return pl.pallas_call(
        functools.partial(_mix_conv_kernel, tm=tm),
        out_shape=jax.ShapeDtypeStruct((rows, D_MODEL), F32),
        grid=(n,),
        in_specs=[row(C_CONV),
                  pl.BlockSpec((CONV_HALO, C_CONV), lambda s: (jnp.maximum(s * ratio - 1, 0), 0)),
                  pl.BlockSpec((CONV_WIDTH, C_CONV), lambda s: (0, 0)), vec, vec, vec,
                  row(ATTN_WIDTH), row(D_MODEL), row(D_MODEL), row(D_MODEL),
                  _mod_spec(0, False),
                  resident((ATTN_WIDTH, D_MODEL)), resident((C_CONV, D_MODEL)), resident((D_MODEL, D_MODEL))],
        out_specs=row(D_MODEL),
        scratch_shapes=[pltpu.VMEM((tm + CONV_HALO, C_CONV), F32), pltpu.VMEM((tm, C_CONV), F32),
                        pltpu.VMEM((tm, C_CONV), BF16)],
        compiler_params=_cparams(1),
        name="mix_conv",
    )(z, z, dw_w, dw_b, ln_g, ln_b, attn, ga, gc, h, mod, w_ao, w_co, w_out)


CONV_S_BB = 32


def _conv_s_kernel(z_ref, hist_ref, w_ref, b_ref, lg_ref, lb_ref, y_ref, ns_ref):
    for t in range(DEC_SEQ):
        acc = jnp.broadcast_to(b_ref[...], (CONV_S_BB, C_CONV))
        for j in range(CONV_WIDTH):
            i = t + j
            src = hist_ref[i] if i < HIST else z_ref[i - HIST]
            acc = acc + w_ref[j:j + 1, :] * src
        y_ref[t] = _ln_silu(acc, lg_ref[...], lb_ref[...]).astype(BF16)
    keep = HIST - DEC_SEQ
    ns_ref[0:keep] = hist_ref[DEC_SEQ:HIST]
    ns_ref[keep:HIST] = z_ref[...]


def _conv_sample(z, hist_t, dw_w, dw_b, ln_g, ln_b):
    vec = pl.BlockSpec((1, C_CONV), lambda i: (0, 0))
    tb = pl.BlockSpec((DEC_SEQ, CONV_S_BB, C_CONV), lambda i: (0, i, 0))
    st = pl.BlockSpec((HIST, CONV_S_BB, C_CONV), lambda i: (0, i, 0))
    return pl.pallas_call(
        _conv_s_kernel,
        out_shape=(jax.ShapeDtypeStruct((DEC_SEQ, DEC_BATCH, C_CONV), BF16),
                   jax.ShapeDtypeStruct(hist_t.shape, F32)),
        grid=(DEC_BATCH // CONV_S_BB,),
        in_specs=[tb, st, pl.BlockSpec((CONV_WIDTH, C_CONV), lambda i: (0, 0)), vec, vec, vec],
        out_specs=(tb, st),
        compiler_params=_cparams(1),
        name="conv_sample",
    )(z, hist_t, dw_w, dw_b, ln_g, ln_b)


def _mix_s_kernel(at_ref, y_ref, ga_ref, gc_ref, h_ref, g2_ref, wao_ref, wco_ref, wout_ref, o_ref):
    a = _dot(at_ref[...].astype(BF16), wao_ref[...])
    c = _dot(y_ref[...], wco_ref[...])
    merged = (ga_ref[...].astype(F32) * a + gc_ref[...].astype(F32) * c).astype(BF16)
    r = _dot(merged, wout_ref[...])
    o_ref[...] = h_ref[...] + g2_ref[...] * r


def _mix_sample(attn, y, ga, gc, h, mod, w_ao, w_co, w_out):
    row = lambda w: pl.BlockSpec((DEC_BATCH, w), lambda t: (t, 0))
    resident = lambda shape: pl.BlockSpec(shape, lambda t: (0, 0), pipeline_mode=pl.Buffered(1))
    return pl.pallas_call(
        _mix_s_kernel,
        out_shape=jax.ShapeDtypeStruct((S_ROWS, D_MODEL), F32),
        grid=(DEC_SEQ,),
        in_specs=[row(ATTN_WIDTH), row(C_CONV), row(D_MODEL), row(D_MODEL), row(D_MODEL),
                  _mod_spec(0, True),
                  resident((ATTN_WIDTH, D_MODEL)), resident((C_CONV, D_MODEL)), resident((D_MODEL, D_MODEL))],
        out_specs=row(D_MODEL),
        compiler_params=_cparams(1),
        name="mix_sample",
    )(attn, y, ga, gc, h, mod, w_ao, w_co, w_out)


def kernel(x_prompt, x_sample, cache_k, cache_v, state_conv, c_prompt, c_sample, rel_bias_table, norm1_g, ffn1_w1, ffn1_w3, ffn1_w2, norm2_g, w_in, attn_sinks, conv_dw_w, conv_dw_b, conv_ln_g, conv_ln_b, w_conv_out, w_attn_out, w_out, norm3_g, ffn2_w1, ffn2_w3, ffn2_w2, w_ada, b_ada, final_norm_g, w_ada_final, b_ada_final):
    c_all = jnp.concatenate([c_sample, c_prompt, jnp.zeros((N_COND - DEC_BATCH - 1, D_MODEL), F32)], axis=0)
    early = 5 * D_MODEL
    mod = _adaln(c_all, w_ada[0], b_ada, early)
    late_ada = [(w_ada[0], b_ada, early, 4 * D_MODEL), (w_ada_final, b_ada_final[None, :], 0, 2 * D_MODEL)]

    sinks = attn_sinks[0]
    tab_flat = rel_bias_table.reshape(-1)
    tab_ext = jnp.concatenate([rel_bias_table, sinks[None, :]], axis=0)
    th = tab_ext.T.reshape(N_KV_HEADS, GROUP, N_BUCKETS + 1).transpose(1, 0, 2)
    th = jnp.tile(th.reshape(N_HEADS, N_BUCKETS + 1), (DEC_SEQ, 1))

    xp = x_prompt[0]
    xs = x_sample.transpose(1, 0, 2).reshape(S_ROWS, D_MODEL)

    conv_w = (conv_dw_w[0], conv_dw_b, conv_ln_g, conv_ln_b)


    def keys_on_lanes(cache):
        return cache[0].transpose(0, 2, 3, 1).reshape(DEC_BATCH, KV_WIDTH, WBUF)

    def keys_on_rows(win):
        return win.reshape(DEC_BATCH, N_KV_HEADS, HEAD_DIM, WBUF).transpose(0, 3, 1, 2)[None]

    def mixers_s(q, kv, z, ga, gc, h, mod_late, mix_w):
        o, kw, vw = _attn_sample(q.reshape(DEC_SEQ, DEC_BATCH, ATTN_WIDTH),
                                 kv.reshape(DEC_SEQ, DEC_BATCH, 2 * KV_WIDTH),
                                 keys_on_lanes(cache_k), keys_on_lanes(cache_v), th)
        y, ns = _conv_sample(z.reshape(DEC_SEQ, DEC_BATCH, C_CONV), state_conv[0].transpose(1, 0, 2), *conv_w)
        h = _mix_sample(o.reshape(S_ROWS, ATTN_WIDTH), y.reshape(S_ROWS, C_CONV), ga, gc, h, mod_late, *mix_w)
        return h, (kw, vw, ns)

    hs, f1 = _ffn(xs, norm1_g, mod, 0, ffn1_w1[0], ffn1_w3[0], ffn1_w2[0], per_row=True,
                  tm=S_ROWS, tf=FFN_TF_CAST, cast_out=True)
    w_in_rows = D_MODEL // (SEQ // FFN_TM)
    w_in_side = (w_in[0], (w_in_rows, W_IN_SIDE_COLS),
                 lambda i, f: (i, jnp.minimum(f, IN_WIDTH // W_IN_SIDE_COLS - 1)))
    hp, (w_in_b,) = _ffn(xp, norm1_g, mod, 0, *f1, per_row=False, tm=FFN_TM, tf=FFN_TF, side=(w_in_side,))

    qs, kv_s, z_s, ga_s, gc_s = _proj(hs, norm2_g, mod, w_in_b, per_row=True, tm=S_ROWS, q_dtype=F32)
    qp, kv_p, z_p, ga_p, gc_p = _proj(hp, norm2_g, mod, w_in_b, per_row=False, tm=PROJ_TM, q_dtype=BF16)

    side = [_row_slab_side(w[0]) for w in (ffn2_w1, ffn2_w3, ffn2_w2, w_attn_out, w_conv_out, w_out)]
    attn_p, side_b, (mod_late, mod_f) = _attn_prompt(qp, kv_p, tab_flat, sinks, side, c_all, late_ada)
    f2, mix_w = side_b[:3], side_b[3:]
    final = (final_norm_g[None, :], mod_f)

    hs, (kw, vw, ns) = mixers_s(qs, kv_s, z_s, ga_s, gc_s, hs, mod_late, mix_w)
    hp = _mix_conv_prompt(z_p, *conv_w, attn_p, ga_p, gc_p, hp, mod_late, *mix_w)

    ys = _ffn(hs, norm3_g, mod_late, 1, *f2, per_row=True, tm=S_ROWS, tf=FFN_TF, final=final)
    yp = _ffn(hp, norm3_g, mod_late, 1, *f2, per_row=False, tm=FFN_TM, tf=FFN_TF, final=final)

    w = min(WINDOW, SEQ)
    kv_shape = (1, 1, w, N_KV_HEADS, HEAD_DIM)
    k_win_p = kv_p[SEQ - w:, :KV_WIDTH].reshape(kv_shape)
    v_win_p = kv_p[SEQ - w:, KV_WIDTH:].reshape(kv_shape)
    conv_p_state = z_p[SEQ - HIST:].reshape(1, 1, HIST, C_CONV)
    s_shape = (1, DEC_BATCH, WBUF, N_KV_HEADS, HEAD_DIM)
    y_prompt = yp[None]
    y_sample = ys.reshape(DEC_SEQ, DEC_BATCH, D_MODEL).transpose(1, 0, 2)
    return (y_prompt, y_sample, k_win_p, v_win_p, conv_p_state,
            keys_on_rows(kw), keys_on_rows(vw), ns.transpose(1, 0, 2)[None])
```

```python
import functools

import numpy as np
import jax
import jax.numpy as jnp
from jax import lax
from jax.experimental import pallas as pl
from jax.experimental.pallas import tpu as pltpu

D_MODEL = 2048
SEQ = 8192
DEC_BATCH = 128
DEC_SEQ = 4
PAST_LEN = 16384
N_HEADS = 32
N_KV_HEADS = 8
HEAD_DIM = 64
GROUP = N_HEADS // N_KV_HEADS
ATTN_WIDTH = N_HEADS * HEAD_DIM
KV_WIDTH = N_KV_HEADS * HEAD_DIM
WINDOW = 128
Q_BLOCK = 128
ATTN_SCALE = HEAD_DIM ** -0.5
N_BUCKETS = 32
MAX_DISTANCE = 128
C_CONV = D_MODEL // 2
CONV_WIDTH = 31
HIST = CONV_WIDTH - 1
D_FF = 5632
NORM_EPS = 1e-6
NEG_INF = -1e30
IN_WIDTH = ATTN_WIDTH + 2 * KV_WIDTH + 2 * C_CONV + 2 * D_MODEL
WBUF = min(WINDOW, PAST_LEN)

VMEM_LIMIT = 60 * 1024 * 1024
SUBLANES = 8
LANES = 128
BF16_SUBLANES = 16

S_ROWS = DEC_BATCH * DEC_SEQ
N_COND = DEC_BATCH + SUBLANES
PROMPT_MOD_BLOCK = DEC_BATCH // SUBLANES
S_EXTRA = SUBLANES

F32 = jnp.float32
BF16 = jnp.bfloat16


def _cparams(n_axes):
    return pltpu.CompilerParams(dimension_semantics=("arbitrary",) * n_axes,
                                vmem_limit_bytes=VMEM_LIMIT)


def _dot(a, b):
    return jnp.dot(a, b, preferred_element_type=F32)


def _rms_mod(x, g, sh, sc):
    ms = jnp.mean(x * x, axis=-1, keepdims=True)
    y = x * lax.rsqrt(ms + NORM_EPS) * g
    return y * (1.0 + sc) + sh


def _mod_val(ref, per_row):
    return ref[...] if per_row else ref[0:1, :]


def _mod_spec(chunk, per_row):
    if per_row:
        return pl.BlockSpec((DEC_BATCH, D_MODEL), lambda *_: (0, chunk))
    return pl.BlockSpec((SUBLANES, D_MODEL), lambda *_: (PROMPT_MOD_BLOCK, chunk))


def _adaln_kernel(c_ref, w_ref, b_ref, o_ref, s_ref):
    @pl.when(pl.program_id(0) == 0)
    def _():
        s_ref[...] = jax.nn.silu(c_ref[...]).astype(BF16)

    o_ref[...] = _dot(s_ref[...], w_ref[...].astype(BF16)) + b_ref[...]


ADALN_TN = 1024


def _adaln(c_all, w, b, n):
    tn = ADALN_TN
    return pl.pallas_call(
        _adaln_kernel,
        out_shape=jax.ShapeDtypeStruct((N_COND, n), F32),
        grid=(n // tn,),
        in_specs=[pl.BlockSpec((N_COND, D_MODEL), lambda j: (0, 0)),
                  pl.BlockSpec((D_MODEL, tn), lambda j: (0, j)),
                  pl.BlockSpec((1, tn), lambda j: (0, j))],
        out_specs=pl.BlockSpec((N_COND, tn), lambda j: (0, j)),
        scratch_shapes=[pltpu.VMEM((N_COND, D_MODEL), BF16)],
        compiler_params=_cparams(1),
        name="adaln",
    )(c_all, w, b)


FFN_ROWS = 512
FFN_TM = 1024
FFN_TF = 512
FFN_TF_CAST = 256
PROJ_TM = 1024
W_IN_SIDE_COLS = 1024


def _side_cast(side_in, side_out):
    for i_ref, o_ref in zip(side_in, side_out):
        o_ref[...] = i_ref[...].astype(BF16)


def _side_specs(side):
    args = [a for a, _, _ in side]
    specs = [pl.BlockSpec(block, index_map) for _, block, index_map in side]
    shapes = [jax.ShapeDtypeStruct(a.shape, BF16) for a in args]
    return args, specs, shapes


def _ffn_kernel(*refs, tm, sub, per_row, n_f, final, cast_out, n_side, ring_x):
    refs = list(refs)
    x_ref, n_ref, sh_ref, sc_ref, g_ref, w1_ref, w3_ref, w2_ref = refs[:8]
    del refs[:8]
    if final:
        gf_ref, shf_ref, scf_ref = refs[:3]
        del refs[:3]
    side_in = refs[:n_side]
    del refs[:n_side]
    o_ref = refs.pop(0)
    if cast_out:
        w1o_ref, w3o_ref, w2o_ref = refs[:3]
        del refs[:3]
    side_out = refs[:n_side]
    del refs[:n_side]
    f = pl.program_id(1)
    if ring_x:
        u_ref, x_buf, x_sem = refs
        i = pl.program_id(0)
        slot = lax.rem(i, 2)

        def x_copy(tile, dst_slot):
            return pltpu.make_async_copy(x_ref.at[pl.ds(tile * tm, tm), :], x_buf.at[dst_slot],
                                         x_sem.at[dst_slot])

        @pl.when((f == 0) & (i == 0))
        def _():
            x_copy(0, 0).start()

        @pl.when(f == 0)
        def _():
            x_copy(i, slot).wait()

        @pl.when((f == 0) & (i + 1 < pl.num_programs(0)))
        def _():
            x_copy(i + 1, 1 - slot).start()

        x_ref = x_buf.at[slot]
    else:
        (u_ref,) = refs
    _side_cast(side_in, side_out)

    if cast_out:
        w1o_ref[...] = w1_ref[...].astype(BF16)
        w3o_ref[...] = w3_ref[...].astype(BF16)
        w2o_ref[...] = w2_ref[...].astype(BF16)
        w1_ref, w3_ref, w2_ref = w1o_ref, w3o_ref, w2o_ref

    def accumulate(first, last):
        for s in range(tm // FFN_ROWS):
            rows = slice(s * FFN_ROWS, (s + 1) * FFN_ROWS)
            subs = [slice(r0, r0 + sub) for r0 in range(s * FFN_ROWS, (s + 1) * FFN_ROWS, sub)]
            if first:
                for r in subs:
                    u_ref[r, :] = _rms_mod(x_ref[r, :], n_ref[...], _mod_val(sh_ref, per_row),
                                           _mod_val(sc_ref, per_row)).astype(BF16)
            u = u_ref[rows, :]
            h1 = _dot(u, w1_ref[...])
            h3 = _dot(u, w3_ref[...])
            a = (jax.nn.silu(h1) * h3).astype(BF16)
            d = _dot(a, w2_ref[...])
            if first:
                o_ref[rows, :] = d
            else:
                o_ref[rows, :] += d
            if last:
                for r in subs:
                    h = x_ref[r, :] + 0.5 * _mod_val(g_ref, per_row) * o_ref[r, :]
                    if final:
                        h = _rms_mod(h, gf_ref[...], _mod_val(shf_ref, per_row), _mod_val(scf_ref, per_row))
                    o_ref[r, :] = h

    assert n_f > 2

    @pl.when(f == 0)
    def _():
        accumulate(True, False)

    @pl.when((f > 0) & (f < n_f - 1))
    def _():
        accumulate(False, False)

    @pl.when(f == n_f - 1)
    def _():
        accumulate(False, True)


def _ffn(x, norm_g, mod, chunk0, w1, w3, w2, *, per_row, tm, tf, final=None, cast_out=False, side=()):
    rows = x.shape[0]
    n_f = D_FF // tf
    sub = DEC_BATCH
    row_spec = pl.BlockSpec((tm, D_MODEL), lambda i, f: (i, 0))
    vec_spec = pl.BlockSpec((1, D_MODEL), lambda i, f: (0, 0))
    w_specs = [pl.BlockSpec((D_MODEL, tf), lambda i, f: (0, f)),
               pl.BlockSpec((D_MODEL, tf), lambda i, f: (0, f)),
               pl.BlockSpec((tf, D_MODEL), lambda i, f: (f, 0))]
    ring_x = rows > tm
    x_spec = pl.BlockSpec(memory_space=pl.ANY) if ring_x else row_spec
    in_specs = [x_spec, vec_spec,
                _mod_spec(chunk0, per_row), _mod_spec(chunk0 + 1, per_row), _mod_spec(chunk0 + 2, per_row)]
    in_specs += w_specs
    args = [x, norm_g, mod, mod, mod, w1, w3, w2]
    if final is not None:
        gf, mod_f = final
        in_specs += [vec_spec, _mod_spec(0, per_row), _mod_spec(1, per_row)]
        args += [gf, mod_f, mod_f]
    side_args, side_specs, side_shapes = _side_specs(side)
    in_specs += side_specs
    args += side_args
    out_shape = [jax.ShapeDtypeStruct((rows, D_MODEL), F32)]
    out_specs = [row_spec]
    if cast_out:
        assert rows == tm, "every weight block must be visited exactly once"
        out_shape += [jax.ShapeDtypeStruct(w.shape, BF16) for w in (w1, w3, w2)]
        out_specs += w_specs
    out_shape += side_shapes
    out_specs += side_specs
    outs = pl.pallas_call(
        functools.partial(_ffn_kernel, tm=tm, sub=sub, per_row=per_row, n_f=n_f,
                          final=final is not None, cast_out=cast_out, n_side=len(side), ring_x=ring_x),
        out_shape=out_shape,
        grid=(rows // tm, n_f),
        in_specs=in_specs,
        out_specs=out_specs,
        scratch_shapes=[pltpu.VMEM((tm, D_MODEL), BF16)]
        + ([pltpu.VMEM((2, tm, D_MODEL), F32), pltpu.SemaphoreType.DMA((2,))] if ring_x else []),
        compiler_params=_cparams(2),
        name="ffn_final" if final is not None else "ffn",
    )(*args)
    return (outs[0], tuple(outs[1:])) if (cast_out or side) else outs[0]


PROJ_CHUNK = 256
PROJ_NORM_ROWS = 512
PROJ_CW = 512


def _proj_layout():
    cw = PROJ_CW
    n_q = ATTN_WIDTH // (2 * cw)
    n_kv = 2 * KV_WIDTH // (2 * cw)
    n_glu = C_CONV // cw
    n_gate = D_MODEL // (2 * cw)
    pre = n_q + n_kv
    steps = pre + n_glu + 2 * n_gate

    def block_a(j):
        return jnp.where((j >= pre) & (j < pre + n_glu), j + pre, 2 * j)

    def block_b(j):
        return jnp.where((j >= pre) & (j < pre + n_glu), j + pre + n_glu, 2 * j + 1)

    return n_q, n_kv, n_glu, n_gate, steps, block_a, block_b


def _proj_kernel(h_ref, n_ref, sh_ref, sc_ref, wa_ref, wb_ref, q_ref, kv_ref, z_ref, ga_ref, gc_ref, u_ref,
                 *, tm, sub, per_row):
    cw = PROJ_CW
    n_q, n_kv, n_glu, n_gate, _, _, _ = _proj_layout()
    j = pl.program_id(1)

    n_chunks = cw // PROJ_CHUNK

    def chunks(w_ref, rows=slice(None)):
        for c in range(n_chunks):
            cols = slice(c * PROJ_CHUNK, (c + 1) * PROJ_CHUNK)
            yield c * PROJ_CHUNK, _dot(u_ref[rows, :], w_ref[:, cols])

    def pair_store(o_ref, fn, rows=slice(None)):
        for half, w_ref in enumerate((wa_ref, wb_ref)):
            for c0, acc in chunks(w_ref, rows):
                o_ref[rows, half * cw + c0:half * cw + c0 + PROJ_CHUNK] = fn(acc).astype(o_ref.dtype)

    @pl.when(j == 0)
    def _():
        for g0 in range(0, tm, PROJ_NORM_ROWS):
            for r0 in range(g0, g0 + PROJ_NORM_ROWS, sub):
                r = slice(r0, r0 + sub)
                u_ref[r, :] = _rms_mod(h_ref[r, :], n_ref[...], _mod_val(sh_ref, per_row),
                                       _mod_val(sc_ref, per_row)).astype(BF16)
            pair_store(q_ref, lambda acc: acc, slice(g0, g0 + PROJ_NORM_ROWS))

    lo = 0

    @pl.when((j > 0) & (j < n_q))
    def _():
        pair_store(q_ref, lambda acc: acc)

    lo += n_q

    @pl.when((j >= lo) & (j < lo + n_kv))
    def _():
        pair_store(kv_ref, lambda acc: acc)

    lo += n_kv

    @pl.when((j >= lo) & (j < lo + n_glu))
    def _():
        for (c0, a), (_, g) in zip(chunks(wa_ref), chunks(wb_ref)):
            z_ref[:, c0:c0 + PROJ_CHUNK] = a * jax.nn.sigmoid(g)

    lo += n_glu

    @pl.when((j >= lo) & (j < lo + n_gate))
    def _():
        pair_store(ga_ref, jax.nn.sigmoid)

    lo += n_gate

    @pl.when(j >= lo)
    def _():
        pair_store(gc_ref, jax.nn.sigmoid)


def _proj(h, norm_g, mod, w, *, per_row, tm, q_dtype):
    rows = h.shape[0]
    sub = DEC_BATCH
    cw = PROJ_CW
    n_q, n_kv, n_glu, n_gate, steps, block_a, block_b = _proj_layout()

    def out_spec(width, first, count):
        return pl.BlockSpec((tm, width), lambda i, j: (i, jnp.clip(j - first, 0, count - 1)))

    wa_spec = pl.BlockSpec((D_MODEL, cw), lambda i, j: (0, block_a(j)))
    wb_spec = pl.BlockSpec((D_MODEL, cw), lambda i, j: (0, block_b(j)))
    out_shape = [jax.ShapeDtypeStruct((rows, ATTN_WIDTH), q_dtype),
                 jax.ShapeDtypeStruct((rows, 2 * KV_WIDTH), F32),
                 jax.ShapeDtypeStruct((rows, C_CONV), F32),
                 jax.ShapeDtypeStruct((rows, D_MODEL), BF16),
                 jax.ShapeDtypeStruct((rows, D_MODEL), BF16)]
    out_specs = [out_spec(2 * cw, 0, n_q),
                 out_spec(2 * cw, n_q, n_kv),
                 out_spec(cw, n_q + n_kv, n_glu),
                 out_spec(2 * cw, n_q + n_kv + n_glu, n_gate),
                 out_spec(2 * cw, n_q + n_kv + n_glu + n_gate, n_gate)]
    return pl.pallas_call(
        functools.partial(_proj_kernel, tm=tm, sub=sub, per_row=per_row),
        out_shape=out_shape,
        grid=(rows // tm, steps),
        in_specs=[pl.BlockSpec((tm, D_MODEL), lambda i, j: (i, 0)),
                  pl.BlockSpec((1, D_MODEL), lambda i, j: (0, 0)),
                  _mod_spec(3, per_row), _mod_spec(4, per_row), wa_spec, wb_spec],
        out_specs=out_specs,
        scratch_shapes=[pltpu.VMEM((tm, D_MODEL), BF16)],
        compiler_params=_cparams(2),
        name="proj",
    )(h, norm_g, mod, mod, w, w)


def _t5_bucket_np(dist):
    exact = N_BUCKETS // 2
    d = np.maximum(dist, 0)
    df = np.maximum(d, 1).astype(np.float32)
    large = exact + (np.log(df / np.float32(exact)) / np.float32(np.log(MAX_DISTANCE / exact))
                     * np.float32(N_BUCKETS - exact)).astype(np.int32)
    large = np.minimum(large, N_BUCKETS - 1)
    return np.where(d < exact, d, large).astype(np.int32)


def _prompt_codes():
    qi = np.arange(Q_BLOCK)[:, None]
    kj = np.arange(2 * Q_BLOCK)[None, :]
    dist = qi + Q_BLOCK - kj
    valid = (dist >= 0) & (dist <= WINDOW)
    return np.where(valid, _t5_bucket_np(dist), -1).astype(np.int32)


def _sample_codes():
    t = np.repeat(np.arange(DEC_SEQ), GROUP * N_KV_HEADS)[:, None]
    j = np.arange(WBUF)[None, :]
    dist = t + WBUF - j
    cache = np.where((dist >= 0) & (dist <= WINDOW), _t5_bucket_np(dist), -1)
    c = np.arange(S_EXTRA)[None, :]
    t_new = c - (S_EXTRA - DEC_SEQ)
    dist = t - t_new
    extra = np.where((t_new >= 0) & (dist >= 0) & (dist <= WINDOW), _t5_bucket_np(dist), -1)
    extra = np.where(c == 0, N_BUCKETS, extra)
    return cache.astype(np.int32), extra.astype(np.int32)


BIAS_ROWS = 64
ATTN_QB = 2


def _attn_p_kernel(*refs, n_side, n_ada):
    refs = list(refs)
    tab_ref, sink_ref, code_ref, q_ref, kvc_ref, kvp_ref = refs[:6]
    del refs[:6]
    side_in = refs[:n_side]
    del refs[:n_side]
    c_ref = refs.pop(0)
    ada_in = [(refs[2 * k], refs[2 * k + 1]) for k in range(n_ada)]
    del refs[:2 * n_ada]
    o_ref = refs.pop(0)
    side_out = refs[:n_side]
    del refs[:n_side]
    ada_out = refs[:n_ada]
    del refs[:n_ada]
    bias_ref, silu_c_ref = refs
    n = pl.program_id(0)
    _side_cast(side_in, side_out)

    @pl.when(n == 0)
    def _():
        silu_c_ref[...] = jax.nn.silu(c_ref[...]).astype(BF16)

    for (w_ref, b_ref), m_ref in zip(ada_in, ada_out):
        m_ref[...] = _dot(silu_c_ref[...], w_ref[...].astype(BF16)) + b_ref[...]

    n_keys = 2 * Q_BLOCK
    wide = GROUP * Q_BLOCK

    @pl.when(n == 0)
    def _():
        def kv_body(kv, carry):
            for g in range(GROUP):
                for r in range(n_keys // BIAS_ROWS):
                    rows = slice(r * BIAS_ROWS, (r + 1) * BIAS_ROWS)
                    code = code_ref[rows, :]
                    acc = jnp.full((BIAS_ROWS, Q_BLOCK), NEG_INF, F32)
                    for bk in range(N_BUCKETS):
                        acc = jnp.where(code == bk, tab_ref[bk * N_HEADS + kv * GROUP + g], acc)
                    bias_ref[kv, rows, g * Q_BLOCK:(g + 1) * Q_BLOCK] = acc
            return carry
        lax.fori_loop(0, N_KV_HEADS, kv_body, 0)

    ones = jnp.ones((BF16_SUBLANES, n_keys), BF16)
    zeros = jnp.zeros((HEAD_DIM, wide), BF16)
    tile = (n_keys, LANES)
    key = lax.broadcasted_iota(jnp.int32, tile, 0)
    lane = lax.broadcasted_iota(jnp.int32, tile, 1)
    pen = jnp.where((lane == 0) & (key < Q_BLOCK) & (n == 0), NEG_INF, 0.0).astype(BF16)
    pen_rows = (lax.broadcasted_iota(jnp.int32, (LANES, wide), 0) == 0).astype(BF16)

    def query_block(a):
        rows = slice(a * Q_BLOCK, (a + 1) * Q_BLOCK)
        q_t = q_ref[rows, :].astype(F32).T

        def keys(cols):
            prev = kvp_ref[:, cols] if a == 0 else kvc_ref[(a - 1) * Q_BLOCK:a * Q_BLOCK, cols]
            return jnp.concatenate([prev, kvc_ref[rows, cols]], axis=0)

        def pair_kv(pair):
            k2 = keys(slice(pair * LANES, (pair + 1) * LANES)).astype(BF16)
            if a == 0:
                k2 = jnp.concatenate([k2, pen], axis=1)
            v2_t = keys(slice(KV_WIDTH + pair * LANES, KV_WIDTH + (pair + 1) * LANES)).T.astype(BF16)
            return k2, v2_t

        def scores(kv, k2):
            q4 = jnp.concatenate(
                [q_t[(kv * GROUP + g) * HEAD_DIM:(kv * GROUP + g + 1) * HEAD_DIM, :] for g in range(GROUP)],
                axis=1)
            q4 = (q4 * ATTN_SCALE).astype(BF16)
            parts = [q4, zeros] if kv % 2 == 0 else [zeros, q4]
            return _dot(k2, jnp.concatenate(parts + ([pen_rows] if a == 0 else []), axis=0))

        kvs = [pair_kv(pair) for pair in range(N_KV_HEADS // 2)]
        for kv in range(N_KV_HEADS):
            pair, parity = divmod(kv, 2)
            s = scores(kv, kvs[pair][0]) + bias_ref[kv]
            sink = jnp.concatenate(
                [jnp.full((1, Q_BLOCK), sink_ref[kv * GROUP + g], F32) for g in range(GROUP)], axis=1)
            m = jnp.maximum(jnp.max(s, axis=0, keepdims=True), sink)
            p = jnp.exp(s - m).astype(BF16)
            lhs = jnp.concatenate([kvs[pair][1][parity * HEAD_DIM:(parity + 1) * HEAD_DIM, :], ones], axis=0)
            o_t = _dot(lhs, p)
            l = o_t[HEAD_DIM:HEAD_DIM + 1, :] + jnp.exp(sink - m)
            o_n = o_t[0:HEAD_DIM, :] * (1.0 / l)
            o_kv = jnp.concatenate([o_n[:, g * Q_BLOCK:(g + 1) * Q_BLOCK] for g in range(GROUP)], axis=0)
            o_ref[rows, kv * GROUP * HEAD_DIM:(kv + 1) * GROUP * HEAD_DIM] = o_kv.T.astype(o_ref.dtype)

    for a in range(ATTN_QB):
        query_block(a)


ATTN_STEPS = SEQ // (ATTN_QB * Q_BLOCK)


def _row_slab_side(w):
    slab = w.shape[0] // ATTN_STEPS
    assert slab * ATTN_STEPS == w.shape[0] and slab % BF16_SUBLANES == 0
    return (w, (slab, w.shape[1]), lambda n: (n, 0))


def _attn_prompt(q, kv, tab_flat, sinks, side, c_all, ada):
    step_rows = ATTN_QB * Q_BLOCK
    code_t = jnp.asarray(np.ascontiguousarray(_prompt_codes().T))
    smem = pl.BlockSpec(memory_space=pltpu.SMEM)
    side_args, side_specs, side_shapes = _side_specs(side)
    ada_args, ada_in_specs, ada_out_specs, ada_shapes = [], [], [], []
    for w, b, col0, n_cols in ada:
        tn = n_cols // ATTN_STEPS
        assert tn * ATTN_STEPS == n_cols and tn % LANES == 0 and col0 % tn == 0
        first = col0 // tn
        ada_args += [w, b]
        ada_in_specs += [pl.BlockSpec((D_MODEL, tn), lambda n, first=first: (0, first + n)),
                         pl.BlockSpec((1, tn), lambda n, first=first: (0, first + n))]
        ada_out_specs.append(pl.BlockSpec((N_COND, tn), lambda n: (0, n)))
        ada_shapes.append(jax.ShapeDtypeStruct((N_COND, n_cols), F32))
    out_spec = pl.BlockSpec((step_rows, ATTN_WIDTH), lambda n: (n, 0))
    outs = pl.pallas_call(
        functools.partial(_attn_p_kernel, n_side=len(side), n_ada=len(ada)),
        out_shape=[jax.ShapeDtypeStruct((SEQ, ATTN_WIDTH), BF16)] + side_shapes + ada_shapes,
        grid=(ATTN_STEPS,),
        in_specs=[smem, smem,
                  pl.BlockSpec((2 * Q_BLOCK, Q_BLOCK), lambda n: (0, 0)),
                  pl.BlockSpec((step_rows, ATTN_WIDTH), lambda n: (n, 0)),
                  pl.BlockSpec((step_rows, 2 * KV_WIDTH), lambda n: (n, 0)),
                  pl.BlockSpec((Q_BLOCK, 2 * KV_WIDTH), lambda n: (jnp.maximum(ATTN_QB * n - 1, 0), 0))]
        + side_specs + [pl.BlockSpec((N_COND, D_MODEL), lambda n: (0, 0))] + ada_in_specs,
        out_specs=[out_spec] + side_specs + ada_out_specs,
        scratch_shapes=[pltpu.VMEM((N_KV_HEADS, 2 * Q_BLOCK, GROUP * Q_BLOCK), F32),
                        pltpu.VMEM((N_COND, D_MODEL), BF16)],
        compiler_params=_cparams(1),
        name="attn_prompt",
    )(tab_flat, sinks, code_t, q, kv, kv, *side_args, c_all, *ada_args)
    n_side = len(side)
    return outs[0], tuple(outs[1:1 + n_side]), tuple(outs[1 + n_side:])


S_BB = 8


def _regroup_heads(x, to_group_major):
    half = HEAD_DIM
    assert LANES == 2 * half
    low = lax.broadcasted_iota(jnp.int32, (x.shape[0], LANES), 1) < half
    n_cols = ATTN_WIDTH // LANES
    outer, inner = (GROUP, N_KV_HEADS) if to_group_major else (N_KV_HEADS, GROUP)
    cols = []
    for c in range(n_cols):
        o, i = divmod(2 * c, inner)
        src = [(i + d) * outer + o for d in range(2)]
        a = x[:, (src[0] // 2) * LANES:(src[0] // 2 + 1) * LANES]
        b = x[:, (src[1] // 2) * LANES:(src[1] // 2 + 1) * LANES]
        assert src[0] % 2 == src[1] % 2
        if src[0] % 2 == 0:
            cols.append(jnp.where(low, a, pltpu.roll(b, half, axis=1)))
        else:
            cols.append(jnp.where(low, pltpu.roll(a, half, axis=1), b))
    return jnp.concatenate(cols, axis=1)


def _attn_s_kernel(th_ref, code_c_ref, code_x_ref, q_ref, kv_ref, ck_ref, cv_ref, o_ref, kw_ref, vw_ref,
                   bias_c_ref, bias_x_ref, nk_ref, nv_ref, qe_ref, qg_ref, og_ref):
    rows_all = DEC_SEQ * N_HEADS
    for t in range(DEC_SEQ):
        qg_ref[t] = _regroup_heads(q_ref[t], True)

    @pl.when(pl.program_id(0) == 0)
    def _():
        for code_ref, bias_ref in ((code_c_ref, bias_c_ref), (code_x_ref, bias_x_ref)):
            code = code_ref[...]
            acc = jnp.full(code.shape, NEG_INF, F32)
            for bk in range(N_BUCKETS + 1):
                acc = jnp.where(code == bk, th_ref[:, bk:bk + 1], acc)
            bias_ref[...] = acc
        nk_ref[...] = jnp.zeros_like(nk_ref)
        nv_ref[...] = jnp.zeros_like(nv_ref)

    lane_head = lax.broadcasted_iota(jnp.int32, (N_KV_HEADS, KV_WIDTH), 1) // HEAD_DIM
    row_head = lax.broadcasted_iota(jnp.int32, (N_KV_HEADS, KV_WIDTH), 0)
    diag = lane_head == row_head
    out_mask = (lax.broadcasted_iota(jnp.int32, (rows_all, KV_WIDTH), 0) % N_KV_HEADS
                == lax.broadcasted_iota(jnp.int32, (rows_all, KV_WIDTH), 1) // HEAD_DIM)
    new_lanes = lax.broadcasted_iota(jnp.int32, (KV_WIDTH, WBUF), 1) >= WBUF - DEC_SEQ
    bias_c = bias_c_ref[...]
    bias_x = bias_x_ref[...]
    nt_dims = (((1,), (1,)), ((), ()))
    x0 = WBUF - S_EXTRA

    for b in range(S_BB):
        for t in range(DEC_SEQ):
            nk_ref[WBUF - DEC_SEQ + t:WBUF - DEC_SEQ + t + 1, :] = kv_ref[t, b:b + 1, 0:KV_WIDTH]
            nv_ref[WBUF - DEC_SEQ + t:WBUF - DEC_SEQ + t + 1, :] = kv_ref[t, b:b + 1, KV_WIDTH:2 * KV_WIDTH]
            for g in range(GROUP):
                q_row = qg_ref[t, b:b + 1, g * KV_WIDTH:(g + 1) * KV_WIDTH]
                piece = jnp.where(diag, jnp.broadcast_to(q_row, (N_KV_HEADS, KV_WIDTH)), 0.0)
                r0 = (t * GROUP + g) * N_KV_HEADS
                qe_ref[r0:r0 + N_KV_HEADS, :] = piece
        k_t = ck_ref[b]
        v_t = cv_ref[b]
        kw_ref[b] = jnp.where(new_lanes, nk_ref[...].T, pltpu.roll(k_t, WBUF - DEC_SEQ, axis=1))
        vw_ref[b] = jnp.where(new_lanes, nv_ref[...].T, pltpu.roll(v_t, WBUF - DEC_SEQ, axis=1))

        qe = (qe_ref[...] * ATTN_SCALE).astype(BF16)
        s_c = _dot(qe, k_t.astype(BF16)) + bias_c
        s_x = lax.dot_general(qe, nk_ref[x0:WBUF, :].astype(BF16), nt_dims,
                              preferred_element_type=F32) + bias_x
        m = jnp.maximum(jnp.max(s_c, axis=-1, keepdims=True), jnp.max(s_x, axis=-1, keepdims=True))
        p_c = jnp.exp(s_c - m)
        p_x = jnp.exp(s_x - m)
        l = jnp.sum(p_c, axis=-1, keepdims=True) + jnp.sum(p_x, axis=-1, keepdims=True)
        o = lax.dot_general(p_c.astype(BF16), v_t.astype(BF16), nt_dims, preferred_element_type=F32)
        o = (o + _dot(p_x.astype(BF16), nv_ref[x0:WBUF, :].astype(BF16))) / l
        o = jnp.where(out_mask, o, 0.0)
        for t in range(DEC_SEQ):
            for g in range(GROUP):
                r0 = (t * GROUP + g) * N_KV_HEADS
                row = jnp.sum(o[r0:r0 + N_KV_HEADS, :], axis=0, keepdims=True)
                og_ref[t, b:b + 1, g * KV_WIDTH:(g + 1) * KV_WIDTH] = row
    for t in range(DEC_SEQ):
        o_ref[t] = _regroup_heads(og_ref[t], False)


def _attn_sample(q, kv, cache_k_t, cache_v_t, th):
    code_c, code_x = (jnp.asarray(c) for c in _sample_codes())
    rows_all = DEC_SEQ * N_HEADS
    full = lambda a: pl.BlockSpec(a.shape, lambda i: (0,) * a.ndim)
    tb_spec = lambda w: pl.BlockSpec((DEC_SEQ, S_BB, w), lambda i: (0, i, 0))
    cache_spec = pl.BlockSpec((S_BB, KV_WIDTH, WBUF), lambda i: (i, 0, 0))
    cache_shape = jax.ShapeDtypeStruct((DEC_BATCH, KV_WIDTH, WBUF), F32)
    return pl.pallas_call(
        _attn_s_kernel,
        out_shape=(jax.ShapeDtypeStruct((DEC_SEQ, DEC_BATCH, ATTN_WIDTH), F32), cache_shape, cache_shape),
        grid=(DEC_BATCH // S_BB,),
        in_specs=[full(th), full(code_c), full(code_x),
                  tb_spec(ATTN_WIDTH), tb_spec(2 * KV_WIDTH), cache_spec, cache_spec],
        out_specs=(tb_spec(ATTN_WIDTH), cache_spec, cache_spec),
        scratch_shapes=[pltpu.VMEM((rows_all, WBUF), F32),
                        pltpu.VMEM((rows_all, S_EXTRA), F32),
                        pltpu.VMEM((WBUF, KV_WIDTH), F32),
                        pltpu.VMEM((WBUF, KV_WIDTH), F32),
                        pltpu.VMEM((rows_all, KV_WIDTH), F32),
                        pltpu.VMEM((DEC_SEQ, S_BB, ATTN_WIDTH), F32),
                        pltpu.VMEM((DEC_SEQ, S_BB, ATTN_WIDTH), F32)],
        compiler_params=_cparams(1),
        name="attn_sample",
    )(th, code_c, code_x, q, kv, cache_k_t, cache_v_t)


CONV_RC = 32
CONV_ROWS = 128
CONV_HALO = 32


def _ln_silu(acc, lg, lb):
    mu = jnp.mean(acc, axis=-1, keepdims=True)
    xc = acc - mu
    var = jnp.mean(xc * xc, axis=-1, keepdims=True)
    return jax.nn.silu(xc * lax.rsqrt(var + NORM_EPS) * lg + lb)


def _conv_stage(first, zc_ref, zh_ref, s_ref):
    s_ref[0:CONV_HALO, :] = jnp.where(first, 0.0, zh_ref[...])
    s_ref[CONV_HALO:, :] = zc_ref[...]


def _conv_rows(rc, w_ref, b_ref, lg_ref, lb_ref, y_ref, s_ref, c_ref):
    off = CONV_HALO - HIST
    groups = [[j for j in range(CONV_WIDTH) if (j + off) % SUBLANES == r] for r in range(SUBLANES)]
    t0 = rc * CONV_ROWS
    for lc in range(C_CONV // LANES):
        lanes = slice(lc * LANES, (lc + 1) * LANES)
        out = jnp.broadcast_to(b_ref[:, lanes], (CONV_ROWS, LANES))
        for r, taps in enumerate(groups):
            n_rows = CONV_ROWS + (SUBLANES if r else 0)
            part = None
            for j in taps:
                base = t0 + (j + off) - r
                term = w_ref[j:j + 1, lanes] * s_ref[base:base + n_rows, lanes]
                part = term if part is None else part + term
            out = out + part[r:r + CONV_ROWS, :]
        c_ref[t0:t0 + CONV_ROWS, lanes] = out
    for r in range(CONV_ROWS // CONV_RC):
        rows = slice(t0 + r * CONV_RC, t0 + (r + 1) * CONV_RC)
        y_ref[rows, :] = _ln_silu(c_ref[rows, :], lg_ref[...], lb_ref[...]).astype(BF16)


def _mix_conv_kernel(zc_ref, zh_ref, w_ref, b_ref, lg_ref, lb_ref,
                     at_ref, ga_ref, gc_ref, h_ref, g2_ref, wao_ref, wco_ref, wout_ref,
                     o_ref, s_ref, c_ref, y_ref, *, tm):
    _conv_stage(pl.program_id(0) == 0, zc_ref, zh_ref, s_ref)
    for rc in range(tm // CONV_ROWS):
        _conv_rows(rc, w_ref, b_ref, lg_ref, lb_ref, y_ref, s_ref, c_ref)
    a = _dot(at_ref[...], wao_ref[...])
    c = _dot(y_ref[...], wco_ref[...])
    merged = (ga_ref[...].astype(F32) * a + gc_ref[...].astype(F32) * c).astype(BF16)
    r = _dot(merged, wout_ref[...])
    o_ref[...] = h_ref[...] + g2_ref[0:1, :] * r


MIX_TM = 256


def _mix_conv_prompt(z, dw_w, dw_b, ln_g, ln_b, attn, ga, gc, h, mod, w_ao, w_co, w_out):
    rows = h.shape[0]
    tm = MIX_TM
    n = rows // tm
    ratio = tm // CONV_HALO
    vec = pl.BlockSpec((1, C_CONV), lambda s: (0, 0))
    row = lambda w: pl.BlockSpec((tm, w), lambda s: (s, 0))
    resident = lambda shape: pl.BlockSpec(shape, lambda s: (0, 0), pipeline_mode=pl.Buffered(1))
    return pl.pallas_call(
        functools.partial(_mix_conv_kernel, tm=tm),
        out_shape=jax.ShapeDtypeStruct((rows, D_MODEL), F32),
        grid=(n,),
        in_specs=[row(C_CONV),
                  pl.BlockSpec((CONV_HALO, C_CONV), lambda s: (jnp.maximum(s * ratio - 1, 0), 0)),
                  pl.BlockSpec((CONV_WIDTH, C_CONV), lambda s: (0, 0)), vec, vec, vec,
                  row(ATTN_WIDTH), row(D_MODEL), row(D_MODEL), row(D_MODEL),
                  _mod_spec(0, False),
                  resident((ATTN_WIDTH, D_MODEL)), resident((C_CONV, D_MODEL)), resident((D_MODEL, D_MODEL))],
        out_specs=row(D_MODEL),
        scratch_shapes=[pltpu.VMEM((tm + CONV_HALO, C_CONV), F32), pltpu.VMEM((tm, C_CONV), F32),
                        pltpu.VMEM((tm, C_CONV), BF16)],
        compiler_params=_cparams(1),
        name="mix_conv",
    )(z, z, dw_w, dw_b, ln_g, ln_b, attn, ga, gc, h, mod, w_ao, w_co, w_out)


CONV_S_BB = 32


def _conv_s_kernel(z_ref, hist_ref, w_ref, b_ref, lg_ref, lb_ref, y_ref, ns_ref):
    for t in range(DEC_SEQ):
        acc = jnp.broadcast_to(b_ref[...], (CONV_S_BB, C_CONV))
        for j in range(CONV_WIDTH):
            i = t + j
            src = hist_ref[i] if i < HIST else z_ref[i - HIST]
            acc = acc + w_ref[j:j + 1, :] * src
        y_ref[t] = _ln_silu(acc, lg_ref[...], lb_ref[...]).astype(BF16)
    keep = HIST - DEC_SEQ
    ns_ref[0:keep] = hist_ref[DEC_SEQ:HIST]
    ns_ref[keep:HIST] = z_ref[...]


def _conv_sample(z, hist_t, dw_w, dw_b, ln_g, ln_b):
    vec = pl.BlockSpec((1, C_CONV), lambda i: (0, 0))
    tb = pl.BlockSpec((DEC_SEQ, CONV_S_BB, C_CONV), lambda i: (0, i, 0))
    st = pl.BlockSpec((HIST, CONV_S_BB, C_CONV), lambda i: (0, i, 0))
    return pl.pallas_call(
        _conv_s_kernel,
        out_shape=(jax.ShapeDtypeStruct((DEC_SEQ, DEC_BATCH, C_CONV), BF16),
                   jax.ShapeDtypeStruct(hist_t.shape, F32)),
        grid=(DEC_BATCH // CONV_S_BB,),
        in_specs=[tb, st, pl.BlockSpec((CONV_WIDTH, C_CONV), lambda i: (0, 0)), vec, vec, vec],
        out_specs=(tb, st),
        compiler_params=_cparams(1),
        name="conv_sample",
    )(z, hist_t, dw_w, dw_b, ln_g, ln_b)


def _mix_s_kernel(at_ref, y_ref, ga_ref, gc_ref, h_ref, g2_ref, wao_ref, wco_ref, wout_ref, o_ref):
    a = _dot(at_ref[...].astype(BF16), wao_ref[...])
    c = _dot(y_ref[...], wco_ref[...])
    merged = (ga_ref[...].astype(F32) * a + gc_ref[...].astype(F32) * c).astype(BF16)
    r = _dot(merged, wout_ref[...])
    o_ref[...] = h_ref[...] + g2_ref[...] * r


def _mix_sample(attn, y, ga, gc, h, mod, w_ao, w_co, w_out):
    row = lambda w: pl.BlockSpec((DEC_BATCH, w), lambda t: (t, 0))
    resident = lambda shape: pl.BlockSpec(shape, lambda t: (0, 0), pipeline_mode=pl.Buffered(1))
    return pl.pallas_call(
        _mix_s_kernel,
        out_shape=jax.ShapeDtypeStruct((S_ROWS, D_MODEL), F32),
        grid=(DEC_SEQ,),
        in_specs=[row(ATTN_WIDTH), row(C_CONV), row(D_MODEL), row(D_MODEL), row(D_MODEL),
                  _mod_spec(0, True),
                  resident((ATTN_WIDTH, D_MODEL)), resident((C_CONV, D_MODEL)), resident((D_MODEL, D_MODEL))],
        out_specs=row(D_MODEL),
        compiler_params=_cparams(1),
        name="mix_sample",
    )(attn, y, ga, gc, h, mod, w_ao, w_co, w_out)


def kernel(x_prompt, x_sample, cache_k, cache_v, state_conv, c_prompt, c_sample, rel_bias_table, norm1_g, ffn1_w1, ffn1_w3, ffn1_w2, norm2_g, w_in, attn_sinks, conv_dw_w, conv_dw_b, conv_ln_g, conv_ln_b, w_conv_out, w_attn_out, w_out, norm3_g, ffn2_w1, ffn2_w3, ffn2_w2, w_ada, b_ada, final_norm_g, w_ada_final, b_ada_final):
    c_all = jnp.concatenate([c_sample, c_prompt, jnp.zeros((N_COND - DEC_BATCH - 1, D_MODEL), F32)], axis=0)
    early = 5 * D_MODEL
    mod = _adaln(c_all, w_ada[0], b_ada, early)
    late_ada = [(w_ada[0], b_ada, early, 4 * D_MODEL), (w_ada_final, b_ada_final[None, :], 0, 2 * D_MODEL)]

    sinks = attn_sinks[0]
    tab_flat = rel_bias_table.reshape(-1)
    tab_ext = jnp.concatenate([rel_bias_table, sinks[None, :]], axis=0)
    th = tab_ext.T.reshape(N_KV_HEADS, GROUP, N_BUCKETS + 1).transpose(1, 0, 2)
    th = jnp.tile(th.reshape(N_HEADS, N_BUCKETS + 1), (DEC_SEQ, 1))

    xp = x_prompt[0]
    xs = x_sample.transpose(1, 0, 2).reshape(S_ROWS, D_MODEL)

    conv_w = (conv_dw_w[0], conv_dw_b, conv_ln_g, conv_ln_b)


    def keys_on_lanes(cache):
        return cache[0].transpose(0, 2, 3, 1).reshape(DEC_BATCH, KV_WIDTH, WBUF)

    def keys_on_rows(win):
        return win.reshape(DEC_BATCH, N_KV_HEADS, HEAD_DIM, WBUF).transpose(0, 3, 1, 2)[None]

    def mixers_s(q, kv, z, ga, gc, h, mod_late, mix_w):
        o, kw, vw = _attn_sample(q.reshape(DEC_SEQ, DEC_BATCH, ATTN_WIDTH),
                                 kv.reshape(DEC_SEQ, DEC_BATCH, 2 * KV_WIDTH),
                                 keys_on_lanes(cache_k), keys_on_lanes(cache_v), th)
        y, ns = _conv_sample(z.reshape(DEC_SEQ, DEC_BATCH, C_CONV), state_conv[0].transpose(1, 0, 2), *conv_w)
        h = _mix_sample(o.reshape(S_ROWS, ATTN_WIDTH), y.reshape(S_ROWS, C_CONV), ga, gc, h, mod_late, *mix_w)
        return h, (kw, vw, ns)

    hs, f1 = _ffn(xs, norm1_g, mod, 0, ffn1_w1[0], ffn1_w3[0], ffn1_w2[0], per_row=True,
                  tm=S_ROWS, tf=FFN_TF_CAST, cast_out=True)
    w_in_rows = D_MODEL // (SEQ // FFN_TM)
    w_in_side = (w_in[0], (w_in_rows, W_IN_SIDE_COLS),
                 lambda i, f: (i, jnp.minimum(f, IN_WIDTH // W_IN_SIDE_COLS - 1)))
    hp, (w_in_b,) = _ffn(xp, norm1_g, mod, 0, *f1, per_row=False, tm=FFN_TM, tf=FFN_TF, side=(w_in_side,))

    qs, kv_s, z_s, ga_s, gc_s = _proj(hs, norm2_g, mod, w_in_b, per_row=True, tm=S_ROWS, q_dtype=F32)
    qp, kv_p, z_p, ga_p, gc_p = _proj(hp, norm2_g, mod, w_in_b, per_row=False, tm=PROJ_TM, q_dtype=BF16)

    side = [_row_slab_side(w[0]) for w in (ffn2_w1, ffn2_w3, ffn2_w2, w_attn_out, w_conv_out, w_out)]
    attn_p, side_b, (mod_late, mod_f) = _attn_prompt(qp, kv_p, tab_flat, sinks, side, c_all, late_ada)
    f2, mix_w = side_b[:3], side_b[3:]
    final = (final_norm_g[None, :], mod_f)

    hs, (kw, vw, ns) = mixers_s(qs, kv_s, z_s, ga_s, gc_s, hs, mod_late, mix_w)
    hp = _mix_conv_prompt(z_p, *conv_w, attn_p, ga_p, gc_p, hp, mod_late, *mix_w)

    ys = _ffn(hs, norm3_g, mod_late, 1, *f2, per_row=True, tm=S_ROWS, tf=FFN_TF, final=final)
    yp = _ffn(hp, norm3_g, mod_late, 1, *f2, per_row=False, tm=FFN_TM, tf=FFN_TF, final=final)

    w = min(WINDOW, SEQ)
    kv_shape = (1, 1, w, N_KV_HEADS, HEAD_DIM)
    k_win_p = kv_p[SEQ - w:, :KV_WIDTH].reshape(kv_shape)
    v_win_p = kv_p[SEQ - w:, KV_WIDTH:].reshape(kv_shape)
    conv_p_state = z_p[SEQ - HIST:].reshape(1, 1, HIST, C_CONV)
    s_shape = (1, DEC_BATCH, WBUF, N_KV_HEADS, HEAD_DIM)
    y_prompt = yp[None]
    y_sample = ys.reshape(DEC_SEQ, DEC_BATCH, D_MODEL).transpose(1, 0, 2)
    return (y_prompt, y_sample, k_win_p, v_win_p, conv_p_state,
            keys_on_rows(kw), keys_on_rows(vw), ns.transpose(1, 0, 2)[None])
```

```python
import functools

import numpy as np
import jax
import jax.numpy as jnp
from jax import lax
from jax.experimental import pallas as pl
from jax.experimental.pallas import tpu as pltpu

D_MODEL = 2048
SEQ = 8192
DEC_BATCH = 128
DEC_SEQ = 4
PAST_LEN = 16384
N_HEADS = 32
N_KV_HEADS = 8
HEAD_DIM = 64
GROUP = N_HEADS // N_KV_HEADS
ATTN_WIDTH = N_HEADS * HEAD_DIM
KV_WIDTH = N_KV_HEADS * HEAD_DIM
WINDOW = 128
Q_BLOCK = 128
ATTN_SCALE = HEAD_DIM ** -0.5
N_BUCKETS = 32
MAX_DISTANCE = 128
C_CONV = D_MODEL // 2
CONV_WIDTH = 31
HIST = CONV_WIDTH - 1
D_FF = 5632
NORM_EPS = 1e-6
NEG_INF = -1e30
IN_WIDTH = ATTN_WIDTH + 2 * KV_WIDTH + 2 * C_CONV + 2 * D_MODEL
WBUF = min(WINDOW, PAST_LEN)

VMEM_LIMIT = 60 * 1024 * 1024
SUBLANES = 8
LANES = 128
BF16_SUBLANES = 16

S_ROWS = DEC_BATCH * DEC_SEQ
N_COND = DEC_BATCH + SUBLANES
PROMPT_MOD_BLOCK = DEC_BATCH // SUBLANES
S_EXTRA = SUBLANES

F32 = jnp.float32
BF16 = jnp.bfloat16


def _cparams(n_axes):
    return pltpu.CompilerParams(dimension_semantics=("arbitrary",) * n_axes,
                                vmem_limit_bytes=VMEM_LIMIT)


def _dot(a, b):
    return jnp.dot(a, b, preferred_element_type=F32)


def _rms_mod(x, g, sh, sc):
    ms = jnp.mean(x * x, axis=-1, keepdims=True)
    y = x * lax.rsqrt(ms + NORM_EPS) * g
    return y * (1.0 + sc) + sh


def _mod_val(ref, per_row):
    return ref[...] if per_row else ref[0:1, :]


def _mod_spec(chunk, per_row):
    if per_row:
        return pl.BlockSpec((DEC_BATCH, D_MODEL), lambda *_: (0, chunk))
    return pl.BlockSpec((SUBLANES, D_MODEL), lambda *_: (PROMPT_MOD_BLOCK, chunk))


def _adaln_kernel(c_ref, w_ref, b_ref, o_ref, s_ref):
    @pl.when(pl.program_id(0) == 0)
    def _():
        s_ref[...] = jax.nn.silu(c_ref[...]).astype(BF16)

    o_ref[...] = _dot(s_ref[...], w_ref[...].astype(BF16)) + b_ref[...]


ADALN_TN = 1024


def _adaln(c_all, w, b, n):
    tn = ADALN_TN
    return pl.pallas_call(
        _adaln_kernel,
        out_shape=jax.ShapeDtypeStruct((N_COND, n), F32),
        grid=(n // tn,),
        in_specs=[pl.BlockSpec((N_COND, D_MODEL), lambda j: (0, 0)),
                  pl.BlockSpec((D_MODEL, tn), lambda j: (0, j)),
                  pl.BlockSpec((1, tn), lambda j: (0, j))],
        out_specs=pl.BlockSpec((N_COND, tn), lambda j: (0, j)),
        scratch_shapes=[pltpu.VMEM((N_COND, D_MODEL), BF16)],
        compiler_params=_cparams(1),
        name="adaln",
    )(c_all, w, b)


FFN_ROWS = 512
FFN_TM = 1024
FFN_TF = 512
FFN_TF_CAST = 256
PROJ_TM = 1024
W_IN_SIDE_COLS = 1024


def _side_cast(side_in, side_out):
    for i_ref, o_ref in zip(side_in, side_out):
        o_ref[...] = i_ref[...].astype(BF16)


def _side_specs(side):
    args = [a for a, _, _ in side]
    specs = [pl.BlockSpec(block, index_map) for _, block, index_map in side]
    shapes = [jax.ShapeDtypeStruct(a.shape, BF16) for a in args]
    return args, specs, shapes


def _ffn_kernel(*refs, tm, sub, per_row, n_f, final, cast_out, n_side, ring_x):
    refs = list(refs)
    x_ref, n_ref, sh_ref, sc_ref, g_ref, w1_ref, w3_ref, w2_ref = refs[:8]
    del refs[:8]
    if final:
        gf_ref, shf_ref, scf_ref = refs[:3]
        del refs[:3]
    side_in = refs[:n_side]
    del refs[:n_side]
    o_ref = refs.pop(0)
    if cast_out:
        w1o_ref, w3o_ref, w2o_ref = refs[:3]
        del refs[:3]
    side_out = refs[:n_side]
    del refs[:n_side]
    f = pl.program_id(1)
    if ring_x:
        assert not per_row
        u_ref, x_buf, x_sem, un_ref = refs
        i = pl.program_id(0)
        slot = lax.rem(i, 2)
        has_next = i + 1 < pl.num_programs(0)

        def x_copy(tile, dst_slot):
            return pltpu.make_async_copy(x_ref.at[pl.ds(tile * tm, tm), :], x_buf.at[dst_slot],
                                         x_sem.at[dst_slot])

        @pl.when((f == 0) & (i == 0))
        def _():
            x_copy(0, 0).start()
            x_copy(0, 0).wait()

        @pl.when((f == 0) & has_next)
        def _():
            x_copy(i + 1, 1 - slot).start()

        x_ref = x_buf.at[slot]
        xn_ref = x_buf.at[1 - slot]
    else:
        (u_ref,) = refs
    _side_cast(side_in, side_out)

    if cast_out:
        w1o_ref[...] = w1_ref[...].astype(BF16)
        w3o_ref[...] = w3_ref[...].astype(BF16)
        w2o_ref[...] = w2_ref[...].astype(BF16)
        w1_ref, w3_ref, w2_ref = w1o_ref, w3o_ref, w2o_ref

    def accumulate(first, last, *, pre=False, prep=False):
        for s in range(tm // FFN_ROWS):
            rows = slice(s * FFN_ROWS, (s + 1) * FFN_ROWS)
            subs = [slice(r0, r0 + sub) for r0 in range(s * FFN_ROWS, (s + 1) * FFN_ROWS, sub)]
            if first and pre and s == 0:
                u = un_ref[...]
                u_ref[rows, :] = u
            else:
                if first:
                    for r in subs:
                        u_ref[r, :] = _rms_mod(x_ref[r, :], n_ref[...], _mod_val(sh_ref, per_row),
                                               _mod_val(sc_ref, per_row)).astype(BF16)
                u = u_ref[rows, :]
            if prep and s == 0:
                for r in subs:
                    un_ref[r, :] = _rms_mod(xn_ref[r, :], n_ref[...], _mod_val(sh_ref, per_row),
                                            _mod_val(sc_ref, per_row)).astype(BF16)
            h1 = _dot(u, w1_ref[...])
            h3 = _dot(u, w3_ref[...])
            a = (jax.nn.silu(h1) * h3).astype(BF16)
            d = _dot(a, w2_ref[...])
            if first:
                o_ref[rows, :] = d
            else:
                o_ref[rows, :] += d
            if last:
                for r in subs:
                    h = x_ref[r, :] + 0.5 * _mod_val(g_ref, per_row) * o_ref[r, :]
                    if final:
                        h = _rms_mod(h, gf_ref[...], _mod_val(shf_ref, per_row), _mod_val(scf_ref, per_row))
                    o_ref[r, :] = h

    assert n_f > 3

    if ring_x:
        @pl.when((f == 0) & (i == 0))
        def _():
            accumulate(True, False)

        @pl.when((f == 0) & (i > 0))
        def _():
            accumulate(True, False, pre=True)

        @pl.when((f > 0) & (f < n_f - 1) & ~((f == n_f - 2) & has_next))
        def _():
            accumulate(False, False)

        @pl.when((f == n_f - 2) & has_next)
        def _():
            x_copy(i + 1, 1 - slot).wait()
            accumulate(False, False, prep=True)
    else:
        @pl.when(f == 0)
        def _():
            accumulate(True, False)

        @pl.when((f > 0) & (f < n_f - 1))
        def _():
            accumulate(False, False)

    @pl.when(f == n_f - 1)
    def _():
        accumulate(False, True)


def _ffn(x, norm_g, mod, chunk0, w1, w3, w2, *, per_row, tm, tf, final=None, cast_out=False, side=()):
    rows = x.shape[0]
    n_f = D_FF // tf
    sub = DEC_BATCH
    row_spec = pl.BlockSpec((tm, D_MODEL), lambda i, f: (i, 0))
    vec_spec = pl.BlockSpec((1, D_MODEL), lambda i, f: (0, 0))
    w_specs = [pl.BlockSpec((D_MODEL, tf), lambda i, f: (0, f)),
               pl.BlockSpec((D_MODEL, tf), lambda i, f: (0, f)),
               pl.BlockSpec((tf, D_MODEL), lambda i, f: (f, 0))]
    ring_x = rows > tm
    x_spec = pl.BlockSpec(memory_space=pl.ANY) if ring_x else row_spec
    in_specs = [x_spec, vec_spec,
                _mod_spec(chunk0, per_row), _mod_spec(chunk0 + 1, per_row), _mod_spec(chunk0 + 2, per_row)]
    in_specs += w_specs
    args = [x, norm_g, mod, mod, mod, w1, w3, w2]
    if final is not None:
        gf, mod_f = final
        in_specs += [vec_spec, _mod_spec(0, per_row), _mod_spec(1, per_row)]
        args += [gf, mod_f, mod_f]
    side_args, side_specs, side_shapes = _side_specs(side)
    in_specs += side_specs
    args += side_args
    out_shape = [jax.ShapeDtypeStruct((rows, D_MODEL), F32)]
    out_specs = [row_spec]
    if cast_out:
        assert rows == tm, "every weight block must be visited exactly once"
        out_shape += [jax.ShapeDtypeStruct(w.shape, BF16) for w in (w1, w3, w2)]
        out_specs += w_specs
    out_shape += side_shapes
    out_specs += side_specs
    outs = pl.pallas_call(
        functools.partial(_ffn_kernel, tm=tm, sub=sub, per_row=per_row, n_f=n_f,
                          final=final is not None, cast_out=cast_out, n_side=len(side), ring_x=ring_x),
        out_shape=out_shape,
        grid=(rows // tm, n_f),
        in_specs=in_specs,
        out_specs=out_specs,
        scratch_shapes=[pltpu.VMEM((tm, D_MODEL), BF16)]
        + ([pltpu.VMEM((2, tm, D_MODEL), F32), pltpu.SemaphoreType.DMA((2,)),
            pltpu.VMEM((FFN_ROWS, D_MODEL), BF16)] if ring_x else []),
        compiler_params=_cparams(2),
        name="ffn_final" if final is not None else "ffn",
    )(*args)
    return (outs[0], tuple(outs[1:])) if (cast_out or side) else outs[0]


PROJ_CHUNK = 256
PROJ_NORM_ROWS = 512
PROJ_CW = 512


def _proj_layout():
    cw = PROJ_CW
    n_q = ATTN_WIDTH // (2 * cw)
    n_kv = 2 * KV_WIDTH // (2 * cw)
    n_glu = C_CONV // cw
    n_gate = D_MODEL // (2 * cw)
    pre = n_q + n_kv
    steps = pre + n_glu + 2 * n_gate

    def block_a(j):
        return jnp.where((j >= pre) & (j < pre + n_glu), j + pre, 2 * j)

    def block_b(j):
        return jnp.where((j >= pre) & (j < pre + n_glu), j + pre + n_glu, 2 * j + 1)

    return n_q, n_kv, n_glu, n_gate, steps, block_a, block_b


def _proj_kernel(h_ref, n_ref, sh_ref, sc_ref, wa_ref, wb_ref, q_ref, kv_ref, z_ref, ga_ref, gc_ref, u_ref,
                 *, tm, sub, per_row):
    cw = PROJ_CW
    n_q, n_kv, n_glu, n_gate, _, _, _ = _proj_layout()
    j = pl.program_id(1)

    n_chunks = cw // PROJ_CHUNK

    def chunks(w_ref, rows=slice(None)):
        for c in range(n_chunks):
            cols = slice(c * PROJ_CHUNK, (c + 1) * PROJ_CHUNK)
            yield c * PROJ_CHUNK, _dot(u_ref[rows, :], w_ref[:, cols])

    def pair_store(o_ref, fn, rows=slice(None)):
        for half, w_ref in enumerate((wa_ref, wb_ref)):
            for c0, acc in chunks(w_ref, rows):
                o_ref[rows, half * cw + c0:half * cw + c0 + PROJ_CHUNK] = fn(acc).astype(o_ref.dtype)

    @pl.when(j == 0)
    def _():
        for g0 in range(0, tm, PROJ_NORM_ROWS):
            for r0 in range(g0, g0 + PROJ_NORM_ROWS, sub):
                r = slice(r0, r0 + sub)
                u_ref[r, :] = _rms_mod(h_ref[r, :], n_ref[...], _mod_val(sh_ref, per_row),
                                       _mod_val(sc_ref, per_row)).astype(BF16)
            pair_store(q_ref, lambda acc: acc, slice(g0, g0 + PROJ_NORM_ROWS))

    lo = 0

    @pl.when((j > 0) & (j < n_q))
    def _():
        pair_store(q_ref, lambda acc: acc)

    lo += n_q

    @pl.when((j >= lo) & (j < lo + n_kv))
    def _():
        pair_store(kv_ref, lambda acc: acc)

    lo += n_kv

    @pl.when((j >= lo) & (j < lo + n_glu))
    def _():
        for (c0, a), (_, g) in zip(chunks(wa_ref), chunks(wb_ref)):
            z_ref[:, c0:c0 + PROJ_CHUNK] = a * jax.nn.sigmoid(g)

    lo += n_glu

    @pl.when((j >= lo) & (j < lo + n_gate))
    def _():
        pair_store(ga_ref, jax.nn.sigmoid)

    lo += n_gate

    @pl.when(j >= lo)
    def _():
        pair_store(gc_ref, jax.nn.sigmoid)


def _proj(h, norm_g, mod, w, *, per_row, tm, q_dtype):
    rows = h.shape[0]
    sub = DEC_BATCH
    cw = PROJ_CW
    n_q, n_kv, n_glu, n_gate, steps, block_a, block_b = _proj_layout()

    def out_spec(width, first, count):
        return pl.BlockSpec((tm, width), lambda i, j: (i, jnp.clip(j - first, 0, count - 1)))

    wa_spec = pl.BlockSpec((D_MODEL, cw), lambda i, j: (0, block_a(j)))
    wb_spec = pl.BlockSpec((D_MODEL, cw), lambda i, j: (0, block_b(j)))
    out_shape = [jax.ShapeDtypeStruct((rows, ATTN_WIDTH), q_dtype),
                 jax.ShapeDtypeStruct((rows, 2 * KV_WIDTH), F32),
                 jax.ShapeDtypeStruct((rows, C_CONV), F32),
                 jax.ShapeDtypeStruct((rows, D_MODEL), BF16),
                 jax.ShapeDtypeStruct((rows, D_MODEL), BF16)]
    out_specs = [out_spec(2 * cw, 0, n_q),
                 out_spec(2 * cw, n_q, n_kv),
                 out_spec(cw, n_q + n_kv, n_glu),
                 out_spec(2 * cw, n_q + n_kv + n_glu, n_gate),
                 out_spec(2 * cw, n_q + n_kv + n_glu + n_gate, n_gate)]
    return pl.pallas_call(
        functools.partial(_proj_kernel, tm=tm, sub=sub, per_row=per_row),
        out_shape=out_shape,
        grid=(rows // tm, steps),
        in_specs=[pl.BlockSpec((tm, D_MODEL), lambda i, j: (i, 0)),
                  pl.BlockSpec((1, D_MODEL), lambda i, j: (0, 0)),
                  _mod_spec(3, per_row), _mod_spec(4, per_row), wa_spec, wb_spec],
        out_specs=out_specs,
        scratch_shapes=[pltpu.VMEM((tm, D_MODEL), BF16)],
        compiler_params=_cparams(2),
        name="proj",
    )(h, norm_g, mod, mod, w, w)


def _t5_bucket_np(dist):
    exact = N_BUCKETS // 2
    d = np.maximum(dist, 0)
    df = np.maximum(d, 1).astype(np.float32)
    large = exact + (np.log(df / np.float32(exact)) / np.float32(np.log(MAX_DISTANCE / exact))
                     * np.float32(N_BUCKETS - exact)).astype(np.int32)
    large = np.minimum(large, N_BUCKETS - 1)
    return np.where(d < exact, d, large).astype(np.int32)


def _prompt_codes():
    qi = np.arange(Q_BLOCK)[:, None]
    kj = np.arange(2 * Q_BLOCK)[None, :]
    dist = qi + Q_BLOCK - kj
    valid = (dist >= 0) & (dist <= WINDOW)
    return np.where(valid, _t5_bucket_np(dist), -1).astype(np.int32)


def _sample_codes():
    t = np.repeat(np.arange(DEC_SEQ), GROUP * N_KV_HEADS)[:, None]
    j = np.arange(WBUF)[None, :]
    dist = t + WBUF - j
    cache = np.where((dist >= 0) & (dist <= WINDOW), _t5_bucket_np(dist), -1)
    c = np.arange(S_EXTRA)[None, :]
    t_new = c - (S_EXTRA - DEC_SEQ)
    dist = t - t_new
    extra = np.where((t_new >= 0) & (dist >= 0) & (dist <= WINDOW), _t5_bucket_np(dist), -1)
    extra = np.where(c == 0, N_BUCKETS, extra)
    return cache.astype(np.int32), extra.astype(np.int32)


BIAS_ROWS = 64
ATTN_QB = 2


def _attn_p_kernel(*refs, n_side, n_ada):
    refs = list(refs)
    tab_ref, sink_ref, code_ref, q_ref, kvc_ref, kvp_ref = refs[:6]
    del refs[:6]
    side_in = refs[:n_side]
    del refs[:n_side]
    c_ref = refs.pop(0)
    ada_in = [(refs[2 * k], refs[2 * k + 1]) for k in range(n_ada)]
    del refs[:2 * n_ada]
    o_ref = refs.pop(0)
    side_out = refs[:n_side]
    del refs[:n_side]
    ada_out = refs[:n_ada]
    del refs[:n_ada]
    bias_ref, silu_c_ref = refs
    n = pl.program_id(0)
    _side_cast(side_in, side_out)

    @pl.when(n == 0)
    def _():
        silu_c_ref[...] = jax.nn.silu(c_ref[...]).astype(BF16)

    for (w_ref, b_ref), m_ref in zip(ada_in, ada_out):
        m_ref[...] = _dot(silu_c_ref[...], w_ref[...].astype(BF16)) + b_ref[...]

    n_keys = 2 * Q_BLOCK
    wide = GROUP * Q_BLOCK

    @pl.when(n == 0)
    def _():
        def kv_body(kv, carry):
            for g in range(GROUP):
                for r in range(n_keys // BIAS_ROWS):
                    rows = slice(r * BIAS_ROWS, (r + 1) * BIAS_ROWS)
                    code = code_ref[rows, :]
                    acc = jnp.full((BIAS_ROWS, Q_BLOCK), NEG_INF, F32)
                    for bk in range(N_BUCKETS):
                        acc = jnp.where(code == bk, tab_ref[bk * N_HEADS + kv * GROUP + g], acc)
                    bias_ref[kv, rows, g * Q_BLOCK:(g + 1) * Q_BLOCK] = acc
            return carry
        lax.fori_loop(0, N_KV_HEADS, kv_body, 0)

    ones = jnp.ones((BF16_SUBLANES, n_keys), BF16)
    zeros = jnp.zeros((HEAD_DIM, wide), BF16)
    tile = (n_keys, LANES)
    key = lax.broadcasted_iota(jnp.int32, tile, 0)
    lane = lax.broadcasted_iota(jnp.int32, tile, 1)
    pen = jnp.where((lane == 0) & (key < Q_BLOCK) & (n == 0), NEG_INF, 0.0).astype(BF16)
    pen_rows = (lax.broadcasted_iota(jnp.int32, (LANES, wide), 0) == 0).astype(BF16)

    def query_block(a):
        rows = slice(a * Q_BLOCK, (a + 1) * Q_BLOCK)
        q_t = q_ref[rows, :].astype(F32).T

        def keys(cols):
            prev = kvp_ref[:, cols] if a == 0 else kvc_ref[(a - 1) * Q_BLOCK:a * Q_BLOCK, cols]
            return jnp.concatenate([prev, kvc_ref[rows, cols]], axis=0)

        def pair_kv(pair):
            k2 = keys(slice(pair * LANES, (pair + 1) * LANES)).astype(BF16)
            if a == 0:
                k2 = jnp.concatenate([k2, pen], axis=1)
            v2_t = keys(slice(KV_WIDTH + pair * LANES, KV_WIDTH + (pair + 1) * LANES)).T.astype(BF16)
            return k2, v2_t

        def scores(kv, k2):
            q4 = jnp.concatenate(
                [q_t[(kv * GROUP + g) * HEAD_DIM:(kv * GROUP + g + 1) * HEAD_DIM, :] for g in range(GROUP)],
                axis=1)
            q4 = (q4 * ATTN_SCALE).astype(BF16)
            parts = [q4, zeros] if kv % 2 == 0 else [zeros, q4]
            return _dot(k2, jnp.concatenate(parts + ([pen_rows] if a == 0 else []), axis=0))

        kvs = [pair_kv(pair) for pair in range(N_KV_HEADS // 2)]
        for kv in range(N_KV_HEADS):
            pair, parity = divmod(kv, 2)
            s = scores(kv, kvs[pair][0]) + bias_ref[kv]
            sink = jnp.concatenate(
                [jnp.full((1, Q_BLOCK), sink_ref[kv * GROUP + g], F32) for g in range(GROUP)], axis=1)
            m = jnp.maximum(jnp.max(s, axis=0, keepdims=True), sink)
            p = jnp.exp(s - m).astype(BF16)
            lhs = jnp.concatenate([kvs[pair][1][parity * HEAD_DIM:(parity + 1) * HEAD_DIM, :], ones], axis=0)
            o_t = _dot(lhs, p)
            l = o_t[HEAD_DIM:HEAD_DIM + 1, :] + jnp.exp(sink - m)
            o_n = o_t[0:HEAD_DIM, :] * (1.0 / l)
            o_kv = jnp.concatenate([o_n[:, g * Q_BLOCK:(g + 1) * Q_BLOCK] for g in range(GROUP)], axis=0)
            o_ref[rows, kv * GROUP * HEAD_DIM:(kv + 1) * GROUP * HEAD_DIM] = o_kv.T.astype(o_ref.dtype)

    for a in range(ATTN_QB):
        query_block(a)


ATTN_STEPS = SEQ // (ATTN_QB * Q_BLOCK)


def _row_slab_side(w):
    slab = w.shape[0] // ATTN_STEPS
    assert slab * ATTN_STEPS == w.shape[0] and slab % BF16_SUBLANES == 0
    return (w, (slab, w.shape[1]), lambda n: (n, 0))


def _attn_prompt(q, kv, tab_flat, sinks, side, c_all, ada):
    step_rows = ATTN_QB * Q_BLOCK
    code_t = jnp.asarray(np.ascontiguousarray(_prompt_codes().T))
    smem = pl.BlockSpec(memory_space=pltpu.SMEM)
    side_args, side_specs, side_shapes = _side_specs(side)
    ada_args, ada_in_specs, ada_out_specs, ada_shapes = [], [], [], []
    for w, b, col0, n_cols in ada:
        tn = n_cols // ATTN_STEPS
        assert tn * ATTN_STEPS == n_cols and tn % LANES == 0 and col0 % tn == 0
        first = col0 // tn
        ada_args += [w, b]
        ada_in_specs += [pl.BlockSpec((D_MODEL, tn), lambda n, first=first: (0, first + n)),
                         pl.BlockSpec((1, tn), lambda n, first=first: (0, first + n))]
        ada_out_specs.append(pl.BlockSpec((N_COND, tn), lambda n: (0, n)))
        ada_shapes.append(jax.ShapeDtypeStruct((N_COND, n_cols), F32))
    out_spec = pl.BlockSpec((step_rows, ATTN_WIDTH), lambda n: (n, 0))
    outs = pl.pallas_call(
        functools.partial(_attn_p_kernel, n_side=len(side), n_ada=len(ada)),
        out_shape=[jax.ShapeDtypeStruct((SEQ, ATTN_WIDTH), BF16)] + side_shapes + ada_shapes,
        grid=(ATTN_STEPS,),
        in_specs=[smem, smem,
                  pl.BlockSpec((2 * Q_BLOCK, Q_BLOCK), lambda n: (0, 0)),
                  pl.BlockSpec((step_rows, ATTN_WIDTH), lambda n: (n, 0)),
                  pl.BlockSpec((step_rows, 2 * KV_WIDTH), lambda n: (n, 0)),
                  pl.BlockSpec((Q_BLOCK, 2 * KV_WIDTH), lambda n: (jnp.maximum(ATTN_QB * n - 1, 0), 0))]
        + side_specs + [pl.BlockSpec((N_COND, D_MODEL), lambda n: (0, 0))] + ada_in_specs,
        out_specs=[out_spec] + side_specs + ada_out_specs,
        scratch_shapes=[pltpu.VMEM((N_KV_HEADS, 2 * Q_BLOCK, GROUP * Q_BLOCK), F32),
                        pltpu.VMEM((N_COND, D_MODEL), BF16)],
        compiler_params=_cparams(1),
        name="attn_prompt",
    )(tab_flat, sinks, code_t, q, kv, kv, *side_args, c_all, *ada_args)
    n_side = len(side)
    return outs[0], tuple(outs[1:1 + n_side]), tuple(outs[1 + n_side:])


S_BB = 8


def _regroup_heads(x, to_group_major):
    half = HEAD_DIM
    assert LANES == 2 * half
    low = lax.broadcasted_iota(jnp.int32, (x.shape[0], LANES), 1) < half
    n_cols = ATTN_WIDTH // LANES
    outer, inner = (GROUP, N_KV_HEADS) if to_group_major else (N_KV_HEADS, GROUP)
    cols = []
    for c in range(n_cols):
        o, i = divmod(2 * c, inner)
        src = [(i + d) * outer + o for d in range(2)]
        a = x[:, (src[0] // 2) * LANES:(src[0] // 2 + 1) * LANES]
        b = x[:, (src[1] // 2) * LANES:(src[1] // 2 + 1) * LANES]
        assert src[0] % 2 == src[1] % 2
        if src[0] % 2 == 0:
            cols.append(jnp.where(low, a, pltpu.roll(b, half, axis=1)))
        else:
            cols.append(jnp.where(low, pltpu.roll(a, half, axis=1), b))
    return jnp.concatenate(cols, axis=1)


def _attn_s_kernel(th_ref, code_c_ref, code_x_ref, q_ref, kv_ref, ck_ref, cv_ref, o_ref, kw_ref, vw_ref,
                   bias_c_ref, bias_x_ref, nk_ref, nv_ref, qe_ref, qg_ref, og_ref):
    rows_all = DEC_SEQ * N_HEADS
    for t in range(DEC_SEQ):
        qg_ref[t] = _regroup_heads(q_ref[t], True)

    @pl.when(pl.program_id(0) == 0)
    def _():
        for code_ref, bias_ref in ((code_c_ref, bias_c_ref), (code_x_ref, bias_x_ref)):
            code = code_ref[...]
            acc = jnp.full(code.shape, NEG_INF, F32)
            for bk in range(N_BUCKETS + 1):
                acc = jnp.where(code == bk, th_ref[:, bk:bk + 1], acc)
            bias_ref[...] = acc
        nk_ref[...] = jnp.zeros_like(nk_ref)
        nv_ref[...] = jnp.zeros_like(nv_ref)

    lane_head = lax.broadcasted_iota(jnp.int32, (N_KV_HEADS, KV_WIDTH), 1) // HEAD_DIM
    row_head = lax.broadcasted_iota(jnp.int32, (N_KV_HEADS, KV_WIDTH), 0)
    diag = lane_head == row_head
    out_mask = (lax.broadcasted_iota(jnp.int32, (rows_all, KV_WIDTH), 0) % N_KV_HEADS
                == lax.broadcasted_iota(jnp.int32, (rows_all, KV_WIDTH), 1) // HEAD_DIM)
    new_lanes = lax.broadcasted_iota(jnp.int32, (KV_WIDTH, WBUF), 1) >= WBUF - DEC_SEQ
    bias_c = bias_c_ref[...]
    bias_x = bias_x_ref[...]
    nt_dims = (((1,), (1,)), ((), ()))
    x0 = WBUF - S_EXTRA

    for b in range(S_BB):
        for t in range(DEC_SEQ):
            nk_ref[WBUF - DEC_SEQ + t:WBUF - DEC_SEQ + t + 1, :] = kv_ref[t, b:b + 1, 0:KV_WIDTH]
            nv_ref[WBUF - DEC_SEQ + t:WBUF - DEC_SEQ + t + 1, :] = kv_ref[t, b:b + 1, KV_WIDTH:2 * KV_WIDTH]
            for g in range(GROUP):
                q_row = qg_ref[t, b:b + 1, g * KV_WIDTH:(g + 1) * KV_WIDTH]
                piece = jnp.where(diag, jnp.broadcast_to(q_row, (N_KV_HEADS, KV_WIDTH)), 0.0)
                r0 = (t * GROUP + g) * N_KV_HEADS
                qe_ref[r0:r0 + N_KV_HEADS, :] = piece
        k_t = ck_ref[b]
        v_t = cv_ref[b]
        kw_ref[b] = jnp.where(new_lanes, nk_ref[...].T, pltpu.roll(k_t, WBUF - DEC_SEQ, axis=1))
        vw_ref[b] = jnp.where(new_lanes, nv_ref[...].T, pltpu.roll(v_t, WBUF - DEC_SEQ, axis=1))

        qe = (qe_ref[...] * ATTN_SCALE).astype(BF16)
        s_c = _dot(qe, k_t.astype(BF16)) + bias_c
        s_x = lax.dot_general(qe, nk_ref[x0:WBUF, :].astype(BF16), nt_dims,
                              preferred_element_type=F32) + bias_x
        m = jnp.maximum(jnp.max(s_c, axis=-1, keepdims=True), jnp.max(s_x, axis=-1, keepdims=True))
        p_c = jnp.exp(s_c - m)
        p_x = jnp.exp(s_x - m)
        l = jnp.sum(p_c, axis=-1, keepdims=True) + jnp.sum(p_x, axis=-1, keepdims=True)
        o = lax.dot_general(p_c.astype(BF16), v_t.astype(BF16), nt_dims, preferred_element_type=F32)
        o = (o + _dot(p_x.astype(BF16), nv_ref[x0:WBUF, :].astype(BF16))) / l
        o = jnp.where(out_mask, o, 0.0)
        for t in range(DEC_SEQ):
            for g in range(GROUP):
                r0 = (t * GROUP + g) * N_KV_HEADS
                row = jnp.sum(o[r0:r0 + N_KV_HEADS, :], axis=0, keepdims=True)
                og_ref[t, b:b + 1, g * KV_WIDTH:(g + 1) * KV_WIDTH] = row
    for t in range(DEC_SEQ):
        o_ref[t] = _regroup_heads(og_ref[t], False)


def _attn_sample(q, kv, cache_k_t, cache_v_t, th):
    code_c, code_x = (jnp.asarray(c) for c in _sample_codes())
    rows_all = DEC_SEQ * N_HEADS
    full = lambda a: pl.BlockSpec(a.shape, lambda i: (0,) * a.ndim)
    tb_spec = lambda w: pl.BlockSpec((DEC_SEQ, S_BB, w), lambda i: (0, i, 0))
    cache_spec = pl.BlockSpec((S_BB, KV_WIDTH, WBUF), lambda i: (i, 0, 0))
    cache_shape = jax.ShapeDtypeStruct((DEC_BATCH, KV_WIDTH, WBUF), F32)
    return pl.pallas_call(
        _attn_s_kernel,
        out_shape=(jax.ShapeDtypeStruct((DEC_SEQ, DEC_BATCH, ATTN_WIDTH), F32), cache_shape, cache_shape),
        grid=(DEC_BATCH // S_BB,),
        in_specs=[full(th), full(code_c), full(code_x),
                  tb_spec(ATTN_WIDTH), tb_spec(2 * KV_WIDTH), cache_spec, cache_spec],
        out_specs=(tb_spec(ATTN_WIDTH), cache_spec, cache_spec),
        scratch_shapes=[pltpu.VMEM((rows_all, WBUF), F32),
                        pltpu.VMEM((rows_all, S_EXTRA), F32),
                        pltpu.VMEM((WBUF, KV_WIDTH), F32),
                        pltpu.VMEM((WBUF, KV_WIDTH), F32),
                        pltpu.VMEM((rows_all, KV_WIDTH), F32),
                        pltpu.VMEM((DEC_SEQ, S_BB, ATTN_WIDTH), F32),
                        pltpu.VMEM((DEC_SEQ, S_BB, ATTN_WIDTH), F32)],
        compiler_params=_cparams(1),
        name="attn_sample",
    )(th, code_c, code_x, q, kv, cache_k_t, cache_v_t)


CONV_RC = 32
CONV_ROWS = 128
CONV_HALO = 32


def _ln_silu(acc, lg, lb):
    mu = jnp.mean(acc, axis=-1, keepdims=True)
    xc = acc - mu
    var = jnp.mean(xc * xc, axis=-1, keepdims=True)
    return jax.nn.silu(xc * lax.rsqrt(var + NORM_EPS) * lg + lb)


def _conv_stage(first, zc_ref, zh_ref, s_ref):
    s_ref[0:CONV_HALO, :] = jnp.where(first, 0.0, zh_ref[...])
    s_ref[CONV_HALO:, :] = zc_ref[...]


def _conv_rows(rc, w_ref, b_ref, lg_ref, lb_ref, y_ref, s_ref, c_ref):
    off = CONV_HALO - HIST
    groups = [[j for j in range(CONV_WIDTH) if (j + off) % SUBLANES == r] for r in range(SUBLANES)]
    t0 = rc * CONV_ROWS
    for lc in range(C_CONV // LANES):
        lanes = slice(lc * LANES, (lc + 1) * LANES)
        out = jnp.broadcast_to(b_ref[:, lanes], (CONV_ROWS, LANES))
        for r, taps in enumerate(groups):
            n_rows = CONV_ROWS + (SUBLANES if r else 0)
            part = None
            for j in taps:
                base = t0 + (j + off) - r
                term = w_ref[j:j + 1, lanes] * s_ref[base:base + n_rows, lanes]
                part = term if part is None else part + term
            out = out + part[r:r + CONV_ROWS, :]
        c_ref[t0:t0 + CONV_ROWS, lanes] = out
    for r in range(CONV_ROWS // CONV_RC):
        rows = slice(t0 + r * CONV_RC, t0 + (r + 1) * CONV_RC)
        y_ref[rows, :] = _ln_silu(c_ref[rows, :], lg_ref[...], lb_ref[...]).astype(BF16)


def _mix_conv_kernel(zc_ref, zh_ref, w_ref, b_ref, lg_ref, lb_ref,
                     at_ref, ga_ref, gc_ref, h_ref, g2_ref, wao_ref, wco_ref, wout_ref,
                     o_ref, s_ref, c_ref, y_ref, *, tm):
    _conv_stage(pl.program_id(0) == 0, zc_ref, zh_ref, s_ref)
    for rc in range(tm // CONV_ROWS):
        _conv_rows(rc, w_ref, b_ref, lg_ref, lb_ref, y_ref, s_ref, c_ref)
    a = _dot(at_ref[...], wao_ref[...])
    c = _dot(y_ref[...], wco_ref[...])
    merged = (ga_ref[...].astype(F32) * a + gc_ref[...].astype(F32) * c).astype(BF16)
    r = _dot(merged, wout_ref[...])
    o_ref[...] = h_ref[...] + g2_ref[0:1, :] * r


MIX_TM = 256


def _mix_conv_prompt(z, dw_w, dw_b, ln_g, ln_b, attn, ga, gc, h, mod, w_ao, w_co, w_out):
    rows = h.shape[0]
    tm = MIX_TM
    n = rows // tm
    ratio = tm // CONV_HALO
    vec = pl.BlockSpec((1, C_CONV), lambda s: (0, 0))
    row = lambda w: pl.BlockSpec((tm, w), lambda s: (s, 0))
    resident = lambda shape: pl.BlockSpec(shape, lambda s: (0, 0), pipeline_mode=pl.Buffered(1))
    return pl.pallas_call(
        functools.partial(_mix_conv_kernel, tm=tm),
        out_shape=jax.ShapeDtypeStruct((rows, D_MODEL), F32),
        grid=(n,),
        in_specs=[row(C_CONV),
                  pl.BlockSpec((CONV_HALO, C_CONV), lambda s: (jnp.maximum(s * ratio - 1, 0), 0)),
                  pl.BlockSpec((CONV_WIDTH, C_CONV), lambda s: (0, 0)), vec, vec, vec,
                  row(ATTN_WIDTH), row(D_MODEL), row(D_MODEL), row(D_MODEL),
                  _mod_spec(0, False),
                  resident((ATTN_WIDTH, D_MODEL)), resident((C_CONV, D_MODEL)), resident((D_MODEL, D_MODEL))],
        out_specs=row(D_MODEL),
        scratch_shapes=[pltpu.VMEM((tm + CONV_HALO, C_CONV), F32), pltpu.VMEM((tm, C_CONV), F32),
                        pltpu.VMEM((tm, C_CONV), BF16)],
        compiler_params=_cparams(1),
        name="mix_conv",
    )(z, z, dw_w, dw_b, ln_g, ln_b, attn, ga, gc, h, mod, w_ao, w_co, w_out)


CONV_S_BB = 32


def _conv_s_kernel(z_ref, hist_ref, w_ref, b_ref, lg_ref, lb_ref, y_ref, ns_ref):
    for t in range(DEC_SEQ):
        acc = jnp.broadcast_to(b_ref[...], (CONV_S_BB, C_CONV))
        for j in range(CONV_WIDTH):
            i = t + j
            src = hist_ref[i] if i < HIST else z_ref[i - HIST]
            acc = acc + w_ref[j:j + 1, :] * src
        y_ref[t] = _ln_silu(acc, lg_ref[...], lb_ref[...]).astype(BF16)
    keep = HIST - DEC_SEQ
    ns_ref[0:keep] = hist_ref[DEC_SEQ:HIST]
    ns_ref[keep:HIST] = z_ref[...]


def _conv_sample(z, hist_t, dw_w, dw_b, ln_g, ln_b):
    vec = pl.BlockSpec((1, C_CONV), lambda i: (0, 0))
    tb = pl.BlockSpec((DEC_SEQ, CONV_S_BB, C_CONV), lambda i: (0, i, 0))
    st = pl.BlockSpec((HIST, CONV_S_BB, C_CONV), lambda i: (0, i, 0))
    return pl.pallas_call(
        _conv_s_kernel,
        out_shape=(jax.ShapeDtypeStruct((DEC_SEQ, DEC_BATCH, C_CONV), BF16),
                   jax.ShapeDtypeStruct(hist_t.shape, F32)),
        grid=(DEC_BATCH // CONV_S_BB,),
        in_specs=[tb, st, pl.BlockSpec((CONV_WIDTH, C_CONV), lambda i: (0, 0)), vec, vec, vec],
        out_specs=(tb, st),
        compiler_params=_cparams(1),
        name="conv_sample",
    )(z, hist_t, dw_w, dw_b, ln_g, ln_b)


def _mix_s_kernel(at_ref, y_ref, ga_ref, gc_ref, h_ref, g2_ref, wao_ref, wco_ref, wout_ref, o_ref):
    a = _dot(at_ref[...].astype(BF16), wao_ref[...])
    c = _dot(y_ref[...], wco_ref[...])
    merged = (ga_ref[...].astype(F32) * a + gc_ref[...].astype(F32) * c).astype(BF16)
    r = _dot(merged, wout_ref[...])
    o_ref[...] = h_ref[...] + g2_ref[...] * r


def _mix_sample(attn, y, ga, gc, h, mod, w_ao, w_co, w_out):
    row = lambda w: pl.BlockSpec((DEC_BATCH, w), lambda t: (t, 0))
    resident = lambda shape: pl.BlockSpec(shape, lambda t: (0, 0), pipeline_mode=pl.Buffered(1))
    return pl.pallas_call(
        _mix_s_kernel,
        out_shape=jax.ShapeDtypeStruct((S_ROWS, D_MODEL), F32),
        grid=(DEC_SEQ,),
        in_specs=[row(ATTN_WIDTH), row(C_CONV), row(D_MODEL), row(D_MODEL), row(D_MODEL),
                  _mod_spec(0, True),
                  resident((ATTN_WIDTH, D_MODEL)), resident((C_CONV, D_MODEL)), resident((D_MODEL, D_MODEL))],
        out_specs=row(D_MODEL),
        compiler_params=_cparams(1),
        name="mix_sample",
    )(attn, y, ga, gc, h, mod, w_ao, w_co, w_out)


def kernel(x_prompt, x_sample, cache_k, cache_v, state_conv, c_prompt, c_sample, rel_bias_table, norm1_g, ffn1_w1, ffn1_w3, ffn1_w2, norm2_g, w_in, attn_sinks, conv_dw_w, conv_dw_b, conv_ln_g, conv_ln_b, w_conv_out, w_attn_out, w_out, norm3_g, ffn2_w1, ffn2_w3, ffn2_w2, w_ada, b_ada, final_norm_g, w_ada_final, b_ada_final):
    c_all = jnp.concatenate([c_sample, c_prompt, jnp.zeros((N_COND - DEC_BATCH - 1, D_MODEL), F32)], axis=0)
    early = 5 * D_MODEL
    mod = _adaln(c_all, w_ada[0], b_ada, early)
    late_ada = [(w_ada[0], b_ada, early, 4 * D_MODEL), (w_ada_final, b_ada_final[None, :], 0, 2 * D_MODEL)]

    sinks = attn_sinks[0]
    tab_flat = rel_bias_table.reshape(-1)
    tab_ext = jnp.concatenate([rel_bias_table, sinks[None, :]], axis=0)
    th = tab_ext.T.reshape(N_KV_HEADS, GROUP, N_BUCKETS + 1).transpose(1, 0, 2)
    th = jnp.tile(th.reshape(N_HEADS, N_BUCKETS + 1), (DEC_SEQ, 1))

    xp = x_prompt[0]
    xs = x_sample.transpose(1, 0, 2).reshape(S_ROWS, D_MODEL)

    conv_w = (conv_dw_w[0], conv_dw_b, conv_ln_g, conv_ln_b)


    def keys_on_lanes(cache):
        return cache[0].transpose(0, 2, 3, 1).reshape(DEC_BATCH, KV_WIDTH, WBUF)

    def keys_on_rows(win):
        return win.reshape(DEC_BATCH, N_KV_HEADS, HEAD_DIM, WBUF).transpose(0, 3, 1, 2)[None]

    def mixers_s(q, kv, z, ga, gc, h, mod_late, mix_w):
        o, kw, vw = _attn_sample(q.reshape(DEC_SEQ, DEC_BATCH, ATTN_WIDTH),
                                 kv.reshape(DEC_SEQ, DEC_BATCH, 2 * KV_WIDTH),
                                 keys_on_lanes(cache_k), keys_on_lanes(cache_v), th)
        y, ns = _conv_sample(z.reshape(DEC_SEQ, DEC_BATCH, C_CONV), state_conv[0].transpose(1, 0, 2), *conv_w)
        h = _mix_sample(o.reshape(S_ROWS, ATTN_WIDTH), y.reshape(S_ROWS, C_CONV), ga, gc, h, mod_late, *mix_w)
        return h, (kw, vw, ns)

    hs, f1 = _ffn(xs, norm1_g, mod, 0, ffn1_w1[0], ffn1_w3[0], ffn1_w2[0], per_row=True,
                  tm=S_ROWS, tf=FFN_TF_CAST, cast_out=True)
    w_in_rows = D_MODEL // (SEQ // FFN_TM)
    w_in_side = (w_in[0], (w_in_rows, W_IN_SIDE_COLS),
                 lambda i, f: (i, jnp.minimum(f, IN_WIDTH // W_IN_SIDE_COLS - 1)))
    hp, (w_in_b,) = _ffn(xp, norm1_g, mod, 0, *f1, per_row=False, tm=FFN_TM, tf=FFN_TF, side=(w_in_side,))

    qs, kv_s, z_s, ga_s, gc_s = _proj(hs, norm2_g, mod, w_in_b, per_row=True, tm=S_ROWS, q_dtype=F32)
    qp, kv_p, z_p, ga_p, gc_p = _proj(hp, norm2_g, mod, w_in_b, per_row=False, tm=PROJ_TM, q_dtype=BF16)

    side = [_row_slab_side(w[0]) for w in (ffn2_w1, ffn2_w3, ffn2_w2, w_attn_out, w_conv_out, w_out)]
    attn_p, side_b, (mod_late, mod_f) = _attn_prompt(qp, kv_p, tab_flat, sinks, side, c_all, late_ada)
    f2, mix_w = side_b[:3], side_b[3:]
    final = (final_norm_g[None, :], mod_f)

    hs, (kw, vw, ns) = mixers_s(qs, kv_s, z_s, ga_s, gc_s, hs, mod_late, mix_w)
    hp = _mix_conv_prompt(z_p, *conv_w, attn_p, ga_p, gc_p, hp, mod_late, *mix_w)

    ys = _ffn(hs, norm3_g, mod_late, 1, *f2, per_row=True, tm=S_ROWS, tf=FFN_TF, final=final)
    yp = _ffn(hp, norm3_g, mod_late, 1, *f2, per_row=False, tm=FFN_TM, tf=FFN_TF, final=final)

    w = min(WINDOW, SEQ)
    kv_shape = (1, 1, w, N_KV_HEADS, HEAD_DIM)
    k_win_p = kv_p[SEQ - w:, :KV_WIDTH].reshape(kv_shape)
    v_win_p = kv_p[SEQ - w:, KV_WIDTH:].reshape(kv_shape)
    conv_p_state = z_p[SEQ - HIST:].reshape(1, 1, HIST, C_CONV)
    s_shape = (1, DEC_BATCH, WBUF, N_KV_HEADS, HEAD_DIM)
    y_prompt = yp[None]
    y_sample = ys.reshape(DEC_SEQ, DEC_BATCH, D_MODEL).transpose(1, 0, 2)
    return (y_prompt, y_sample, k_win_p, v_win_p, conv_p_state,
            keys_on_rows(kw), keys_on_rows(vw), ns.transpose(1, 0, 2)[None])
```

```python
import functools

import numpy as np
import jax
import jax.numpy as jnp
from jax import lax
from jax.experimental import pallas as pl
from jax.experimental.pallas import tpu as pltpu

D_MODEL = 2048
SEQ = 8192
DEC_BATCH = 128
DEC_SEQ = 4
PAST_LEN = 16384
N_HEADS = 32
N_KV_HEADS = 8
HEAD_DIM = 64
GROUP = N_HEADS // N_KV_HEADS
ATTN_WIDTH = N_HEADS * HEAD_DIM
KV_WIDTH = N_KV_HEADS * HEAD_DIM
WINDOW = 128
Q_BLOCK = 128
ATTN_SCALE = HEAD_DIM ** -0.5
N_BUCKETS = 32
MAX_DISTANCE = 128
C_CONV = D_MODEL // 2
CONV_WIDTH = 31
HIST = CONV_WIDTH - 1
D_FF = 5632
NORM_EPS = 1e-6
NEG_INF = -1e30
IN_WIDTH = ATTN_WIDTH + 2 * KV_WIDTH + 2 * C_CONV + 2 * D_MODEL
WBUF = min(WINDOW, PAST_LEN)

VMEM_LIMIT = 60 * 1024 * 1024
SUBLANES = 8
LANES = 128
BF16_SUBLANES = 16

S_ROWS = DEC_BATCH * DEC_SEQ
N_COND = DEC_BATCH + SUBLANES
PROMPT_MOD_BLOCK = DEC_BATCH // SUBLANES
S_EXTRA = SUBLANES

F32 = jnp.float32
BF16 = jnp.bfloat16


def _cparams(n_axes):
    return pltpu.CompilerParams(dimension_semantics=("arbitrary",) * n_axes,
                                vmem_limit_bytes=VMEM_LIMIT)


def _dot(a, b):
    return jnp.dot(a, b, preferred_element_type=F32)


def _rms_mod(x, g, sh, sc):
    ms = jnp.mean(x * x, axis=-1, keepdims=True)
    y = x * lax.rsqrt(ms + NORM_EPS) * g
    return y * (1.0 + sc) + sh


def _mod_val(ref, per_row):
    return ref[...] if per_row else ref[0:1, :]


def _mod_spec(chunk, per_row):
    if per_row:
        return pl.BlockSpec((DEC_BATCH, D_MODEL), lambda *_: (0, chunk))
    return pl.BlockSpec((SUBLANES, D_MODEL), lambda *_: (PROMPT_MOD_BLOCK, chunk))


def _adaln_kernel(c_ref, w_ref, b_ref, o_ref, s_ref):
    @pl.when(pl.program_id(0) == 0)
    def _():
        s_ref[...] = jax.nn.silu(c_ref[...]).astype(BF16)

    o_ref[...] = _dot(s_ref[...], w_ref[...].astype(BF16)) + b_ref[...]


ADALN_TN = 1024


def _adaln(c_all, w, b, n):
    tn = ADALN_TN
    return pl.pallas_call(
        _adaln_kernel,
        out_shape=jax.ShapeDtypeStruct((N_COND, n), F32),
        grid=(n // tn,),
        in_specs=[pl.BlockSpec((N_COND, D_MODEL), lambda j: (0, 0)),
                  pl.BlockSpec((D_MODEL, tn), lambda j: (0, j)),
                  pl.BlockSpec((1, tn), lambda j: (0, j))],
        out_specs=pl.BlockSpec((N_COND, tn), lambda j: (0, j)),
        scratch_shapes=[pltpu.VMEM((N_COND, D_MODEL), BF16)],
        compiler_params=_cparams(1),
        name="adaln",
    )(c_all, w, b)


FFN_ROWS = 512
FFN_TM = 1024
FFN_TF = 512
FFN_TF_CAST = 256
PROJ_TM = 1024
W_IN_SIDE_COLS = 1024


def _side_cast(side_in, side_out):
    for i_ref, o_ref in zip(side_in, side_out):
        o_ref[...] = i_ref[...].astype(BF16)


def _side_specs(side):
    args = [a for a, _, _ in side]
    specs = [pl.BlockSpec(block, index_map) for _, block, index_map in side]
    shapes = [jax.ShapeDtypeStruct(a.shape, BF16) for a in args]
    return args, specs, shapes


def _ffn_kernel(*refs, tm, sub, per_row, n_f, final, cast_out, n_side, ring_x):
    refs = list(refs)
    x_ref, n_ref, sh_ref, sc_ref, g_ref, w1_ref, w3_ref, w2_ref = refs[:8]
    del refs[:8]
    if final:
        gf_ref, shf_ref, scf_ref = refs[:3]
        del refs[:3]
    side_in = refs[:n_side]
    del refs[:n_side]
    o_ref = refs.pop(0)
    if cast_out:
        w1o_ref, w3o_ref, w2o_ref = refs[:3]
        del refs[:3]
    side_out = refs[:n_side]
    del refs[:n_side]
    f = pl.program_id(1)
    if ring_x:
        assert not per_row
        u_ref, x_buf, x_sem, un_ref = refs
        i = pl.program_id(0)
        slot = lax.rem(i, 2)
        has_next = i + 1 < pl.num_programs(0)

        def x_copy(tile, dst_slot):
            return pltpu.make_async_copy(x_ref.at[pl.ds(tile * tm, tm), :], x_buf.at[dst_slot],
                                         x_sem.at[dst_slot])

        @pl.when((f == 0) & (i == 0))
        def _():
            x_copy(0, 0).start()
            x_copy(0, 0).wait()

        @pl.when((f == 0) & has_next)
        def _():
            x_copy(i + 1, 1 - slot).start()

        x_ref = x_buf.at[slot]
        xn_ref = x_buf.at[1 - slot]
    else:
        (u_ref,) = refs
    _side_cast(side_in, side_out)

    if cast_out:
        w1o_ref[...] = w1_ref[...].astype(BF16)
        w3o_ref[...] = w3_ref[...].astype(BF16)
        w2o_ref[...] = w2_ref[...].astype(BF16)
        w1_ref, w3_ref, w2_ref = w1o_ref, w3o_ref, w2o_ref

    def accumulate(first, last, *, pre=False, prep=False):
        for s in range(tm // FFN_ROWS):
            rows = slice(s * FFN_ROWS, (s + 1) * FFN_ROWS)
            subs = [slice(r0, r0 + sub) for r0 in range(s * FFN_ROWS, (s + 1) * FFN_ROWS, sub)]
            if first and pre and s == 0:
                for r in subs:
                    u_ref[r, :] = un_ref[r, :]
            elif first:
                for r in subs:
                    u_ref[r, :] = _rms_mod(x_ref[r, :], n_ref[...], _mod_val(sh_ref, per_row),
                                           _mod_val(sc_ref, per_row)).astype(BF16)
            if prep and s == 1:
                for r0 in range(0, FFN_ROWS, sub):
                    r = slice(r0, r0 + sub)
                    un_ref[r, :] = _rms_mod(xn_ref[r, :], n_ref[...], _mod_val(sh_ref, per_row),
                                            _mod_val(sc_ref, per_row)).astype(BF16)
            u = u_ref[rows, :]
            h1 = _dot(u, w1_ref[...])
            h3 = _dot(u, w3_ref[...])
            a = (jax.nn.silu(h1) * h3).astype(BF16)
            d = _dot(a, w2_ref[...])
            if first:
                o_ref[rows, :] = d
            else:
                o_ref[rows, :] += d
            if last:
                for r in subs:
                    h = x_ref[r, :] + 0.5 * _mod_val(g_ref, per_row) * o_ref[r, :]
                    if final:
                        h = _rms_mod(h, gf_ref[...], _mod_val(shf_ref, per_row), _mod_val(scf_ref, per_row))
                    o_ref[r, :] = h

    assert n_f > 3

    if ring_x:
        @pl.when((f == 0) & (i == 0))
        def _():
            accumulate(True, False)

        @pl.when((f == 0) & (i > 0))
        def _():
            accumulate(True, False, pre=True)

        @pl.when((f > 0) & (f < n_f - 1) & ~((f == n_f - 2) & has_next))
        def _():
            accumulate(False, False)

        @pl.when((f == n_f - 2) & has_next)
        def _():
            x_copy(i + 1, 1 - slot).wait()
            accumulate(False, False, prep=True)
    else:
        @pl.when(f == 0)
        def _():
            accumulate(True, False)

        @pl.when((f > 0) & (f < n_f - 1))
        def _():
            accumulate(False, False)

    @pl.when(f == n_f - 1)
    def _():
        accumulate(False, True)


def _ffn(x, norm_g, mod, chunk0, w1, w3, w2, *, per_row, tm, tf, final=None, cast_out=False, side=()):
    rows = x.shape[0]
    n_f = D_FF // tf
    sub = DEC_BATCH
    row_spec = pl.BlockSpec((tm, D_MODEL), lambda i, f: (i, 0))
    vec_spec = pl.BlockSpec((1, D_MODEL), lambda i, f: (0, 0))
    w_specs = [pl.BlockSpec((D_MODEL, tf), lambda i, f: (0, f)),
               pl.BlockSpec((D_MODEL, tf), lambda i, f: (0, f)),
               pl.BlockSpec((tf, D_MODEL), lambda i, f: (f, 0))]
    ring_x = rows > tm
    x_spec = pl.BlockSpec(memory_space=pl.ANY) if ring_x else row_spec
    in_specs = [x_spec, vec_spec,
                _mod_spec(chunk0, per_row), _mod_spec(chunk0 + 1, per_row), _mod_spec(chunk0 + 2, per_row)]
    in_specs += w_specs
    args = [x, norm_g, mod, mod, mod, w1, w3, w2]
    if final is not None:
        gf, mod_f = final
        in_specs += [vec_spec, _mod_spec(0, per_row), _mod_spec(1, per_row)]
        args += [gf, mod_f, mod_f]
    side_args, side_specs, side_shapes = _side_specs(side)
    in_specs += side_specs
    args += side_args
    out_shape = [jax.ShapeDtypeStruct((rows, D_MODEL), F32)]
    out_specs = [row_spec]
    if cast_out:
        assert rows == tm, "every weight block must be visited exactly once"
        out_shape += [jax.ShapeDtypeStruct(w.shape, BF16) for w in (w1, w3, w2)]
        out_specs += w_specs
    out_shape += side_shapes
    out_specs += side_specs
    outs = pl.pallas_call(
        functools.partial(_ffn_kernel, tm=tm, sub=sub, per_row=per_row, n_f=n_f,
                          final=final is not None, cast_out=cast_out, n_side=len(side), ring_x=ring_x),
        out_shape=out_shape,
        grid=(rows // tm, n_f),
        in_specs=in_specs,
        out_specs=out_specs,
        scratch_shapes=[pltpu.VMEM((tm, D_MODEL), BF16)]
        + ([pltpu.VMEM((2, tm, D_MODEL), F32), pltpu.SemaphoreType.DMA((2,)),
            pltpu.VMEM((FFN_ROWS, D_MODEL), BF16)] if ring_x else []),
        compiler_params=_cparams(2),
        name="ffn_final" if final is not None else "ffn",
    )(*args)
    return (outs[0], tuple(outs[1:])) if (cast_out or side) else outs[0]


PROJ_CHUNK = 256
PROJ_NORM_ROWS = 512
PROJ_CW = 512


def _proj_layout():
    cw = PROJ_CW
    n_q = ATTN_WIDTH // (2 * cw)
    n_kv = 2 * KV_WIDTH // (2 * cw)
    n_glu = C_CONV // cw
    n_gate = D_MODEL // (2 * cw)
    pre = n_q + n_kv
    steps = pre + n_glu + 2 * n_gate

    def block_a(j):
        return jnp.where((j >= pre) & (j < pre + n_glu), j + pre, 2 * j)

    def block_b(j):
        return jnp.where((j >= pre) & (j < pre + n_glu), j + pre + n_glu, 2 * j + 1)

    return n_q, n_kv, n_glu, n_gate, steps, block_a, block_b


def _proj_kernel(h_ref, n_ref, sh_ref, sc_ref, wa_ref, wb_ref, q_ref, kv_ref, z_ref, ga_ref, gc_ref, u_ref,
                 *, tm, sub, per_row):
    cw = PROJ_CW
    n_q, n_kv, n_glu, n_gate, _, _, _ = _proj_layout()
    j = pl.program_id(1)

    n_chunks = cw // PROJ_CHUNK

    def chunks(w_ref, rows=slice(None)):
        for c in range(n_chunks):
            cols = slice(c * PROJ_CHUNK, (c + 1) * PROJ_CHUNK)
            yield c * PROJ_CHUNK, _dot(u_ref[rows, :], w_ref[:, cols])

    def pair_store(o_ref, fn, rows=slice(None)):
        for half, w_ref in enumerate((wa_ref, wb_ref)):
            for c0, acc in chunks(w_ref, rows):
                o_ref[rows, half * cw + c0:half * cw + c0 + PROJ_CHUNK] = fn(acc).astype(o_ref.dtype)

    @pl.when(j == 0)
    def _():
        for g0 in range(0, tm, PROJ_NORM_ROWS):
            for r0 in range(g0, g0 + PROJ_NORM_ROWS, sub):
                r = slice(r0, r0 + sub)
                u_ref[r, :] = _rms_mod(h_ref[r, :], n_ref[...], _mod_val(sh_ref, per_row),
                                       _mod_val(sc_ref, per_row)).astype(BF16)
            pair_store(q_ref, lambda acc: acc, slice(g0, g0 + PROJ_NORM_ROWS))

    lo = 0

    @pl.when((j > 0) & (j < n_q))
    def _():
        pair_store(q_ref, lambda acc: acc)

    lo += n_q

    @pl.when((j >= lo) & (j < lo + n_kv))
    def _():
        pair_store(kv_ref, lambda acc: acc)

    lo += n_kv

    @pl.when((j >= lo) & (j < lo + n_glu))
    def _():
        for (c0, a), (_, g) in zip(chunks(wa_ref), chunks(wb_ref)):
            z_ref[:, c0:c0 + PROJ_CHUNK] = a * jax.nn.sigmoid(g)

    lo += n_glu

    @pl.when((j >= lo) & (j < lo + n_gate))
    def _():
        pair_store(ga_ref, jax.nn.sigmoid)

    lo += n_gate

    @pl.when(j >= lo)
    def _():
        pair_store(gc_ref, jax.nn.sigmoid)


def _proj(h, norm_g, mod, w, *, per_row, tm, q_dtype):
    rows = h.shape[0]
    sub = DEC_BATCH
    cw = PROJ_CW
    n_q, n_kv, n_glu, n_gate, steps, block_a, block_b = _proj_layout()

    def out_spec(width, first, count):
        return pl.BlockSpec((tm, width), lambda i, j: (i, jnp.clip(j - first, 0, count - 1)))

    wa_spec = pl.BlockSpec((D_MODEL, cw), lambda i, j: (0, block_a(j)))
    wb_spec = pl.BlockSpec((D_MODEL, cw), lambda i, j: (0, block_b(j)))
    out_shape = [jax.ShapeDtypeStruct((rows, ATTN_WIDTH), q_dtype),
                 jax.ShapeDtypeStruct((rows, 2 * KV_WIDTH), F32),
                 jax.ShapeDtypeStruct((rows, C_CONV), F32),
                 jax.ShapeDtypeStruct((rows, D_MODEL), BF16),
                 jax.ShapeDtypeStruct((rows, D_MODEL), BF16)]
    out_specs = [out_spec(2 * cw, 0, n_q),
                 out_spec(2 * cw, n_q, n_kv),
                 out_spec(cw, n_q + n_kv, n_glu),
                 out_spec(2 * cw, n_q + n_kv + n_glu, n_gate),
                 out_spec(2 * cw, n_q + n_kv + n_glu + n_gate, n_gate)]
    return pl.pallas_call(
        functools.partial(_proj_kernel, tm=tm, sub=sub, per_row=per_row),
        out_shape=out_shape,
        grid=(rows // tm, steps),
        in_specs=[pl.BlockSpec((tm, D_MODEL), lambda i, j: (i, 0)),
                  pl.BlockSpec((1, D_MODEL), lambda i, j: (0, 0)),
                  _mod_spec(3, per_row), _mod_spec(4, per_row), wa_spec, wb_spec],
        out_specs=out_specs,
        scratch_shapes=[pltpu.VMEM((tm, D_MODEL), BF16)],
        compiler_params=_cparams(2),
        name="proj",
    )(h, norm_g, mod, mod, w, w)


def _t5_bucket_np(dist):
    exact = N_BUCKETS // 2
    d = np.maximum(dist, 0)
    df = np.maximum(d, 1).astype(np.float32)
    large = exact + (np.log(df / np.float32(exact)) / np.float32(np.log(MAX_DISTANCE / exact))
                     * np.float32(N_BUCKETS - exact)).astype(np.int32)
    large = np.minimum(large, N_BUCKETS - 1)
    return np.where(d < exact, d, large).astype(np.int32)


def _prompt_codes():
    qi = np.arange(Q_BLOCK)[:, None]
    kj = np.arange(2 * Q_BLOCK)[None, :]
    dist = qi + Q_BLOCK - kj
    valid = (dist >= 0) & (dist <= WINDOW)
    return np.where(valid, _t5_bucket_np(dist), -1).astype(np.int32)


def _sample_codes():
    t = np.repeat(np.arange(DEC_SEQ), GROUP * N_KV_HEADS)[:, None]
    j = np.arange(WBUF)[None, :]
    dist = t + WBUF - j
    cache = np.where((dist >= 0) & (dist <= WINDOW), _t5_bucket_np(dist), -1)
    c = np.arange(S_EXTRA)[None, :]
    t_new = c - (S_EXTRA - DEC_SEQ)
    dist = t - t_new
    extra = np.where((t_new >= 0) & (dist >= 0) & (dist <= WINDOW), _t5_bucket_np(dist), -1)
    extra = np.where(c == 0, N_BUCKETS, extra)
    return cache.astype(np.int32), extra.astype(np.int32)


BIAS_ROWS = 64
ATTN_QB = 2


def _attn_p_kernel(*refs, n_side, n_ada):
    refs = list(refs)
    tab_ref, sink_ref, code_ref, q_ref, kvc_ref, kvp_ref = refs[:6]
    del refs[:6]
    side_in = refs[:n_side]
    del refs[:n_side]
    c_ref = refs.pop(0)
    ada_in = [(refs[2 * k], refs[2 * k + 1]) for k in range(n_ada)]
    del refs[:2 * n_ada]
    o_ref = refs.pop(0)
    side_out = refs[:n_side]
    del refs[:n_side]
    ada_out = refs[:n_ada]
    del refs[:n_ada]
    bias_ref, silu_c_ref = refs
    n = pl.program_id(0)
    _side_cast(side_in, side_out)

    @pl.when(n == 0)
    def _():
        silu_c_ref[...] = jax.nn.silu(c_ref[...]).astype(BF16)

    for (w_ref, b_ref), m_ref in zip(ada_in, ada_out):
        m_ref[...] = _dot(silu_c_ref[...], w_ref[...].astype(BF16)) + b_ref[...]

    n_keys = 2 * Q_BLOCK
    wide = GROUP * Q_BLOCK

    @pl.when(n == 0)
    def _():
        def kv_body(kv, carry):
            for g in range(GROUP):
                for r in range(n_keys // BIAS_ROWS):
                    rows = slice(r * BIAS_ROWS, (r + 1) * BIAS_ROWS)
                    code = code_ref[rows, :]
                    acc = jnp.full((BIAS_ROWS, Q_BLOCK), NEG_INF, F32)
                    for bk in range(N_BUCKETS):
                        acc = jnp.where(code == bk, tab_ref[bk * N_HEADS + kv * GROUP + g], acc)
                    bias_ref[kv, rows, g * Q_BLOCK:(g + 1) * Q_BLOCK] = acc
            return carry
        lax.fori_loop(0, N_KV_HEADS, kv_body, 0)

    ones = jnp.ones((BF16_SUBLANES, n_keys), BF16)
    zeros = jnp.zeros((HEAD_DIM, wide), BF16)
    tile = (n_keys, LANES)
    key = lax.broadcasted_iota(jnp.int32, tile, 0)
    lane = lax.broadcasted_iota(jnp.int32, tile, 1)
    pen = jnp.where((lane == 0) & (key < Q_BLOCK) & (n == 0), NEG_INF, 0.0).astype(BF16)
    pen_rows = (lax.broadcasted_iota(jnp.int32, (LANES, wide), 0) == 0).astype(BF16)

    def query_block(a):
        rows = slice(a * Q_BLOCK, (a + 1) * Q_BLOCK)
        q_t = q_ref[rows, :].astype(F32).T

        def keys(cols):
            prev = kvp_ref[:, cols] if a == 0 else kvc_ref[(a - 1) * Q_BLOCK:a * Q_BLOCK, cols]
            return jnp.concatenate([prev, kvc_ref[rows, cols]], axis=0)

        def pair_kv(pair):
            k2 = keys(slice(pair * LANES, (pair + 1) * LANES)).astype(BF16)
            if a == 0:
                k2 = jnp.concatenate([k2, pen], axis=1)
            v2_t = keys(slice(KV_WIDTH + pair * LANES, KV_WIDTH + (pair + 1) * LANES)).T.astype(BF16)
            return k2, v2_t

        def scores(kv, k2):
            q4 = jnp.concatenate(
                [q_t[(kv * GROUP + g) * HEAD_DIM:(kv * GROUP + g + 1) * HEAD_DIM, :] for g in range(GROUP)],
                axis=1)
            q4 = (q4 * ATTN_SCALE).astype(BF16)
            parts = [q4, zeros] if kv % 2 == 0 else [zeros, q4]
            return _dot(k2, jnp.concatenate(parts + ([pen_rows] if a == 0 else []), axis=0))

        kvs = [pair_kv(pair) for pair in range(N_KV_HEADS // 2)]
        for kv in range(N_KV_HEADS):
            pair, parity = divmod(kv, 2)
            s = scores(kv, kvs[pair][0]) + bias_ref[kv]
            sink = jnp.concatenate(
                [jnp.full((1, Q_BLOCK), sink_ref[kv * GROUP + g], F32) for g in range(GROUP)], axis=1)
            m = jnp.maximum(jnp.max(s, axis=0, keepdims=True), sink)
            p = jnp.exp(s - m).astype(BF16)
            lhs = jnp.concatenate([kvs[pair][1][parity * HEAD_DIM:(parity + 1) * HEAD_DIM, :], ones], axis=0)
            o_t = _dot(lhs, p)
            l = o_t[HEAD_DIM:HEAD_DIM + 1, :] + jnp.exp(sink - m)
            o_n = o_t[0:HEAD_DIM, :] * (1.0 / l)
            o_kv = jnp.concatenate([o_n[:, g * Q_BLOCK:(g + 1) * Q_BLOCK] for g in range(GROUP)], axis=0)
            o_ref[rows, kv * GROUP * HEAD_DIM:(kv + 1) * GROUP * HEAD_DIM] = o_kv.T.astype(o_ref.dtype)

    for a in range(ATTN_QB):
        query_block(a)


ATTN_STEPS = SEQ // (ATTN_QB * Q_BLOCK)


def _row_slab_side(w):
    slab = w.shape[0] // ATTN_STEPS
    assert slab * ATTN_STEPS == w.shape[0] and slab % BF16_SUBLANES == 0
    return (w, (slab, w.shape[1]), lambda n: (n, 0))


def _attn_prompt(q, kv, tab_flat, sinks, side, c_all, ada):
    step_rows = ATTN_QB * Q_BLOCK
    code_t = jnp.asarray(np.ascontiguousarray(_prompt_codes().T))
    smem = pl.BlockSpec(memory_space=pltpu.SMEM)
    side_args, side_specs, side_shapes = _side_specs(side)
    ada_args, ada_in_specs, ada_out_specs, ada_shapes = [], [], [], []
    for w, b, col0, n_cols in ada:
        tn = n_cols // ATTN_STEPS
        assert tn * ATTN_STEPS == n_cols and tn % LANES == 0 and col0 % tn == 0
        first = col0 // tn
        ada_args += [w, b]
        ada_in_specs += [pl.BlockSpec((D_MODEL, tn), lambda n, first=first: (0, first + n)),
                         pl.BlockSpec((1, tn), lambda n, first=first: (0, first + n))]
        ada_out_specs.append(pl.BlockSpec((N_COND, tn), lambda n: (0, n)))
        ada_shapes.append(jax.ShapeDtypeStruct((N_COND, n_cols), F32))
    out_spec = pl.BlockSpec((step_rows, ATTN_WIDTH), lambda n: (n, 0))
    outs = pl.pallas_call(
        functools.partial(_attn_p_kernel, n_side=len(side), n_ada=len(ada)),
        out_shape=[jax.ShapeDtypeStruct((SEQ, ATTN_WIDTH), BF16)] + side_shapes + ada_shapes,
        grid=(ATTN_STEPS,),
        in_specs=[smem, smem,
                  pl.BlockSpec((2 * Q_BLOCK, Q_BLOCK), lambda n: (0, 0)),
                  pl.BlockSpec((step_rows, ATTN_WIDTH), lambda n: (n, 0)),
                  pl.BlockSpec((step_rows, 2 * KV_WIDTH), lambda n: (n, 0)),
                  pl.BlockSpec((Q_BLOCK, 2 * KV_WIDTH), lambda n: (jnp.maximum(ATTN_QB * n - 1, 0), 0))]
        + side_specs + [pl.BlockSpec((N_COND, D_MODEL), lambda n: (0, 0))] + ada_in_specs,
        out_specs=[out_spec] + side_specs + ada_out_specs,
        scratch_shapes=[pltpu.VMEM((N_KV_HEADS, 2 * Q_BLOCK, GROUP * Q_BLOCK), F32),
                        pltpu.VMEM((N_COND, D_MODEL), BF16)],
        compiler_params=_cparams(1),
        name="attn_prompt",
    )(tab_flat, sinks, code_t, q, kv, kv, *side_args, c_all, *ada_args)
    n_side = len(side)
    return outs[0], tuple(outs[1:1 + n_side]), tuple(outs[1 + n_side:])


S_BB = 8


def _regroup_heads(x, to_group_major):
    half = HEAD_DIM
    assert LANES == 2 * half
    low = lax.broadcasted_iota(jnp.int32, (x.shape[0], LANES), 1) < half
    n_cols = ATTN_WIDTH // LANES
    outer, inner = (GROUP, N_KV_HEADS) if to_group_major else (N_KV_HEADS, GROUP)
    cols = []
    for c in range(n_cols):
        o, i = divmod(2 * c, inner)
        src = [(i + d) * outer + o for d in range(2)]
        a = x[:, (src[0] // 2) * LANES:(src[0] // 2 + 1) * LANES]
        b = x[:, (src[1] // 2) * LANES:(src[1] // 2 + 1) * LANES]
        assert src[0] % 2 == src[1] % 2
        if src[0] % 2 == 0:
            cols.append(jnp.where(low, a, pltpu.roll(b, half, axis=1)))
        else:
            cols.append(jnp.where(low, pltpu.roll(a, half, axis=1), b))
    return jnp.concatenate(cols, axis=1)


def _attn_s_kernel(th_ref, code_c_ref, code_x_ref, q_ref, kv_ref, ck_ref, cv_ref, o_ref, kw_ref, vw_ref,
                   bias_c_ref, bias_x_ref, nk_ref, nv_ref, qe_ref, qg_ref, og_ref):
    rows_all = DEC_SEQ * N_HEADS
    for t in range(DEC_SEQ):
        qg_ref[t] = _regroup_heads(q_ref[t], True)

    @pl.when(pl.program_id(0) == 0)
    def _():
        for code_ref, bias_ref in ((code_c_ref, bias_c_ref), (code_x_ref, bias_x_ref)):
            code = code_ref[...]
            acc = jnp.full(code.shape, NEG_INF, F32)
            for bk in range(N_BUCKETS + 1):
                acc = jnp.where(code == bk, th_ref[:, bk:bk + 1], acc)
            bias_ref[...] = acc
        nk_ref[...] = jnp.zeros_like(nk_ref)
        nv_ref[...] = jnp.zeros_like(nv_ref)

    lane_head = lax.broadcasted_iota(jnp.int32, (N_KV_HEADS, KV_WIDTH), 1) // HEAD_DIM
    row_head = lax.broadcasted_iota(jnp.int32, (N_KV_HEADS, KV_WIDTH), 0)
    diag = lane_head == row_head
    out_mask = (lax.broadcasted_iota(jnp.int32, (rows_all, KV_WIDTH), 0) % N_KV_HEADS
                == lax.broadcasted_iota(jnp.int32, (rows_all, KV_WIDTH), 1) // HEAD_DIM)
    new_lanes = lax.broadcasted_iota(jnp.int32, (KV_WIDTH, WBUF), 1) >= WBUF - DEC_SEQ
    bias_c = bias_c_ref[...]
    bias_x = bias_x_ref[...]
    nt_dims = (((1,), (1,)), ((), ()))
    x0 = WBUF - S_EXTRA

    for b in range(S_BB):
        for t in range(DEC_SEQ):
            nk_ref[WBUF - DEC_SEQ + t:WBUF - DEC_SEQ + t + 1, :] = kv_ref[t, b:b + 1, 0:KV_WIDTH]
            nv_ref[WBUF - DEC_SEQ + t:WBUF - DEC_SEQ + t + 1, :] = kv_ref[t, b:b + 1, KV_WIDTH:2 * KV_WIDTH]
            for g in range(GROUP):
                q_row = qg_ref[t, b:b + 1, g * KV_WIDTH:(g + 1) * KV_WIDTH]
                piece = jnp.where(diag, jnp.broadcast_to(q_row, (N_KV_HEADS, KV_WIDTH)), 0.0)
                r0 = (t * GROUP + g) * N_KV_HEADS
                qe_ref[r0:r0 + N_KV_HEADS, :] = piece
        k_t = ck_ref[b]
        v_t = cv_ref[b]
        kw_ref[b] = jnp.where(new_lanes, nk_ref[...].T, pltpu.roll(k_t, WBUF - DEC_SEQ, axis=1))
        vw_ref[b] = jnp.where(new_lanes, nv_ref[...].T, pltpu.roll(v_t, WBUF - DEC_SEQ, axis=1))

        qe = (qe_ref[...] * ATTN_SCALE).astype(BF16)
        s_c = _dot(qe, k_t.astype(BF16)) + bias_c
        s_x = lax.dot_general(qe, nk_ref[x0:WBUF, :].astype(BF16), nt_dims,
                              preferred_element_type=F32) + bias_x
        m = jnp.maximum(jnp.max(s_c, axis=-1, keepdims=True), jnp.max(s_x, axis=-1, keepdims=True))
        p_c = jnp.exp(s_c - m)
        p_x = jnp.exp(s_x - m)
        l = jnp.sum(p_c, axis=-1, keepdims=True) + jnp.sum(p_x, axis=-1, keepdims=True)
        o = lax.dot_general(p_c.astype(BF16), v_t.astype(BF16), nt_dims, preferred_element_type=F32)
        o = (o + _dot(p_x.astype(BF16), nv_ref[x0:WBUF, :].astype(BF16))) / l
        o = jnp.where(out_mask, o, 0.0)
        for t in range(DEC_SEQ):
            for g in range(GROUP):
                r0 = (t * GROUP + g) * N_KV_HEADS
                row = jnp.sum(o[r0:r0 + N_KV_HEADS, :], axis=0, keepdims=True)
                og_ref[t, b:b + 1, g * KV_WIDTH:(g + 1) * KV_WIDTH] = row
    for t in range(DEC_SEQ):
        o_ref[t] = _regroup_heads(og_ref[t], False)


def _attn_sample(q, kv, cache_k_t, cache_v_t, th):
    code_c, code_x = (jnp.asarray(c) for c in _sample_codes())
    rows_all = DEC_SEQ * N_HEADS
    full = lambda a: pl.BlockSpec(a.shape, lambda i: (0,) * a.ndim)
    tb_spec = lambda w: pl.BlockSpec((DEC_SEQ, S_BB, w), lambda i: (0, i, 0))
    cache_spec = pl.BlockSpec((S_BB, KV_WIDTH, WBUF), lambda i: (i, 0, 0))
    cache_shape = jax.ShapeDtypeStruct((DEC_BATCH, KV_WIDTH, WBUF), F32)
    return pl.pallas_call(
        _attn_s_kernel,
        out_shape=(jax.ShapeDtypeStruct((DEC_SEQ, DEC_BATCH, ATTN_WIDTH), F32), cache_shape, cache_shape),
        grid=(DEC_BATCH // S_BB,),
        in_specs=[full(th), full(code_c), full(code_x),
                  tb_spec(ATTN_WIDTH), tb_spec(2 * KV_WIDTH), cache_spec, cache_spec],
        out_specs=(tb_spec(ATTN_WIDTH), cache_spec, cache_spec),
        scratch_shapes=[pltpu.VMEM((rows_all, WBUF), F32),
                        pltpu.VMEM((rows_all, S_EXTRA), F32),
                        pltpu.VMEM((WBUF, KV_WIDTH), F32),
                        pltpu.VMEM((WBUF, KV_WIDTH), F32),
                        pltpu.VMEM((rows_all, KV_WIDTH), F32),
                        pltpu.VMEM((DEC_SEQ, S_BB, ATTN_WIDTH), F32),
                        pltpu.VMEM((DEC_SEQ, S_BB, ATTN_WIDTH), F32)],
        compiler_params=_cparams(1),
        name="attn_sample",
    )(th, code_c, code_x, q, kv, cache_k_t, cache_v_t)


CONV_RC = 32
CONV_ROWS = 128
CONV_HALO = 32


def _ln_silu(acc, lg, lb):
    mu = jnp.mean(acc, axis=-1, keepdims=True)
    xc = acc - mu
    var = jnp.mean(xc * xc, axis=-1, keepdims=True)
    return jax.nn.silu(xc * lax.rsqrt(var + NORM_EPS) * lg + lb)


def _conv_stage(first, zc_ref, zh_ref, s_ref):
    s_ref[0:CONV_HALO, :] = jnp.where(first, 0.0, zh_ref[...])
    s_ref[CONV_HALO:, :] = zc_ref[...]


def _conv_rows(rc, w_ref, b_ref, lg_ref, lb_ref, y_ref, s_ref, c_ref):
    off = CONV_HALO - HIST
    groups = [[j for j in range(CONV_WIDTH) if (j + off) % SUBLANES == r] for r in range(SUBLANES)]
    t0 = rc * CONV_ROWS
    for lc in range(C_CONV // LANES):
        lanes = slice(lc * LANES, (lc + 1) * LANES)
        out = jnp.broadcast_to(b_ref[:, lanes], (CONV_ROWS, LANES))
        for r, taps in enumerate(groups):
            n_rows = CONV_ROWS + (SUBLANES if r else 0)
            part = None
            for j in taps:
                base = t0 + (j + off) - r
                term = w_ref[j:j + 1, lanes] * s_ref[base:base + n_rows, lanes]
                part = term if part is None else part + term
            out = out + part[r:r + CONV_ROWS, :]
        c_ref[t0:t0 + CONV_ROWS, lanes] = out
    for r in range(CONV_ROWS // CONV_RC):
        rows = slice(t0 + r * CONV_RC, t0 + (r + 1) * CONV_RC)
        y_ref[rows, :] = _ln_silu(c_ref[rows, :], lg_ref[...], lb_ref[...]).astype(BF16)


def _mix_conv_kernel(zc_ref, zh_ref, w_ref, b_ref, lg_ref, lb_ref,
                     at_ref, ga_ref, gc_ref, h_ref, g2_ref, wao_ref, wco_ref, wout_ref,
                     o_ref, s_ref, c_ref, y_ref, *, tm):
    _conv_stage(pl.program_id(0) == 0, zc_ref, zh_ref, s_ref)
    for rc in range(tm // CONV_ROWS):
        _conv_rows(rc, w_ref, b_ref, lg_ref, lb_ref, y_ref, s_ref, c_ref)
    a = _dot(at_ref[...], wao_ref[...])
    c = _dot(y_ref[...], wco_ref[...])
    merged = (ga_ref[...].astype(F32) * a + gc_ref[...].astype(F32) * c).astype(BF16)
    r = _dot(merged, wout_ref[...])
    o_ref[...] = h_ref[...] + g2_ref[0:1, :] * r


MIX_TM = 256


def _mix_conv_prompt(z, dw_w, dw_b, ln_g, ln_b, attn, ga, gc, h, mod, w_ao, w_co, w_out):
    rows = h.shape[0]
    tm = MIX_TM
    n = rows // tm
    ratio = tm // CONV_HALO
    vec = pl.BlockSpec((1, C_CONV), lambda s: (0, 0))
    row = lambda w: pl.BlockSpec((tm, w), lambda s: (s, 0))
    resident = lambda shape: pl.BlockSpec(shape, lambda s: (0, 0), pipeline_mode=pl.Buffered(1))
    return pl.pallas_call(
        functools.partial(_mix_conv_kernel, tm=tm),
        out_shape=jax.ShapeDtypeStruct((rows, D_MODEL), F32),
        grid=(n,),
        in_specs=[row(C_CONV),
                  pl.BlockSpec((CONV_HALO, C_CONV), lambda s: (jnp.maximum(s * ratio - 1, 0), 0)),
                  pl.BlockSpec((CONV_WIDTH, C_CONV), lambda s: (0, 0)), vec, vec, vec,
                  row(ATTN_WIDTH), row(D_MODEL), row(D_MODEL), row(D_MODEL),
                  _mod_spec(0, False),
                  resident((ATTN_WIDTH, D_MODEL)), resident((C_CONV, D_MODEL)), resident((D_MODEL, D_MODEL))],
        out_specs=row(D_MODEL),
        scratch_shapes=[pltpu.VMEM((tm + CONV_HALO, C_CONV), F32), pltpu.VMEM((tm, C_CONV), F32),
                        pltpu.VMEM((tm, C_CONV), BF16)],
        compiler_params=_cparams(1),
        name="mix_conv",
    )(z, z, dw_w, dw_b, ln_g, ln_b, attn, ga, gc, h, mod, w_ao, w_co, w_out)


CONV_S_BB = 32


def _conv_s_kernel(z_ref, hist_ref, w_ref, b_ref, lg_ref, lb_ref, y_ref, ns_ref):
    for t in range(DEC_SEQ):
        acc = jnp.broadcast_to(b_ref[...], (CONV_S_BB, C_CONV))
        for j in range(CONV_WIDTH):
            i = t + j
            src = hist_ref[i] if i < HIST else z_ref[i - HIST]
            acc = acc + w_ref[j:j + 1, :] * src
        y_ref[t] = _ln_silu(acc, lg_ref[...], lb_ref[...]).astype(BF16)
    keep = HIST - DEC_SEQ
    ns_ref[0:keep] = hist_ref[DEC_SEQ:HIST]
    ns_ref[keep:HIST] = z_ref[...]


def _conv_sample(z, hist_t, dw_w, dw_b, ln_g, ln_b):
    vec = pl.BlockSpec((1, C_CONV), lambda i: (0, 0))
    tb = pl.BlockSpec((DEC_SEQ, CONV_S_BB, C_CONV), lambda i: (0, i, 0))
    st = pl.BlockSpec((HIST, CONV_S_BB, C_CONV), lambda i: (0, i, 0))
    return pl.pallas_call(
        _conv_s_kernel,
        out_shape=(jax.ShapeDtypeStruct((DEC_SEQ, DEC_BATCH, C_CONV), BF16),
                   jax.ShapeDtypeStruct(hist_t.shape, F32)),
        grid=(DEC_BATCH // CONV_S_BB,),
        in_specs=[tb, st, pl.BlockSpec((CONV_WIDTH, C_CONV), lambda i: (0, 0)), vec, vec, vec],
        out_specs=(tb, st),
        compiler_params=_cparams(1),
        name="conv_sample",
    )(z, hist_t, dw_w, dw_b, ln_g, ln_b)


def _mix_s_kernel(at_ref, y_ref, ga_ref, gc_ref, h_ref, g2_ref, wao_ref, wco_ref, wout_ref, o_ref):
    a = _dot(at_ref[...].astype(BF16), wao_ref[...])
    c = _dot(y_ref[...], wco_ref[...])
    merged = (ga_ref[...].astype(F32) * a + gc_ref[...].astype(F32) * c).astype(BF16)
    r = _dot(merged, wout_ref[...])
    o_ref[...] = h_ref[...] + g2_ref[...] * r


def _mix_sample(attn, y, ga, gc, h, mod, w_ao, w_co, w_out):
    row = lambda w: pl.BlockSpec((DEC_BATCH, w), lambda t: (t, 0))
    resident = lambda shape: pl.BlockSpec(shape, lambda t: (0, 0), pipeline_mode=pl.Buffered(1))
    return pl.pallas_call(
        _mix_s_kernel,
        out_shape=jax.ShapeDtypeStruct((S_ROWS, D_MODEL), F32),
        grid=(DEC_SEQ,),
        in_specs=[row(ATTN_WIDTH), row(C_CONV), row(D_MODEL), row(D_MODEL), row(D_MODEL),
                  _mod_spec(0, True),
                  resident((ATTN_WIDTH, D_MODEL)), resident((C_CONV, D_MODEL)), resident((D_MODEL, D_MODEL))],
        out_specs=row(D_MODEL),
        compiler_params=_cparams(1),
        name="mix_sample",
    )(attn, y, ga, gc, h, mod, w_ao, w_co, w_out)


def kernel(x_prompt, x_sample, cache_k, cache_v, state_conv, c_prompt, c_sample, rel_bias_table, norm1_g, ffn1_w1, ffn1_w3, ffn1_w2, norm2_g, w_in, attn_sinks, conv_dw_w, conv_dw_b, conv_ln_g, conv_ln_b, w_conv_out, w_attn_out, w_out, norm3_g, ffn2_w1, ffn2_w3, ffn2_w2, w_ada, b_ada, final_norm_g, w_ada_final, b_ada_final):
    c_all = jnp.concatenate([c_sample, c_prompt, jnp.zeros((N_COND - DEC_BATCH - 1, D_MODEL), F32)], axis=0)
    early = 5 * D_MODEL
    mod = _adaln(c_all, w_ada[0], b_ada, early)
    late_ada = [(w_ada[0], b_ada, early, 4 * D_MODEL), (w_ada_final, b_ada_final[None, :], 0, 2 * D_MODEL)]

    sinks = attn_sinks[0]
    tab_flat = rel_bias_table.reshape(-1)
    tab_ext = jnp.concatenate([rel_bias_table, sinks[None, :]], axis=0)
    th = tab_ext.T.reshape(N_KV_HEADS, GROUP, N_BUCKETS + 1).transpose(1, 0, 2)
    th = jnp.tile(th.reshape(N_HEADS, N_BUCKETS + 1), (DEC_SEQ, 1))

    xp = x_prompt[0]
    xs = x_sample.transpose(1, 0, 2).reshape(S_ROWS, D_MODEL)

    conv_w = (conv_dw_w[0], conv_dw_b, conv_ln_g, conv_ln_b)


    def keys_on_lanes(cache):
        return cache[0].transpose(0, 2, 3, 1).reshape(DEC_BATCH, KV_WIDTH, WBUF)

    def keys_on_rows(win):
        return win.reshape(DEC_BATCH, N_KV_HEADS, HEAD_DIM, WBUF).transpose(0, 3, 1, 2)[None]

    def mixers_s(q, kv, z, ga, gc, h, mod_late, mix_w):
        o, kw, vw = _attn_sample(q.reshape(DEC_SEQ, DEC_BATCH, ATTN_WIDTH),
                                 kv.reshape(DEC_SEQ, DEC_BATCH, 2 * KV_WIDTH),
                                 keys_on_lanes(cache_k), keys_on_lanes(cache_v), th)
        y, ns = _conv_sample(z.reshape(DEC_SEQ, DEC_BATCH, C_CONV), state_conv[0].transpose(1, 0, 2), *conv_w)
        h = _mix_sample(o.reshape(S_ROWS, ATTN_WIDTH), y.reshape(S_ROWS, C_CONV), ga, gc, h, mod_late, *mix_w)
        return h, (kw, vw, ns)

    hs, f1 = _ffn(xs, norm1_g, mod, 0, ffn1_w1[0], ffn1_w3[0], ffn1_w2[0], per_row=True,
                  tm=S_ROWS, tf=FFN_TF_CAST, cast_out=True)
    w_in_rows = D_MODEL // (SEQ // FFN_TM)
    w_in_side = (w_in[0], (w_in_rows, W_IN_SIDE_COLS),
                 lambda i, f: (i, jnp.minimum(f, IN_WIDTH // W_IN_SIDE_COLS - 1)))
    hp, (w_in_b,) = _ffn(xp, norm1_g, mod, 0, *f1, per_row=False, tm=FFN_TM, tf=FFN_TF, side=(w_in_side,))

    qs, kv_s, z_s, ga_s, gc_s = _proj(hs, norm2_g, mod, w_in_b, per_row=True, tm=S_ROWS, q_dtype=F32)
    qp, kv_p, z_p, ga_p, gc_p = _proj(hp, norm2_g, mod, w_in_b, per_row=False, tm=PROJ_TM, q_dtype=BF16)

    side = [_row_slab_side(w[0]) for w in (ffn2_w1, ffn2_w3, ffn2_w2, w_attn_out, w_conv_out, w_out)]
    attn_p, side_b, (mod_late, mod_f) = _attn_prompt(qp, kv_p, tab_flat, sinks, side, c_all, late_ada)
    f2, mix_w = side_b[:3], side_b[3:]
    final = (final_norm_g[None, :], mod_f)

    hs, (kw, vw, ns) = mixers_s(qs, kv_s, z_s, ga_s, gc_s, hs, mod_late, mix_w)
    hp = _mix_conv_prompt(z_p, *conv_w, attn_p, ga_p, gc_p, hp, mod_late, *mix_w)

    ys = _ffn(hs, norm3_g, mod_late, 1, *f2, per_row=True, tm=S_ROWS, tf=FFN_TF, final=final)
    yp = _ffn(hp, norm3_g, mod_late, 1, *f2, per_row=False, tm=FFN_TM, tf=FFN_TF, final=final)

    w = min(WINDOW, SEQ)
    kv_shape = (1, 1, w, N_KV_HEADS, HEAD_DIM)
    k_win_p = kv_p[SEQ - w:, :KV_WIDTH].reshape(kv_shape)
    v_win_p = kv_p[SEQ - w:, KV_WIDTH:].reshape(kv_shape)
    conv_p_state = z_p[SEQ - HIST:].reshape(1, 1, HIST, C_CONV)
    s_shape = (1, DEC_BATCH, WBUF, N_KV_HEADS, HEAD_DIM)
    y_prompt = yp[None]
    y_sample = ys.reshape(DEC_SEQ, DEC_BATCH, D_MODEL).transpose(1, 0, 2)
    return (y_prompt, y_sample, k_win_p, v_win_p, conv_p_state,
            keys_on_rows(kw), keys_on_rows(vw), ns.transpose(1, 0, 2)[None])
```
